```python
import jax, jax.numpy as jnp
from jax import lax
import numpy as np

D_MODEL = 1024
BATCH = 1
SEQ = 16384
DEPTH = 4
DEC_BATCH = 16
DEC_SEQ = 16
PAST_LEN = 2048

CHUNK = 64
N_MIXERS = 4
Q_BLOCK = 128
EPS = 1e-6
HEAD_DIM = 64
N_HEADS = D_MODEL // HEAD_DIM
MLA_HEADS = N_HEADS
MLA_NOPE = 64
MLA_ROPE = 32
MLA_V = D_MODEL // MLA_HEADS
MLA_Q_LORA = 384
MLA_KV_LORA = 256
ROPE_BASE = 10000.0
DSA_HEADS = N_HEADS
DSA_KV_HEADS = 4
IDX_HEADS = 8
IDX_DIM = 64
TOPK_MAX = 256
SWA_HEADS = N_HEADS
SWA_KV_HEADS = 4
WINDOW = 128
WIN_CHUNKS = WINDOW // CHUNK
FOX_HEADS = N_HEADS
FORGET_BIAS = 2.0
MLA_IN = MLA_Q_LORA + MLA_KV_LORA + MLA_ROPE + MLA_HEADS * MLA_V
DSA_IN = (2 * DSA_HEADS + 2 * DSA_KV_HEADS) * HEAD_DIM + IDX_HEADS * IDX_DIM + IDX_DIM + IDX_HEADS
SWA_IN = (2 * SWA_HEADS + 2 * SWA_KV_HEADS) * HEAD_DIM
FOX_IN = 4 * FOX_HEADS * HEAD_DIM + FOX_HEADS

kernel_name = 'hybrid_chunk_stream_encoder_step'


def _rms(x, g):
    xf = x.astype(jnp.float32)
    y = xf * lax.rsqrt(jnp.mean(xf * xf, axis=-1, keepdims=True) + EPS)
    return (y * g.astype(jnp.float32)).astype(x.dtype)


def _alibi_slopes(n):
    return jnp.asarray(2.0 ** (-8.0 * np.arange(1, n + 1) / n), dtype=jnp.float32)


def _rope(x, pos):
    half = x.shape[-1] // 2
    inv = ROPE_BASE ** (-jnp.arange(half, dtype=jnp.float32) / half)
    ang = pos.astype(jnp.float32)[:, None] * inv[None, :]
    cos = jnp.cos(ang)[:, None, :]
    sin = jnp.sin(ang)[:, None, :]
    xf = x.astype(jnp.float32)
    x1, x2 = xf[..., :half], xf[..., half:]
    return jnp.concatenate([x1 * cos - x2 * sin, x2 * cos + x1 * sin], axis=-1).astype(x.dtype)


def _chunk_causal(q_pos, k_pos):
    return (k_pos[None, :] // CHUNK) <= (q_pos[:, None] // CHUNK)


def _cat(past, new):
    return jnp.concatenate([past.astype(new.dtype), new], axis=1)


def _sweep(attend, q_args, q_pos):
    n = q_pos.shape[0]
    if n <= Q_BLOCK:
        return attend(q_args, q_pos)
    def body(i):
        st = i * Q_BLOCK
        blk = tuple(lax.dynamic_slice_in_dim(a, st, Q_BLOCK, axis=1) for a in q_args)
        return attend(blk, lax.dynamic_slice_in_dim(q_pos, st, Q_BLOCK))
    out = lax.map(body, jnp.arange(n // Q_BLOCK))
    out = jnp.moveaxis(out, 0, 1)
    return out.reshape((out.shape[0], n) + out.shape[3:])


def _prenorm(x, c, g, w_ada, b_ada):
    mod = jax.nn.silu(c) @ w_ada + b_ada
    shift, scale, gate = jnp.split(mod, 3, axis=-1)
    h = _rms(x, g) * (1 + scale[:, None, :]) + shift[:, None, :]
    return h, gate[:, None, :]


def _mla_mixer(h, past, start, w_in, g_qa, w_qb, g_kva, w_kvb, g_qn, g_qr, g_kn, g_kr, w_out):
    B, S, _ = h.shape
    pos = start + jnp.arange(S)
    c1 = MLA_Q_LORA
    c2 = c1 + MLA_KV_LORA
    c3 = c2 + MLA_ROPE
    cq, ckv, kr, z = jnp.split(h @ w_in, [c1, c2, c3], axis=-1)
    q = (_rms(cq, g_qa) @ w_qb).reshape(B, S, MLA_HEADS, MLA_NOPE + MLA_ROPE)
    q_nope = _rms(q[..., :MLA_NOPE], g_qn)
    q_rope = _rope(_rms(q[..., MLA_NOPE:], g_qr), pos)
    latent = _rms(ckv, g_kva)
    k_rope = _rope(_rms(kr, g_kr)[:, :, None, :], pos)[:, :, 0, :]
    new = (latent, k_rope)
    if past is None:
        lat_all, kr_all = latent, k_rope
    else:
        lat_all, kr_all = _cat(past[0], latent), _cat(past[1], k_rope)
    L = lat_all.shape[1]
    kv = (lat_all @ w_kvb).reshape(B, L, MLA_HEADS, MLA_NOPE + MLA_V)
    k_nope = _rms(kv[..., :MLA_NOPE], g_kn)
    v = kv[..., MLA_NOPE:]
    k_pos = jnp.arange(L)
    scale = (MLA_NOPE + MLA_ROPE) ** -0.5

    def attend(qa, qp):
        qn, qr = qa
        s = jnp.einsum('bqhd,bkhd->bhqk', qn, k_nope) + jnp.einsum('bqhr,bkr->bhqk', qr, kr_all)
        s = jnp.where(_chunk_causal(qp, k_pos), s.astype(jnp.float32) * scale, -jnp.inf)
        p = jax.nn.softmax(s, axis=-1).astype(v.dtype)
        return jnp.einsum('bhqk,bkhd->bqhd', p, v)

    o = _sweep(attend, (q_nope, q_rope), pos)
    out = (o.reshape(B, S, -1) * jax.nn.silu(z)) @ w_out
    return out, new


def _dsa_mixer(h, past, start, w_in, g_q, g_k, w_out):
    B, S, _ = h.shape
    G = DSA_HEADS // DSA_KV_HEADS
    pos = start + jnp.arange(S)
    cuts = np.cumsum([DSA_HEADS * HEAD_DIM, DSA_KV_HEADS * HEAD_DIM, DSA_KV_HEADS * HEAD_DIM,
                      IDX_HEADS * IDX_DIM, IDX_DIM, IDX_HEADS]).tolist()
    q, k, v, qi, ki, wi, z = jnp.split(h @ w_in, cuts, axis=-1)
    q = _rms(q.reshape(B, S, DSA_KV_HEADS, G, HEAD_DIM), g_q)
    k = _rms(k.reshape(B, S, DSA_KV_HEADS, HEAD_DIM), g_k)
    v = v.reshape(B, S, DSA_KV_HEADS, HEAD_DIM)
    qi = qi.reshape(B, S, IDX_HEADS, IDX_DIM)
    new = (k, v, ki)
    if past is None:
        k_all, v_all, ki_all = k, v, ki
    else:
        k_all, v_all, ki_all = _cat(past[0], k), _cat(past[1], v), _cat(past[2], ki)
    L = k_all.shape[1]
    topk = min(TOPK_MAX, L // 4)
    k_pos = jnp.arange(L)
    slopes = _alibi_slopes(DSA_HEADS).reshape(DSA_KV_HEADS, G)[:, :, None, None]
    gather = jax.vmap(lambda rows, idx: rows[idx])

    def attend(qa, qp):
        qb, qib, wib = qa
        nb, nq = qb.shape[0], qb.shape[1]
        rel = jax.nn.relu(jnp.einsum('bqhd,bkd->bqhk', qib, ki_all).astype(jnp.float32))
        score = jnp.einsum('bqh,bqhk->bqk', wib.astype(jnp.float32), rel)
        score = jnp.where(_chunk_causal(qp, k_pos)[None], score, -jnp.inf)
        _, idx = lax.top_k(score, topk)
        kg = gather(k_all, idx)
        vg = gather(v_all, idx)
        kp = k_pos[idx]
        valid = (kp // CHUNK) <= (qp // CHUNK)[None, :, None]
        dist = jnp.abs(qp[None, :, None] - kp).astype(jnp.float32)
        s = jnp.einsum('bqngd,bqknd->bngqk', qb, kg).astype(jnp.float32) * HEAD_DIM ** -0.5
        s = s - slopes * dist[:, None, None]
        s = jnp.where(valid[:, None, None], s, -jnp.inf)
        p = jax.nn.softmax(s, axis=-1).astype(vg.dtype)
        o = jnp.einsum('bngqk,bqknd->bqngd', p, vg)
        return o.reshape(nb, nq, DSA_HEADS, HEAD_DIM)

    o = _sweep(attend, (q, qi, wi), pos)
    out = (o.reshape(B, S, -1) * jax.nn.silu(z)) @ w_out
    return out, new


def _sink_attend(q, k, v, q_pos, k_pos, slopes, sink):
    s = jnp.einsum('bcqngd,bcknd->bcngqk', q, k).astype(jnp.float32) * HEAD_DIM ** -0.5
    qc = (q_pos // CHUNK)[:, :, None]
    kc = (k_pos // CHUNK)[:, None, :]
    valid = (k_pos >= 0)[:, None, :] & (kc <= qc) & (qc - kc <= WIN_CHUNKS)
    dist = jnp.abs(q_pos[:, :, None] - k_pos[:, None, :]).astype(jnp.float32)
    s = s - slopes * dist[:, None, None]
    s = jnp.where(valid[:, None, None], s, -jnp.inf)
    m = jnp.maximum(jnp.max(s, axis=-1, keepdims=True), sink)
    p = jnp.exp(s - m)
    w = p / (jnp.sum(p, axis=-1, keepdims=True) + jnp.exp(sink - m))
    return jnp.einsum('bcngqk,bcknd->bcqngd', w.astype(v.dtype), v)


def _swa_mixer(h, past, start, w_in, g_q, g_k, sinks, w_out):
    B, S, _ = h.shape
    G = SWA_HEADS // SWA_KV_HEADS
    pos = start + jnp.arange(S)
    c1 = SWA_HEADS * HEAD_DIM
    c2 = c1 + SWA_KV_HEADS * HEAD_DIM
    c3 = c2 + SWA_KV_HEADS * HEAD_DIM
    q, k, v, z = jnp.split(h @ w_in, [c1, c2, c3], axis=-1)
    q = _rms(q.reshape(B, S, SWA_KV_HEADS, G, HEAD_DIM), g_q)
    k = _rms(k.reshape(B, S, SWA_KV_HEADS, HEAD_DIM), g_k)
    v = v.reshape(B, S, SWA_KV_HEADS, HEAD_DIM)
    slopes = _alibi_slopes(SWA_HEADS).reshape(SWA_KV_HEADS, G)[:, :, None, None]
    sink = sinks.astype(jnp.float32).reshape(SWA_KV_HEADS, G)[:, :, None, None]
    if past is None:
        nC = S // CHUNK
        pad = WIN_CHUNKS * CHUNK

        def windows(a):
            ap = jnp.pad(a, ((0, 0), (pad, 0)) + ((0, 0),) * (a.ndim - 2))
            ap = ap.reshape((B, nC + WIN_CHUNKS, CHUNK) + a.shape[2:])
            return jnp.concatenate([ap[:, j:j + nC] for j in range(WIN_CHUNKS + 1)], axis=2)

        kp = jnp.arange(-pad, S).reshape(nC + WIN_CHUNKS, CHUNK)
        kpw = jnp.concatenate([kp[j:j + nC] for j in range(WIN_CHUNKS + 1)], axis=1)
        o = _sink_attend(q.reshape(B, nC, CHUNK, SWA_KV_HEADS, G, HEAD_DIM), windows(k), windows(v),
                         pos.reshape(nC, CHUNK), kpw, slopes, sink)
        new = (k[:, S - WINDOW:], v[:, S - WINDOW:])
    else:
        k_all, v_all = _cat(past[0], k), _cat(past[1], v)
        W = past[0].shape[1]
        kpos = jnp.arange(start - W, start + S)
        o = _sink_attend(q[:, None], k_all[:, None], v_all[:, None], pos[None], kpos[None], slopes, sink)[:, 0]
        new = (k_all[:, S:], v_all[:, S:])
    out = (o.reshape(B, S, -1) * jax.nn.silu(z)) @ w_out
    return out, new


def _fox_mixer(h, past, start, w_in, b_f, g_q, g_k, w_out):
    B, S, _ = h.shape
    pos = start + jnp.arange(S)
    hw = FOX_HEADS * HEAD_DIM
    q, k, v, f, z = jnp.split(h @ w_in, [hw, 2 * hw, 3 * hw, 3 * hw + FOX_HEADS], axis=-1)
    q = _rms(q.reshape(B, S, FOX_HEADS, HEAD_DIM), g_q)
    k = _rms(k.reshape(B, S, FOX_HEADS, HEAD_DIM), g_k)
    v = v.reshape(B, S, FOX_HEADS, HEAD_DIM)
    logf = jax.nn.log_sigmoid((f + b_f).astype(jnp.float32))
    new = (k, v, logf)
    if past is None:
        k_all, v_all, lf_all = k, v, logf
    else:
        k_all, v_all, lf_all = _cat(past[0], k), _cat(past[1], v), _cat(past[2], logf)
    F = jnp.cumsum(lf_all, axis=1)
    L = F.shape[1]
    Fq = F[:, L - S:]
    Fk = jnp.swapaxes(F, 1, 2)[:, :, None, :]
    k_pos = jnp.arange(L)

    def attend(qa, qp):
        qb, fq = qa
        s = jnp.einsum('bqhd,bkhd->bhqk', qb, k_all).astype(jnp.float32) * HEAD_DIM ** -0.5
        s = s + jnp.swapaxes(fq, 1, 2)[..., None] - Fk
        s = jnp.where(k_pos[None, :] <= qp[:, None], s, -jnp.inf)
        p = jax.nn.softmax(s, axis=-1).astype(v_all.dtype)
        return jnp.einsum('bhqk,bkhd->bqhd', p, v_all)

    o = _sweep(attend, (q, Fq), pos)
    out = (o.reshape(B, S, -1) * jax.nn.silu(z)) @ w_out
    return out, new


def setup_inputs(seed: int = 0) -> dict:
    key = jax.random.key(seed)
    keys = jax.random.split(key, 64)
    count = [0]

    def nrm(shape, scale=1.0):
        k = keys[count[0]]
        count[0] += 1
        return jax.random.normal(k, shape, jnp.float32) * scale

    def gain(n):
        return 1.0 + 0.05 * nrm((n,))

    D = D_MODEL
    return {
        'x_prompt': nrm((BATCH, SEQ, D)),
        'x_sample': nrm((DEC_BATCH, DEC_SEQ, D)),
        'cache_mla_latent': nrm((DEC_BATCH, PAST_LEN, MLA_KV_LORA)),
        'cache_mla_krope': nrm((DEC_BATCH, PAST_LEN, MLA_ROPE)),
        'cache_dsa_k': nrm((DEC_BATCH, PAST_LEN, DSA_KV_HEADS, HEAD_DIM)),
        'cache_dsa_v': nrm((DEC_BATCH, PAST_LEN, DSA_KV_HEADS, HEAD_DIM)),
        'cache_dsa_kidx': nrm((DEC_BATCH, PAST_LEN, IDX_DIM)),
        'state_swa_k': nrm((DEC_BATCH, WINDOW, SWA_KV_HEADS, HEAD_DIM)),
        'state_swa_v': nrm((DEC_BATCH, WINDOW, SWA_KV_HEADS, HEAD_DIM)),
        'cache_fox_k': nrm((DEC_BATCH, PAST_LEN, FOX_HEADS, HEAD_DIM)),
        'cache_fox_v': nrm((DEC_BATCH, PAST_LEN, FOX_HEADS, HEAD_DIM)),
        'cache_fox_logf': jax.nn.log_sigmoid(FORGET_BIAS + 0.5 * nrm((DEC_BATCH, PAST_LEN, FOX_HEADS))),
        'c_prompt': nrm((BATCH, D)),
        'c_sample': nrm((DEC_BATCH, D)),
        'norm_g': 1.0 + 0.05 * nrm((DEPTH, D)),
        'ada_w': nrm((DEPTH, D, 3 * D), 0.5 * D ** -0.5),
        'ada_b': nrm((DEPTH, 3 * D), 0.01),
        'mla_w_in': nrm((D, MLA_IN), D ** -0.5),
        'mla_g_qa': gain(MLA_Q_LORA),
        'mla_w_qb': nrm((MLA_Q_LORA, MLA_HEADS * (MLA_NOPE + MLA_ROPE)), MLA_Q_LORA ** -0.5),
        'mla_g_kva': gain(MLA_KV_LORA),
        'mla_w_kvb': nrm((MLA_KV_LORA, MLA_HEADS * (MLA_NOPE + MLA_V)), MLA_KV_LORA ** -0.5),
        'mla_g_qn': gain(MLA_NOPE),
        'mla_g_qr': gain(MLA_ROPE),
        'mla_g_kn': gain(MLA_NOPE),
        'mla_g_kr': gain(MLA_ROPE),
        'mla_w_out': nrm((MLA_HEADS * MLA_V, D), (MLA_HEADS * MLA_V) ** -0.5),
        'dsa_w_in': nrm((D, DSA_IN), D ** -0.5),
        'dsa_g_q': gain(HEAD_DIM),
        'dsa_g_k': gain(HEAD_DIM),
        'dsa_w_out': nrm((DSA_HEADS * HEAD_DIM, D), (DSA_HEADS * HEAD_DIM) ** -0.5),
        'swa_w_in': nrm((D, SWA_IN), D ** -0.5),
        'swa_g_q': gain(HEAD_DIM),
        'swa_g_k': gain(HEAD_DIM),
        'swa_sinks': nrm((SWA_HEADS,), 0.5),
        'swa_w_out': nrm((SWA_HEADS * HEAD_DIM, D), (SWA_HEADS * HEAD_DIM) ** -0.5),
        'fox_w_in': nrm((D, FOX_IN), D ** -0.5),
        'fox_b_f': FORGET_BIAS + 0.1 * nrm((FOX_HEADS,)),
        'fox_g_q': gain(HEAD_DIM),
        'fox_g_k': gain(HEAD_DIM),
        'fox_w_out': nrm((FOX_HEADS * HEAD_DIM, D), (FOX_HEADS * HEAD_DIM) ** -0.5),
    }


def reference(x_prompt, x_sample, cache_mla_latent, cache_mla_krope, cache_dsa_k, cache_dsa_v,
              cache_dsa_kidx, state_swa_k, state_swa_v, cache_fox_k, cache_fox_v, cache_fox_logf,
              c_prompt, c_sample, norm_g, ada_w, ada_b,
              mla_w_in, mla_g_qa, mla_w_qb, mla_g_kva, mla_w_kvb, mla_g_qn, mla_g_qr, mla_g_kn,
              mla_g_kr, mla_w_out, dsa_w_in, dsa_g_q, dsa_g_k, dsa_w_out,
              swa_w_in, swa_g_q, swa_g_k, swa_sinks, swa_w_out,
              fox_w_in, fox_b_f, fox_g_q, fox_g_k, fox_w_out):
    def mixer(kind, h, past, start):
        if kind == 0:
            return _mla_mixer(h, past, start, mla_w_in, mla_g_qa, mla_w_qb, mla_g_kva, mla_w_kvb,
                              mla_g_qn, mla_g_qr, mla_g_kn, mla_g_kr, mla_w_out)
        if kind == 1:
            return _dsa_mixer(h, past, start, dsa_w_in, dsa_g_q, dsa_g_k, dsa_w_out)
        if kind == 2:
            return _swa_mixer(h, past, start, swa_w_in, swa_g_q, swa_g_k, swa_sinks, swa_w_out)
        return _fox_mixer(h, past, start, fox_w_in, fox_b_f, fox_g_q, fox_g_k, fox_w_out)

    pasts = ((cache_mla_latent, cache_mla_krope),
             (cache_dsa_k, cache_dsa_v, cache_dsa_kidx),
             (state_swa_k, state_swa_v),
             (cache_fox_k, cache_fox_v, cache_fox_logf))
    xp, xs = x_prompt, x_sample
    new_p, new_s = [], []
    for layer in range(DEPTH):
        kind = layer % N_MIXERS
        h, gate = _prenorm(xp, c_prompt, norm_g[layer], ada_w[layer], ada_b[layer])
        out, st = mixer(kind, h, None, 0)
        xp = xp + gate * out
        new_p.append(st)
        h, gate = _prenorm(xs, c_sample, norm_g[layer], ada_w[layer], ada_b[layer])
        out, st = mixer(kind, h, pasts[kind], PAST_LEN)
        xs = xs + gate * out
        new_s.append(st)
    (p_mla_lat, p_mla_kr), (p_dsa_k, p_dsa_v, p_dsa_ki), (p_swa_k, p_swa_v), (p_fox_k, p_fox_v, p_fox_lf) = new_p
    (s_mla_lat, s_mla_kr), (s_dsa_k, s_dsa_v, s_dsa_ki), (s_swa_k, s_swa_v), (s_fox_k, s_fox_v, s_fox_lf) = new_s
    return (xp, xs,
            p_mla_lat, p_mla_kr, p_dsa_k, p_dsa_v, p_dsa_ki, p_swa_k, p_swa_v, p_fox_k, p_fox_v, p_fox_lf,
            s_mla_lat, s_mla_kr, s_dsa_k, s_dsa_v, s_dsa_ki, s_swa_k, s_swa_v, s_fox_k, s_fox_v, s_fox_lf)
```

```python
import functools

import numpy as np
import jax
import jax.numpy as jnp
from jax import lax
from jax.experimental import pallas as pl
from jax.experimental.pallas import tpu as pltpu

F32 = jnp.float32
BF16 = jnp.bfloat16
I32 = jnp.int32

D_MODEL = 1024
HEAD_DIM = 64
N_HEADS = 16
KV_HEADS = 4
CHUNK = 64
CHUNK_SHIFT = 6
WINDOW = 128
WIN_CHUNKS = WINDOW // CHUNK
EPS = 1e-6
ROPE_BASE = 10000.0
MLA_NOPE, MLA_ROPE, MLA_Q_LORA, MLA_KV_LORA = 64, 32, 384, 256
IDX_HEADS, IDX_DIM, TOPK_MAX = 8, 64, 256
LOG2E = 1.4426950408889634
NEG = -1e30
INT_MIN = -(2 ** 31)
INT_MAX = 2 ** 31 - 1

TS = 256
TQ = 256
TK = 256
TQ_DEC = 128
KT_LANES = 256
VMEM_LIMIT = 56 * 1024 * 1024


def _row_tile(r):
    return TS if r % TS == 0 else r


def _cparams(sem, vmem=VMEM_LIMIT):
    return pltpu.CompilerParams(dimension_semantics=sem, vmem_limit_bytes=vmem)


def _dotf(a, b):
    return jnp.dot(a, b, preferred_element_type=F32)


def _dot_nt(a, b):
    return lax.dot_general(a, b, (((1,), (1,)), ((), ())), preferred_element_type=F32)


def _dot_tn(a, b):
    return lax.dot_general(a, b, (((0,), (0,)), ((), ())), preferred_element_type=F32)


def _split3(x):
    hi = x.astype(BF16)
    r = x - hi.astype(F32)
    mid = r.astype(BF16)
    lo = (r - mid.astype(F32)).astype(BF16)
    return hi, mid, lo


def _silu(x):
    return x / (1.0 + jnp.exp(-x))


def _full_spec(arr):
    nd = arr.ndim
    return pl.BlockSpec(arr.shape, lambda *_: (0,) * nd)


def _row_spec(arr, ts):
    if arr.shape[0] == 1:
        return pl.BlockSpec((1, arr.shape[1]), lambda i: (0, 0))
    return pl.BlockSpec((ts, arr.shape[1]), lambda i: (i, 0))


def _col_spec(arr, ts):
    return pl.BlockSpec((arr.shape[0], ts), lambda i: (0, i))


def _ada_body(c_ref, w_ref, b_ref, o_ref):
    a = _silu(c_ref[...])
    w = w_ref[0]
    a_hi = a.astype(BF16)
    a_lo = (a - a_hi.astype(F32)).astype(BF16)
    w_hi = w.astype(BF16)
    w_lo = (w - w_hi.astype(F32)).astype(BF16)
    o_ref[0] = _dotf(a_hi, w_hi) + _dotf(a_hi, w_lo) + _dotf(a_lo, w_hi) + b_ref[0]


def _ada_mod(c_all, ada_w, ada_b):
    depth, d, n3 = ada_w.shape
    bp = c_all.shape[0]
    tn = 768
    return pl.pallas_call(
        _ada_body,
        grid=(depth, n3 // tn),
        in_specs=[
            pl.BlockSpec((bp, d), lambda l, j: (0, 0)),
            pl.BlockSpec((1, d, tn), lambda l, j: (l, 0, j)),
            pl.BlockSpec((1, 1, tn), lambda l, j: (l, 0, j)),
        ],
        out_specs=pl.BlockSpec((1, bp, tn), lambda l, j: (l, 0, j)),
        out_shape=jax.ShapeDtypeStruct((depth, bp, n3), F32),
        compiler_params=_cparams(("arbitrary", "arbitrary")),
        name="ada_mod",
    )(c_all, ada_w, ada_b.reshape(depth, 1, n3))


def _prenorm(x_ref, g_ref, sc_ref, sh_ref):
    x = x_ref[...]
    ms = jnp.mean(x * x, axis=-1, keepdims=True)
    xn = x * lax.rsqrt(ms + EPS) * g_ref[...]
    return (xn * (1.0 + sc_ref[...]) + sh_ref[...]).astype(BF16)


def _group_sumsq(y, bd_ref):
    sq = y * y
    hi = sq.astype(BF16)
    lo = (sq - hi.astype(F32)).astype(BF16)
    bd = bd_ref[...]
    return _dotf(hi, bd) + _dotf(lo, bd)


def _head_rms_rows(q, g_col):
    ms = jnp.mean(q * q, axis=0, keepdims=True)
    return q * lax.rsqrt(ms + EPS) * g_col


def _place_rows(piece, row0, total):
    ts = piece.shape[1]
    parts = []
    if row0 > 0:
        parts.append(jnp.zeros((row0, ts), F32))
    parts.append(piece)
    rest = total - row0 - piece.shape[0]
    if rest > 0:
        parts.append(jnp.zeros((rest, ts), F32))
    return jnp.concatenate(parts, axis=0) if len(parts) > 1 else piece


def _proj_call(body, name, r, ts, row_in, col_in, const_in, outs):
    in_specs = ([_row_spec(a, ts) for a in row_in] + [_col_spec(a, ts) for a in col_in]
                + [_full_spec(a) for a in const_in])
    return pl.pallas_call(
        body,
        grid=(r // ts,),
        in_specs=in_specs,
        out_specs=[pl.BlockSpec(blk, im) for (_, _, blk, im) in outs],
        out_shape=[jax.ShapeDtypeStruct(s, dt) for (s, dt, _, _) in outs],
        compiler_params=_cparams(("arbitrary",)),
        name=name,
    )(*row_in, *col_in, *const_in)


def _rows_out(r, c, ts, dtype=F32):
    return ((r, c), dtype, (ts, c), lambda i: (i, 0))


def _cols_out(c, r, ts, dtype=F32):
    return ((c, r), dtype, (c, ts), lambda i: (0, i))


def _qp_out(r, ts):
    return ((N_HEADS, KT_LANES, r), BF16, (N_HEADS, KT_LANES, ts), lambda i: (0, 0, i))


def _mla_in_body(x_ref, sc_ref, sh_ref, cos_ref, sin_ref, g_ref, gqa, gkva, gkr, gkrp,
                 wcq, wckv, wkr, wkrp, wzt, cqn_o, lat_o, kr_o, zt_o):
    h = _prenorm(x_ref, g_ref, sc_ref, sh_ref)
    cq = _dotf(h, wcq[...])
    cqn_o[...] = (cq * lax.rsqrt(jnp.mean(cq * cq, axis=-1, keepdims=True) + EPS) * gqa[...]).astype(BF16)
    ckv = _dotf(h, wckv[...])
    lat_o[...] = ckv * lax.rsqrt(jnp.mean(ckv * ckv, axis=-1, keepdims=True) + EPS) * gkva[...]
    kr = _dotf(h, wkr[...])
    krp = _dotf(h, wkrp[...])
    inv = lax.rsqrt(jnp.sum(kr * kr, axis=-1, keepdims=True) * (1.0 / MLA_ROPE) + EPS)
    kr_o[...] = (kr * gkr[...] * cos_ref[...] + krp * gkrp[...] * sin_ref[...]) * inv
    zt_o[...] = _dot_nt(wzt[...], h)


def _mla_q_body(cqn_ref, cos_ref, sin_ref, wqt, gqn, gqr, qp_o):
    qt = _dot_nt(wqt[...], cqn_ref[...])
    ts = qt.shape[1]
    sc = (MLA_NOPE + MLA_ROPE) ** -0.5 * LOG2E
    c = cos_ref[...]
    s = sin_ref[...]
    half = MLA_ROPE // 2
    width = MLA_NOPE + MLA_ROPE
    for h in range(N_HEADS):
        qn = _head_rms_rows(qt[width * h:width * h + MLA_NOPE], gqn[...]) * sc
        qr = _head_rms_rows(qt[width * h + MLA_NOPE:width * (h + 1)], gqr[...]) * sc
        x1, x2 = qr[:half], qr[half:]
        o1 = x1 * c - x2 * s
        o2 = x2 * c + x1 * s
        a = h % 2
        pieces = [qn, jnp.zeros((HEAD_DIM, ts), F32)]
        if a:
            pieces = pieces[::-1]
        pieces += [o1, o2, jnp.zeros((KT_LANES - 2 * HEAD_DIM - MLA_ROPE, ts), F32)]
        qp_o[h] = jnp.concatenate(pieces, axis=0).astype(BF16)


def _mla_kv_body(lat_ref, kr_ref, gkn, wkn, wvt, bd, kt_o, vt_o):
    lat = lat_ref[...].astype(BF16)
    kn = _dotf(lat, wkn[...])
    ss = _group_sumsq(kn, bd)
    kn = kn * lax.rsqrt(ss * (1.0 / MLA_NOPE) + EPS) * gkn[...]
    kr = kr_ref[...].astype(BF16)
    for p in range(N_HEADS // 2):
        kt_o[p, :, 0:128] = kn[:, 128 * p:128 * (p + 1)].astype(BF16)
        kt_o[p, :, 128:256] = kr
    vt_o[0] = _dot_nt(wvt[...], lat).astype(BF16)


def _q_heads_to_qp(qt, gq, qp_o, row_of_head, extra_of_head=None):
    sc = HEAD_DIM ** -0.5 * LOG2E
    ts = qt.shape[1]
    for h in range(N_HEADS):
        qh = _head_rms_rows(qt[HEAD_DIM * h:HEAD_DIM * (h + 1)], gq[...]) * sc
        blk = _place_rows(qh, row_of_head(h), KT_LANES)
        if extra_of_head is not None:
            lo, hi = extra_of_head(h)
            rows = lax.broadcasted_iota(I32, (KT_LANES, ts), 0)
            blk = blk + jnp.where((rows >= lo) & (rows < hi), 1.0, 0.0)
        qp_o[h] = blk.astype(BF16)


def _dsa_in_body(x_ref, sc_ref, sh_ref, g_ref, gk, gq, wqt, wk, wv, wqit, wki, wwit, wzt, bd,
                 qp_o, k_o, v_o, qi_o, ki_o, wi_o, zt_o):
    h = _prenorm(x_ref, g_ref, sc_ref, sh_ref)
    _q_heads_to_qp(_dot_nt(wqt[...], h), gq, qp_o, lambda hh: HEAD_DIM * (hh // (N_HEADS // KV_HEADS)))
    k = _dotf(h, wk[...])
    k_o[...] = k * lax.rsqrt(_group_sumsq(k, bd) * (1.0 / HEAD_DIM) + EPS) * gk[...]
    v_o[...] = _dotf(h, wv[...])
    qit = _dot_nt(wqit[...], h)
    for ih in range(IDX_HEADS):
        qi_o[ih] = qit[IDX_DIM * ih:IDX_DIM * (ih + 1)].astype(BF16)
    ki_o[...] = _dotf(h, wki[...])
    wi_o[...] = _dot_nt(wwit[...], h)
    zt_o[...] = _dot_nt(wzt[...], h)


def _swa_in_body(x_ref, sc_ref, sh_ref, g_ref, gk, gq, wqt, wk, wv, wzt, bd, qp_o, k_o, v_o, zt_o):
    h = _prenorm(x_ref, g_ref, sc_ref, sh_ref)
    _q_heads_to_qp(_dot_nt(wqt[...], h), gq, qp_o, lambda hh: HEAD_DIM * (hh // (N_HEADS // KV_HEADS)))
    k = _dotf(h, wk[...])
    k_o[...] = k * lax.rsqrt(_group_sumsq(k, bd) * (1.0 / HEAD_DIM) + EPS) * gk[...]
    v_o[...] = _dotf(h, wv[...])
    zt_o[...] = _dot_nt(wzt[...], h)


def _fox_in_body(x_ref, sc_ref, sh_ref, g_ref, gk, bf, gq, wqt, wk, wv, wf, wzt, bd,
                 qp_o, k_o, v_o, lf_o, zt_o):
    h = _prenorm(x_ref, g_ref, sc_ref, sh_ref)
    _q_heads_to_qp(_dot_nt(wqt[...], h), gq, qp_o, lambda hh: HEAD_DIM * (hh % 2),
                   lambda hh: (128 + 3 * (hh % 2), 128 + 3 * (hh % 2) + 3))
    k = _dotf(h, wk[...])
    k_o[...] = k * lax.rsqrt(_group_sumsq(k, bd) * (1.0 / HEAD_DIM) + EPS) * gk[...]
    v_o[...] = _dotf(h, wv[...])
    f = _dotf(h, wf[...]) + bf[...]
    lf_o[...] = jnp.minimum(f, 0.0) - jnp.log1p(jnp.exp(-jnp.abs(f)))
    zt_o[...] = _dot_nt(wzt[...], h)


def _out_body(x_ref, gate_ref, ot_ref, zt_ref, wo, o_ref):
    u = (ot_ref[...] * _silu(zt_ref[...])).astype(BF16)
    o_ref[...] = x_ref[...] + gate_ref[...] * _dot_tn(u, wo[...])


def _out_proj(x, gate, ot, zt, wo):
    r = x.shape[0]
    ts = _row_tile(r)
    (out,) = _proj_call(_out_body, "out_proj", r, ts, [x, gate], [ot, zt], [wo],
                        [_rows_out(r, D_MODEL, ts)])
    return out


def _gqa_prep_body(*refs, with_ki):
    if with_ki:
        k_ref, v_ref, ki_ref, kt_o, vt_o, ki_o = refs
        ki_o[0] = ki_ref[0].astype(BF16)
    else:
        k_ref, v_ref, kt_o, vt_o = refs
    kt_o[0] = k_ref[0].astype(BF16)
    vt_o[0, 0] = v_ref[0].T.astype(BF16)


def _gqa_prep(k, v, ki=None):
    b, lp, c = k.shape
    nkb = lp // TK
    ins = [k, v] + ([ki] if ki is not None else [])
    in_specs = [pl.BlockSpec((1, TK, a.shape[2]), lambda bb, j: (bb, j, 0)) for a in ins]
    out_specs = [pl.BlockSpec((1, TK, c), lambda bb, j: (bb, j, 0)),
                 pl.BlockSpec((1, 1, c, TK), lambda bb, j: (bb, j, 0, 0))]
    out_shape = [jax.ShapeDtypeStruct((b, lp, c), BF16), jax.ShapeDtypeStruct((b, nkb, c, TK), BF16)]
    if ki is not None:
        out_specs.append(pl.BlockSpec((1, TK, ki.shape[2]), lambda bb, j: (bb, j, 0)))
        out_shape.append(jax.ShapeDtypeStruct(ki.shape, BF16))
    return pl.pallas_call(
        functools.partial(_gqa_prep_body, with_ki=ki is not None),
        grid=(b, nkb), in_specs=in_specs, out_specs=out_specs, out_shape=out_shape,
        compiler_params=_cparams(("arbitrary", "arbitrary")), name="gqa_prep",
    )(*ins)


def _fox_prep_body(k_ref, v_ref, lf_ref, sel_ref, kt_o, vt_o, carry_ref):
    kb = pl.program_id(1)

    @pl.when(kb == 0)
    def _():
        carry_ref[...] = jnp.zeros_like(carry_ref)

    r = lax.broadcasted_iota(I32, (TK, TK), 0)
    c = lax.broadcasted_iota(I32, (TK, TK), 1)
    tri = jnp.where(c <= r, 1.0, 0.0).astype(BF16)
    hi, mid, lo = _split3(lf_ref[0])
    cum = _dotf(tri, hi) + _dotf(tri, mid) + _dotf(tri, lo) + carry_ref[...]
    carry_ref[...] = cum[TK - 1:TK, :]
    a, b, d = _split3(-(cum * LOG2E))
    ex = _dotf(a, sel_ref[0]) + _dotf(b, sel_ref[1]) + _dotf(d, sel_ref[2])
    k = k_ref[0]
    for p in range(N_HEADS // 2):
        kt_o[p, :, 0:128] = k[:, 128 * p:128 * (p + 1)].astype(BF16)
        kt_o[p, :, 128:256] = ex[:, 128 * p:128 * (p + 1)].astype(BF16)
    vt_o[0, 0] = v_ref[0].T.astype(BF16)


def _fox_sel():
    sel = np.zeros((3, 128, N_HEADS * HEAD_DIM), np.float32)
    for h in range(N_HEADS):
        for j in range(3):
            sel[j, h, 128 * (h // 2) + 3 * (h % 2) + j] = 1.0
    return jnp.asarray(sel, BF16)


def _fox_prep(k, v, lf128):
    b, lp, c = k.shape
    nkb = lp // TK
    npair = N_HEADS // 2
    sel = _fox_sel()
    return pl.pallas_call(
        _fox_prep_body,
        grid=(b, nkb),
        in_specs=[pl.BlockSpec((1, TK, c), lambda bb, j: (bb, j, 0)),
                  pl.BlockSpec((1, TK, c), lambda bb, j: (bb, j, 0)),
                  pl.BlockSpec((1, TK, 128), lambda bb, j: (bb, j, 0)),
                  _full_spec(sel)],
        out_specs=[pl.BlockSpec((npair, TK, KT_LANES), lambda bb, j: (bb, j, 0)),
                   pl.BlockSpec((1, 1, c, TK), lambda bb, j: (bb, j, 0, 0))],
        out_shape=[jax.ShapeDtypeStruct((b * npair, lp, KT_LANES), BF16),
                   jax.ShapeDtypeStruct((b, nkb, c, TK), BF16)],
        scratch_shapes=[pltpu.VMEM((1, 128), F32)],
        compiler_params=_cparams(("arbitrary", "arbitrary")), name="fox_prep",
    )(k, v, lf128, sel)


def _softmax_step(s, m, l, acc, vt_blk, c=None):
    mb = jnp.max(s, axis=0, keepdims=True)
    if c is not None:
        mb = mb + c
    m_new = jnp.maximum(m, mb)
    alpha = jnp.exp2(m - m_new)
    p = jnp.exp2(s - (m_new if c is None else m_new - c))
    l = alpha * l + jnp.sum(p, axis=0, keepdims=True)
    acc = alpha * acc + _dotf(vt_blk, p.astype(BF16))
    return m_new, l, acc


def _visible_end(qpos, mode, n_keys, key0):
    if mode == "causal":
        end = qpos + 1 - key0
    else:
        end = (((qpos >> CHUNK_SHIFT) + 1) << CHUNK_SHIFT) - key0
    return jnp.minimum(end, n_keys)


def _flash_body(hp_ref, qp_ref, kt_ref, vt_ref, o_ref, *, tq, mode, q0, n_real_q, n_keys, key0,
                alibi, sinks):
    i = pl.program_id(1)
    q = qp_ref[0]
    q_first = q0 + i * tq
    q_last = q_first + (n_real_q - 1)
    n_tot = (_visible_end(q_last, mode, n_keys, key0) + (TK - 1)) // TK
    if mode == "window":
        first_key = jnp.maximum((((q_first >> CHUNK_SHIFT) - WIN_CHUNKS) << CHUNK_SHIFT) - key0, 0)
        n_lo = first_key // TK
        n_full = n_lo
    else:
        n_lo = 0
        n_full = _visible_end(q_first, mode, n_keys, key0) // TK
    ik = lax.broadcasted_iota(I32, (TK, tq), 0)
    iq = lax.broadcasted_iota(I32, (TK, tq), 1)
    qpos = q_first + iq
    slope2 = hp_ref[0, :, 0:1] if alibi else None
    a_tab = slope2 * ik.astype(F32) if alibi else None

    def step(kb, carry, masked):
        m, l, acc = carry
        k0 = pl.multiple_of(kb * TK, TK)
        s = _dotf(kt_ref[0, pl.ds(k0, TK), :], q)
        c = None
        if masked:
            kidx = k0 + ik
            kpos = kidx + key0
            if mode == "causal":
                valid = kpos <= qpos
            else:
                qc = qpos >> CHUNK_SHIFT
                kc = kpos >> CHUNK_SHIFT
                valid = kc <= qc
                if mode == "window":
                    valid = valid & (qc - kc <= WIN_CHUNKS) & (kpos >= 0)
            valid = valid & (kidx < n_keys)
            if alibi:
                s = s - slope2 * jnp.abs(qpos - kpos).astype(F32) + slope2 * iq.astype(F32)
            s = jnp.where(valid, s, NEG)
        elif alibi:
            s = s + a_tab
            c = slope2 * (k0 + key0 - q_first).astype(F32)
        return _softmax_step(s, m, l, acc, vt_ref[0, kb], c)

    init = (jnp.full((1, tq), NEG, F32), jnp.zeros((1, tq), F32), jnp.zeros((HEAD_DIM, tq), F32))
    carry = lax.fori_loop(n_lo, n_full, lambda kb, cr: step(kb, cr, False), init)
    m, l, acc = lax.fori_loop(n_full, n_tot, lambda kb, cr: step(kb, cr, True), carry)
    if sinks:
        sink2 = hp_ref[0, :, 1:2]
        if alibi:
            sink2 = sink2 + slope2 * lax.broadcasted_iota(I32, (1, tq), 1).astype(F32)
        m_fin = jnp.maximum(m, sink2)
        a = jnp.exp2(m - m_fin)
        l = l * a + jnp.exp2(sink2 - m_fin)
        acc = acc * a
    o_ref[0] = acc / l


def _flash(hp, qp, kt, vt, *, tq, mode, q0, n_real_q, n_keys, key0=0, alibi=False, sinks=False,
           tile_fn, vrow_fn, heads_per_batch=N_HEADS):
    bh, _, sq = qp.shape
    lp = kt.shape[1]
    nkb = vt.shape[1]
    body = functools.partial(_flash_body, tq=tq, mode=mode, q0=q0, n_real_q=n_real_q, n_keys=n_keys,
                             key0=key0, alibi=alibi, sinks=sinks)
    return pl.pallas_call(
        body,
        grid=(bh, sq // tq),
        in_specs=[pl.BlockSpec((1, 1, 128), lambda g, i: (g % heads_per_batch, 0, 0)),
                  pl.BlockSpec((1, KT_LANES, tq), lambda g, i: (g, 0, i)),
                  pl.BlockSpec((1, lp, KT_LANES), lambda g, i: (tile_fn(g), 0, 0)),
                  pl.BlockSpec((1, nkb, HEAD_DIM, TK), lambda g, i: (g // heads_per_batch, 0, vrow_fn(g), 0))],
        out_specs=pl.BlockSpec((1, HEAD_DIM, tq), lambda g, i: (g, 0, i)),
        out_shape=jax.ShapeDtypeStruct((bh, HEAD_DIM, sq), F32),
        compiler_params=_cparams(("arbitrary", "arbitrary")),
        name="flash_" + mode,
    )(hp, qp, kt, vt)


def _dsa_body(hp_ref, qi_ref, wi_ref, qp_ref, ki_ref, kt_ref, vt_ref, o_ref, keys_ref, *,
              tq, q0, n_real_q, n_keys, topk):
    i = pl.program_id(1)
    q_first = q0 + i * tq
    q_last = q_first + (n_real_q - 1)
    n_tot = (_visible_end(q_last, "chunk", n_keys, 0) + (TK - 1)) // TK
    n_past = jnp.minimum(q_first, n_keys) // TK
    ik = lax.broadcasted_iota(I32, (TK, tq), 0)
    iq = lax.broadcasted_iota(I32, (TK, tq), 1)
    qpos = q_first + iq
    tf = float(topk)

    def blk(kb):
        return pl.ds(pl.multiple_of(kb * TK, TK), TK)

    def score_blk(kb, _):
        ki = ki_ref[0, blk(kb), :]
        acc = jnp.zeros((TK, tq), F32)
        for h in range(IDX_HEADS):
            acc = acc + wi_ref[0, h:h + 1, :] * jnp.maximum(_dotf(ki, qi_ref[0, h]), 0.0)
        kpos = kb * TK + ik
        valid = ((kpos >> CHUNK_SHIFT) <= (qpos >> CHUNK_SHIFT)) & (kpos < n_keys)
        bits = pltpu.bitcast(acc, I32)
        key = jnp.where(bits < 0, bits ^ INT_MAX, bits)
        keys_ref[blk(kb), :] = jnp.where(valid, key, INT_MIN)
        return 0

    lax.fori_loop(0, n_tot, score_blk, 0)

    def count_ge(mid):
        def body(kb, c):
            return c + jnp.sum(jnp.where(keys_ref[blk(kb), :] >= mid, 1.0, 0.0), axis=0, keepdims=True)
        return lax.fori_loop(0, n_tot, body, jnp.zeros((1, tq), F32))

    def avg(lo, hi):
        return (lo >> 1) + (hi >> 1) + (lo & hi & 1)

    qrow = q_first + lax.broadcasted_iota(I32, (1, tq), 1)
    n_vis = _visible_end(qrow, "chunk", n_keys, 0).astype(F32)
    lo0 = jnp.full((1, tq), INT_MIN + 1, I32)
    hi0 = jnp.full((1, tq), INT_MAX, I32)
    pending0 = jnp.sum(jnp.where(n_vis > tf, 1.0, 0.0))

    def cond(st):
        return (st[0] < 40) & (st[5] > 0.5)

    def body(st):
        it, lo, hi, cl, ch, _ = st
        mid = avg(lo, hi)
        cnt = count_ge(mid)
        ge = cnt >= tf
        lo = jnp.where(ge, mid, lo)
        cl = jnp.where(ge, cnt, cl)
        hi = jnp.where(ge, hi, mid)
        ch = jnp.where(ge, ch, cnt)
        done = (cl <= tf) | (avg(lo, hi) == lo)
        return it + 1, lo, hi, cl, ch, jnp.sum(jnp.where(done, 0.0, 1.0))

    _, lo, _, cl, ch, _ = lax.while_loop(
        cond, body, (jnp.int32(0), lo0, hi0, n_vis, jnp.zeros((1, tq), F32), pending0))

    need = tf - ch

    @pl.when(jnp.sum(jnp.where(cl > tf, 1.0, 0.0)) > 0.5)
    def _():
        r = lax.broadcasted_iota(I32, (TK, TK), 0)
        c = lax.broadcasted_iota(I32, (TK, TK), 1)
        tri = jnp.where(c < r, 1.0, 0.0).astype(BF16)

        def fix(kb, carry):
            kblk = keys_ref[blk(kb), :]
            eq = kblk == lo
            e = jnp.where(eq, 1.0, 0.0)
            before = _dotf(tri, e.astype(BF16)) + carry
            keys_ref[blk(kb), :] = jnp.where(eq & (before >= need), INT_MIN, kblk)
            return carry + jnp.sum(e, axis=0, keepdims=True)

        lax.fori_loop(0, n_tot, fix, jnp.zeros((1, tq), F32))

    def to_bias(kb, _):
        keys_ref[blk(kb), :] = pltpu.bitcast(jnp.where(keys_ref[blk(kb), :] >= lo, 0.0, NEG), I32)
        return 0

    lax.fori_loop(0, n_tot, to_bias, 0)

    group = N_HEADS // KV_HEADS

    def head_body(h, _):
        q = qp_ref[h]
        slope2 = hp_ref[h][:, 0:1]
        a_tab = slope2 * ik.astype(F32)
        vrow = pl.ds(pl.multiple_of((h // group) * HEAD_DIM, HEAD_DIM), HEAD_DIM)

        def step(kb, carry, diag):
            m, l, acc = carry
            s = _dotf(kt_ref[0, blk(kb), :], q) + pltpu.bitcast(keys_ref[blk(kb), :], F32)
            if diag:
                kpos = kb * TK + ik
                s = s - slope2 * jnp.abs(qpos - kpos).astype(F32) + slope2 * iq.astype(F32)
                c = None
            else:
                s = s + a_tab
                c = slope2 * (kb * TK - q_first).astype(F32)
            return _softmax_step(s, m, l, acc, vt_ref[0, kb, vrow, :], c)

        init = (jnp.full((1, tq), NEG, F32), jnp.zeros((1, tq), F32), jnp.zeros((HEAD_DIM, tq), F32))
        carry = lax.fori_loop(0, n_past, lambda kb, cr: step(kb, cr, False), init)
        m, l, acc = lax.fori_loop(n_past, n_tot, lambda kb, cr: step(kb, cr, True), carry)
        o_ref[h] = acc / l
        return 0

    lax.fori_loop(0, N_HEADS, head_body, 0)


def _dsa_attend(hp, qi, wi, qp, ki, kt, vt, *, tq, q0, n_real_q, n_keys):
    b = ki.shape[0]
    sq = qp.shape[2]
    lp = kt.shape[1]
    nkb = vt.shape[1]
    topk = min(TOPK_MAX, n_keys // 4)
    body = functools.partial(_dsa_body, tq=tq, q0=q0, n_real_q=n_real_q, n_keys=n_keys, topk=topk)
    once = pl.Buffered(1)
    return pl.pallas_call(
        body,
        grid=(b, sq // tq),
        in_specs=[_full_spec(hp),
                  pl.BlockSpec((1, IDX_HEADS, IDX_DIM, tq), lambda bb, i: (bb, 0, 0, i)),
                  pl.BlockSpec((1, IDX_HEADS, tq), lambda bb, i: (bb, 0, i)),
                  pl.BlockSpec((N_HEADS, KT_LANES, tq), lambda bb, i: (bb, 0, i)),
                  pl.BlockSpec((1, lp, IDX_DIM), lambda bb, i: (bb, 0, 0), pipeline_mode=once),
                  pl.BlockSpec((1, lp, KT_LANES), lambda bb, i: (bb, 0, 0), pipeline_mode=once),
                  pl.BlockSpec((1, nkb, KV_HEADS * HEAD_DIM, TK), lambda bb, i: (bb, 0, 0, 0),
                               pipeline_mode=once)],
        out_specs=pl.BlockSpec((N_HEADS, HEAD_DIM, tq), lambda bb, i: (bb, 0, i)),
        out_shape=jax.ShapeDtypeStruct((b * N_HEADS, HEAD_DIM, sq), F32),
        scratch_shapes=[pltpu.VMEM((lp, tq), I32)],
        compiler_params=_cparams(("arbitrary", "arbitrary")),
        name="dsa_attend",
    )(hp, qi, wi, qp, ki, kt, vt)


def _block_diag(c):
    g = np.arange(c) // HEAD_DIM
    return jnp.asarray(g[:, None] == g[None, :], BF16)


def _tile_gain(g, n):
    return jnp.tile(g.astype(F32), n).reshape(1, -1)


def _col(g):
    return g.astype(F32).reshape(-1, 1)


def _pad_cols(w, n):
    return jnp.pad(w, ((0, 0), (0, n - w.shape[1])))


def _rope_tables(pos):
    half = MLA_ROPE // 2
    inv = ROPE_BASE ** (-jnp.arange(half, dtype=F32) / half)
    ang = pos.astype(F32)[:, None] * inv[None, :]
    return jnp.cos(ang), jnp.sin(ang)


def _alibi_slopes():
    return np.asarray(2.0 ** (-8.0 * np.arange(1, N_HEADS + 1) / N_HEADS), dtype=np.float32)


def _head_params(sinks=None):
    hp = jnp.zeros((N_HEADS, 1, 128), F32)
    hp = hp.at[:, 0, 0].set(jnp.asarray(_alibi_slopes()) * LOG2E)
    if sinks is not None:
        hp = hp.at[:, 0, 1].set(sinks.astype(F32) * LOG2E)
    return hp


class _Stream:
    def __init__(self, batch, seq, past):
        self.b, self.s, self.p = batch, seq, past
        self.r = batch * seq
        self.decode = past > 0
        self.tq = TQ_DEC if self.decode else TQ
        self.sq = self.tq if self.decode else seq
        self.n_keys = past + seq
        self.lp = -(-self.n_keys // TK) * TK
        self.pos = past + np.tile(np.arange(seq), batch)

    def pad_keys(self, past_arr, new_arr, n_keys=None, lp=None):
        n_keys = self.n_keys if n_keys is None else n_keys
        lp = self.lp if lp is None else lp
        new_arr = new_arr.reshape(self.b, self.s, -1)
        parts = [new_arr] if past_arr is None else [past_arr.astype(F32), new_arr]
        if lp > n_keys:
            parts.append(jnp.zeros((self.b, lp - n_keys, new_arr.shape[-1]), F32))
        return jnp.concatenate(parts, axis=1) if len(parts) > 1 else new_arr

    def qp_blocks(self, qp):
        if not self.decode:
            return qp
        x = qp.reshape(N_HEADS, KT_LANES, self.b, self.s).transpose(2, 0, 1, 3)
        x = jnp.pad(x, ((0, 0), (0, 0), (0, 0), (0, self.tq - self.s)))
        return x.reshape(self.b * N_HEADS, KT_LANES, self.tq)

    def lanes(self, x):
        n, c, _ = x.shape
        if not self.decode:
            return x[None]
        x = x.reshape(n, c, self.b, self.s).transpose(2, 0, 1, 3)
        return jnp.pad(x, ((0, 0), (0, 0), (0, 0), (0, self.tq - self.s)))

    def ot_cols(self, ot):
        if not self.decode:
            return ot.reshape(N_HEADS * HEAD_DIM, self.r)
        x = ot.reshape(self.b, N_HEADS, HEAD_DIM, self.tq)[..., :self.s]
        return x.transpose(1, 2, 0, 3).reshape(N_HEADS * HEAD_DIM, self.r)


def _mod_rows(st, mod_l, row0):
    m = mod_l[row0:row0 + st.b]
    if st.b > 1:
        m = jnp.repeat(m, st.s, axis=0)
    return m[:, :D_MODEL], m[:, D_MODEL:2 * D_MODEL], m[:, 2 * D_MODEL:]


def _mla_layer(st, x, mod, g, w, past):
    shift, scale, gate = mod
    r = st.r
    ts = _row_tile(r)
    cos, sin = _rope_tables(jnp.asarray(st.pos))
    zpad = jnp.zeros((r, 128 - MLA_ROPE), F32)
    cos_p = jnp.concatenate([cos, cos, zpad], axis=1)
    sin_p = jnp.concatenate([-sin, sin, zpad], axis=1)
    cqn, lat, kr, zt = _proj_call(
        _mla_in_body, "mla_in", r, ts,
        [x, scale, shift, cos_p, sin_p, g, w["gqa"], w["gkva"], w["gkr"], w["gkrp"]], [],
        [w["wcq"], w["wckv"], w["wkr"], w["wkrp"], w["wzt"]],
        [_rows_out(r, MLA_Q_LORA, ts, BF16), _rows_out(r, MLA_KV_LORA, ts), _rows_out(r, 128, ts),
         _cols_out(D_MODEL, r, ts)])
    (qp,) = _proj_call(_mla_q_body, "mla_q", r, ts, [cqn], [cos.T, sin.T], [w["wqt"], w["gqn"], w["gqr"]],
                       [_qp_out(r, ts)])
    past_lat, past_kr = (None, None) if past is None else past
    if past_kr is not None:
        past_kr = jnp.pad(past_kr.astype(F32), ((0, 0), (0, 0), (0, 128 - MLA_ROPE)))
    lat_all = st.pad_keys(past_lat, lat).reshape(st.b * st.lp, MLA_KV_LORA)
    kr_all = st.pad_keys(past_kr, kr).reshape(st.b * st.lp, 128)
    rk = st.b * st.lp
    npair = N_HEADS // 2
    kt, vt = _proj_call(
        _mla_kv_body, "mla_kv", rk, TK, [lat_all, kr_all, w["gkn"]], [], [w["wkn"], w["wvt"], w["bd"]],
        [((npair, rk, KT_LANES), BF16, (npair, TK, KT_LANES), lambda i: (0, i, 0)),
         ((rk // TK, N_HEADS * HEAD_DIM, TK), BF16, (1, N_HEADS * HEAD_DIM, TK), lambda i: (i, 0, 0))])
    kt = kt.reshape(npair * st.b, st.lp, KT_LANES)
    vt = vt.reshape(st.b, st.lp // TK, N_HEADS * HEAD_DIM, TK)
    b = st.b
    ot = _flash(_head_params(), st.qp_blocks(qp), kt, vt, tq=st.tq, mode="chunk", q0=st.p,
                n_real_q=st.tq if not st.decode else st.s, n_keys=st.n_keys,
                tile_fn=lambda gi: ((gi % N_HEADS) // 2) * b + gi // N_HEADS,
                vrow_fn=lambda gi: gi % N_HEADS)
    x = _out_proj(x, gate, st.ot_cols(ot), zt, w["wo"])
    return x, (lat, kr[:, :MLA_ROPE])


def _gqa_common(st, k, v, past, extra=None):
    pk, pv = (None, None) if past is None else (past[0], past[1])
    c = k.shape[-1]
    if pk is not None:
        pk = pk.reshape(st.b, -1, c)
        pv = pv.reshape(st.b, -1, c)
    return pk, pv


def _dsa_layer(st, x, mod, g, w, past):
    shift, scale, gate = mod
    r = st.r
    ts = _row_tile(r)
    c = KV_HEADS * HEAD_DIM
    qp, k, v, qi, ki, wi, zt = _proj_call(
        _dsa_in_body, "dsa_in", r, ts, [x, scale, shift, g, w["gk"]], [],
        [w["gq"], w["wqt"], w["wk"], w["wv"], w["wqit"], w["wki"], w["wwit"], w["wzt"], w["bd"]],
        [_qp_out(r, ts), _rows_out(r, c, ts), _rows_out(r, c, ts),
         ((IDX_HEADS, IDX_DIM, r), BF16, (IDX_HEADS, IDX_DIM, ts), lambda i: (0, 0, i)),
         _rows_out(r, IDX_DIM, ts), _cols_out(IDX_HEADS, r, ts), _cols_out(D_MODEL, r, ts)])
    if past is None:
        pk = pv = pki = None
    else:
        pk, pv, pki = past[0].reshape(st.b, -1, c), past[1].reshape(st.b, -1, c), past[2]
    kt, vt, kib = _gqa_prep(st.pad_keys(pk, k), st.pad_keys(pv, v), st.pad_keys(pki, ki))
    qi_b = st.lanes(qi)
    wi_b = st.lanes(wi[None])[:, 0]
    ot = _dsa_attend(_head_params(), qi_b, wi_b, st.qp_blocks(qp), kib, kt, vt, tq=st.tq, q0=st.p,
                     n_real_q=st.tq if not st.decode else st.s, n_keys=st.n_keys)
    x = _out_proj(x, gate, st.ot_cols(ot), zt, w["wo"])
    return x, (k, v, ki)


def _swa_layer(st, x, mod, g, w, past):
    shift, scale, gate = mod
    r = st.r
    ts = _row_tile(r)
    c = KV_HEADS * HEAD_DIM
    qp, k, v, zt = _proj_call(
        _swa_in_body, "swa_in", r, ts, [x, scale, shift, g, w["gk"]], [],
        [w["gq"], w["wqt"], w["wk"], w["wv"], w["wzt"], w["bd"]],
        [_qp_out(r, ts), _rows_out(r, c, ts), _rows_out(r, c, ts), _cols_out(D_MODEL, r, ts)])
    group = N_HEADS // KV_HEADS
    if past is None:
        n_keys, lp, key0 = st.s, st.lp, 0
        k_all, v_all = st.pad_keys(None, k), st.pad_keys(None, v)
        new = (k.reshape(st.b, st.s, c)[:, st.s - WINDOW:], v.reshape(st.b, st.s, c)[:, st.s - WINDOW:])
    else:
        win = past[0].shape[1]
        n_keys = win + st.s
        lp = -(-n_keys // TK) * TK
        key0 = st.p - win
        k_all = st.pad_keys(past[0].reshape(st.b, win, c), k, n_keys, lp)
        v_all = st.pad_keys(past[1].reshape(st.b, win, c), v, n_keys, lp)
        new = (k_all[:, st.s:n_keys], v_all[:, st.s:n_keys])
    kt, vt = _gqa_prep(k_all, v_all)
    ot = _flash(_head_params(w["sinks"]), st.qp_blocks(qp), kt, vt, tq=st.tq, mode="window", q0=st.p,
                n_real_q=st.tq if not st.decode else st.s, n_keys=n_keys, key0=key0, alibi=True, sinks=True,
                tile_fn=lambda gi: gi // N_HEADS, vrow_fn=lambda gi: (gi % N_HEADS) // group)
    x = _out_proj(x, gate, st.ot_cols(ot), zt, w["wo"])
    return x, new


def _fox_layer(st, x, mod, g, w, past):
    shift, scale, gate = mod
    r = st.r
    ts = _row_tile(r)
    c = N_HEADS * HEAD_DIM
    qp, k, v, lf, zt = _proj_call(
        _fox_in_body, "fox_in", r, ts, [x, scale, shift, g, w["gk"], w["bf"]], [],
        [w["gq"], w["wqt"], w["wk"], w["wv"], w["wf"], w["wzt"], w["bd"]],
        [_qp_out(r, ts), _rows_out(r, c, ts), _rows_out(r, c, ts), _rows_out(r, 128, ts),
         _cols_out(D_MODEL, r, ts)])
    if past is None:
        pk = pv = plf = None
    else:
        pk, pv = past[0].reshape(st.b, -1, c), past[1].reshape(st.b, -1, c)
        plf = jnp.pad(past[2].astype(F32), ((0, 0), (0, 0), (0, 128 - N_HEADS)))
    kt, vt = _fox_prep(st.pad_keys(pk, k), st.pad_keys(pv, v), st.pad_keys(plf, lf))
    ot = _flash(_head_params(), st.qp_blocks(qp), kt, vt, tq=st.tq, mode="causal", q0=st.p,
                n_real_q=st.tq if not st.decode else st.s, n_keys=st.n_keys,
                tile_fn=lambda gi: gi // 2, vrow_fn=lambda gi: gi % N_HEADS)
    x = _out_proj(x, gate, st.ot_cols(ot), zt, w["wo"])
    return x, (k, v, lf[:, :N_HEADS])


def _prep_weights(mla_w_in, mla_g_qa, mla_w_qb, mla_g_kva, mla_w_kvb, mla_g_qn, mla_g_qr, mla_g_kn,
                  mla_g_kr, mla_w_out, dsa_w_in, dsa_g_q, dsa_g_k, dsa_w_out, swa_w_in, swa_g_q,
                  swa_g_k, swa_sinks, swa_w_out, fox_w_in, fox_b_f, fox_g_q, fox_g_k, fox_w_out):
    bf = lambda a: a.astype(BF16)
    row = lambda a: a.astype(F32).reshape(1, -1)
    half = MLA_ROPE // 2
    c1, c2, c3 = MLA_Q_LORA, MLA_Q_LORA + MLA_KV_LORA, MLA_Q_LORA + MLA_KV_LORA + MLA_ROPE
    wkr = mla_w_in[:, c2:c3]
    wkrp = jnp.concatenate([wkr[:, half:], wkr[:, :half]], axis=1)
    gkr = mla_g_kr.astype(F32)
    gkrp = jnp.concatenate([gkr[half:], gkr[:half]])
    kvb = mla_w_kvb.reshape(MLA_KV_LORA, N_HEADS, MLA_NOPE + HEAD_DIM)
    mla = dict(
        wcq=bf(mla_w_in[:, :c1]), wckv=bf(mla_w_in[:, c1:c2]), wkr=bf(_pad_cols(wkr, 128)),
        wkrp=bf(_pad_cols(wkrp, 128)), wzt=bf(mla_w_in[:, c3:].T),
        gqa=row(mla_g_qa), gkva=row(mla_g_kva), gkr=row(jnp.pad(gkr, (0, 128 - MLA_ROPE))),
        gkrp=row(jnp.pad(gkrp, (0, 128 - MLA_ROPE))),
        wqt=bf(mla_w_qb.T), gqn=_col(mla_g_qn), gqr=_col(mla_g_qr),
        wkn=bf(kvb[:, :, :MLA_NOPE].reshape(MLA_KV_LORA, -1)),
        wvt=bf(kvb[:, :, MLA_NOPE:].reshape(MLA_KV_LORA, -1).T),
        gkn=_tile_gain(mla_g_kn, N_HEADS), bd=_block_diag(N_HEADS * HEAD_DIM), wo=bf(mla_w_out))
    hq, hk = N_HEADS * HEAD_DIM, KV_HEADS * HEAD_DIM
    cuts = np.cumsum([hq, hk, hk, IDX_HEADS * IDX_DIM, IDX_DIM, IDX_HEADS]).tolist()
    dsa = dict(
        wqt=bf(dsa_w_in[:, :cuts[0]].T), wk=bf(dsa_w_in[:, cuts[0]:cuts[1]]),
        wv=bf(dsa_w_in[:, cuts[1]:cuts[2]]), wqit=bf(dsa_w_in[:, cuts[2]:cuts[3]].T),
        wki=bf(dsa_w_in[:, cuts[3]:cuts[4]]), wwit=bf(dsa_w_in[:, cuts[4]:cuts[5]].T),
        wzt=bf(dsa_w_in[:, cuts[5]:].T), gq=_col(dsa_g_q), gk=_tile_gain(dsa_g_k, KV_HEADS),
        bd=_block_diag(hk), wo=bf(dsa_w_out))
    swa = dict(
        wqt=bf(swa_w_in[:, :hq].T), wk=bf(swa_w_in[:, hq:hq + hk]), wv=bf(swa_w_in[:, hq + hk:hq + 2 * hk]),
        wzt=bf(swa_w_in[:, hq + 2 * hk:].T), gq=_col(swa_g_q), gk=_tile_gain(swa_g_k, KV_HEADS),
        bd=_block_diag(hk), wo=bf(swa_w_out), sinks=swa_sinks)
    fox = dict(
        wqt=bf(fox_w_in[:, :hq].T), wk=bf(fox_w_in[:, hq:2 * hq]), wv=bf(fox_w_in[:, 2 * hq:3 * hq]),
        wf=bf(_pad_cols(fox_w_in[:, 3 * hq:3 * hq + N_HEADS], 128)), wzt=bf(fox_w_in[:, 3 * hq + N_HEADS:].T),
        bf=row(jnp.pad(fox_b_f.astype(F32), (0, 128 - N_HEADS))), gq=_col(fox_g_q),
        gk=_tile_gain(fox_g_k, N_HEADS), bd=_block_diag(hq), wo=bf(fox_w_out))
    return [mla, dsa, swa, fox]


def kernel(x_prompt, x_sample, cache_mla_latent, cache_mla_krope, cache_dsa_k, cache_dsa_v, cache_dsa_kidx, state_swa_k, state_swa_v, cache_fox_k, cache_fox_v, cache_fox_logf, c_prompt, c_sample, norm_g, ada_w, ada_b, mla_w_in, mla_g_qa, mla_w_qb, mla_g_kva, mla_w_kvb, mla_g_qn, mla_g_qr, mla_g_kn, mla_g_kr, mla_w_out, dsa_w_in, dsa_g_q, dsa_g_k, dsa_w_out, swa_w_in, swa_g_q, swa_g_k, swa_sinks, swa_w_out, fox_w_in, fox_b_f, fox_g_q, fox_g_k, fox_w_out):
    bp, sp, _ = x_prompt.shape
    bs, ss, _ = x_sample.shape
    past_len = cache_mla_latent.shape[1]
    depth = norm_g.shape[0]
    assert bp == 1 and sp % TQ == 0 and sp % TS == 0 and (bs * ss) % 8 == 0 and ss <= TQ_DEC
    assert past_len % TK == 0 and past_len >= WINDOW

    weights = _prep_weights(mla_w_in, mla_g_qa, mla_w_qb, mla_g_kva, mla_w_kvb, mla_g_qn, mla_g_qr,
                            mla_g_kn, mla_g_kr, mla_w_out, dsa_w_in, dsa_g_q, dsa_g_k, dsa_w_out,
                            swa_w_in, swa_g_q, swa_g_k, swa_sinks, swa_w_out, fox_w_in, fox_b_f,
                            fox_g_q, fox_g_k, fox_w_out)
    rows = bp + bs
    rows_p = -(-rows // 8) * 8
    c_all = jnp.concatenate([c_prompt, c_sample, jnp.zeros((rows_p - rows, D_MODEL), F32)], axis=0)
    mod = _ada_mod(c_all, ada_w, ada_b)

    st_p = _Stream(bp, sp, 0)
    st_s = _Stream(bs, ss, past_len)
    pasts = ((cache_mla_latent, cache_mla_krope), (cache_dsa_k, cache_dsa_v, cache_dsa_kidx),
             (state_swa_k, state_swa_v), (cache_fox_k, cache_fox_v, cache_fox_logf))
    layers = (_mla_layer, _dsa_layer, _swa_layer, _fox_layer)
    xp = x_prompt.reshape(st_p.r, D_MODEL)
    xs = x_sample.reshape(st_s.r, D_MODEL)
    new_p, new_s = [], []
    for layer in range(depth):
        kind = layer % len(layers)
        g = norm_g[layer].astype(F32).reshape(1, -1)
        xp, n = layers[kind](st_p, xp, _mod_rows(st_p, mod[layer], 0), g, weights[kind], None)
        new_p.append(n)
        xs, n = layers[kind](st_s, xs, _mod_rows(st_s, mod[layer], bp), g, weights[kind], pasts[kind])
        new_s.append(n)

    def shaped(st, new):
        (lat, kr), (dk, dv, dki), (sk, sv), (fk, fv, flf) = new
        b, s = st.b, st.s
        return (lat.reshape(b, s, -1), kr.reshape(b, s, -1),
                dk.reshape(b, s, KV_HEADS, HEAD_DIM), dv.reshape(b, s, KV_HEADS, HEAD_DIM),
                dki.reshape(b, s, -1),
                sk.reshape(b, -1, KV_HEADS, HEAD_DIM), sv.reshape(b, -1, KV_HEADS, HEAD_DIM),
                fk.reshape(b, s, N_HEADS, HEAD_DIM), fv.reshape(b, s, N_HEADS, HEAD_DIM),
                flf.reshape(b, s, -1))

    return (xp.reshape(x_prompt.shape), xs.reshape(x_sample.shape)) + shaped(st_p, new_p) + shaped(st_s, new_s)
```

```python
import functools

import numpy as np
import jax
import jax.numpy as jnp
from jax import lax
from jax.experimental import pallas as pl
from jax.experimental.pallas import tpu as pltpu

F32 = jnp.float32
BF16 = jnp.bfloat16
I32 = jnp.int32

D_MODEL = 1024
HEAD_DIM = 64
N_HEADS = 16
KV_HEADS = 4
CHUNK = 64
CHUNK_SHIFT = 6
WINDOW = 128
WIN_CHUNKS = WINDOW // CHUNK
EPS = 1e-6
ROPE_BASE = 10000.0
MLA_NOPE, MLA_ROPE, MLA_Q_LORA, MLA_KV_LORA = 64, 32, 384, 256
IDX_HEADS, IDX_DIM, TOPK_MAX = 8, 64, 256
LOG2E = 1.4426950408889634
NEG = -1e30
INT_MIN = -(2 ** 31)
INT_MAX = 2 ** 31 - 1

TS = 256
TQ = 256
TQ_FLASH = 2048
TK = 512
TQ_DEC = 128
KT_LANES = 256
VMEM_LIMIT = 56 * 1024 * 1024


def _row_tile(r):
    return TS if r % TS == 0 else r


def _cparams(sem, vmem=VMEM_LIMIT):
    return pltpu.CompilerParams(dimension_semantics=sem, vmem_limit_bytes=vmem)


def _dotf(a, b):
    return jnp.dot(a, b, preferred_element_type=F32)


def _dot_nt(a, b):
    return lax.dot_general(a, b, (((1,), (1,)), ((), ())), preferred_element_type=F32)


def _dot_tn(a, b):
    return lax.dot_general(a, b, (((0,), (0,)), ((), ())), preferred_element_type=F32)


def _split3(x):
    hi = x.astype(BF16)
    r = x - hi.astype(F32)
    mid = r.astype(BF16)
    lo = (r - mid.astype(F32)).astype(BF16)
    return hi, mid, lo


def _silu(x):
    return x / (1.0 + jnp.exp(-x))


def _full_spec(arr):
    nd = arr.ndim
    return pl.BlockSpec(arr.shape, lambda *_: (0,) * nd)


def _row_spec(arr, ts):
    if arr.shape[0] == 1:
        return pl.BlockSpec((1, arr.shape[1]), lambda i: (0, 0))
    return pl.BlockSpec((ts, arr.shape[1]), lambda i: (i, 0))


def _col_spec(arr, ts):
    return pl.BlockSpec((arr.shape[0], ts), lambda i: (0, i))


def _ada_body(c_ref, w_ref, b_ref, o_ref):
    a = _silu(c_ref[...])
    w = w_ref[0]
    a_hi = a.astype(BF16)
    a_lo = (a - a_hi.astype(F32)).astype(BF16)
    w_hi = w.astype(BF16)
    w_lo = (w - w_hi.astype(F32)).astype(BF16)
    o_ref[0] = _dotf(a_hi, w_hi) + _dotf(a_hi, w_lo) + _dotf(a_lo, w_hi) + b_ref[0]


def _ada_mod(c_all, ada_w, ada_b):
    depth, d, n3 = ada_w.shape
    bp = c_all.shape[0]
    tn = 768
    return pl.pallas_call(
        _ada_body,
        grid=(depth, n3 // tn),
        in_specs=[
            pl.BlockSpec((bp, d), lambda l, j: (0, 0)),
            pl.BlockSpec((1, d, tn), lambda l, j: (l, 0, j)),
            pl.BlockSpec((1, 1, tn), lambda l, j: (l, 0, j)),
        ],
        out_specs=pl.BlockSpec((1, bp, tn), lambda l, j: (l, 0, j)),
        out_shape=jax.ShapeDtypeStruct((depth, bp, n3), F32),
        compiler_params=_cparams(("arbitrary", "arbitrary")),
        name="ada_mod",
    )(c_all, ada_w, ada_b.reshape(depth, 1, n3))


def _prenorm(x_ref, g_ref, sc_ref, sh_ref):
    x = x_ref[...]
    ms = jnp.mean(x * x, axis=-1, keepdims=True)
    xn = x * lax.rsqrt(ms + EPS) * g_ref[...]
    return (xn * (1.0 + sc_ref[...]) + sh_ref[...]).astype(BF16)


def _group_sumsq(y, bd_ref):
    sq = y * y
    hi = sq.astype(BF16)
    lo = (sq - hi.astype(F32)).astype(BF16)
    bd = bd_ref[...]
    return _dotf(hi, bd) + _dotf(lo, bd)


def _head_rms_rows(q, g_col):
    ms = jnp.mean(q * q, axis=0, keepdims=True)
    return q * lax.rsqrt(ms + EPS) * g_col


def _place_rows(piece, row0, total):
    ts = piece.shape[1]
    parts = []
    if row0 > 0:
        parts.append(jnp.zeros((row0, ts), F32))
    parts.append(piece)
    rest = total - row0 - piece.shape[0]
    if rest > 0:
        parts.append(jnp.zeros((rest, ts), F32))
    return jnp.concatenate(parts, axis=0) if len(parts) > 1 else piece


def _proj_call(body, name, r, ts, row_in, col_in, const_in, outs):
    in_specs = ([_row_spec(a, ts) for a in row_in] + [_col_spec(a, ts) for a in col_in]
                + [_full_spec(a) for a in const_in])
    return pl.pallas_call(
        body,
        grid=(r // ts,),
        in_specs=in_specs,
        out_specs=[pl.BlockSpec(blk, im) for (_, _, blk, im) in outs],
        out_shape=[jax.ShapeDtypeStruct(s, dt) for (s, dt, _, _) in outs],
        compiler_params=_cparams(("arbitrary",)),
        name=name,
    )(*row_in, *col_in, *const_in)


def _rows_out(r, c, ts, dtype=F32):
    return ((r, c), dtype, (ts, c), lambda i: (i, 0))


def _cols_out(c, r, ts, dtype=F32):
    return ((c, r), dtype, (c, ts), lambda i: (0, i))


def _qp_out(r, ts):
    return ((N_HEADS, KT_LANES, r), BF16, (N_HEADS, KT_LANES, ts), lambda i: (0, 0, i))


def _mla_in_body(x_ref, sc_ref, sh_ref, cos_ref, sin_ref, g_ref, gqa, gkva, gkr, gkrp,
                 wcq, wckv, wkr, wkrp, wzt, cqn_o, lat_o, kr_o, zt_o):
    h = _prenorm(x_ref, g_ref, sc_ref, sh_ref)
    cq = _dotf(h, wcq[...])
    cqn_o[...] = (cq * lax.rsqrt(jnp.mean(cq * cq, axis=-1, keepdims=True) + EPS) * gqa[...]).astype(BF16)
    ckv = _dotf(h, wckv[...])
    lat_o[...] = ckv * lax.rsqrt(jnp.mean(ckv * ckv, axis=-1, keepdims=True) + EPS) * gkva[...]
    kr = _dotf(h, wkr[...])
    krp = _dotf(h, wkrp[...])
    inv = lax.rsqrt(jnp.sum(kr * kr, axis=-1, keepdims=True) * (1.0 / MLA_ROPE) + EPS)
    kr_o[...] = (kr * gkr[...] * cos_ref[...] + krp * gkrp[...] * sin_ref[...]) * inv
    zt_o[...] = _dot_nt(wzt[...], h)


def _mla_q_body(cqn_ref, cos_ref, sin_ref, wqt, gqn, gqr, qp_o):
    qt = _dot_nt(wqt[...], cqn_ref[...])
    ts = qt.shape[1]
    sc = (MLA_NOPE + MLA_ROPE) ** -0.5 * LOG2E
    c = cos_ref[...]
    s = sin_ref[...]
    half = MLA_ROPE // 2
    width = MLA_NOPE + MLA_ROPE
    for h in range(N_HEADS):
        qn = _head_rms_rows(qt[width * h:width * h + MLA_NOPE], gqn[...]) * sc
        qr = _head_rms_rows(qt[width * h + MLA_NOPE:width * (h + 1)], gqr[...]) * sc
        x1, x2 = qr[:half], qr[half:]
        o1 = x1 * c - x2 * s
        o2 = x2 * c + x1 * s
        a = h % 2
        pieces = [qn, jnp.zeros((HEAD_DIM, ts), F32)]
        if a:
            pieces = pieces[::-1]
        pieces += [o1, o2, jnp.zeros((KT_LANES - 2 * HEAD_DIM - MLA_ROPE, ts), F32)]
        qp_o[h] = jnp.concatenate(pieces, axis=0).astype(BF16)


def _mla_kv_body(lat_ref, kr_ref, gkn, wkn, wvt, bd, kt_o, vt_o):
    lat = lat_ref[...].astype(BF16)
    kn = _dotf(lat, wkn[...])
    ss = _group_sumsq(kn, bd)
    kn = kn * lax.rsqrt(ss * (1.0 / MLA_NOPE) + EPS) * gkn[...]
    kr = kr_ref[...].astype(BF16)
    for p in range(N_HEADS // 2):
        kt_o[p, :, 0:128] = kn[:, 128 * p:128 * (p + 1)].astype(BF16)
        kt_o[p, :, 128:256] = kr
    vt_o[0] = _dot_nt(wvt[...], lat).astype(BF16)


def _q_heads_to_qp(qt, gq, qp_o, row_of_head, extra_of_head=None):
    sc = HEAD_DIM ** -0.5 * LOG2E
    ts = qt.shape[1]
    for h in range(N_HEADS):
        qh = _head_rms_rows(qt[HEAD_DIM * h:HEAD_DIM * (h + 1)], gq[...]) * sc
        blk = _place_rows(qh, row_of_head(h), KT_LANES)
        if extra_of_head is not None:
            lo, hi = extra_of_head(h)
            rows = lax.broadcasted_iota(I32, (KT_LANES, ts), 0)
            blk = blk + jnp.where((rows >= lo) & (rows < hi), 1.0, 0.0)
        qp_o[h] = blk.astype(BF16)


def _dsa_in_body(x_ref, sc_ref, sh_ref, g_ref, gk, gq, wqt, wk, wv, wqit, wki, wwit, wzt, bd,
                 qp_o, k_o, v_o, qi_o, ki_o, wi_o, zt_o):
    h = _prenorm(x_ref, g_ref, sc_ref, sh_ref)
    _q_heads_to_qp(_dot_nt(wqt[...], h), gq, qp_o, lambda hh: HEAD_DIM * (hh // (N_HEADS // KV_HEADS)))
    k = _dotf(h, wk[...])
    k_o[...] = k * lax.rsqrt(_group_sumsq(k, bd) * (1.0 / HEAD_DIM) + EPS) * gk[...]
    v_o[...] = _dotf(h, wv[...])
    qit = _dot_nt(wqit[...], h)
    for ih in range(IDX_HEADS):
        qi_o[ih] = qit[IDX_DIM * ih:IDX_DIM * (ih + 1)].astype(BF16)
    ki_o[...] = _dotf(h, wki[...])
    wi_o[...] = _dot_nt(wwit[...], h)
    zt_o[...] = _dot_nt(wzt[...], h)


def _swa_in_body(x_ref, sc_ref, sh_ref, g_ref, gk, gq, wqt, wk, wv, wzt, bd, qp_o, k_o, v_o, zt_o):
    h = _prenorm(x_ref, g_ref, sc_ref, sh_ref)
    _q_heads_to_qp(_dot_nt(wqt[...], h), gq, qp_o, lambda hh: HEAD_DIM * (hh // (N_HEADS // KV_HEADS)))
    k = _dotf(h, wk[...])
    k_o[...] = k * lax.rsqrt(_group_sumsq(k, bd) * (1.0 / HEAD_DIM) + EPS) * gk[...]
    v_o[...] = _dotf(h, wv[...])
    zt_o[...] = _dot_nt(wzt[...], h)


def _fox_in_body(x_ref, sc_ref, sh_ref, g_ref, gk, bf, gq, wqt, wk, wv, wf, wzt, bd,
                 qp_o, k_o, v_o, lf_o, zt_o):
    h = _prenorm(x_ref, g_ref, sc_ref, sh_ref)
    _q_heads_to_qp(_dot_nt(wqt[...], h), gq, qp_o, lambda hh: HEAD_DIM * (hh % 2),
                   lambda hh: (128 + 3 * (hh % 2), 128 + 3 * (hh % 2) + 3))
    k = _dotf(h, wk[...])
    k_o[...] = k * lax.rsqrt(_group_sumsq(k, bd) * (1.0 / HEAD_DIM) + EPS) * gk[...]
    v_o[...] = _dotf(h, wv[...])
    f = _dotf(h, wf[...]) + bf[...]
    lf_o[...] = jnp.minimum(f, 0.0) - jnp.log1p(jnp.exp(-jnp.abs(f)))
    zt_o[...] = _dot_nt(wzt[...], h)


def _out_body(x_ref, gate_ref, ot_ref, zt_ref, wo, o_ref):
    u = (ot_ref[...] * _silu(zt_ref[...])).astype(BF16)
    o_ref[...] = x_ref[...] + gate_ref[...] * _dot_tn(u, wo[...])


def _out_proj(x, gate, ot, zt, wo):
    r = x.shape[0]
    ts = _row_tile(r)
    (out,) = _proj_call(_out_body, "out_proj", r, ts, [x, gate], [ot, zt], [wo],
                        [_rows_out(r, D_MODEL, ts)])
    return out


def _gqa_prep_body(*refs, with_ki):
    if with_ki:
        k_ref, v_ref, ki_ref, kt_o, vt_o, ki_o = refs
        ki_o[0] = ki_ref[0].astype(BF16)
    else:
        k_ref, v_ref, kt_o, vt_o = refs
    kt_o[0] = k_ref[0].astype(BF16)
    vt_o[0, 0] = v_ref[0].T.astype(BF16)


def _gqa_prep(k, v, ki=None):
    b, lp, c = k.shape
    nkb = lp // TK
    ins = [k, v] + ([ki] if ki is not None else [])
    in_specs = [pl.BlockSpec((1, TK, a.shape[2]), lambda bb, j: (bb, j, 0)) for a in ins]
    out_specs = [pl.BlockSpec((1, TK, c), lambda bb, j: (bb, j, 0)),
                 pl.BlockSpec((1, 1, c, TK), lambda bb, j: (bb, j, 0, 0))]
    out_shape = [jax.ShapeDtypeStruct((b, lp, c), BF16), jax.ShapeDtypeStruct((b, nkb, c, TK), BF16)]
    if ki is not None:
        out_specs.append(pl.BlockSpec((1, TK, ki.shape[2]), lambda bb, j: (bb, j, 0)))
        out_shape.append(jax.ShapeDtypeStruct(ki.shape, BF16))
    return pl.pallas_call(
        functools.partial(_gqa_prep_body, with_ki=ki is not None),
        grid=(b, nkb), in_specs=in_specs, out_specs=out_specs, out_shape=out_shape,
        compiler_params=_cparams(("arbitrary", "arbitrary")), name="gqa_prep",
    )(*ins)


def _fox_prep_body(k_ref, v_ref, lf_ref, sel_ref, kt_o, vt_o, carry_ref):
    kb = pl.program_id(1)

    @pl.when(kb == 0)
    def _():
        carry_ref[...] = jnp.zeros_like(carry_ref)

    r = lax.broadcasted_iota(I32, (TK, TK), 0)
    c = lax.broadcasted_iota(I32, (TK, TK), 1)
    tri = jnp.where(c <= r, 1.0, 0.0).astype(BF16)
    hi, mid, lo = _split3(lf_ref[0])
    cum = _dotf(tri, hi) + _dotf(tri, mid) + _dotf(tri, lo) + carry_ref[...]
    carry_ref[...] = cum[TK - 1:TK, :]
    a, b, d = _split3(-(cum * LOG2E))
    ex = _dotf(a, sel_ref[0]) + _dotf(b, sel_ref[1]) + _dotf(d, sel_ref[2])
    k = k_ref[0]
    for p in range(N_HEADS // 2):
        kt_o[p, :, 0:128] = k[:, 128 * p:128 * (p + 1)].astype(BF16)
        kt_o[p, :, 128:256] = ex[:, 128 * p:128 * (p + 1)].astype(BF16)
    vt_o[0, 0] = v_ref[0].T.astype(BF16)


def _fox_sel():
    sel = np.zeros((3, 128, N_HEADS * HEAD_DIM), np.float32)
    for h in range(N_HEADS):
        for j in range(3):
            sel[j, h, 128 * (h // 2) + 3 * (h % 2) + j] = 1.0
    return jnp.asarray(sel, BF16)


def _fox_prep(k, v, lf128):
    b, lp, c = k.shape
    nkb = lp // TK
    npair = N_HEADS // 2
    sel = _fox_sel()
    return pl.pallas_call(
        _fox_prep_body,
        grid=(b, nkb),
        in_specs=[pl.BlockSpec((1, TK, c), lambda bb, j: (bb, j, 0)),
                  pl.BlockSpec((1, TK, c), lambda bb, j: (bb, j, 0)),
                  pl.BlockSpec((1, TK, 128), lambda bb, j: (bb, j, 0)),
                  _full_spec(sel)],
        out_specs=[pl.BlockSpec((npair, TK, KT_LANES), lambda bb, j: (bb, j, 0)),
                   pl.BlockSpec((1, 1, c, TK), lambda bb, j: (bb, j, 0, 0))],
        out_shape=[jax.ShapeDtypeStruct((b * npair, lp, KT_LANES), BF16),
                   jax.ShapeDtypeStruct((b, nkb, c, TK), BF16)],
        scratch_shapes=[pltpu.VMEM((1, 128), F32)],
        compiler_params=_cparams(("arbitrary", "arbitrary")), name="fox_prep",
    )(k, v, lf128, sel)


def _softmax_step(s, m, l, acc, vt_blk, c=None):
    mb = jnp.max(s, axis=0, keepdims=True)
    if c is not None:
        mb = mb + c
    m_new = jnp.maximum(m, mb)
    alpha = jnp.exp2(m - m_new)
    p = jnp.exp2(s - (m_new if c is None else m_new - c))
    l = alpha * l + jnp.sum(p, axis=0, keepdims=True)
    acc = alpha * acc + _dotf(vt_blk, p.astype(BF16))
    return m_new, l, acc


def _visible_end(qpos, mode, n_keys, key0):
    if mode == "causal":
        end = qpos + 1 - key0
    else:
        end = (((qpos >> CHUNK_SHIFT) + 1) << CHUNK_SHIFT) - key0
    return jnp.minimum(end, n_keys)


def _flash_body(hp_ref, qp_ref, kt_ref, vt_ref, o_ref, *, tq, mode, q0, n_real_q, n_keys, key0,
                alibi, sinks):
    i = pl.program_id(1)
    q = qp_ref[0]
    q_first = q0 + i * tq
    q_last = q_first + (n_real_q - 1)
    n_tot = (_visible_end(q_last, mode, n_keys, key0) + (TK - 1)) // TK
    if mode == "window":
        first_key = jnp.maximum((((q_first >> CHUNK_SHIFT) - WIN_CHUNKS) << CHUNK_SHIFT) - key0, 0)
        n_lo = first_key // TK
        n_full = n_lo
    else:
        n_lo = 0
        n_full = _visible_end(q_first, mode, n_keys, key0) // TK
    ik = lax.broadcasted_iota(I32, (TK, tq), 0)
    iq = lax.broadcasted_iota(I32, (TK, tq), 1)
    qpos = q_first + iq
    slope2 = hp_ref[0, :, 0:1] if alibi else None
    a_tab = slope2 * ik.astype(F32) if alibi else None

    def step(kb, carry, masked):
        m, l, acc = carry
        k0 = pl.multiple_of(kb * TK, TK)
        s = _dotf(kt_ref[0, pl.ds(k0, TK), :], q)
        c = None
        if masked:
            kidx = k0 + ik
            kpos = kidx + key0
            if mode == "causal":
                valid = kpos <= qpos
            else:
                qc = qpos >> CHUNK_SHIFT
                kc = kpos >> CHUNK_SHIFT
                valid = kc <= qc
                if mode == "window":
                    valid = valid & (qc - kc <= WIN_CHUNKS) & (kpos >= 0)
            valid = valid & (kidx < n_keys)
            if alibi:
                s = s - slope2 * jnp.abs(qpos - kpos).astype(F32) + slope2 * iq.astype(F32)
            s = jnp.where(valid, s, NEG)
        elif alibi:
            s = s + a_tab
            c = slope2 * (k0 + key0 - q_first).astype(F32)
        return _softmax_step(s, m, l, acc, vt_ref[0, kb], c)

    init = (jnp.full((1, tq), NEG, F32), jnp.zeros((1, tq), F32), jnp.zeros((HEAD_DIM, tq), F32))
    carry = lax.fori_loop(n_lo, n_full, lambda kb, cr: step(kb, cr, False), init)
    m, l, acc = lax.fori_loop(n_full, n_tot, lambda kb, cr: step(kb, cr, True), carry)
    if sinks:
        sink2 = hp_ref[0, :, 1:2]
        if alibi:
            sink2 = sink2 + slope2 * lax.broadcasted_iota(I32, (1, tq), 1).astype(F32)
        m_fin = jnp.maximum(m, sink2)
        a = jnp.exp2(m - m_fin)
        l = l * a + jnp.exp2(sink2 - m_fin)
        acc = acc * a
    o_ref[0] = acc / l


def _flash(hp, qp, kt, vt, *, tq, mode, q0, n_real_q, n_keys, key0=0, alibi=False, sinks=False,
           tile_fn, vrow_fn, heads_per_batch=N_HEADS):
    bh, _, sq = qp.shape
    lp = kt.shape[1]
    nkb = vt.shape[1]
    body = functools.partial(_flash_body, tq=tq, mode=mode, q0=q0, n_real_q=n_real_q, n_keys=n_keys,
                             key0=key0, alibi=alibi, sinks=sinks)
    return pl.pallas_call(
        body,
        grid=(bh, sq // tq),
        in_specs=[pl.BlockSpec((1, 1, 128), lambda g, i: (g % heads_per_batch, 0, 0)),
                  pl.BlockSpec((1, KT_LANES, tq), lambda g, i: (g, 0, i)),
                  pl.BlockSpec((1, lp, KT_LANES), lambda g, i: (tile_fn(g), 0, 0)),
                  pl.BlockSpec((1, nkb, HEAD_DIM, TK), lambda g, i: (g // heads_per_batch, 0, vrow_fn(g), 0))],
        out_specs=pl.BlockSpec((1, HEAD_DIM, tq), lambda g, i: (g, 0, i)),
        out_shape=jax.ShapeDtypeStruct((bh, HEAD_DIM, sq), F32),
        compiler_params=_cparams(("arbitrary", "arbitrary")),
        name="flash_" + mode,
    )(hp, qp, kt, vt)


def _dsa_body(hp_ref, qi_ref, wi_ref, qp_ref, ki_ref, kt_ref, vt_ref, o_ref, keys_ref, *,
              tq, q0, n_real_q, n_keys, topk):
    i = pl.program_id(1)
    q_first = q0 + i * tq
    q_last = q_first + (n_real_q - 1)
    n_tot = (_visible_end(q_last, "chunk", n_keys, 0) + (TK - 1)) // TK
    n_past = jnp.minimum(q_first, n_keys) // TK
    ik = lax.broadcasted_iota(I32, (TK, tq), 0)
    iq = lax.broadcasted_iota(I32, (TK, tq), 1)
    qpos = q_first + iq
    tf = float(topk)

    def blk(kb):
        return pl.ds(pl.multiple_of(kb * TK, TK), TK)

    def score_blk(kb, _):
        ki = ki_ref[0, blk(kb), :]
        acc = jnp.zeros((TK, tq), F32)
        for h in range(IDX_HEADS):
            acc = acc + wi_ref[0, h:h + 1, :] * jnp.maximum(_dotf(ki, qi_ref[0, h]), 0.0)
        kpos = kb * TK + ik
        valid = ((kpos >> CHUNK_SHIFT) <= (qpos >> CHUNK_SHIFT)) & (kpos < n_keys)
        bits = pltpu.bitcast(acc, I32)
        key = jnp.where(bits < 0, bits ^ INT_MAX, bits)
        keys_ref[blk(kb), :] = jnp.where(valid, key, INT_MIN)
        return 0

    lax.fori_loop(0, n_tot, score_blk, 0)

    def count_ge(mid):
        def body(kb, c):
            return c + jnp.sum(jnp.where(keys_ref[blk(kb), :] >= mid, 1.0, 0.0), axis=0, keepdims=True)
        return lax.fori_loop(0, n_tot, body, jnp.zeros((1, tq), F32))

    def avg(lo, hi):
        return (lo >> 1) + (hi >> 1) + (lo & hi & 1)

    qrow = q_first + lax.broadcasted_iota(I32, (1, tq), 1)
    n_vis = _visible_end(qrow, "chunk", n_keys, 0).astype(F32)
    lo0 = jnp.full((1, tq), INT_MIN + 1, I32)
    hi0 = jnp.full((1, tq), INT_MAX, I32)
    pending0 = jnp.sum(jnp.where(n_vis > tf, 1.0, 0.0))

    def cond(st):
        return (st[0] < 40) & (st[5] > 0.5)

    def body(st):
        it, lo, hi, cl, ch, _ = st
        mid = avg(lo, hi)
        cnt = count_ge(mid)
        ge = cnt >= tf
        lo = jnp.where(ge, mid, lo)
        cl = jnp.where(ge, cnt, cl)
        hi = jnp.where(ge, hi, mid)
        ch = jnp.where(ge, ch, cnt)
        done = (cl <= tf) | (avg(lo, hi) == lo)
        return it + 1, lo, hi, cl, ch, jnp.sum(jnp.where(done, 0.0, 1.0))

    _, lo, _, cl, ch, _ = lax.while_loop(
        cond, body, (jnp.int32(0), lo0, hi0, n_vis, jnp.zeros((1, tq), F32), pending0))

    need = tf - ch

    @pl.when(jnp.sum(jnp.where(cl > tf, 1.0, 0.0)) > 0.5)
    def _():
        r = lax.broadcasted_iota(I32, (TK, TK), 0)
        c = lax.broadcasted_iota(I32, (TK, TK), 1)
        tri = jnp.where(c < r, 1.0, 0.0).astype(BF16)

        def fix(kb, carry):
            kblk = keys_ref[blk(kb), :]
            eq = kblk == lo
            e = jnp.where(eq, 1.0, 0.0)
            before = _dotf(tri, e.astype(BF16)) + carry
            keys_ref[blk(kb), :] = jnp.where(eq & (before >= need), INT_MIN, kblk)
            return carry + jnp.sum(e, axis=0, keepdims=True)

        lax.fori_loop(0, n_tot, fix, jnp.zeros((1, tq), F32))

    def to_bias(kb, _):
        keys_ref[blk(kb), :] = pltpu.bitcast(jnp.where(keys_ref[blk(kb), :] >= lo, 0.0, NEG), I32)
        return 0

    lax.fori_loop(0, n_tot, to_bias, 0)

    group = N_HEADS // KV_HEADS
    wide = group * tq

    def lanes(x):
        return jnp.concatenate([x] * group, axis=1)

    def group_body(n, _):
        h0 = n * group
        q = jnp.concatenate([qp_ref[h0 + g] for g in range(group)], axis=1)
        slope2 = jnp.concatenate(
            [jnp.broadcast_to(hp_ref[h0 + g][:, 0:1], (1, tq)) for g in range(group)], axis=1)
        a_tab = slope2 * lanes(ik.astype(F32))
        vrow = pl.ds(pl.multiple_of(n * HEAD_DIM, HEAD_DIM), HEAD_DIM)

        def step(kb, carry, diag):
            m, l, acc = carry
            bias = pltpu.bitcast(keys_ref[blk(kb), :], F32)
            s = _dotf(kt_ref[0, blk(kb), :], q) + lanes(bias)
            if diag:
                kpos = kb * TK + ik
                rel = iq.astype(F32) - jnp.abs(qpos - kpos).astype(F32)
                s = s + slope2 * lanes(rel)
                c = None
            else:
                s = s + a_tab
                c = slope2 * (kb * TK - q_first).astype(F32)
            return _softmax_step(s, m, l, acc, vt_ref[0, kb, vrow, :], c)

        init = (jnp.full((1, wide), NEG, F32), jnp.zeros((1, wide), F32), jnp.zeros((HEAD_DIM, wide), F32))
        carry = lax.fori_loop(0, n_past, lambda kb, cr: step(kb, cr, False), init)
        m, l, acc = lax.fori_loop(n_past, n_tot, lambda kb, cr: step(kb, cr, True), carry)
        out = acc / l
        for g in range(group):
            o_ref[h0 + g] = out[:, g * tq:(g + 1) * tq]
        return 0

    lax.fori_loop(0, KV_HEADS, group_body, 0)


def _dsa_attend(hp, qi, wi, qp, ki, kt, vt, *, tq, q0, n_real_q, n_keys):
    b = ki.shape[0]
    sq = qp.shape[2]
    lp = kt.shape[1]
    nkb = vt.shape[1]
    topk = min(TOPK_MAX, n_keys // 4)
    body = functools.partial(_dsa_body, tq=tq, q0=q0, n_real_q=n_real_q, n_keys=n_keys, topk=topk)
    once = pl.Buffered(1)
    return pl.pallas_call(
        body,
        grid=(b, sq // tq),
        in_specs=[_full_spec(hp),
                  pl.BlockSpec((1, IDX_HEADS, IDX_DIM, tq), lambda bb, i: (bb, 0, 0, i)),
                  pl.BlockSpec((1, IDX_HEADS, tq), lambda bb, i: (bb, 0, i)),
                  pl.BlockSpec((N_HEADS, KT_LANES, tq), lambda bb, i: (bb, 0, i)),
                  pl.BlockSpec((1, lp, IDX_DIM), lambda bb, i: (bb, 0, 0), pipeline_mode=once),
                  pl.BlockSpec((1, lp, KT_LANES), lambda bb, i: (bb, 0, 0), pipeline_mode=once),
                  pl.BlockSpec((1, nkb, KV_HEADS * HEAD_DIM, TK), lambda bb, i: (bb, 0, 0, 0),
                               pipeline_mode=once)],
        out_specs=pl.BlockSpec((N_HEADS, HEAD_DIM, tq), lambda bb, i: (bb, 0, i)),
        out_shape=jax.ShapeDtypeStruct((b * N_HEADS, HEAD_DIM, sq), F32),
        scratch_shapes=[pltpu.VMEM((lp, tq), I32)],
        compiler_params=_cparams(("arbitrary", "arbitrary")),
        name="dsa_attend",
    )(hp, qi, wi, qp, ki, kt, vt)


def _block_diag(c):
    g = np.arange(c) // HEAD_DIM
    return jnp.asarray(g[:, None] == g[None, :], BF16)


def _tile_gain(g, n):
    return jnp.tile(g.astype(F32), n).reshape(1, -1)


def _col(g):
    return g.astype(F32).reshape(-1, 1)


def _pad_cols(w, n):
    return jnp.pad(w, ((0, 0), (0, n - w.shape[1])))


def _rope_tables(pos):
    half = MLA_ROPE // 2
    inv = ROPE_BASE ** (-jnp.arange(half, dtype=F32) / half)
    ang = pos.astype(F32)[:, None] * inv[None, :]
    return jnp.cos(ang), jnp.sin(ang)


def _alibi_slopes():
    return np.asarray(2.0 ** (-8.0 * np.arange(1, N_HEADS + 1) / N_HEADS), dtype=np.float32)


def _head_params(sinks=None):
    hp = jnp.zeros((N_HEADS, 1, 128), F32)
    hp = hp.at[:, 0, 0].set(jnp.asarray(_alibi_slopes()) * LOG2E)
    if sinks is not None:
        hp = hp.at[:, 0, 1].set(sinks.astype(F32) * LOG2E)
    return hp


class _Stream:
    def __init__(self, batch, seq, past):
        self.b, self.s, self.p = batch, seq, past
        self.r = batch * seq
        self.decode = past > 0
        self.tq = TQ_DEC if self.decode else TQ
        self.tqf = TQ_DEC if self.decode else min(TQ_FLASH, seq)
        self.sq = self.tq if self.decode else seq
        self.n_keys = past + seq
        self.lp = -(-self.n_keys // TK) * TK
        self.pos = past + np.tile(np.arange(seq), batch)

    def pad_keys(self, past_arr, new_arr, n_keys=None, lp=None):
        n_keys = self.n_keys if n_keys is None else n_keys
        lp = self.lp if lp is None else lp
        new_arr = new_arr.reshape(self.b, self.s, -1)
        parts = [new_arr] if past_arr is None else [past_arr.astype(F32), new_arr]
        if lp > n_keys:
            parts.append(jnp.zeros((self.b, lp - n_keys, new_arr.shape[-1]), F32))
        return jnp.concatenate(parts, axis=1) if len(parts) > 1 else new_arr

    def qp_blocks(self, qp):
        if not self.decode:
            return qp
        x = qp.reshape(N_HEADS, KT_LANES, self.b, self.s).transpose(2, 0, 1, 3)
        x = jnp.pad(x, ((0, 0), (0, 0), (0, 0), (0, self.tq - self.s)))
        return x.reshape(self.b * N_HEADS, KT_LANES, self.tq)

    def lanes(self, x):
        n, c, _ = x.shape
        if not self.decode:
            return x[None]
        x = x.reshape(n, c, self.b, self.s).transpose(2, 0, 1, 3)
        return jnp.pad(x, ((0, 0), (0, 0), (0, 0), (0, self.tq - self.s)))

    def ot_cols(self, ot):
        if not self.decode:
            return ot.reshape(N_HEADS * HEAD_DIM, self.r)
        x = ot.reshape(self.b, N_HEADS, HEAD_DIM, self.tq)[..., :self.s]
        return x.transpose(1, 2, 0, 3).reshape(N_HEADS * HEAD_DIM, self.r)


def _mod_rows(st, mod_l, row0):
    m = mod_l[row0:row0 + st.b]
    if st.b > 1:
        m = jnp.repeat(m, st.s, axis=0)
    return m[:, :D_MODEL], m[:, D_MODEL:2 * D_MODEL], m[:, 2 * D_MODEL:]


def _mla_layer(st, x, mod, g, w, past):
    shift, scale, gate = mod
    r = st.r
    ts = _row_tile(r)
    cos, sin = _rope_tables(jnp.asarray(st.pos))
    zpad = jnp.zeros((r, 128 - MLA_ROPE), F32)
    cos_p = jnp.concatenate([cos, cos, zpad], axis=1)
    sin_p = jnp.concatenate([-sin, sin, zpad], axis=1)
    cqn, lat, kr, zt = _proj_call(
        _mla_in_body, "mla_in", r, ts,
        [x, scale, shift, cos_p, sin_p, g, w["gqa"], w["gkva"], w["gkr"], w["gkrp"]], [],
        [w["wcq"], w["wckv"], w["wkr"], w["wkrp"], w["wzt"]],
        [_rows_out(r, MLA_Q_LORA, ts, BF16), _rows_out(r, MLA_KV_LORA, ts), _rows_out(r, 128, ts),
         _cols_out(D_MODEL, r, ts)])
    (qp,) = _proj_call(_mla_q_body, "mla_q", r, ts, [cqn], [cos.T, sin.T], [w["wqt"], w["gqn"], w["gqr"]],
                       [_qp_out(r, ts)])
    past_lat, past_kr = (None, None) if past is None else past
    if past_kr is not None:
        past_kr = jnp.pad(past_kr.astype(F32), ((0, 0), (0, 0), (0, 128 - MLA_ROPE)))
    lat_all = st.pad_keys(past_lat, lat).reshape(st.b * st.lp, MLA_KV_LORA)
    kr_all = st.pad_keys(past_kr, kr).reshape(st.b * st.lp, 128)
    rk = st.b * st.lp
    npair = N_HEADS // 2
    kt, vt = _proj_call(
        _mla_kv_body, "mla_kv", rk, TK, [lat_all, kr_all, w["gkn"]], [], [w["wkn"], w["wvt"], w["bd"]],
        [((npair, rk, KT_LANES), BF16, (npair, TK, KT_LANES), lambda i: (0, i, 0)),
         ((rk // TK, N_HEADS * HEAD_DIM, TK), BF16, (1, N_HEADS * HEAD_DIM, TK), lambda i: (i, 0, 0))])
    kt = kt.reshape(npair * st.b, st.lp, KT_LANES)
    vt = vt.reshape(st.b, st.lp // TK, N_HEADS * HEAD_DIM, TK)
    b = st.b
    ot = _flash(_head_params(), st.qp_blocks(qp), kt, vt, tq=st.tqf, mode="chunk", q0=st.p,
                n_real_q=st.tqf if not st.decode else st.s, n_keys=st.n_keys,
                tile_fn=lambda gi: ((gi % N_HEADS) // 2) * b + gi // N_HEADS,
                vrow_fn=lambda gi: gi % N_HEADS)
    x = _out_proj(x, gate, st.ot_cols(ot), zt, w["wo"])
    return x, (lat, kr[:, :MLA_ROPE])


def _gqa_common(st, k, v, past, extra=None):
    pk, pv = (None, None) if past is None else (past[0], past[1])
    c = k.shape[-1]
    if pk is not None:
        pk = pk.reshape(st.b, -1, c)
        pv = pv.reshape(st.b, -1, c)
    return pk, pv


def _dsa_layer(st, x, mod, g, w, past):
    shift, scale, gate = mod
    r = st.r
    ts = _row_tile(r)
    c = KV_HEADS * HEAD_DIM
    qp, k, v, qi, ki, wi, zt = _proj_call(
        _dsa_in_body, "dsa_in", r, ts, [x, scale, shift, g, w["gk"]], [],
        [w["gq"], w["wqt"], w["wk"], w["wv"], w["wqit"], w["wki"], w["wwit"], w["wzt"], w["bd"]],
        [_qp_out(r, ts), _rows_out(r, c, ts), _rows_out(r, c, ts),
         ((IDX_HEADS, IDX_DIM, r), BF16, (IDX_HEADS, IDX_DIM, ts), lambda i: (0, 0, i)),
         _rows_out(r, IDX_DIM, ts), _cols_out(IDX_HEADS, r, ts), _cols_out(D_MODEL, r, ts)])
    if past is None:
        pk = pv = pki = None
    else:
        pk, pv, pki = past[0].reshape(st.b, -1, c), past[1].reshape(st.b, -1, c), past[2]
    kt, vt, kib = _gqa_prep(st.pad_keys(pk, k), st.pad_keys(pv, v), st.pad_keys(pki, ki))
    qi_b = st.lanes(qi)
    wi_b = st.lanes(wi[None])[:, 0]
    ot = _dsa_attend(_head_params(), qi_b, wi_b, st.qp_blocks(qp), kib, kt, vt, tq=st.tq, q0=st.p,
                     n_real_q=st.tq if not st.decode else st.s, n_keys=st.n_keys)
    x = _out_proj(x, gate, st.ot_cols(ot), zt, w["wo"])
    return x, (k, v, ki)


def _swa_layer(st, x, mod, g, w, past):
    shift, scale, gate = mod
    r = st.r
    ts = _row_tile(r)
    c = KV_HEADS * HEAD_DIM
    qp, k, v, zt = _proj_call(
        _swa_in_body, "swa_in", r, ts, [x, scale, shift, g, w["gk"]], [],
        [w["gq"], w["wqt"], w["wk"], w["wv"], w["wzt"], w["bd"]],
        [_qp_out(r, ts), _rows_out(r, c, ts), _rows_out(r, c, ts), _cols_out(D_MODEL, r, ts)])
    group = N_HEADS // KV_HEADS
    if past is None:
        n_keys, lp, key0 = st.s, st.lp, 0
        k_all, v_all = st.pad_keys(None, k), st.pad_keys(None, v)
        new = (k.reshape(st.b, st.s, c)[:, st.s - WINDOW:], v.reshape(st.b, st.s, c)[:, st.s - WINDOW:])
    else:
        win = past[0].shape[1]
        n_keys = win + st.s
        lp = -(-n_keys // TK) * TK
        key0 = st.p - win
        k_all = st.pad_keys(past[0].reshape(st.b, win, c), k, n_keys, lp)
        v_all = st.pad_keys(past[1].reshape(st.b, win, c), v, n_keys, lp)
        new = (k_all[:, st.s:n_keys], v_all[:, st.s:n_keys])
    kt, vt = _gqa_prep(k_all, v_all)
    ot = _flash(_head_params(w["sinks"]), st.qp_blocks(qp), kt, vt, tq=st.tqf, mode="window", q0=st.p,
                n_real_q=st.tqf if not st.decode else st.s, n_keys=n_keys, key0=key0, alibi=True, sinks=True,
                tile_fn=lambda gi: gi // N_HEADS, vrow_fn=lambda gi: (gi % N_HEADS) // group)
    x = _out_proj(x, gate, st.ot_cols(ot), zt, w["wo"])
    return x, new


def _fox_layer(st, x, mod, g, w, past):
    shift, scale, gate = mod
    r = st.r
    ts = _row_tile(r)
    c = N_HEADS * HEAD_DIM
    qp, k, v, lf, zt = _proj_call(
        _fox_in_body, "fox_in", r, ts, [x, scale, shift, g, w["gk"], w["bf"]], [],
        [w["gq"], w["wqt"], w["wk"], w["wv"], w["wf"], w["wzt"], w["bd"]],
        [_qp_out(r, ts), _rows_out(r, c, ts), _rows_out(r, c, ts), _rows_out(r, 128, ts),
         _cols_out(D_MODEL, r, ts)])
    if past is None:
        pk = pv = plf = None
    else:
        pk, pv = past[0].reshape(st.b, -1, c), past[1].reshape(st.b, -1, c)
        plf = jnp.pad(past[2].astype(F32), ((0, 0), (0, 0), (0, 128 - N_HEADS)))
    kt, vt = _fox_prep(st.pad_keys(pk, k), st.pad_keys(pv, v), st.pad_keys(plf, lf))
    ot = _flash(_head_params(), st.qp_blocks(qp), kt, vt, tq=st.tqf, mode="causal", q0=st.p,
                n_real_q=st.tqf if not st.decode else st.s, n_keys=st.n_keys,
                tile_fn=lambda gi: gi // 2, vrow_fn=lambda gi: gi % N_HEADS)
    x = _out_proj(x, gate, st.ot_cols(ot), zt, w["wo"])
    return x, (k, v, lf[:, :N_HEADS])


def _prep_weights(mla_w_in, mla_g_qa, mla_w_qb, mla_g_kva, mla_w_kvb, mla_g_qn, mla_g_qr, mla_g_kn,
                  mla_g_kr, mla_w_out, dsa_w_in, dsa_g_q, dsa_g_k, dsa_w_out, swa_w_in, swa_g_q,
                  swa_g_k, swa_sinks, swa_w_out, fox_w_in, fox_b_f, fox_g_q, fox_g_k, fox_w_out):
    bf = lambda a: a.astype(BF16)
    row = lambda a: a.astype(F32).reshape(1, -1)
    half = MLA_ROPE // 2
    c1, c2, c3 = MLA_Q_LORA, MLA_Q_LORA + MLA_KV_LORA, MLA_Q_LORA + MLA_KV_LORA + MLA_ROPE
    wkr = mla_w_in[:, c2:c3]
    wkrp = jnp.concatenate([wkr[:, half:], wkr[:, :half]], axis=1)
    gkr = mla_g_kr.astype(F32)
    gkrp = jnp.concatenate([gkr[half:], gkr[:half]])
    kvb = mla_w_kvb.reshape(MLA_KV_LORA, N_HEADS, MLA_NOPE + HEAD_DIM)
    mla = dict(
        wcq=bf(mla_w_in[:, :c1]), wckv=bf(mla_w_in[:, c1:c2]), wkr=bf(_pad_cols(wkr, 128)),
        wkrp=bf(_pad_cols(wkrp, 128)), wzt=bf(mla_w_in[:, c3:].T),
        gqa=row(mla_g_qa), gkva=row(mla_g_kva), gkr=row(jnp.pad(gkr, (0, 128 - MLA_ROPE))),
        gkrp=row(jnp.pad(gkrp, (0, 128 - MLA_ROPE))),
        wqt=bf(mla_w_qb.T), gqn=_col(mla_g_qn), gqr=_col(mla_g_qr),
        wkn=bf(kvb[:, :, :MLA_NOPE].reshape(MLA_KV_LORA, -1)),
        wvt=bf(kvb[:, :, MLA_NOPE:].reshape(MLA_KV_LORA, -1).T),
        gkn=_tile_gain(mla_g_kn, N_HEADS), bd=_block_diag(N_HEADS * HEAD_DIM), wo=bf(mla_w_out))
    hq, hk = N_HEADS * HEAD_DIM, KV_HEADS * HEAD_DIM
    cuts = np.cumsum([hq, hk, hk, IDX_HEADS * IDX_DIM, IDX_DIM, IDX_HEADS]).tolist()
    dsa = dict(
        wqt=bf(dsa_w_in[:, :cuts[0]].T), wk=bf(dsa_w_in[:, cuts[0]:cuts[1]]),
        wv=bf(dsa_w_in[:, cuts[1]:cuts[2]]), wqit=bf(dsa_w_in[:, cuts[2]:cuts[3]].T),
        wki=bf(dsa_w_in[:, cuts[3]:cuts[4]]), wwit=bf(dsa_w_in[:, cuts[4]:cuts[5]].T),
        wzt=bf(dsa_w_in[:, cuts[5]:].T), gq=_col(dsa_g_q), gk=_tile_gain(dsa_g_k, KV_HEADS),
        bd=_block_diag(hk), wo=bf(dsa_w_out))
    swa = dict(
        wqt=bf(swa_w_in[:, :hq].T), wk=bf(swa_w_in[:, hq:hq + hk]), wv=bf(swa_w_in[:, hq + hk:hq + 2 * hk]),
        wzt=bf(swa_w_in[:, hq + 2 * hk:].T), gq=_col(swa_g_q), gk=_tile_gain(swa_g_k, KV_HEADS),
        bd=_block_diag(hk), wo=bf(swa_w_out), sinks=swa_sinks)
    fox = dict(
        wqt=bf(fox_w_in[:, :hq].T), wk=bf(fox_w_in[:, hq:2 * hq]), wv=bf(fox_w_in[:, 2 * hq:3 * hq]),
        wf=bf(_pad_cols(fox_w_in[:, 3 * hq:3 * hq + N_HEADS], 128)), wzt=bf(fox_w_in[:, 3 * hq + N_HEADS:].T),
        bf=row(jnp.pad(fox_b_f.astype(F32), (0, 128 - N_HEADS))), gq=_col(fox_g_q),
        gk=_tile_gain(fox_g_k, N_HEADS), bd=_block_diag(hq), wo=bf(fox_w_out))
    return [mla, dsa, swa, fox]


def kernel(x_prompt, x_sample, cache_mla_latent, cache_mla_krope, cache_dsa_k, cache_dsa_v, cache_dsa_kidx, state_swa_k, state_swa_v, cache_fox_k, cache_fox_v, cache_fox_logf, c_prompt, c_sample, norm_g, ada_w, ada_b, mla_w_in, mla_g_qa, mla_w_qb, mla_g_kva, mla_w_kvb, mla_g_qn, mla_g_qr, mla_g_kn, mla_g_kr, mla_w_out, dsa_w_in, dsa_g_q, dsa_g_k, dsa_w_out, swa_w_in, swa_g_q, swa_g_k, swa_sinks, swa_w_out, fox_w_in, fox_b_f, fox_g_q, fox_g_k, fox_w_out):
    bp, sp, _ = x_prompt.shape
    bs, ss, _ = x_sample.shape
    past_len = cache_mla_latent.shape[1]
    depth = norm_g.shape[0]
    assert bp == 1 and sp % TQ == 0 and sp % min(TQ_FLASH, sp) == 0 and sp % TS == 0 and (bs * ss) % 8 == 0 and ss <= TQ_DEC
    assert past_len % TK == 0 and past_len >= WINDOW

    weights = _prep_weights(mla_w_in, mla_g_qa, mla_w_qb, mla_g_kva, mla_w_kvb, mla_g_qn, mla_g_qr,
                            mla_g_kn, mla_g_kr, mla_w_out, dsa_w_in, dsa_g_q, dsa_g_k, dsa_w_out,
                            swa_w_in, swa_g_q, swa_g_k, swa_sinks, swa_w_out, fox_w_in, fox_b_f,
                            fox_g_q, fox_g_k, fox_w_out)
    rows = bp + bs
    rows_p = -(-rows // 8) * 8
    c_all = jnp.concatenate([c_prompt, c_sample, jnp.zeros((rows_p - rows, D_MODEL), F32)], axis=0)
    mod = _ada_mod(c_all, ada_w, ada_b)

    st_p = _Stream(bp, sp, 0)
    st_s = _Stream(bs, ss, past_len)
    pasts = ((cache_mla_latent, cache_mla_krope), (cache_dsa_k, cache_dsa_v, cache_dsa_kidx),
             (state_swa_k, state_swa_v), (cache_fox_k, cache_fox_v, cache_fox_logf))
    layers = (_mla_layer, _dsa_layer, _swa_layer, _fox_layer)
    xp = x_prompt.reshape(st_p.r, D_MODEL)
    xs = x_sample.reshape(st_s.r, D_MODEL)
    new_p, new_s = [], []
    for layer in range(depth):
        kind = layer % len(layers)
        g = norm_g[layer].astype(F32).reshape(1, -1)
        xp, n = layers[kind](st_p, xp, _mod_rows(st_p, mod[layer], 0), g, weights[kind], None)
        new_p.append(n)
        xs, n = layers[kind](st_s, xs, _mod_rows(st_s, mod[layer], bp), g, weights[kind], pasts[kind])
        new_s.append(n)

    def shaped(st, new):
        (lat, kr), (dk, dv, dki), (sk, sv), (fk, fv, flf) = new
        b, s = st.b, st.s
        return (lat.reshape(b, s, -1), kr.reshape(b, s, -1),
                dk.reshape(b, s, KV_HEADS, HEAD_DIM), dv.reshape(b, s, KV_HEADS, HEAD_DIM),
                dki.reshape(b, s, -1),
                sk.reshape(b, -1, KV_HEADS, HEAD_DIM), sv.reshape(b, -1, KV_HEADS, HEAD_DIM),
                fk.reshape(b, s, N_HEADS, HEAD_DIM), fv.reshape(b, s, N_HEADS, HEAD_DIM),
                flf.reshape(b, s, -1))

    return (xp.reshape(x_prompt.shape), xs.reshape(x_sample.shape)) + shaped(st_p, new_p) + shaped(st_s, new_s)
```

```python
import functools

import numpy as np
import jax
import jax.numpy as jnp
from jax import lax
from jax.experimental import pallas as pl
from jax.experimental.pallas import tpu as pltpu

F32 = jnp.float32
BF16 = jnp.bfloat16
I32 = jnp.int32

D_MODEL = 1024
HEAD_DIM = 64
N_HEADS = 16
KV_HEADS = 4
CHUNK = 64
CHUNK_SHIFT = 6
WINDOW = 128
WIN_CHUNKS = WINDOW // CHUNK
EPS = 1e-6
ROPE_BASE = 10000.0
MLA_NOPE, MLA_ROPE, MLA_Q_LORA, MLA_KV_LORA = 64, 32, 384, 256
IDX_HEADS, IDX_DIM, TOPK_MAX = 8, 64, 256
LOG2E = 1.4426950408889634
NEG = -1e30
INT_MIN = -(2 ** 31)
INT_MAX = 2 ** 31 - 1
I16 = jnp.int16
I16_MIN, I16_MAX = -(2 ** 15), 2 ** 15 - 1

TS = 256
TQ = 256
TQ_FLASH = 2048
TK = 512
TQ_DEC = 128
BD_LANES = 256
SUM_ROWS = 16
KT_LANES = 256
VMEM_LIMIT = 56 * 1024 * 1024


def _row_tile(r):
    return TS if r % TS == 0 else r


def _cparams(sem, vmem=VMEM_LIMIT):
    return pltpu.CompilerParams(dimension_semantics=sem, vmem_limit_bytes=vmem)


def _dotf(a, b):
    return jnp.dot(a, b, preferred_element_type=F32)


def _dot_nt(a, b):
    return lax.dot_general(a, b, (((1,), (1,)), ((), ())), preferred_element_type=F32)


def _dot_tn(a, b):
    return lax.dot_general(a, b, (((0,), (0,)), ((), ())), preferred_element_type=F32)


def _split3(x):
    hi = x.astype(BF16)
    r = x - hi.astype(F32)
    mid = r.astype(BF16)
    lo = (r - mid.astype(F32)).astype(BF16)
    return hi, mid, lo


def _silu(x):
    return x / (1.0 + jnp.exp(-x))


def _full_spec(arr):
    nd = arr.ndim
    return pl.BlockSpec(arr.shape, lambda *_: (0,) * nd)


def _row_spec(arr, ts):
    if arr.shape[0] == 1:
        return pl.BlockSpec((1, arr.shape[1]), lambda i: (0, 0))
    return pl.BlockSpec((ts, arr.shape[1]), lambda i: (i, 0))


def _col_spec(arr, ts):
    return pl.BlockSpec((arr.shape[0], ts), lambda i: (0, i))


def _ada_body(c_ref, w_ref, b_ref, o_ref):
    a = _silu(c_ref[...])
    w = w_ref[0]
    a_hi = a.astype(BF16)
    a_lo = (a - a_hi.astype(F32)).astype(BF16)
    w_hi = w.astype(BF16)
    w_lo = (w - w_hi.astype(F32)).astype(BF16)
    o_ref[0] = _dotf(a_hi, w_hi) + _dotf(a_hi, w_lo) + _dotf(a_lo, w_hi) + b_ref[0]


def _ada_mod(c_all, ada_w, ada_b):
    depth, d, n3 = ada_w.shape
    bp = c_all.shape[0]
    tn = 768
    return pl.pallas_call(
        _ada_body,
        grid=(depth, n3 // tn),
        in_specs=[
            pl.BlockSpec((bp, d), lambda l, j: (0, 0)),
            pl.BlockSpec((1, d, tn), lambda l, j: (l, 0, j)),
            pl.BlockSpec((1, 1, tn), lambda l, j: (l, 0, j)),
        ],
        out_specs=pl.BlockSpec((1, bp, tn), lambda l, j: (l, 0, j)),
        out_shape=jax.ShapeDtypeStruct((depth, bp, n3), F32),
        compiler_params=_cparams(("arbitrary", "arbitrary")),
        name="ada_mod",
    )(c_all, ada_w, ada_b.reshape(depth, 1, n3))


def _prenorm(x_ref, g_ref, sc_ref, sh_ref):
    x = x_ref[...]
    ms = jnp.mean(x * x, axis=-1, keepdims=True)
    xn = x * lax.rsqrt(ms + EPS) * g_ref[...]
    return (xn * (1.0 + sc_ref[...]) + sh_ref[...]).astype(BF16)


def _group_sumsq(y, bd_ref):
    sq = y * y
    hi = sq.astype(BF16)
    lo = (sq - hi.astype(F32)).astype(BF16)
    bd = bd_ref[...]
    chunks = []
    for c in range(y.shape[1] // BD_LANES):
        sl = slice(c * BD_LANES, (c + 1) * BD_LANES)
        chunks.append(_dotf(hi[:, sl], bd) + _dotf(lo[:, sl], bd))
    return jnp.concatenate(chunks, axis=1) if len(chunks) > 1 else chunks[0]


def _head_rms_rows(q, g_col):
    ms = jnp.mean(q * q, axis=0, keepdims=True)
    return q * lax.rsqrt(ms + EPS) * g_col


def _place_rows(piece, row0, total):
    ts = piece.shape[1]
    parts = []
    if row0 > 0:
        parts.append(jnp.zeros((row0, ts), F32))
    parts.append(piece)
    rest = total - row0 - piece.shape[0]
    if rest > 0:
        parts.append(jnp.zeros((rest, ts), F32))
    return jnp.concatenate(parts, axis=0) if len(parts) > 1 else piece


def _proj_call(body, name, r, ts, row_in, col_in, const_in, outs):
    in_specs = ([_row_spec(a, ts) for a in row_in] + [_col_spec(a, ts) for a in col_in]
                + [_full_spec(a) for a in const_in])
    return pl.pallas_call(
        body,
        grid=(r // ts,),
        in_specs=in_specs,
        out_specs=[pl.BlockSpec(blk, im) for (_, _, blk, im) in outs],
        out_shape=[jax.ShapeDtypeStruct(s, dt) for (s, dt, _, _) in outs],
        compiler_params=_cparams(("arbitrary",)),
        name=name,
    )(*row_in, *col_in, *const_in)


def _rows_out(r, c, ts, dtype=F32):
    return ((r, c), dtype, (ts, c), lambda i: (i, 0))


def _cols_out(c, r, ts, dtype=F32):
    return ((c, r), dtype, (c, ts), lambda i: (0, i))


def _qp_out(r, ts):
    return ((N_HEADS, KT_LANES, r), BF16, (N_HEADS, KT_LANES, ts), lambda i: (0, 0, i))


def _mla_in_body(x_ref, sc_ref, sh_ref, cos_ref, sin_ref, g_ref, gqa, gkva, gkr, gkrp,
                 wcq, wckv, wkr, wkrp, wzt, cqn_o, lat_o, kr_o, zt_o):
    h = _prenorm(x_ref, g_ref, sc_ref, sh_ref)
    cq = _dotf(h, wcq[...])
    cqn_o[...] = (cq * lax.rsqrt(jnp.mean(cq * cq, axis=-1, keepdims=True) + EPS) * gqa[...]).astype(BF16)
    ckv = _dotf(h, wckv[...])
    lat_o[...] = ckv * lax.rsqrt(jnp.mean(ckv * ckv, axis=-1, keepdims=True) + EPS) * gkva[...]
    kr = _dotf(h, wkr[...])
    krp = _dotf(h, wkrp[...])
    inv = lax.rsqrt(jnp.sum(kr * kr, axis=-1, keepdims=True) * (1.0 / MLA_ROPE) + EPS)
    kr_o[...] = (kr * gkr[...] * cos_ref[...] + krp * gkrp[...] * sin_ref[...]) * inv
    zt_o[...] = _dot_nt(wzt[...], h)


def _mla_q_body(cqn_ref, cos_ref, sin_ref, wqt, gqn, gqr, qp_o):
    qt = _dot_nt(wqt[...], cqn_ref[...])
    ts = qt.shape[1]
    sc = (MLA_NOPE + MLA_ROPE) ** -0.5 * LOG2E
    c = cos_ref[...]
    s = sin_ref[...]
    half = MLA_ROPE // 2
    width = MLA_NOPE + MLA_ROPE
    for h in range(N_HEADS):
        qn = _head_rms_rows(qt[width * h:width * h + MLA_NOPE], gqn[...]) * sc
        qr = _head_rms_rows(qt[width * h + MLA_NOPE:width * (h + 1)], gqr[...]) * sc
        x1, x2 = qr[:half], qr[half:]
        o1 = x1 * c - x2 * s
        o2 = x2 * c + x1 * s
        a = h % 2
        pieces = [qn, jnp.zeros((HEAD_DIM, ts), F32)]
        if a:
            pieces = pieces[::-1]
        pieces += [o1, o2, jnp.zeros((KT_LANES - 2 * HEAD_DIM - MLA_ROPE, ts), F32)]
        qp_o[h] = jnp.concatenate(pieces, axis=0).astype(BF16)


def _mla_kv_body(lat_ref, kr_ref, gkn, wkn, wvt, bd, kt_o, vt_o):
    lat = lat_ref[...].astype(BF16)
    kn = _dotf(lat, wkn[...])
    ss = _group_sumsq(kn, bd)
    kn = kn * lax.rsqrt(ss * (1.0 / MLA_NOPE) + EPS) * gkn[...]
    kr = kr_ref[...].astype(BF16)
    for p in range(N_HEADS // 2):
        kt_o[p, :, 0:128] = kn[:, 128 * p:128 * (p + 1)].astype(BF16)
        kt_o[p, :, 128:256] = kr
    vt_o[0] = _dot_nt(wvt[...], lat).astype(BF16)


def _q_heads_to_qp(qt, gq, qp_o, row_of_head, extra_of_head=None):
    sc = HEAD_DIM ** -0.5 * LOG2E
    ts = qt.shape[1]
    for h in range(N_HEADS):
        qh = _head_rms_rows(qt[HEAD_DIM * h:HEAD_DIM * (h + 1)], gq[...]) * sc
        blk = _place_rows(qh, row_of_head(h), KT_LANES)
        if extra_of_head is not None:
            lo, hi = extra_of_head(h)
            rows = lax.broadcasted_iota(I32, (KT_LANES, ts), 0)
            blk = blk + jnp.where((rows >= lo) & (rows < hi), 1.0, 0.0)
        qp_o[h] = blk.astype(BF16)


def _dsa_in_body(x_ref, sc_ref, sh_ref, g_ref, gk, gq, wqt, wk, wv, wqit, wki, wwit, wzt, bd,
                 qp_o, k_o, v_o, qi_o, ki_o, wi_o, zt_o):
    h = _prenorm(x_ref, g_ref, sc_ref, sh_ref)
    _q_heads_to_qp(_dot_nt(wqt[...], h), gq, qp_o, lambda hh: HEAD_DIM * (hh // (N_HEADS // KV_HEADS)))
    k = _dotf(h, wk[...])
    k_o[...] = k * lax.rsqrt(_group_sumsq(k, bd) * (1.0 / HEAD_DIM) + EPS) * gk[...]
    v_o[...] = _dotf(h, wv[...])
    qit = _dot_nt(wqit[...], h)
    for ih in range(IDX_HEADS):
        qi_o[ih] = qit[IDX_DIM * ih:IDX_DIM * (ih + 1)].astype(BF16)
    ki_o[...] = _dotf(h, wki[...])
    wi_o[...] = _dot_nt(wwit[...], h)
    zt_o[...] = _dot_nt(wzt[...], h)


def _swa_in_body(x_ref, sc_ref, sh_ref, g_ref, gk, gq, wqt, wk, wv, wzt, bd, qp_o, k_o, v_o, zt_o):
    h = _prenorm(x_ref, g_ref, sc_ref, sh_ref)
    _q_heads_to_qp(_dot_nt(wqt[...], h), gq, qp_o, lambda hh: HEAD_DIM * (hh // (N_HEADS // KV_HEADS)))
    k = _dotf(h, wk[...])
    k_o[...] = k * lax.rsqrt(_group_sumsq(k, bd) * (1.0 / HEAD_DIM) + EPS) * gk[...]
    v_o[...] = _dotf(h, wv[...])
    zt_o[...] = _dot_nt(wzt[...], h)


def _fox_in_body(x_ref, sc_ref, sh_ref, g_ref, gk, bf, gq, wqt, wk, wv, wf, wzt, bd,
                 qp_o, k_o, v_o, lf_o, zt_o):
    h = _prenorm(x_ref, g_ref, sc_ref, sh_ref)
    _q_heads_to_qp(_dot_nt(wqt[...], h), gq, qp_o, lambda hh: HEAD_DIM * (hh % 2),
                   lambda hh: (128 + 3 * (hh % 2), 128 + 3 * (hh % 2) + 3))
    k = _dotf(h, wk[...])
    k_o[...] = k * lax.rsqrt(_group_sumsq(k, bd) * (1.0 / HEAD_DIM) + EPS) * gk[...]
    v_o[...] = _dotf(h, wv[...])
    f = _dotf(h, wf[...]) + bf[...]
    lf_o[...] = jnp.minimum(f, 0.0) - jnp.log1p(jnp.exp(-jnp.abs(f)))
    zt_o[...] = _dot_nt(wzt[...], h)


def _out_body(x_ref, gate_ref, ot_ref, zt_ref, wo, o_ref):
    u = (ot_ref[...] * _silu(zt_ref[...])).astype(BF16)
    o_ref[...] = x_ref[...] + gate_ref[...] * _dot_tn(u, wo[...])


def _out_proj(x, gate, ot, zt, wo):
    r = x.shape[0]
    ts = _row_tile(r)
    (out,) = _proj_call(_out_body, "out_proj", r, ts, [x, gate], [ot, zt], [wo],
                        [_rows_out(r, D_MODEL, ts)])
    return out


def _src_arrays(src):
    past, new = src
    return [new] if past is None else [past, new]


def _src_specs(src):
    past, new = src
    c = new.shape[2]
    if past is None:
        return [pl.BlockSpec((1, TK, c), lambda bb, j: (bb, j, 0))]
    npb = past.shape[1] // TK
    return [pl.BlockSpec((1, TK, c), lambda bb, j: (bb, jnp.minimum(j, npb - 1), 0)),
            pl.BlockSpec((1, TK, c), lambda bb, j: (bb, jnp.maximum(j - npb, 0), 0))]


def _src_rows(src):
    past, new = src
    return new.shape[1] + (0 if past is None else past.shape[1])


def _src_load(refs, src_past_blocks):
    if src_past_blocks is None:
        return refs[0][0]
    return jnp.where(pl.program_id(1) < src_past_blocks, refs[0][0], refs[1][0])


def _split_refs(refs, past_blocks):
    vals, pos = [], 0
    for npb in past_blocks:
        cnt = 1 if npb is None else 2
        vals.append(_src_load(refs[pos:pos + cnt], npb))
        pos += cnt
    return vals, refs[pos:]


def _past_blocks(srcs):
    return tuple(None if s[0] is None else s[0].shape[1] // TK for s in srcs)


def _gqa_prep_body(*refs, past_blocks):
    vals, outs = _split_refs(refs, past_blocks)
    outs[0][0] = vals[0].astype(BF16)
    outs[1][0, 0] = vals[1].T.astype(BF16)
    if len(vals) > 2:
        outs[2][0] = vals[2].astype(BF16)


def _gqa_prep(k, v, ki=None):
    srcs = [k, v] + ([ki] if ki is not None else [])
    b, c = k[1].shape[0], k[1].shape[2]
    lp = _src_rows(k)
    nkb = lp // TK
    out_specs = [pl.BlockSpec((1, TK, c), lambda bb, j: (bb, j, 0)),
                 pl.BlockSpec((1, 1, c, TK), lambda bb, j: (bb, j, 0, 0))]
    out_shape = [jax.ShapeDtypeStruct((b, lp, c), BF16), jax.ShapeDtypeStruct((b, nkb, c, TK), BF16)]
    if ki is not None:
        ci = ki[1].shape[2]
        out_specs.append(pl.BlockSpec((1, TK, ci), lambda bb, j: (bb, j, 0)))
        out_shape.append(jax.ShapeDtypeStruct((b, lp, ci), BF16))
    return pl.pallas_call(
        functools.partial(_gqa_prep_body, past_blocks=_past_blocks(srcs)),
        grid=(b, nkb), in_specs=[sp for s in srcs for sp in _src_specs(s)],
        out_specs=out_specs, out_shape=out_shape,
        compiler_params=_cparams(("arbitrary", "arbitrary")), name="gqa_prep",
    )(*[a for s in srcs for a in _src_arrays(s)])


def _fox_prep_body(*refs, past_blocks):
    (k, v, lf), (sel_ref, kt_o, vt_o, carry_ref) = _split_refs(refs, past_blocks)
    kb = pl.program_id(1)

    @pl.when(kb == 0)
    def _():
        carry_ref[...] = jnp.zeros_like(carry_ref)

    r = lax.broadcasted_iota(I32, (TK, TK), 0)
    c = lax.broadcasted_iota(I32, (TK, TK), 1)
    tri = jnp.where(c <= r, 1.0, 0.0).astype(BF16)
    hi, mid, lo = _split3(lf)
    cum = _dotf(tri, hi) + _dotf(tri, mid) + _dotf(tri, lo) + carry_ref[...]
    carry_ref[...] = cum[TK - 1:TK, :]
    a, b, d = _split3(-(cum * LOG2E))
    ex = _dotf(a, sel_ref[0]) + _dotf(b, sel_ref[1]) + _dotf(d, sel_ref[2])
    for p in range(N_HEADS // 2):
        kt_o[p, :, 0:128] = k[:, 128 * p:128 * (p + 1)].astype(BF16)
        kt_o[p, :, 128:256] = ex[:, 128 * p:128 * (p + 1)].astype(BF16)
    vt_o[0, 0] = v.T.astype(BF16)


def _fox_sel():
    sel = np.zeros((3, 128, N_HEADS * HEAD_DIM), np.float32)
    for h in range(N_HEADS):
        for j in range(3):
            sel[j, h, 128 * (h // 2) + 3 * (h % 2) + j] = 1.0
    return jnp.asarray(sel, BF16)


def _fox_prep(k, v, lf128):
    srcs = [k, v, lf128]
    b, c = k[1].shape[0], k[1].shape[2]
    lp = _src_rows(k)
    nkb = lp // TK
    npair = N_HEADS // 2
    sel = _fox_sel()
    return pl.pallas_call(
        functools.partial(_fox_prep_body, past_blocks=_past_blocks(srcs)),
        grid=(b, nkb),
        in_specs=[sp for s in srcs for sp in _src_specs(s)] + [_full_spec(sel)],
        out_specs=[pl.BlockSpec((npair, TK, KT_LANES), lambda bb, j: (bb, j, 0)),
                   pl.BlockSpec((1, 1, c, TK), lambda bb, j: (bb, j, 0, 0))],
        out_shape=[jax.ShapeDtypeStruct((b * npair, lp, KT_LANES), BF16),
                   jax.ShapeDtypeStruct((b, nkb, c, TK), BF16)],
        scratch_shapes=[pltpu.VMEM((1, 128), F32)],
        compiler_params=_cparams(("arbitrary", "arbitrary")), name="fox_prep",
    )(*[a for s in srcs for a in _src_arrays(s)], sel)


def _visible_end(qpos, mode, n_keys):
    if mode == "causal":
        end = qpos + 1
    else:
        end = ((qpos >> CHUNK_SHIFT) + 1) << CHUNK_SHIFT
    return jnp.minimum(end, n_keys)


def _softmax_update(s, mx8, m_ref, c8=None):
    m8 = m_ref[...]
    m_new8 = jnp.maximum(m8, mx8)
    alpha8 = jnp.exp2(m8 - m_new8)
    shift = m_new8[0:1] if c8 is None else (m_new8 - c8)[0:1]
    m_ref[...] = m_new8
    return jnp.exp2(s - shift).astype(BF16), alpha8[0:1]


def _pv_and_sum(vt_blk, pb):
    ones = jnp.ones((SUM_ROWS, vt_blk.shape[1]), BF16)
    return _dotf(jnp.concatenate([vt_blk, ones], axis=0), pb)


def _run_two_stage(n_full, n_masked, stage_a, stage_b):
    odd = n_full % 2

    @pl.when(odd == 1)
    def _():
        stage_a(0, 0, False)
        stage_b(0, 0, False)

    n_pairs = (n_full - odd) // 2

    @pl.when(n_pairs >= 1)
    def _():
        stage_a(odd, 0, False)

        def body(u, _):
            kb = odd + 2 * u
            stage_b(kb, 0, False)
            stage_a(kb + 1, 1, False)
            stage_b(kb + 1, 1, False)
            stage_a(kb + 2, 0, False)
            return 0

        lax.fori_loop(0, n_pairs - 1, body, 0)
        stage_b(n_full - 2, 0, False)
        stage_a(n_full - 1, 1, False)
        stage_b(n_full - 1, 1, False)

    stage_a(n_full, 0, True)
    for j in range(1, n_masked):
        stage_b(n_full + j - 1, (j - 1) % 2, True)
        stage_a(n_full + j, j % 2, True)
    stage_b(n_full + n_masked - 1, (n_masked - 1) % 2, True)


def _flash_body(qp_ref, kt_ref, vt_ref, o_ref, s_buf, mx_buf, m_ref, acc_ref, *,
                tq, mode, q0, n_keys, paired, n_masked):
    i = pl.program_id(1)
    q_first = q0 + i * tq
    n_full = _visible_end(q_first, mode, n_keys) // TK
    ns = N_HEADS // 2 if paired else 1
    per = 2 if paired else 1
    ws, rows = per * tq, per * HEAD_DIM
    w = ns * ws
    ik = lax.broadcasted_iota(I32, (TK, ws), 0)
    iq = lax.broadcasted_iota(I32, (TK, tq), 1)
    qpos = q_first + (jnp.concatenate([iq] * per, axis=1) if paired else iq)
    qs = [jnp.concatenate([qp_ref[per * si + a] for a in range(per)], axis=1) if paired else qp_ref[si]
          for si in range(ns)]

    m_ref[...] = jnp.full((8, w), NEG, F32)
    acc_ref[...] = jnp.zeros((rows + SUM_ROWS, w), F32)

    def stage_a(kb, slot, masked):
        k0 = pl.multiple_of(kb * TK, TK)
        if masked:
            kpos = k0 + ik
            valid = (kpos <= qpos) if mode == "causal" else ((kpos >> CHUNK_SHIFT) <= (qpos >> CHUNK_SHIFT))
            valid = valid & (kpos < n_keys)
        for si in range(ns):
            s = _dotf(kt_ref[si, pl.ds(k0, TK), :], qs[si])
            if masked:
                s = jnp.where(valid, s, NEG)
            s_buf[slot, :, si * ws:(si + 1) * ws] = s
            mx_buf[slot, :, si * ws:(si + 1) * ws] = jnp.broadcast_to(jnp.max(s, axis=0, keepdims=True), (8, ws))

    def stage_b(kb, slot, masked):
        del masked
        pb, alpha = _softmax_update(s_buf[slot], mx_buf[slot], m_ref)
        for si in range(ns):
            sl = slice(si * ws, (si + 1) * ws)
            acc_ref[:, sl] = alpha[:, sl] * acc_ref[:, sl] + _pv_and_sum(
                vt_ref[0, kb, rows * si:rows * (si + 1), :], pb[:, sl])

    _run_two_stage(n_full, n_masked, stage_a, stage_b)
    out = acc_ref[0:rows, :] / acc_ref[rows:rows + 1, :]
    for si in range(ns):
        for a in range(per):
            o_ref[per * si + a] = out[HEAD_DIM * a:HEAD_DIM * (a + 1), si * ws + tq * a:si * ws + tq * (a + 1)]


def _visible_end_static(qpos, mode, n_keys):
    end = qpos + 1 if mode == "causal" else ((qpos >> CHUNK_SHIFT) + 1) << CHUNK_SHIFT
    return min(end, n_keys)


def _flash(qp, kt, vt, *, tq, mode, q0, n_real_q, n_keys, paired):
    bh, _, sq = qp.shape
    lp = kt.shape[1]
    nkb = vt.shape[1]
    assert (q0 % TK == 0) and (tq % TK == 0 or sq == tq)
    n_masked = (-(-_visible_end_static(q0 + n_real_q - 1, mode, n_keys) // TK)
                - _visible_end_static(q0, mode, n_keys) // TK)
    body = functools.partial(_flash_body, tq=tq, mode=mode, q0=q0, n_keys=n_keys, paired=paired,
                             n_masked=n_masked)
    if not paired:
        ns, w, rows = 1, tq, HEAD_DIM
        kt_spec = pl.BlockSpec((1, lp, KT_LANES), lambda g, i: (g // 2, 0, 0))
        vt_spec = pl.BlockSpec((1, nkb, HEAD_DIM, TK), lambda g, i: (g // N_HEADS, 0, g % N_HEADS, 0))
    else:
        ns, w, rows = N_HEADS, N_HEADS * tq, 2 * HEAD_DIM
        kt_spec = pl.BlockSpec((ns // 2, lp, KT_LANES), lambda g, i: (g, 0, 0))
        vt_spec = pl.BlockSpec((1, nkb, ns * HEAD_DIM, TK), lambda g, i: (g, 0, 0, 0))
    return pl.pallas_call(
        body,
        grid=(bh // ns, sq // tq),
        in_specs=[pl.BlockSpec((ns, KT_LANES, tq), lambda g, i: (g, 0, i)), kt_spec, vt_spec],
        out_specs=pl.BlockSpec((ns, HEAD_DIM, tq), lambda g, i: (g, 0, i)),
        out_shape=jax.ShapeDtypeStruct((bh, HEAD_DIM, sq), F32),
        scratch_shapes=[pltpu.VMEM((2, TK, w), F32), pltpu.VMEM((2, 8, w), F32), pltpu.VMEM((8, w), F32),
                        pltpu.VMEM((rows + SUM_ROWS, w), F32)],
        compiler_params=_cparams(("arbitrary", "arbitrary")),
        name="flash_" + mode,
    )(qp, kt, vt)


def _swa_body(hp_ref, qp_ref, *refs, tq, q0, n_pieces, n_rows):
    k_refs, v_refs, o_ref = refs[:n_pieces], refs[n_pieces:2 * n_pieces], refs[2 * n_pieces]
    i = pl.program_id(1)
    q_first = q0 + i * tq
    k = jnp.concatenate([r[0].astype(BF16) for r in k_refs], axis=0)
    v = jnp.concatenate([r[0].astype(BF16) for r in v_refs], axis=0)
    kw = k.shape[0]
    row = lax.broadcasted_iota(I32, (kw, tq), 0)
    kpos = (q_first - WINDOW) + row
    qpos = q_first + lax.broadcasted_iota(I32, (kw, tq), 1)
    qc = qpos >> CHUNK_SHIFT
    kc = kpos >> CHUNK_SHIFT
    valid = (kpos >= 0) & (kc <= qc) & (qc - kc <= WIN_CHUNKS) & (row < n_rows)
    dist = jnp.abs(qpos - kpos).astype(F32)
    group = N_HEADS // KV_HEADS

    def lanes(x):
        return jnp.concatenate([x] * group, axis=1)

    for n in range(KV_HEADS):
        h0 = n * group

        def per_head(col):
            return jnp.concatenate(
                [jnp.broadcast_to(hp_ref[h0 + g][:, col:col + 1], (1, tq)) for g in range(group)], axis=1)

        q = jnp.concatenate([qp_ref[h0 + g] for g in range(group)], axis=1)
        slope2, sink2 = per_head(0), per_head(1)
        s = _dotf(k, q) - slope2 * lanes(dist)
        s = jnp.where(lanes(valid), s, NEG)
        m = jnp.maximum(jnp.max(s, axis=0, keepdims=True), sink2)
        p = jnp.exp2(s - m)
        l = jnp.sum(p, axis=0, keepdims=True) + jnp.exp2(sink2 - m)
        acc = _dot_tn(v, p.astype(BF16))
        out = acc[HEAD_DIM * n:HEAD_DIM * (n + 1)] / l
        for g in range(group):
            o_ref[h0 + g] = out[:, g * tq:(g + 1) * tq]


def _swa_attend(hp, qp, k_pieces, v_pieces, *, tq, q0, n_rows):
    b = k_pieces[0][0].shape[0]
    sq = qp.shape[2]
    n_pieces = len(k_pieces)

    def spec(piece):
        _, rows, idx = piece
        return pl.BlockSpec((1, rows, KV_HEADS * HEAD_DIM), lambda bb, i: (bb, idx(i), 0))

    body = functools.partial(_swa_body, tq=tq, q0=q0, n_pieces=n_pieces, n_rows=n_rows)
    return pl.pallas_call(
        body,
        grid=(b, sq // tq),
        in_specs=[_full_spec(hp), pl.BlockSpec((N_HEADS, KT_LANES, tq), lambda bb, i: (bb, 0, i))]
        + [spec(p) for p in k_pieces] + [spec(p) for p in v_pieces],
        out_specs=pl.BlockSpec((N_HEADS, HEAD_DIM, tq), lambda bb, i: (bb, 0, i)),
        out_shape=jax.ShapeDtypeStruct((b * N_HEADS, HEAD_DIM, sq), F32),
        compiler_params=_cparams(("arbitrary", "arbitrary")),
        name="swa_attend",
    )(hp, qp, *[p[0] for p in k_pieces], *[p[0] for p in v_pieces])


def _dsa_body(hp_ref, qi_ref, wi_ref, qp_ref, ki_ref, kt_ref, vt_ref, o_ref, hi_ref, lo_ref,
              s_buf, mx_buf, m_ref, acc_ref, *, tq, q0, n_real_q, n_keys, topk):
    i = pl.program_id(1)
    q_first = q0 + i * tq
    q_last = q_first + (n_real_q - 1)
    n_tot = (_visible_end(q_last, "chunk", n_keys) + (TK - 1)) // TK
    n_past = jnp.minimum(q_first, n_keys) // TK
    ik = lax.broadcasted_iota(I32, (TK, tq), 0)
    iq = lax.broadcasted_iota(I32, (TK, tq), 1)
    qpos = q_first + iq
    tf = float(topk)

    def blk(kb):
        return pl.ds(pl.multiple_of(kb * TK, TK), TK)

    def score_blk(kb, _):
        ki = ki_ref[0, blk(kb), :]
        acc = jnp.zeros((TK, tq), F32)
        for h in range(IDX_HEADS):
            acc = acc + wi_ref[0, h:h + 1, :] * jnp.maximum(_dotf(ki, qi_ref[0, h]), 0.0)
        kpos = kb * TK + ik
        valid = ((kpos >> CHUNK_SHIFT) <= (qpos >> CHUNK_SHIFT)) & (kpos < n_keys)
        bits = pltpu.bitcast(acc, I32)
        key = jnp.where(valid, jnp.where(bits < 0, bits ^ INT_MAX, bits), INT_MIN)
        hi_ref[blk(kb), :] = (key >> 16).astype(I16)
        lo_ref[blk(kb), :] = ((key & 0xFFFF) - 32768).astype(I16)
        return 0

    lax.fori_loop(0, n_tot, score_blk, 0)

    def count_ge(ref, mid):
        mid16 = mid.astype(I16)

        def body(kb, acc):
            ge = jnp.where(ref[blk(kb), :] >= mid16, jnp.int16(1), jnp.int16(0))
            parts = [ge[16 * r:16 * (r + 1)] for r in range(TK // 16)]
            while len(parts) > 1:
                parts = [parts[j] + parts[j + 1] for j in range(0, len(parts), 2)]
            return acc + parts[0]

        acc = lax.fori_loop(0, n_tot, body, jnp.zeros((16, tq), I16))
        return jnp.sum(acc.astype(I32), axis=0, keepdims=True).astype(F32)

    def bisect_step(ref, target, st):
        lo, hi, cl, ch = st
        mid = (lo + hi) >> 1
        cnt = count_ge(ref, mid)
        ge = cnt >= target
        return jnp.where(ge, mid, lo), jnp.where(ge, hi, mid), jnp.where(ge, cnt, cl), jnp.where(ge, ch, cnt)

    qrow = q_first + lax.broadcasted_iota(I32, (1, tq), 1)
    n_vis = _visible_end(qrow, "chunk", n_keys).astype(F32)
    zero = jnp.zeros((1, tq), F32)
    st1 = (jnp.full((1, tq), I16_MIN + 1, I32), jnp.full((1, tq), I16_MAX + 1, I32), n_vis, zero)
    h_thr, _, cl1, ch1 = lax.fori_loop(0, 16, lambda _, st: bisect_step(hi_ref, tf, st), st1)
    h16 = h_thr.astype(I16)

    def mask_lo(kb, _):
        lo_ref[blk(kb), :] = jnp.where(hi_ref[blk(kb), :] == h16, lo_ref[blk(kb), :], jnp.int16(I16_MIN))
        return 0

    lax.fori_loop(0, n_tot, mask_lo, 0)
    t2 = tf - ch1

    def cond(st):
        return (st[0] < 16) & (st[2] > 0.5)

    def body(st):
        lo, hi, cl, ch = bisect_step(lo_ref, t2, st[1])
        done = (cl <= t2) | (hi - lo <= 1)
        return st[0] + 1, (lo, hi, cl, ch), jnp.sum(jnp.where(done, 0.0, 1.0))

    cl2_0 = cl1 - ch1
    st2 = (jnp.full((1, tq), I16_MIN, I32), jnp.full((1, tq), I16_MAX + 1, I32), cl2_0, zero)
    _, (l_thr, _, cl2, ch2), _ = lax.while_loop(
        cond, body, (jnp.int32(0), st2, jnp.sum(jnp.where(cl2_0 > t2, 1.0, 0.0))))
    l16 = l_thr.astype(I16)

    need = t2 - ch2

    @pl.when(jnp.sum(jnp.where(cl2 > t2, 1.0, 0.0)) > 0.5)
    def _():
        r = lax.broadcasted_iota(I32, (TK, TK), 0)
        c = lax.broadcasted_iota(I32, (TK, TK), 1)
        tri = jnp.where(c < r, 1.0, 0.0).astype(BF16)

        def fix(kb, carry):
            hb = hi_ref[blk(kb), :]
            e16 = jnp.where(hb == h16, jnp.where(lo_ref[blk(kb), :] == l16, jnp.int16(1), jnp.int16(0)),
                            jnp.int16(0))
            e = e16.astype(I32).astype(F32)
            before = _dotf(tri, e.astype(BF16)) + carry
            drop = jnp.where((e > 0.5) & (before >= need), 1, 0).astype(I16)
            hi_ref[blk(kb), :] = jnp.where(drop == jnp.int16(1), jnp.int16(I16_MIN), hb)
            return carry + jnp.sum(e, axis=0, keepdims=True)

        lax.fori_loop(0, n_tot, fix, jnp.zeros((1, tq), F32))

    def to_bias(kb, _):
        hb = hi_ref[blk(kb), :]
        zero_b, neg_b = jnp.bfloat16(0.0), jnp.bfloat16(NEG)
        at_thr = jnp.where(lo_ref[blk(kb), :] >= l16, zero_b, neg_b)
        bias = jnp.where(hb > h16, zero_b, jnp.where(hb == h16, at_thr, neg_b))
        hi_ref[blk(kb), :] = pltpu.bitcast(bias, I16)
        return 0

    lax.fori_loop(0, n_tot, to_bias, 0)

    group = N_HEADS // KV_HEADS
    wide = group * tq

    def lanes(x):
        return jnp.concatenate([x] * group, axis=1)

    def group_body(n, _):
        h0 = n * group
        q = jnp.concatenate([qp_ref[h0 + g] for g in range(group)], axis=1)
        slope2 = jnp.concatenate(
            [jnp.broadcast_to(hp_ref[h0 + g][:, 0:1], (1, tq)) for g in range(group)], axis=1)
        slope8 = jnp.broadcast_to(slope2, (8, wide))
        a_tab = slope2 * lanes(ik.astype(F32))
        vrow = pl.ds(pl.multiple_of(n * HEAD_DIM, HEAD_DIM), HEAD_DIM)
        m_ref[...] = jnp.full((8, wide), NEG, F32)
        acc_ref[...] = jnp.zeros((HEAD_DIM + SUM_ROWS, wide), F32)

        def c8(kb):
            return slope8 * (kb * TK - q_first).astype(F32)

        def stage_a(kb, slot, diag):
            bias = pltpu.bitcast(hi_ref[blk(kb), :], jnp.bfloat16).astype(F32)
            s = _dotf(kt_ref[0, blk(kb), :], q) + lanes(bias)
            if diag:
                kpos = kb * TK + ik
                rel = iq.astype(F32) - jnp.abs(qpos - kpos).astype(F32)
                s = s + slope2 * lanes(rel)
            else:
                s = s + a_tab
            mx8 = jnp.broadcast_to(jnp.max(s, axis=0, keepdims=True), (8, wide))
            s_buf[slot] = s
            mx_buf[slot] = mx8 if diag else mx8 + c8(kb)

        def stage_b(kb, slot, diag):
            pb, alpha = _softmax_update(s_buf[slot], mx_buf[slot], m_ref, None if diag else c8(kb))
            acc_ref[...] = alpha * acc_ref[...] + _pv_and_sum(vt_ref[0, kb, vrow, :], pb)

        _run_two_stage(n_past, 1, stage_a, stage_b)
        out = acc_ref[0:HEAD_DIM, :] / acc_ref[HEAD_DIM:HEAD_DIM + 1, :]
        for g in range(group):
            o_ref[h0 + g] = out[:, g * tq:(g + 1) * tq]
        return 0

    lax.fori_loop(0, KV_HEADS, group_body, 0)


def _dsa_attend(hp, qi, wi, qp, ki, kt, vt, *, tq, q0, n_real_q, n_keys):
    b = ki.shape[0]
    sq = qp.shape[2]
    lp = kt.shape[1]
    nkb = vt.shape[1]
    topk = min(TOPK_MAX, n_keys // 4)
    assert TK % tq == 0 and q0 % TK == 0 and n_real_q <= tq
    wide = (N_HEADS // KV_HEADS) * tq
    body = functools.partial(_dsa_body, tq=tq, q0=q0, n_real_q=n_real_q, n_keys=n_keys, topk=topk)
    once = pl.Buffered(1)
    return pl.pallas_call(
        body,
        grid=(b, sq // tq),
        in_specs=[_full_spec(hp),
                  pl.BlockSpec((1, IDX_HEADS, IDX_DIM, tq), lambda bb, i: (bb, 0, 0, i)),
                  pl.BlockSpec((1, IDX_HEADS, tq), lambda bb, i: (bb, 0, i)),
                  pl.BlockSpec((N_HEADS, KT_LANES, tq), lambda bb, i: (bb, 0, i)),
                  pl.BlockSpec((1, lp, IDX_DIM), lambda bb, i: (bb, 0, 0), pipeline_mode=once),
                  pl.BlockSpec((1, lp, KT_LANES), lambda bb, i: (bb, 0, 0), pipeline_mode=once),
                  pl.BlockSpec((1, nkb, KV_HEADS * HEAD_DIM, TK), lambda bb, i: (bb, 0, 0, 0),
                               pipeline_mode=once)],
        out_specs=pl.BlockSpec((N_HEADS, HEAD_DIM, tq), lambda bb, i: (bb, 0, i)),
        out_shape=jax.ShapeDtypeStruct((b * N_HEADS, HEAD_DIM, sq), F32),
        scratch_shapes=[pltpu.VMEM((lp, tq), I16), pltpu.VMEM((lp, tq), I16),
                        pltpu.VMEM((2, TK, wide), F32), pltpu.VMEM((2, 8, wide), F32),
                        pltpu.VMEM((8, wide), F32), pltpu.VMEM((HEAD_DIM + SUM_ROWS, wide), F32)],
        compiler_params=_cparams(("arbitrary", "arbitrary")),
        name="dsa_attend",
    )(hp, qi, wi, qp, ki, kt, vt)


def _block_diag(c):
    g = np.arange(c) // HEAD_DIM
    return jnp.asarray(g[:, None] == g[None, :], BF16)


def _tile_gain(g, n):
    return jnp.tile(g.astype(F32), n).reshape(1, -1)


def _col(g):
    return g.astype(F32).reshape(-1, 1)


def _pad_cols(w, n):
    return jnp.pad(w, ((0, 0), (0, n - w.shape[1])))


def _rope_tables(pos):
    half = MLA_ROPE // 2
    inv = ROPE_BASE ** (-jnp.arange(half, dtype=F32) / half)
    ang = pos.astype(F32)[:, None] * inv[None, :]
    return jnp.cos(ang), jnp.sin(ang)


def _alibi_slopes():
    return np.asarray(2.0 ** (-8.0 * np.arange(1, N_HEADS + 1) / N_HEADS), dtype=np.float32)


def _head_params(sinks=None):
    hp = jnp.zeros((N_HEADS, 1, 128), F32)
    hp = hp.at[:, 0, 0].set(jnp.asarray(_alibi_slopes()) * LOG2E)
    if sinks is not None:
        hp = hp.at[:, 0, 1].set(sinks.astype(F32) * LOG2E)
    return hp


class _Stream:
    def __init__(self, batch, seq, past):
        self.b, self.s, self.p = batch, seq, past
        self.r = batch * seq
        self.decode = past > 0
        self.tq = TQ_DEC if self.decode else TQ
        self.tqf = TQ_DEC if self.decode else min(TQ_FLASH, seq)
        self.sq =self.tq if self.decode else seq
        self.n_keys = past + seq
        self.lp = -(-self.n_keys // TK) * TK
        self.pos = past + np.tile(np.arange(seq), batch)

    def pad_keys(self, past_arr, new_arr, n_keys=None, lp=None):
        n_keys = self.n_keys if n_keys is None else n_keys
        lp = self.lp if lp is None else lp
        new_arr = new_arr.reshape(self.b, self.s, -1)
        parts = [new_arr] if past_arr is None else [past_arr.astype(F32), new_arr]
        if lp > n_keys:
            parts.append(jnp.zeros((self.b, lp - n_keys, new_arr.shape[-1]), F32))
        return jnp.concatenate(parts, axis=1) if len(parts) > 1 else new_arr

    def key_source(self, past_arr, new_arr):
        new_arr = new_arr.reshape(self.b, self.s, -1)
        if past_arr is None:
            return (None, new_arr)
        assert past_arr.shape[1] == self.p and self.p % TK == 0
        pad = self.lp - self.n_keys
        return (past_arr.astype(F32), jnp.pad(new_arr, ((0, 0), (0, pad), (0, 0))))

    def qp_blocks(self, qp):
        if not self.decode:
            return qp
        x = qp.reshape(N_HEADS, KT_LANES, self.b, self.s).transpose(2, 0, 1, 3)
        x = jnp.pad(x, ((0, 0), (0, 0), (0, 0), (0, self.tq - self.s)))
        return x.reshape(self.b * N_HEADS, KT_LANES, self.tq)

    def lanes(self, x):
        n, c, _ = x.shape
        if not self.decode:
            return x[None]
        x = x.reshape(n, c, self.b, self.s).transpose(2, 0, 1, 3)
        return jnp.pad(x, ((0, 0), (0, 0), (0, 0), (0, self.tq - self.s)))

    def ot_cols(self, ot):
        if not self.decode:
            return ot.reshape(N_HEADS * HEAD_DIM, self.r)
        x = ot.reshape(self.b, N_HEADS, HEAD_DIM, self.tq)[..., :self.s]
        return x.transpose(1, 2, 0, 3).reshape(N_HEADS * HEAD_DIM, self.r)


def _mod_rows(st, mod_l, row0):
    m = mod_l[row0:row0 + st.b]
    if st.b > 1:
        m = jnp.repeat(m, st.s, axis=0)
    return m[:, :D_MODEL], m[:, D_MODEL:2 * D_MODEL], m[:, 2 * D_MODEL:]


def _mla_layer(st, x, mod, g, w, past):
    shift, scale, gate = mod
    r = st.r
    ts = _row_tile(r)
    cos, sin = _rope_tables(jnp.asarray(st.pos))
    zpad = jnp.zeros((r, 128 - MLA_ROPE), F32)
    cos_p = jnp.concatenate([cos, cos, zpad], axis=1)
    sin_p = jnp.concatenate([-sin, sin, zpad], axis=1)
    cqn, lat, kr, zt = _proj_call(
        _mla_in_body, "mla_in", r, ts,
        [x, scale, shift, cos_p, sin_p, g, w["gqa"], w["gkva"], w["gkr"], w["gkrp"]], [],
        [w["wcq"], w["wckv"], w["wkr"], w["wkrp"], w["wzt"]],
        [_rows_out(r, MLA_Q_LORA, ts, BF16), _rows_out(r, MLA_KV_LORA, ts), _rows_out(r, 128, ts),
         _cols_out(D_MODEL, r, ts)])
    (qp,) = _proj_call(_mla_q_body, "mla_q", r, ts, [cqn], [cos.T, sin.T], [w["wqt"], w["gqn"], w["gqr"]],
                       [_qp_out(r, ts)])
    past_lat, past_kr = (None, None) if past is None else past
    if past_kr is not None:
        past_kr = jnp.pad(past_kr.astype(F32), ((0, 0), (0, 0), (0, 128 - MLA_ROPE)))
    lat_all = st.pad_keys(past_lat, lat).reshape(st.b * st.lp, MLA_KV_LORA)
    kr_all = st.pad_keys(past_kr, kr).reshape(st.b * st.lp, 128)
    rk = st.b * st.lp
    npair = N_HEADS // 2
    nkb = st.lp // TK
    kt, vt = _proj_call(
        _mla_kv_body, "mla_kv", rk, TK, [lat_all, kr_all, w["gkn"]], [], [w["wkn"], w["wvt"], w["bd"]],
        [((npair * st.b, st.lp, KT_LANES), BF16, (npair, TK, KT_LANES), lambda i: (i // nkb, i % nkb, 0)),
         ((rk // TK, N_HEADS * HEAD_DIM, TK), BF16, (1, N_HEADS * HEAD_DIM, TK), lambda i: (i, 0, 0))])
    vt = vt.reshape(st.b, nkb, N_HEADS * HEAD_DIM, TK)
    ot = _flash(st.qp_blocks(qp), kt, vt, tq=st.tqf, mode="chunk", q0=st.p,
                n_real_q=st.tqf if not st.decode else st.s, n_keys=st.n_keys, paired=st.decode)
    x = _out_proj(x, gate, st.ot_cols(ot), zt, w["wo"])
    return x, (lat, kr[:, :MLA_ROPE])


def _dsa_layer(st, x, mod, g, w, past):
    shift, scale, gate = mod
    r = st.r
    ts = _row_tile(r)
    c = KV_HEADS * HEAD_DIM
    qp, k, v, qi, ki, wi, zt = _proj_call(
        _dsa_in_body, "dsa_in", r, ts, [x, scale, shift, g, w["gk"]], [],
        [w["gq"], w["wqt"], w["wk"], w["wv"], w["wqit"], w["wki"], w["wwit"], w["wzt"], w["bd"]],
        [_qp_out(r, ts), _rows_out(r, c, ts), _rows_out(r, c, ts),
         ((IDX_HEADS, IDX_DIM, r), BF16, (IDX_HEADS, IDX_DIM, ts), lambda i: (0, 0, i)),
         _rows_out(r, IDX_DIM, ts), _cols_out(IDX_HEADS, r, ts), _cols_out(D_MODEL, r, ts)])
    if past is None:
        pk = pv = pki = None
    else:
        pk, pv, pki = past[0].reshape(st.b, -1, c), past[1].reshape(st.b, -1, c), past[2]
    kt, vt, kib = _gqa_prep(st.key_source(pk, k), st.key_source(pv, v), st.key_source(pki, ki))
    qi_b = st.lanes(qi)
    wi_b = st.lanes(wi[None])[:, 0]
    ot = _dsa_attend(_head_params(), qi_b, wi_b, st.qp_blocks(qp), kib, kt, vt, tq=st.tq, q0=st.p,
                     n_real_q=st.tq if not st.decode else st.s, n_keys=st.n_keys)
    x = _out_proj(x, gate, st.ot_cols(ot), zt, w["wo"])
    return x, (k, v, ki)


def _swa_layer(st, x, mod, g, w, past):
    shift, scale, gate = mod
    r = st.r
    ts = _row_tile(r)
    c = KV_HEADS * HEAD_DIM
    qp, k, v, zt = _proj_call(
        _swa_in_body, "swa_in", r, ts, [x, scale, shift, g, w["gk"]], [],
        [w["gq"], w["wqt"], w["wk"], w["wv"], w["wzt"], w["bd"]],
        [_qp_out(r, ts), _rows_out(r, c, ts), _rows_out(r, c, ts), _cols_out(D_MODEL, r, ts)])
    k3, v3 = k.reshape(st.b, st.s, c), v.reshape(st.b, st.s, c)
    if past is None:
        per = st.tq // WINDOW
        idx = [lambda i: jnp.maximum(per * i - 1, 0), lambda i: per * i, lambda i: per * i + 1]
        k_pieces = [(k3, WINDOW, f) for f in idx]
        v_pieces = [(v3, WINDOW, f) for f in idx]
        n_rows = WINDOW + st.tq
        new = (k3[:, st.s - WINDOW:], v3[:, st.s - WINDOW:])
    else:
        win = past[0].shape[1]
        assert win == WINDOW and st.s <= WINDOW
        pad = ((0, 0), (0, WINDOW - st.s), (0, 0))
        pk, pv = past[0].reshape(st.b, win, c).astype(F32), past[1].reshape(st.b, win, c).astype(F32)
        zero = lambda i: 0
        k_pieces = [(pk, WINDOW, zero), (jnp.pad(k3, pad), WINDOW, zero)]
        v_pieces = [(pv, WINDOW, zero), (jnp.pad(v3, pad), WINDOW, zero)]
        n_rows = win + st.s
        new = (jnp.concatenate([pk, k3], axis=1)[:, st.s:], jnp.concatenate([pv, v3], axis=1)[:, st.s:])
    ot = _swa_attend(_head_params(w["sinks"]), st.qp_blocks(qp), k_pieces, v_pieces, tq=st.tq, q0=st.p,
                     n_rows=n_rows)
    x = _out_proj(x, gate, st.ot_cols(ot), zt, w["wo"])
    return x, new


def _fox_layer(st, x, mod, g, w, past):
    shift, scale, gate = mod
    r = st.r
    ts = _row_tile(r)
    c = N_HEADS * HEAD_DIM
    qp, k, v, lf, zt = _proj_call(
        _fox_in_body, "fox_in", r, ts, [x, scale, shift, g, w["gk"], w["bf"]], [],
        [w["gq"], w["wqt"], w["wk"], w["wv"], w["wf"], w["wzt"], w["bd"]],
        [_qp_out(r, ts), _rows_out(r, c, ts), _rows_out(r, c, ts), _rows_out(r, 128, ts),
         _cols_out(D_MODEL, r, ts)])
    if past is None:
        pk = pv = plf = None
    else:
        pk, pv = past[0].reshape(st.b, -1, c), past[1].reshape(st.b, -1, c)
        plf = jnp.pad(past[2].astype(F32), ((0, 0), (0, 0), (0, 128 - N_HEADS)))
    kt, vt = _fox_prep(st.key_source(pk, k), st.key_source(pv, v), st.key_source(plf, lf))
    ot = _flash(st.qp_blocks(qp), kt, vt, tq=st.tqf, mode="causal", q0=st.p,
                n_real_q=st.tqf if not st.decode else st.s, n_keys=st.n_keys, paired=st.decode)
    x = _out_proj(x, gate, st.ot_cols(ot), zt, w["wo"])
    return x, (k, v, lf[:, :N_HEADS])


def _prep_weights(mla_w_in, mla_g_qa, mla_w_qb, mla_g_kva, mla_w_kvb, mla_g_qn, mla_g_qr, mla_g_kn,
                  mla_g_kr, mla_w_out, dsa_w_in, dsa_g_q, dsa_g_k, dsa_w_out, swa_w_in, swa_g_q,
                  swa_g_k, swa_sinks, swa_w_out, fox_w_in, fox_b_f, fox_g_q, fox_g_k, fox_w_out):
    bf = lambda a: a.astype(BF16)
    row = lambda a: a.astype(F32).reshape(1, -1)
    half = MLA_ROPE // 2
    c1, c2, c3 = MLA_Q_LORA, MLA_Q_LORA + MLA_KV_LORA, MLA_Q_LORA + MLA_KV_LORA + MLA_ROPE
    wkr = mla_w_in[:, c2:c3]
    wkrp = jnp.concatenate([wkr[:, half:], wkr[:, :half]], axis=1)
    gkr = mla_g_kr.astype(F32)
    gkrp = jnp.concatenate([gkr[half:], gkr[:half]])
    kvb = mla_w_kvb.reshape(MLA_KV_LORA, N_HEADS, MLA_NOPE + HEAD_DIM)
    mla = dict(
        wcq=bf(mla_w_in[:, :c1]), wckv=bf(mla_w_in[:, c1:c2]), wkr=bf(_pad_cols(wkr, 128)),
        wkrp=bf(_pad_cols(wkrp, 128)), wzt=bf(mla_w_in[:, c3:].T),
        gqa=row(mla_g_qa), gkva=row(mla_g_kva), gkr=row(jnp.pad(gkr, (0, 128 - MLA_ROPE))),
        gkrp=row(jnp.pad(gkrp, (0, 128 - MLA_ROPE))),
        wqt=bf(mla_w_qb.T), gqn=_col(mla_g_qn), gqr=_col(mla_g_qr),
        wkn=bf(kvb[:, :, :MLA_NOPE].reshape(MLA_KV_LORA, -1)),
        wvt=bf(kvb[:, :, MLA_NOPE:].reshape(MLA_KV_LORA, -1).T),
        gkn=_tile_gain(mla_g_kn, N_HEADS), bd=_block_diag(BD_LANES), wo=bf(mla_w_out))
    hq, hk = N_HEADS * HEAD_DIM, KV_HEADS * HEAD_DIM
    cuts = np.cumsum([hq, hk, hk, IDX_HEADS * IDX_DIM, IDX_DIM, IDX_HEADS]).tolist()
    dsa = dict(
        wqt=bf(dsa_w_in[:, :cuts[0]].T), wk=bf(dsa_w_in[:, cuts[0]:cuts[1]]),
        wv=bf(dsa_w_in[:, cuts[1]:cuts[2]]), wqit=bf(dsa_w_in[:, cuts[2]:cuts[3]].T),
        wki=bf(dsa_w_in[:, cuts[3]:cuts[4]]), wwit=bf(dsa_w_in[:, cuts[4]:cuts[5]].T),
        wzt=bf(dsa_w_in[:, cuts[5]:].T), gq=_col(dsa_g_q), gk=_tile_gain(dsa_g_k, KV_HEADS),
        bd=_block_diag(BD_LANES), wo=bf(dsa_w_out))
    swa = dict(
        wqt=bf(swa_w_in[:, :hq].T), wk=bf(swa_w_in[:, hq:hq + hk]), wv=bf(swa_w_in[:, hq + hk:hq + 2 * hk]),
        wzt=bf(swa_w_in[:, hq + 2 * hk:].T), gq=_col(swa_g_q), gk=_tile_gain(swa_g_k, KV_HEADS),
        bd=_block_diag(BD_LANES), wo=bf(swa_w_out), sinks=swa_sinks)
    fox = dict(
        wqt=bf(fox_w_in[:, :hq].T), wk=bf(fox_w_in[:, hq:2 * hq]), wv=bf(fox_w_in[:, 2 * hq:3 * hq]),
        wf=bf(_pad_cols(fox_w_in[:, 3 * hq:3 * hq + N_HEADS], 128)), wzt=bf(fox_w_in[:, 3 * hq + N_HEADS:].T),
        bf=row(jnp.pad(fox_b_f.astype(F32), (0, 128 - N_HEADS))), gq=_col(fox_g_q),
        gk=_tile_gain(fox_g_k, N_HEADS), bd=_block_diag(BD_LANES), wo=bf(fox_w_out))
    return [mla, dsa, swa, fox]


def kernel(x_prompt, x_sample, cache_mla_latent, cache_mla_krope, cache_dsa_k, cache_dsa_v, cache_dsa_kidx, state_swa_k, state_swa_v, cache_fox_k, cache_fox_v, cache_fox_logf, c_prompt, c_sample, norm_g, ada_w, ada_b, mla_w_in, mla_g_qa, mla_w_qb, mla_g_kva, mla_w_kvb, mla_g_qn, mla_g_qr, mla_g_kn, mla_g_kr, mla_w_out, dsa_w_in, dsa_g_q, dsa_g_k, dsa_w_out, swa_w_in, swa_g_q, swa_g_k, swa_sinks, swa_w_out, fox_w_in, fox_b_f, fox_g_q, fox_g_k, fox_w_out):
    bp, sp, _ = x_prompt.shape
    bs, ss, _ = x_sample.shape
    past_len = cache_mla_latent.shape[1]
    depth = norm_g.shape[0]
    assert bp == 1 and sp % TQ == 0 and sp % min(TQ_FLASH, sp) == 0 and sp % TS == 0 and (bs * ss) % 8 == 0 and ss <= TQ_DEC
    assert past_len % TK == 0 and past_len >= WINDOW

    weights = _prep_weights(mla_w_in, mla_g_qa, mla_w_qb, mla_g_kva, mla_w_kvb, mla_g_qn, mla_g_qr,
                            mla_g_kn, mla_g_kr, mla_w_out, dsa_w_in, dsa_g_q, dsa_g_k, dsa_w_out,
                            swa_w_in, swa_g_q, swa_g_k, swa_sinks, swa_w_out, fox_w_in, fox_b_f,
                            fox_g_q, fox_g_k, fox_w_out)
    rows = bp + bs
    rows_p = -(-rows // 8) * 8
    c_all = jnp.concatenate([c_prompt, c_sample, jnp.zeros((rows_p - rows, D_MODEL), F32)], axis=0)
    mod = _ada_mod(c_all, ada_w, ada_b)

    st_p = _Stream(bp, sp, 0)
    st_s = _Stream(bs, ss, past_len)
    pasts = ((cache_mla_latent, cache_mla_krope), (cache_dsa_k, cache_dsa_v, cache_dsa_kidx),
             (state_swa_k, state_swa_v), (cache_fox_k, cache_fox_v, cache_fox_logf))
    layers = (_mla_layer, _dsa_layer, _swa_layer, _fox_layer)
    xp = x_prompt.reshape(st_p.r, D_MODEL)
    xs = x_sample.reshape(st_s.r, D_MODEL)
    new_p, new_s = [], []
    for layer in range(depth):
        kind = layer % len(layers)
        g = norm_g[layer].astype(F32).reshape(1, -1)
        xp, n = layers[kind](st_p, xp, _mod_rows(st_p, mod[layer], 0), g, weights[kind], None)
        new_p.append(n)
        xs, n = layers[kind](st_s, xs, _mod_rows(st_s, mod[layer], bp), g, weights[kind], pasts[kind])
        new_s.append(n)

    def shaped(st, new):
        (lat, kr), (dk, dv, dki), (sk, sv), (fk, fv, flf) = new
        b, s = st.b, st.s
        return (lat.reshape(b, s, -1), kr.reshape(b, s, -1),
                dk.reshape(b, s, KV_HEADS, HEAD_DIM), dv.reshape(b, s, KV_HEADS, HEAD_DIM),
                dki.reshape(b, s, -1),
                sk.reshape(b, -1, KV_HEADS, HEAD_DIM), sv.reshape(b, -1, KV_HEADS, HEAD_DIM),
                fk.reshape(b, s, N_HEADS, HEAD_DIM), fv.reshape(b, s, N_HEADS, HEAD_DIM),
                flf.reshape(b, s, -1))

    return (xp.reshape(x_prompt.shape), xs.reshape(x_sample.shape)) + shaped(st_p, new_p) + shaped(st_s, new_s)
```

```python
import functools

import numpy as np
import jax
import jax.numpy as jnp
from jax import lax
from jax.experimental import pallas as pl
from jax.experimental.pallas import tpu as pltpu

F32 = jnp.float32
BF16 = jnp.bfloat16
I32 = jnp.int32

D_MODEL = 1024
HEAD_DIM = 64
N_HEADS = 16
KV_HEADS = 4
CHUNK = 64
CHUNK_SHIFT = 6
WINDOW = 128
WIN_CHUNKS = WINDOW // CHUNK
EPS = 1e-6
ROPE_BASE = 10000.0
MLA_NOPE, MLA_ROPE, MLA_Q_LORA, MLA_KV_LORA = 64, 32, 384, 256
IDX_HEADS, IDX_DIM, TOPK_MAX = 8, 64, 256
LOG2E = 1.4426950408889634
NEG = -1e30
INT_MIN = -(2 ** 31)
INT_MAX = 2 ** 31 - 1
I16 = jnp.int16
I16_MIN, I16_MAX = -(2 ** 15), 2 ** 15 - 1

TS = 256
TQ = 256
TQ_FLASH = 2048
TK = 512
TQ_DEC = 128
BD_LANES = 256
SUM_ROWS = 16
KT_LANES = 256
VMEM_LIMIT = 56 * 1024 * 1024


def _row_tile(r):
    return TS if r % TS == 0 else r


def _cparams(sem, vmem=VMEM_LIMIT):
    return pltpu.CompilerParams(dimension_semantics=sem, vmem_limit_bytes=vmem)


def _dotf(a, b):
    return jnp.dot(a, b, preferred_element_type=F32)


def _dot_nt(a, b):
    return lax.dot_general(a, b, (((1,), (1,)), ((), ())), preferred_element_type=F32)


def _dot_tn(a, b):
    return lax.dot_general(a, b, (((0,), (0,)), ((), ())), preferred_element_type=F32)


def _split3(x):
    hi = x.astype(BF16)
    r = x - hi.astype(F32)
    mid = r.astype(BF16)
    lo = (r - mid.astype(F32)).astype(BF16)
    return hi, mid, lo


def _silu(x):
    return x / (1.0 + jnp.exp(-x))


def _full_spec(arr):
    nd = arr.ndim
    return pl.BlockSpec(arr.shape, lambda *_: (0,) * nd)


def _row_spec(arr, ts):
    if arr.shape[0] == 1:
        return pl.BlockSpec((1, arr.shape[1]), lambda i: (0, 0))
    return pl.BlockSpec((ts, arr.shape[1]), lambda i: (i, 0))


def _col_spec(arr, ts):
    return pl.BlockSpec((arr.shape[0], ts), lambda i: (0, i))


def _ada_body(c_ref, w_ref, b_ref, o_ref):
    a = _silu(c_ref[...])
    w = w_ref[0]
    a_hi = a.astype(BF16)
    a_lo = (a - a_hi.astype(F32)).astype(BF16)
    w_hi = w.astype(BF16)
    w_lo = (w - w_hi.astype(F32)).astype(BF16)
    o_ref[0] = _dotf(a_hi, w_hi) + _dotf(a_hi, w_lo) + _dotf(a_lo, w_hi) + b_ref[0]


def _ada_mod(c_all, ada_w, ada_b):
    depth, d, n3 = ada_w.shape
    bp = c_all.shape[0]
    tn = 768
    return pl.pallas_call(
        _ada_body,
        grid=(depth, n3 // tn),
        in_specs=[
            pl.BlockSpec((bp, d), lambda l, j: (0, 0)),
            pl.BlockSpec((1, d, tn), lambda l, j: (l, 0, j)),
            pl.BlockSpec((1, 1, tn), lambda l, j: (l, 0, j)),
        ],
        out_specs=pl.BlockSpec((1, bp, tn), lambda l, j: (l, 0, j)),
        out_shape=jax.ShapeDtypeStruct((depth, bp, n3), F32),
        compiler_params=_cparams(("arbitrary", "arbitrary")),
        name="ada_mod",
    )(c_all, ada_w, ada_b.reshape(depth, 1, n3))


def _prenorm(x_ref, g_ref, sc_ref, sh_ref):
    x = x_ref[...]
    ms = jnp.mean(x * x, axis=-1, keepdims=True)
    xn = x * lax.rsqrt(ms + EPS) * g_ref[...]
    return (xn * (1.0 + sc_ref[...]) + sh_ref[...]).astype(BF16)


def _group_sumsq(y, bd_ref):
    sq = y * y
    hi = sq.astype(BF16)
    lo = (sq - hi.astype(F32)).astype(BF16)
    bd = bd_ref[...]
    chunks = []
    for c in range(y.shape[1] // BD_LANES):
        sl = slice(c * BD_LANES, (c + 1) * BD_LANES)
        chunks.append(_dotf(hi[:, sl], bd) + _dotf(lo[:, sl], bd))
    return jnp.concatenate(chunks, axis=1) if len(chunks) > 1 else chunks[0]


def _head_rms_rows(q, g_col):
    ms = jnp.mean(q * q, axis=0, keepdims=True)
    return q * lax.rsqrt(ms + EPS) * g_col


def _place_rows(piece, row0, total):
    ts = piece.shape[1]
    parts = []
    if row0 > 0:
        parts.append(jnp.zeros((row0, ts), F32))
    parts.append(piece)
    rest = total - row0 - piece.shape[0]
    if rest > 0:
        parts.append(jnp.zeros((rest, ts), F32))
    return jnp.concatenate(parts, axis=0) if len(parts) > 1 else piece


def _proj_call(body, name, r, ts, row_in, col_in, const_in, outs):
    in_specs = ([_row_spec(a, ts) for a in row_in] + [_col_spec(a, ts) for a in col_in]
                + [_full_spec(a) for a in const_in])
    return pl.pallas_call(
        body,
        grid=(r // ts,),
        in_specs=in_specs,
        out_specs=[pl.BlockSpec(blk, im) for (_, _, blk, im) in outs],
        out_shape=[jax.ShapeDtypeStruct(s, dt) for (s, dt, _, _) in outs],
        compiler_params=_cparams(("arbitrary",)),
        name=name,
    )(*row_in, *col_in, *const_in)


def _rows_out(r, c, ts, dtype=F32):
    return ((r, c), dtype, (ts, c), lambda i: (i, 0))


def _cols_out(c, r, ts, dtype=F32):
    return ((c, r), dtype, (c, ts), lambda i: (0, i))


def _qp_out(r, ts):
    return ((N_HEADS, KT_LANES, r), BF16, (N_HEADS, KT_LANES, ts), lambda i: (0, 0, i))


def _mla_in_body(x_ref, sc_ref, sh_ref, cos_ref, sin_ref, g_ref, gqa, gkva, gkr, gkrp,
                 wcq, wckv, wkr, wkrp, wzt, cqn_o, lat_o, kr_o, zt_o):
    h = _prenorm(x_ref, g_ref, sc_ref, sh_ref)
    cq = _dotf(h, wcq[...])
    cqn_o[...] = (cq * lax.rsqrt(jnp.mean(cq * cq, axis=-1, keepdims=True) + EPS) * gqa[...]).astype(BF16)
    ckv = _dotf(h, wckv[...])
    lat_o[...] = ckv * lax.rsqrt(jnp.mean(ckv * ckv, axis=-1, keepdims=True) + EPS) * gkva[...]
    kr = _dotf(h, wkr[...])
    krp = _dotf(h, wkrp[...])
    inv = lax.rsqrt(jnp.sum(kr * kr, axis=-1, keepdims=True) * (1.0 / MLA_ROPE) + EPS)
    kr_o[...] = (kr * gkr[...] * cos_ref[...] + krp * gkrp[...] * sin_ref[...]) * inv
    zt_o[...] = _dot_nt(wzt[...], h)


def _mla_q_body(cqn_ref, cos_ref, sin_ref, wqt, gqn, gqr, qp_o):
    qt = _dot_nt(wqt[...], cqn_ref[...])
    ts = qt.shape[1]
    sc = (MLA_NOPE + MLA_ROPE) ** -0.5 * LOG2E
    c = cos_ref[...]
    s = sin_ref[...]
    half = MLA_ROPE // 2
    width = MLA_NOPE + MLA_ROPE
    for h in range(N_HEADS):
        qn = _head_rms_rows(qt[width * h:width * h + MLA_NOPE], gqn[...]) * sc
        qr = _head_rms_rows(qt[width * h + MLA_NOPE:width * (h + 1)], gqr[...]) * sc
        x1, x2 = qr[:half], qr[half:]
        o1 = x1 * c - x2 * s
        o2 = x2 * c + x1 * s
        a = h % 2
        pieces = [qn, jnp.zeros((HEAD_DIM, ts), F32)]
        if a:
            pieces = pieces[::-1]
        pieces += [o1, o2, jnp.zeros((KT_LANES - 2 * HEAD_DIM - MLA_ROPE, ts), F32)]
        qp_o[h] = jnp.concatenate(pieces, axis=0).astype(BF16)


def _mla_kv_body(lat_ref, kr_ref, gkn, wkn, wvt, bd, kt_o, vt_o):
    lat = lat_ref[...].astype(BF16)
    kn = _dotf(lat, wkn[...])
    ss = _group_sumsq(kn, bd)
    kn = kn * lax.rsqrt(ss * (1.0 / MLA_NOPE) + EPS) * gkn[...]
    kr = kr_ref[...].astype(BF16)
    for p in range(N_HEADS // 2):
        kt_o[p, :, 0:128] = kn[:, 128 * p:128 * (p + 1)].astype(BF16)
        kt_o[p, :, 128:256] = kr
    vt_o[0] = _dot_nt(wvt[...], lat).astype(BF16)


def _q_heads_to_qp(qt, gq, qp_o, row_of_head, extra_of_head=None):
    sc = HEAD_DIM ** -0.5 * LOG2E
    ts = qt.shape[1]
    for h in range(N_HEADS):
        qh = _head_rms_rows(qt[HEAD_DIM * h:HEAD_DIM * (h + 1)], gq[...]) * sc
        blk = _place_rows(qh, row_of_head(h), KT_LANES)
        if extra_of_head is not None:
            lo, hi = extra_of_head(h)
            rows = lax.broadcasted_iota(I32, (KT_LANES, ts), 0)
            blk = blk + jnp.where((rows >= lo) & (rows < hi), 1.0, 0.0)
        qp_o[h] = blk.astype(BF16)


def _dsa_in_body(x_ref, sc_ref, sh_ref, g_ref, gk, gq, wqt, wk, wv, wqit, wki, wwit, wzt, bd,
                 qp_o, k_o, v_o, qi_o, ki_o, wi_o, zt_o):
    h = _prenorm(x_ref, g_ref, sc_ref, sh_ref)
    _q_heads_to_qp(_dot_nt(wqt[...], h), gq, qp_o, lambda hh: HEAD_DIM * (hh // (N_HEADS // KV_HEADS)))
    k = _dotf(h, wk[...])
    k_o[...] = k * lax.rsqrt(_group_sumsq(k, bd) * (1.0 / HEAD_DIM) + EPS) * gk[...]
    v_o[...] = _dotf(h, wv[...])
    qit = _dot_nt(wqit[...], h)
    for ih in range(IDX_HEADS):
        qi_o[ih] = qit[IDX_DIM * ih:IDX_DIM * (ih + 1)].astype(BF16)
    ki_o[...] = _dotf(h, wki[...])
    wi_o[...] = _dot_nt(wwit[...], h)
    zt_o[...] = _dot_nt(wzt[...], h)


def _swa_in_body(x_ref, sc_ref, sh_ref, g_ref, gk, gq, wqt, wk, wv, wzt, bd, qp_o, k_o, v_o, zt_o):
    h = _prenorm(x_ref, g_ref, sc_ref, sh_ref)
    _q_heads_to_qp(_dot_nt(wqt[...], h), gq, qp_o, lambda hh: HEAD_DIM * (hh // (N_HEADS // KV_HEADS)))
    k = _dotf(h, wk[...])
    k_o[...] = k * lax.rsqrt(_group_sumsq(k, bd) * (1.0 / HEAD_DIM) + EPS) * gk[...]
    v_o[...] = _dotf(h, wv[...])
    zt_o[...] = _dot_nt(wzt[...], h)


def _fox_in_body(x_ref, sc_ref, sh_ref, g_ref, gk, bf, gq, wqt, wk, wv, wf, wzt, bd,
                 qp_o, k_o, v_o, lf_o, zt_o):
    h = _prenorm(x_ref, g_ref, sc_ref, sh_ref)
    _q_heads_to_qp(_dot_nt(wqt[...], h), gq, qp_o, lambda hh: HEAD_DIM * (hh % 2),
                   lambda hh: (128 + 3 * (hh % 2), 128 + 3 * (hh % 2) + 3))
    k = _dotf(h, wk[...])
    k_o[...] = k * lax.rsqrt(_group_sumsq(k, bd) * (1.0 / HEAD_DIM) + EPS) * gk[...]
    v_o[...] = _dotf(h, wv[...])
    f = _dotf(h, wf[...]) + bf[...]
    lf_o[...] = jnp.minimum(f, 0.0) - jnp.log1p(jnp.exp(-jnp.abs(f)))
    zt_o[...] = _dot_nt(wzt[...], h)


def _out_body(x_ref, gate_ref, ot_ref, zt_ref, wo, o_ref):
    u = (ot_ref[...] * _silu(zt_ref[...])).astype(BF16)
    o_ref[...] = x_ref[...] + gate_ref[...] * _dot_tn(u, wo[...])


def _out_proj(x, gate, ot, zt, wo):
    r = x.shape[0]
    ts = _row_tile(r)
    (out,) = _proj_call(_out_body, "out_proj", r, ts, [x, gate], [ot, zt], [wo],
                        [_rows_out(r, D_MODEL, ts)])
    return out


def _src_arrays(src):
    past, new = src
    return [new] if past is None else [past, new]


def _src_specs(src):
    past, new = src
    c = new.shape[2]
    if past is None:
        return [pl.BlockSpec((1, TK, c), lambda bb, j: (bb, j, 0))]
    npb = past.shape[1] // TK
    return [pl.BlockSpec((1, TK, c), lambda bb, j: (bb, jnp.minimum(j, npb - 1), 0)),
            pl.BlockSpec((1, TK, c), lambda bb, j: (bb, jnp.maximum(j - npb, 0), 0))]


def _src_rows(src):
    past, new = src
    return new.shape[1] + (0 if past is None else past.shape[1])


def _src_load(refs, src_past_blocks):
    if src_past_blocks is None:
        return refs[0][0]
    return jnp.where(pl.program_id(1) < src_past_blocks, refs[0][0], refs[1][0])


def _split_refs(refs, past_blocks):
    vals, pos = [], 0
    for npb in past_blocks:
        cnt = 1 if npb is None else 2
        vals.append(_src_load(refs[pos:pos + cnt], npb))
        pos += cnt
    return vals, refs[pos:]


def _past_blocks(srcs):
    return tuple(None if s[0] is None else s[0].shape[1] // TK for s in srcs)


def _gqa_prep_body(*refs, past_blocks):
    vals, outs = _split_refs(refs, past_blocks)
    outs[0][0] = vals[0].astype(BF16)
    outs[1][0, 0] = vals[1].T.astype(BF16)
    if len(vals) > 2:
        outs[2][0] = vals[2].astype(BF16)


def _gqa_prep(k, v, ki=None):
    srcs = [k, v] + ([ki] if ki is not None else [])
    b, c = k[1].shape[0], k[1].shape[2]
    lp = _src_rows(k)
    nkb = lp // TK
    out_specs = [pl.BlockSpec((1, TK, c), lambda bb, j: (bb, j, 0)),
                 pl.BlockSpec((1, 1, c, TK), lambda bb, j: (bb, j, 0, 0))]
    out_shape = [jax.ShapeDtypeStruct((b, lp, c), BF16), jax.ShapeDtypeStruct((b, nkb, c, TK), BF16)]
    if ki is not None:
        ci = ki[1].shape[2]
        out_specs.append(pl.BlockSpec((1, TK, ci), lambda bb, j: (bb, j, 0)))
        out_shape.append(jax.ShapeDtypeStruct((b, lp, ci), BF16))
    return pl.pallas_call(
        functools.partial(_gqa_prep_body, past_blocks=_past_blocks(srcs)),
        grid=(b, nkb), in_specs=[sp for s in srcs for sp in _src_specs(s)],
        out_specs=out_specs, out_shape=out_shape,
        compiler_params=_cparams(("arbitrary", "arbitrary")), name="gqa_prep",
    )(*[a for s in srcs for a in _src_arrays(s)])


def _fox_prep_body(*refs, past_blocks):
    (k, v, lf), (sel_ref, kt_o, vt_o, carry_ref) = _split_refs(refs, past_blocks)
    kb = pl.program_id(1)

    @pl.when(kb == 0)
    def _():
        carry_ref[...] = jnp.zeros_like(carry_ref)

    r = lax.broadcasted_iota(I32, (TK, TK), 0)
    c = lax.broadcasted_iota(I32, (TK, TK), 1)
    tri = jnp.where(c <= r, 1.0, 0.0).astype(BF16)
    hi, mid, lo = _split3(lf)
    cum = _dotf(tri, hi) + _dotf(tri, mid) + _dotf(tri, lo) + carry_ref[...]
    carry_ref[...] = cum[TK - 1:TK, :]
    a, b, d = _split3(-(cum * LOG2E))
    ex = _dotf(a, sel_ref[0]) + _dotf(b, sel_ref[1]) + _dotf(d, sel_ref[2])
    for p in range(N_HEADS // 2):
        kt_o[p, :, 0:128] = k[:, 128 * p:128 * (p + 1)].astype(BF16)
        kt_o[p, :, 128:256] = ex[:, 128 * p:128 * (p + 1)].astype(BF16)
    vt_o[0, 0] = v.T.astype(BF16)


def _fox_sel():
    sel = np.zeros((3, 128, N_HEADS * HEAD_DIM), np.float32)
    for h in range(N_HEADS):
        for j in range(3):
            sel[j, h, 128 * (h // 2) + 3 * (h % 2) + j] = 1.0
    return jnp.asarray(sel, BF16)


def _fox_prep(k, v, lf128):
    srcs = [k, v, lf128]
    b, c = k[1].shape[0], k[1].shape[2]
    lp = _src_rows(k)
    nkb = lp // TK
    npair = N_HEADS // 2
    sel = _fox_sel()
    return pl.pallas_call(
        functools.partial(_fox_prep_body, past_blocks=_past_blocks(srcs)),
        grid=(b, nkb),
        in_specs=[sp for s in srcs for sp in _src_specs(s)] + [_full_spec(sel)],
        out_specs=[pl.BlockSpec((npair, TK, KT_LANES), lambda bb, j: (bb, j, 0)),
                   pl.BlockSpec((1, 1, c, TK), lambda bb, j: (bb, j, 0, 0))],
        out_shape=[jax.ShapeDtypeStruct((b * npair, lp, KT_LANES), BF16),
                   jax.ShapeDtypeStruct((b, nkb, c, TK), BF16)],
        scratch_shapes=[pltpu.VMEM((1, 128), F32)],
        compiler_params=_cparams(("arbitrary", "arbitrary")), name="fox_prep",
    )(*[a for s in srcs for a in _src_arrays(s)], sel)


def _visible_end(qpos, mode, n_keys):
    if mode == "causal":
        end = qpos + 1
    else:
        end = ((qpos >> CHUNK_SHIFT) + 1) << CHUNK_SHIFT
    return jnp.minimum(end, n_keys)


def _softmax_update(s, mx8, m_ref, c8=None, lane0=0):
    m8 = m_ref[:, lane0:]
    m_new8 = jnp.maximum(m8, mx8)
    alpha8 = jnp.exp2(m8 - m_new8)
    shift = m_new8[0:1] if c8 is None else (m_new8 - c8)[0:1]
    m_ref[:, lane0:] = m_new8
    return jnp.exp2(s - shift).astype(BF16), alpha8[0:1]


def _pv_and_sum(vt_blk, pb):
    ones = jnp.ones((SUM_ROWS, vt_blk.shape[1]), BF16)
    return _dotf(jnp.concatenate([vt_blk, ones], axis=0), pb)


def _run_two_stage(n_full, n_masked, stage_a, stage_b, trim=lambda j: 0):
    odd = n_full % 2

    @pl.when(odd == 1)
    def _():
        stage_a(0, 0, False, 0)
        stage_b(0, 0, False, 0)

    n_pairs = (n_full - odd) // 2

    @pl.when(n_pairs >= 1)
    def _():
        stage_a(odd, 0, False, 0)

        def body(u, _):
            kb = odd + 2 * u
            stage_b(kb, 0, False, 0)
            stage_a(kb + 1, 1, False, 0)
            stage_b(kb + 1, 1, False, 0)
            stage_a(kb + 2, 0, False, 0)
            return 0

        lax.fori_loop(0, n_pairs - 1, body, 0)
        stage_b(n_full - 2, 0, False, 0)
        stage_a(n_full - 1, 1, False, 0)
        stage_b(n_full - 1, 1, False, 0)
        stage_a(n_full, 0, True, trim(0))

    @pl.when(n_pairs < 1)
    def _():
        stage_a(n_full, 0, True, trim(0))

    for j in range(1, n_masked):
        stage_b(n_full + j - 1, (j - 1) % 2, True, trim(j - 1))
        stage_a(n_full + j, j % 2, True, trim(j))
    stage_b(n_full + n_masked - 1, (n_masked - 1) % 2, True, trim(n_masked - 1))


def _flash_body(qp_ref, kt_ref, vt_ref, o_ref, s_buf, mx_buf, m_ref, acc_ref, *,
                tq, mode, q0, n_keys, paired, n_masked):
    i = pl.program_id(1)
    q_first = q0 + i * tq
    n_full = _visible_end(q_first, mode, n_keys) // TK
    ns = N_HEADS // 2 if paired else 1
    per = 2 if paired else 1
    ws, rows = per * tq, per * HEAD_DIM
    w = ns * ws
    ik = lax.broadcasted_iota(I32, (TK, ws), 0)
    iq = lax.broadcasted_iota(I32, (TK, tq), 1)
    qpos = q_first + (jnp.concatenate([iq] * per, axis=1) if paired else iq)
    qs = [jnp.concatenate([qp_ref[per * si + a] for a in range(per)], axis=1) if paired else qp_ref[si]
          for si in range(ns)]

    m_ref[...] = jnp.full((8, w), NEG, F32)
    acc_ref[...] = jnp.zeros((rows + SUM_ROWS, w), F32)

    def stage_a(kb, slot, masked, lane0):
        assert lane0 == 0 or ns == 1
        k0 = pl.multiple_of(kb * TK, TK)
        wl = ws - lane0
        if masked:
            if lane0:
                kpos = k0 + lax.broadcasted_iota(I32, (TK, wl), 0)
                qp = q_first + lane0 + lax.broadcasted_iota(I32, (TK, wl), 1)
            else:
                kpos, qp = k0 + ik, qpos
            valid = (kpos <= qp) if mode == "causal" else ((kpos >> CHUNK_SHIFT) <= (qp >> CHUNK_SHIFT))
            valid = valid & (kpos < n_keys)
        for si in range(ns):
            s = _dotf(kt_ref[si, pl.ds(k0, TK), :], qs[si][:, lane0:])
            if masked:
                s = jnp.where(valid, s, NEG)
            s_buf[slot, :, si * ws + lane0:(si + 1) * ws] = s
            mx_buf[slot, :, si * ws + lane0:(si + 1) * ws] = jnp.broadcast_to(
                jnp.max(s, axis=0, keepdims=True), (8, wl))

    def stage_b(kb, slot, masked, lane0):
        del masked
        pb, alpha = _softmax_update(s_buf[slot, :, lane0:], mx_buf[slot, :, lane0:], m_ref, lane0=lane0)
        for si in range(ns):
            loc = slice(si * ws, (si + 1) * ws - lane0)
            glob = slice(si * ws + lane0, (si + 1) * ws)
            acc_ref[:, glob] = alpha[:, loc] * acc_ref[:, glob] + _pv_and_sum(
                vt_ref[0, kb, rows * si:rows * (si + 1), :], pb[:, loc])

    trim = (lambda j: j * TK) if (not paired and tq > TK) else (lambda j: 0)
    _run_two_stage(n_full, n_masked, stage_a, stage_b, trim)
    out = acc_ref[0:rows, :] / acc_ref[rows:rows + 1, :]
    for si in range(ns):
        for a in range(per):
            o_ref[per * si + a] = out[HEAD_DIM * a:HEAD_DIM * (a + 1), si * ws + tq * a:si * ws + tq * (a + 1)]


def _visible_end_static(qpos, mode, n_keys):
    end = qpos + 1 if mode == "causal" else ((qpos >> CHUNK_SHIFT) + 1) << CHUNK_SHIFT
    return min(end, n_keys)


def _flash(qp, kt, vt, *, tq, mode, q0, n_real_q, n_keys, paired):
    bh, _, sq = qp.shape
    lp = kt.shape[1]
    nkb = vt.shape[1]
    assert (q0 % TK == 0) and (tq % TK == 0 or sq == tq)
    n_masked = (-(-_visible_end_static(q0 + n_real_q - 1, mode, n_keys) // TK)
                - _visible_end_static(q0, mode, n_keys) // TK)
    body = functools.partial(_flash_body, tq=tq, mode=mode, q0=q0, n_keys=n_keys, paired=paired,
                             n_masked=n_masked)
    if not paired:
        ns, w, rows = 1, tq, HEAD_DIM
        kt_spec = pl.BlockSpec((1, lp, KT_LANES), lambda g, i: (g // 2, 0, 0))
        vt_spec = pl.BlockSpec((1, nkb, HEAD_DIM, TK), lambda g, i: (g // N_HEADS, 0, g % N_HEADS, 0))
    else:
        ns, w, rows = N_HEADS, N_HEADS * tq, 2 * HEAD_DIM
        kt_spec = pl.BlockSpec((ns // 2, lp, KT_LANES), lambda g, i: (g, 0, 0))
        vt_spec = pl.BlockSpec((1, nkb, ns * HEAD_DIM, TK), lambda g, i: (g, 0, 0, 0))
    return pl.pallas_call(
        body,
        grid=(bh // ns, sq // tq),
        in_specs=[pl.BlockSpec((ns, KT_LANES, tq), lambda g, i: (g, 0, i)), kt_spec, vt_spec],
        out_specs=pl.BlockSpec((ns, HEAD_DIM, tq), lambda g, i: (g, 0, i)),
        out_shape=jax.ShapeDtypeStruct((bh, HEAD_DIM, sq), F32),
        scratch_shapes=[pltpu.VMEM((2, TK, w), F32), pltpu.VMEM((2, 8, w), F32), pltpu.VMEM((8, w), F32),
                        pltpu.VMEM((rows + SUM_ROWS, w), F32)],
        compiler_params=_cparams(("arbitrary", "arbitrary")),
        name="flash_" + mode,
    )(qp, kt, vt)


def _swa_body(hp_ref, qp_ref, *refs, tq, q0, n_pieces, n_rows):
    k_refs, v_refs, o_ref = refs[:n_pieces], refs[n_pieces:2 * n_pieces], refs[2 * n_pieces]
    i = pl.program_id(1)
    q_first = q0 + i * tq
    k = jnp.concatenate([r[0].astype(BF16) for r in k_refs], axis=0)
    v = jnp.concatenate([r[0].astype(BF16) for r in v_refs], axis=0)
    kw = k.shape[0]
    row = lax.broadcasted_iota(I32, (kw, tq), 0)
    kpos = (q_first - WINDOW) + row
    qpos = q_first + lax.broadcasted_iota(I32, (kw, tq), 1)
    qc = qpos >> CHUNK_SHIFT
    kc = kpos >> CHUNK_SHIFT
    valid = (kpos >= 0) & (kc <= qc) & (qc - kc <= WIN_CHUNKS) & (row < n_rows)
    dist = jnp.abs(qpos - kpos).astype(F32)
    group = N_HEADS // KV_HEADS

    def lanes(x):
        return jnp.concatenate([x] * group, axis=1)

    for n in range(KV_HEADS):
        h0 = n * group

        def per_head(col):
            return jnp.concatenate(
                [jnp.broadcast_to(hp_ref[h0 + g][:, col:col + 1], (1, tq)) for g in range(group)], axis=1)

        q = jnp.concatenate([qp_ref[h0 + g] for g in range(group)], axis=1)
        slope2, sink2 = per_head(0), per_head(1)
        s = _dotf(k, q) - slope2 * lanes(dist)
        s = jnp.where(lanes(valid), s, NEG)
        m = jnp.maximum(jnp.max(s, axis=0, keepdims=True), sink2)
        p = jnp.exp2(s - m)
        l = jnp.sum(p, axis=0, keepdims=True) + jnp.exp2(sink2 - m)
        acc = _dot_tn(v, p.astype(BF16))
        out = acc[HEAD_DIM * n:HEAD_DIM * (n + 1)] / l
        for g in range(group):
            o_ref[h0 + g] = out[:, g * tq:(g + 1) * tq]


def _swa_attend(hp, qp, k_pieces, v_pieces, *, tq, q0, n_rows):
    b = k_pieces[0][0].shape[0]
    sq = qp.shape[2]
    n_pieces = len(k_pieces)

    def spec(piece):
        _, rows, idx = piece
        return pl.BlockSpec((1, rows, KV_HEADS * HEAD_DIM), lambda bb, i: (bb, idx(i), 0))

    body = functools.partial(_swa_body, tq=tq, q0=q0, n_pieces=n_pieces, n_rows=n_rows)
    return pl.pallas_call(
        body,
        grid=(b, sq // tq),
        in_specs=[_full_spec(hp), pl.BlockSpec((N_HEADS, KT_LANES, tq), lambda bb, i: (bb, 0, i))]
        + [spec(p) for p in k_pieces] + [spec(p) for p in v_pieces],
        out_specs=pl.BlockSpec((N_HEADS, HEAD_DIM, tq), lambda bb, i: (bb, 0, i)),
        out_shape=jax.ShapeDtypeStruct((b * N_HEADS, HEAD_DIM, sq), F32),
        compiler_params=_cparams(("arbitrary", "arbitrary")),
        name="swa_attend",
    )(hp, qp, *[p[0] for p in k_pieces], *[p[0] for p in v_pieces])


def _dsa_body(hp_ref, qi_ref, wi_ref, qp_ref, ki_ref, kt_ref, vt_ref, o_ref, hi_ref, lo_ref,
              s_buf, mx_buf, m_ref, acc_ref, *, tq, q0, n_real_q, n_keys, topk):
    i = pl.program_id(1)
    q_first = q0 + i * tq
    q_last = q_first + (n_real_q - 1)
    n_tot = (_visible_end(q_last, "chunk", n_keys) + (TK - 1)) // TK
    n_past = jnp.minimum(q_first, n_keys) // TK
    ik = lax.broadcasted_iota(I32, (TK, tq), 0)
    iq = lax.broadcasted_iota(I32, (TK, tq), 1)
    qpos = q_first + iq
    tf = float(topk)

    def blk(kb):
        return pl.ds(pl.multiple_of(kb * TK, TK), TK)

    def score_blk(kb, _):
        ki = ki_ref[0, blk(kb), :]
        acc = jnp.zeros((TK, tq), F32)
        for h in range(IDX_HEADS):
            acc = acc + wi_ref[0, h:h + 1, :] * jnp.maximum(_dotf(ki, qi_ref[0, h]), 0.0)
        kpos = kb * TK + ik
        valid = ((kpos >> CHUNK_SHIFT) <= (qpos >> CHUNK_SHIFT)) & (kpos < n_keys)
        bits = pltpu.bitcast(acc, I32)
        key = jnp.where(valid, jnp.where(bits < 0, bits ^ INT_MAX, bits), INT_MIN)
        hi_ref[blk(kb), :] = (key >> 16).astype(I16)
        lo_ref[blk(kb), :] = ((key & 0xFFFF) - 32768).astype(I16)
        return 0

    lax.fori_loop(0, n_tot, score_blk, 0)

    def count_ge(ref, mid):
        mid16 = mid.astype(I16)

        def body(kb, acc):
            ge = jnp.where(ref[blk(kb), :] >= mid16, jnp.int16(1), jnp.int16(0))
            parts = [ge[16 * r:16 * (r + 1)] for r in range(TK // 16)]
            while len(parts) > 1:
                parts = [parts[j] + parts[j + 1] for j in range(0, len(parts), 2)]
            return acc + parts[0]

        acc = lax.fori_loop(0, n_tot, body, jnp.zeros((16, tq), I16))
        return jnp.sum(acc.astype(I32), axis=0, keepdims=True).astype(F32)

    def bisect_step(ref, target, st):
        lo, hi, cl, ch = st
        mid = (lo + hi) >> 1
        cnt = count_ge(ref, mid)
        ge = cnt >= target
        return jnp.where(ge, mid, lo), jnp.where(ge, hi, mid), jnp.where(ge, cnt, cl), jnp.where(ge, ch, cnt)

    qrow = q_first + lax.broadcasted_iota(I32, (1, tq), 1)
    n_vis = _visible_end(qrow, "chunk", n_keys).astype(F32)
    zero = jnp.zeros((1, tq), F32)
    st1 = (jnp.full((1, tq), I16_MIN + 1, I32), jnp.full((1, tq), I16_MAX + 1, I32), n_vis, zero)
    h_thr, _, cl1, ch1 = lax.fori_loop(0, 16, lambda _, st: bisect_step(hi_ref, tf, st), st1)
    h16 = h_thr.astype(I16)

    def mask_lo(kb, _):
        lo_ref[blk(kb), :] = jnp.where(hi_ref[blk(kb), :] == h16, lo_ref[blk(kb), :], jnp.int16(I16_MIN))
        return 0

    lax.fori_loop(0, n_tot, mask_lo, 0)
    t2 = tf - ch1

    def cond(st):
        return (st[0] < 16) & (st[2] > 0.5)

    def body(st):
        lo, hi, cl, ch = bisect_step(lo_ref, t2, st[1])
        done = (cl <= t2) | (hi - lo <= 1)
        return st[0] + 1, (lo, hi, cl, ch), jnp.sum(jnp.where(done, 0.0, 1.0))

    cl2_0 = cl1 - ch1
    st2 = (jnp.full((1, tq), I16_MIN, I32), jnp.full((1, tq), I16_MAX + 1, I32), cl2_0, zero)
    _, (l_thr, _, cl2, ch2), _ = lax.while_loop(
        cond, body, (jnp.int32(0), st2, jnp.sum(jnp.where(cl2_0 > t2, 1.0, 0.0))))
    l16 = l_thr.astype(I16)

    need = t2 - ch2

    @pl.when(jnp.sum(jnp.where(cl2 > t2, 1.0, 0.0)) > 0.5)
    def _():
        r = lax.broadcasted_iota(I32, (TK, TK), 0)
        c = lax.broadcasted_iota(I32, (TK, TK), 1)
        tri = jnp.where(c < r, 1.0, 0.0).astype(BF16)

        def fix(kb, carry):
            hb = hi_ref[blk(kb), :]
            e16 = jnp.where(hb == h16, jnp.where(lo_ref[blk(kb), :] == l16, jnp.int16(1), jnp.int16(0)),
                            jnp.int16(0))
            e = e16.astype(I32).astype(F32)
            before = _dotf(tri, e.astype(BF16)) + carry
            drop = jnp.where((e > 0.5) & (before >= need), 1, 0).astype(I16)
            hi_ref[blk(kb), :] = jnp.where(drop == jnp.int16(1), jnp.int16(I16_MIN), hb)
            return carry + jnp.sum(e, axis=0, keepdims=True)

        lax.fori_loop(0, n_tot, fix, jnp.zeros((1, tq), F32))

    def to_bias(kb, _):
        hb = hi_ref[blk(kb), :]
        zero_b, neg_b = jnp.bfloat16(0.0), jnp.bfloat16(NEG)
        at_thr = jnp.where(lo_ref[blk(kb), :] >= l16, zero_b, neg_b)
        bias = jnp.where(hb > h16, zero_b, jnp.where(hb == h16, at_thr, neg_b))
        hi_ref[blk(kb), :] = pltpu.bitcast(bias, I16)
        return 0

    lax.fori_loop(0, n_tot, to_bias, 0)

    group = N_HEADS // KV_HEADS
    wide = group * tq

    def lanes(x):
        return jnp.concatenate([x] * group, axis=1)

    def group_body(n, _):
        h0 = n * group
        q = jnp.concatenate([qp_ref[h0 + g] for g in range(group)], axis=1)
        slope2 = jnp.concatenate(
            [jnp.broadcast_to(hp_ref[h0 + g][:, 0:1], (1, tq)) for g in range(group)], axis=1)
        slope8 = jnp.broadcast_to(slope2, (8, wide))
        a_tab = slope2 * lanes(ik.astype(F32))
        vrow = pl.ds(pl.multiple_of(n * HEAD_DIM, HEAD_DIM), HEAD_DIM)
        m_ref[...] = jnp.full((8, wide), NEG, F32)
        acc_ref[...] = jnp.zeros((HEAD_DIM + SUM_ROWS, wide), F32)

        def c8(kb):
            return slope8 * (kb * TK - q_first).astype(F32)

        def stage_a(kb, slot, diag, lane0):
            del lane0
            bias = pltpu.bitcast(hi_ref[blk(kb), :], jnp.bfloat16).astype(F32)
            s = _dotf(kt_ref[0, blk(kb), :], q) + lanes(bias)
            if diag:
                kpos = kb * TK + ik
                rel = iq.astype(F32) - jnp.abs(qpos - kpos).astype(F32)
                s = s + slope2 * lanes(rel)
            else:
                s = s + a_tab
            mx8 = jnp.broadcast_to(jnp.max(s, axis=0, keepdims=True), (8, wide))
            s_buf[slot] = s
            mx_buf[slot] = mx8 if diag else mx8 + c8(kb)

        def stage_b(kb, slot, diag, lane0):
            del lane0
            pb, alpha = _softmax_update(s_buf[slot], mx_buf[slot], m_ref, None if diag else c8(kb))
            acc_ref[...] = alpha * acc_ref[...] + _pv_and_sum(vt_ref[0, kb, vrow, :], pb)

        _run_two_stage(n_past, 1, stage_a, stage_b)
        out = acc_ref[0:HEAD_DIM, :] / acc_ref[HEAD_DIM:HEAD_DIM + 1, :]
        for g in range(group):
            o_ref[h0 + g] = out[:, g * tq:(g + 1) * tq]
        return 0

    lax.fori_loop(0, KV_HEADS, group_body, 0)


def _dsa_attend(hp, qi, wi, qp, ki, kt, vt, *, tq, q0, n_real_q, n_keys):
    b = ki.shape[0]
    sq = qp.shape[2]
    lp = kt.shape[1]
    nkb = vt.shape[1]
    topk = min(TOPK_MAX, n_keys // 4)
    assert TK % tq == 0 and q0 % TK == 0 and n_real_q <= tq
    wide = (N_HEADS // KV_HEADS) * tq
    body = functools.partial(_dsa_body, tq=tq, q0=q0, n_real_q=n_real_q, n_keys=n_keys, topk=topk)
    once = pl.Buffered(1)
    return pl.pallas_call(
        body,
        grid=(b, sq // tq),
        in_specs=[_full_spec(hp),
                  pl.BlockSpec((1, IDX_HEADS, IDX_DIM, tq), lambda bb, i: (bb, 0, 0, i)),
                  pl.BlockSpec((1, IDX_HEADS, tq), lambda bb, i: (bb, 0, i)),
                  pl.BlockSpec((N_HEADS, KT_LANES, tq), lambda bb, i: (bb, 0, i)),
                  pl.BlockSpec((1, lp, IDX_DIM), lambda bb, i: (bb, 0, 0), pipeline_mode=once),
                  pl.BlockSpec((1, lp, KT_LANES), lambda bb, i: (bb, 0, 0), pipeline_mode=once),
                  pl.BlockSpec((1, nkb, KV_HEADS * HEAD_DIM, TK), lambda bb, i: (bb, 0, 0, 0),
                               pipeline_mode=once)],
        out_specs=pl.BlockSpec((N_HEADS, HEAD_DIM, tq), lambda bb, i: (bb, 0, i)),
        out_shape=jax.ShapeDtypeStruct((b * N_HEADS, HEAD_DIM, sq), F32),
        scratch_shapes=[pltpu.VMEM((lp, tq), I16), pltpu.VMEM((lp, tq), I16),
                        pltpu.VMEM((2, TK, wide), F32), pltpu.VMEM((2, 8, wide), F32),
                        pltpu.VMEM((8, wide), F32), pltpu.VMEM((HEAD_DIM + SUM_ROWS, wide), F32)],
        compiler_params=_cparams(("arbitrary", "arbitrary")),
        name="dsa_attend",
    )(hp, qi, wi, qp, ki, kt, vt)


def _block_diag(c):
    g = np.arange(c) // HEAD_DIM
    return jnp.asarray(g[:, None] == g[None, :], BF16)


def _tile_gain(g, n):
    return jnp.tile(g.astype(F32), n).reshape(1, -1)


def _col(g):
    return g.astype(F32).reshape(-1, 1)


def _pad_cols(w, n):
    return jnp.pad(w, ((0, 0), (0, n - w.shape[1])))


def _rope_tables(pos):
    half = MLA_ROPE // 2
    inv = ROPE_BASE ** (-jnp.arange(half, dtype=F32) / half)
    ang = pos.astype(F32)[:, None] * inv[None, :]
    return jnp.cos(ang), jnp.sin(ang)


def _alibi_slopes():
    return np.asarray(2.0 ** (-8.0 * np.arange(1, N_HEADS + 1) / N_HEADS), dtype=np.float32)


def _head_params(sinks=None):
    hp = jnp.zeros((N_HEADS, 1, 128), F32)
    hp = hp.at[:, 0, 0].set(jnp.asarray(_alibi_slopes()) * LOG2E)
    if sinks is not None:
        hp = hp.at[:, 0, 1].set(sinks.astype(F32) * LOG2E)
    return hp


class _Stream:
    def __init__(self, batch, seq, past):
        self.b, self.s, self.p = batch, seq, past
        self.r = batch * seq
        self.decode = past > 0
        self.tq = TQ_DEC if self.decode else TQ
        self.tqf = TQ_DEC if self.decode else min(TQ_FLASH, seq)
        self.sq =self.tq if self.decode else seq
        self.n_keys = past + seq
        self.lp = -(-self.n_keys // TK) * TK
        self.pos = past + np.tile(np.arange(seq), batch)

    def pad_keys(self, past_arr, new_arr, n_keys=None, lp=None):
        n_keys = self.n_keys if n_keys is None else n_keys
        lp = self.lp if lp is None else lp
        new_arr = new_arr.reshape(self.b, self.s, -1)
        parts = [new_arr] if past_arr is None else [past_arr.astype(F32), new_arr]
        if lp > n_keys:
            parts.append(jnp.zeros((self.b, lp - n_keys, new_arr.shape[-1]), F32))
        return jnp.concatenate(parts, axis=1) if len(parts) > 1 else new_arr

    def key_source(self, past_arr, new_arr):
        new_arr = new_arr.reshape(self.b, self.s, -1)
        if past_arr is None:
            return (None, new_arr)
        assert past_arr.shape[1] == self.p and self.p % TK == 0
        pad = self.lp - self.n_keys
        return (past_arr.astype(F32), jnp.pad(new_arr, ((0, 0), (0, pad), (0, 0))))

    def qp_blocks(self, qp):
        if not self.decode:
            return qp
        x = qp.reshape(N_HEADS, KT_LANES, self.b, self.s).transpose(2, 0, 1, 3)
        x = jnp.pad(x, ((0, 0), (0, 0), (0, 0), (0, self.tq - self.s)))
        return x.reshape(self.b * N_HEADS, KT_LANES, self.tq)

    def lanes(self, x):
        n, c, _ = x.shape
        if not self.decode:
            return x[None]
        x = x.reshape(n, c, self.b, self.s).transpose(2, 0, 1, 3)
        return jnp.pad(x, ((0, 0), (0, 0), (0, 0), (0, self.tq - self.s)))

    def ot_cols(self, ot):
        if not self.decode:
            return ot.reshape(N_HEADS * HEAD_DIM, self.r)
        x = ot.reshape(self.b, N_HEADS, HEAD_DIM, self.tq)[..., :self.s]
        return x.transpose(1, 2, 0, 3).reshape(N_HEADS * HEAD_DIM, self.r)


def _mod_rows(st, mod_l, row0):
    m = mod_l[row0:row0 + st.b]
    if st.b > 1:
        m = jnp.repeat(m, st.s, axis=0)
    return m[:, :D_MODEL], m[:, D_MODEL:2 * D_MODEL], m[:, 2 * D_MODEL:]


def _mla_layer(st, x, mod, g, w, past):
    shift, scale, gate = mod
    r = st.r
    ts = _row_tile(r)
    cos, sin = _rope_tables(jnp.asarray(st.pos))
    zpad = jnp.zeros((r, 128 - MLA_ROPE), F32)
    cos_p = jnp.concatenate([cos, cos, zpad], axis=1)
    sin_p = jnp.concatenate([-sin, sin, zpad], axis=1)
    cqn, lat, kr, zt = _proj_call(
        _mla_in_body, "mla_in", r, ts,
        [x, scale, shift, cos_p, sin_p, g, w["gqa"], w["gkva"], w["gkr"], w["gkrp"]], [],
        [w["wcq"], w["wckv"], w["wkr"], w["wkrp"], w["wzt"]],
        [_rows_out(r, MLA_Q_LORA, ts, BF16), _rows_out(r, MLA_KV_LORA, ts), _rows_out(r, 128, ts),
         _cols_out(D_MODEL, r, ts)])
    (qp,) = _proj_call(_mla_q_body, "mla_q", r, ts, [cqn], [cos.T, sin.T], [w["wqt"], w["gqn"], w["gqr"]],
                       [_qp_out(r, ts)])
    past_lat, past_kr = (None, None) if past is None else past
    if past_kr is not None:
        past_kr = jnp.pad(past_kr.astype(F32), ((0, 0), (0, 0), (0, 128 - MLA_ROPE)))
    lat_all = st.pad_keys(past_lat, lat).reshape(st.b * st.lp, MLA_KV_LORA)
    kr_all = st.pad_keys(past_kr, kr).reshape(st.b * st.lp, 128)
    rk = st.b * st.lp
    npair = N_HEADS // 2
    nkb = st.lp // TK
    kt, vt = _proj_call(
        _mla_kv_body, "mla_kv", rk, TK, [lat_all, kr_all, w["gkn"]], [], [w["wkn"], w["wvt"], w["bd"]],
        [((npair * st.b, st.lp, KT_LANES), BF16, (npair, TK, KT_LANES), lambda i: (i // nkb, i % nkb, 0)),
         ((rk // TK, N_HEADS * HEAD_DIM, TK), BF16, (1, N_HEADS * HEAD_DIM, TK), lambda i: (i, 0, 0))])
    vt = vt.reshape(st.b, nkb, N_HEADS * HEAD_DIM, TK)
    ot = _flash(st.qp_blocks(qp), kt, vt, tq=st.tqf, mode="chunk", q0=st.p,
                n_real_q=st.tqf if not st.decode else st.s, n_keys=st.n_keys, paired=st.decode)
    x = _out_proj(x, gate, st.ot_cols(ot), zt, w["wo"])
    return x, (lat, kr[:, :MLA_ROPE])


def _dsa_layer(st, x, mod, g, w, past):
    shift, scale, gate = mod
    r = st.r
    ts = _row_tile(r)
    c = KV_HEADS * HEAD_DIM
    qp, k, v, qi, ki, wi, zt = _proj_call(
        _dsa_in_body, "dsa_in", r, ts, [x, scale, shift, g, w["gk"]], [],
        [w["gq"], w["wqt"], w["wk"], w["wv"], w["wqit"], w["wki"], w["wwit"], w["wzt"], w["bd"]],
        [_qp_out(r, ts), _rows_out(r, c, ts), _rows_out(r, c, ts),
         ((IDX_HEADS, IDX_DIM, r), BF16, (IDX_HEADS, IDX_DIM, ts), lambda i: (0, 0, i)),
         _rows_out(r, IDX_DIM, ts), _cols_out(IDX_HEADS, r, ts), _cols_out(D_MODEL, r, ts)])
    if past is None:
        pk = pv = pki = None
    else:
        pk, pv, pki = past[0].reshape(st.b, -1, c), past[1].reshape(st.b, -1, c), past[2]
    kt, vt, kib = _gqa_prep(st.key_source(pk, k), st.key_source(pv, v), st.key_source(pki, ki))
    qi_b = st.lanes(qi)
    wi_b = st.lanes(wi[None])[:, 0]
    ot = _dsa_attend(_head_params(), qi_b, wi_b, st.qp_blocks(qp), kib, kt, vt, tq=st.tq, q0=st.p,
                     n_real_q=st.tq if not st.decode else st.s, n_keys=st.n_keys)
    x = _out_proj(x, gate, st.ot_cols(ot), zt, w["wo"])
    return x, (k, v, ki)


def _swa_layer(st, x, mod, g, w, past):
    shift, scale, gate = mod
    r = st.r
    ts = _row_tile(r)
    c = KV_HEADS * HEAD_DIM
    qp, k, v, zt = _proj_call(
        _swa_in_body, "swa_in", r, ts, [x, scale, shift, g, w["gk"]], [],
        [w["gq"], w["wqt"], w["wk"], w["wv"], w["wzt"], w["bd"]],
        [_qp_out(r, ts), _rows_out(r, c, ts), _rows_out(r, c, ts), _cols_out(D_MODEL, r, ts)])
    k3, v3 = k.reshape(st.b, st.s, c), v.reshape(st.b, st.s, c)
    if past is None:
        per = st.tq // WINDOW
        idx = [lambda i: jnp.maximum(per * i - 1, 0), lambda i: per * i, lambda i: per * i + 1]
        k_pieces = [(k3, WINDOW, f) for f in idx]
        v_pieces = [(v3, WINDOW, f) for f in idx]
        n_rows = WINDOW + st.tq
        new = (k3[:, st.s - WINDOW:], v3[:, st.s - WINDOW:])
    else:
        win = past[0].shape[1]
        assert win == WINDOW and st.s <= WINDOW
        pad = ((0, 0), (0, WINDOW - st.s), (0, 0))
        pk, pv = past[0].reshape(st.b, win, c).astype(F32), past[1].reshape(st.b, win, c).astype(F32)
        zero = lambda i: 0
        k_pieces = [(pk, WINDOW, zero), (jnp.pad(k3, pad), WINDOW, zero)]
        v_pieces = [(pv, WINDOW, zero), (jnp.pad(v3, pad), WINDOW, zero)]
        n_rows = win + st.s
        new = (jnp.concatenate([pk, k3], axis=1)[:, st.s:], jnp.concatenate([pv, v3], axis=1)[:, st.s:])
    ot = _swa_attend(_head_params(w["sinks"]), st.qp_blocks(qp), k_pieces, v_pieces, tq=st.tq, q0=st.p,
                     n_rows=n_rows)
    x = _out_proj(x, gate, st.ot_cols(ot), zt, w["wo"])
    return x, new


def _fox_layer(st, x, mod, g, w, past):
    shift, scale, gate = mod
    r = st.r
    ts = _row_tile(r)
    c = N_HEADS * HEAD_DIM
    qp, k, v, lf, zt = _proj_call(
        _fox_in_body, "fox_in", r, ts, [x, scale, shift, g, w["gk"], w["bf"]], [],
        [w["gq"], w["wqt"], w["wk"], w["wv"], w["wf"], w["wzt"], w["bd"]],
        [_qp_out(r, ts), _rows_out(r, c, ts), _rows_out(r, c, ts), _rows_out(r, 128, ts),
         _cols_out(D_MODEL, r, ts)])
    if past is None:
        pk = pv = plf = None
    else:
        pk, pv = past[0].reshape(st.b, -1, c), past[1].reshape(st.b, -1, c)
        plf = jnp.pad(past[2].astype(F32), ((0, 0), (0, 0), (0, 128 - N_HEADS)))
    kt, vt = _fox_prep(st.key_source(pk, k), st.key_source(pv, v), st.key_source(plf, lf))
    ot = _flash(st.qp_blocks(qp), kt, vt, tq=st.tqf, mode="causal", q0=st.p,
                n_real_q=st.tqf if not st.decode else st.s, n_keys=st.n_keys, paired=st.decode)
    x = _out_proj(x, gate, st.ot_cols(ot), zt, w["wo"])
    return x, (k, v, lf[:, :N_HEADS])


def _prep_weights(mla_w_in, mla_g_qa, mla_w_qb, mla_g_kva, mla_w_kvb, mla_g_qn, mla_g_qr, mla_g_kn,
                  mla_g_kr, mla_w_out, dsa_w_in, dsa_g_q, dsa_g_k, dsa_w_out, swa_w_in, swa_g_q,
                  swa_g_k, swa_sinks, swa_w_out, fox_w_in, fox_b_f, fox_g_q, fox_g_k, fox_w_out):
    bf = lambda a: a.astype(BF16)
    row = lambda a: a.astype(F32).reshape(1, -1)
    half = MLA_ROPE // 2
    c1, c2, c3 = MLA_Q_LORA, MLA_Q_LORA + MLA_KV_LORA, MLA_Q_LORA + MLA_KV_LORA + MLA_ROPE
    wkr = mla_w_in[:, c2:c3]
    wkrp = jnp.concatenate([wkr[:, half:], wkr[:, :half]], axis=1)
    gkr = mla_g_kr.astype(F32)
    gkrp = jnp.concatenate([gkr[half:], gkr[:half]])
    kvb = mla_w_kvb.reshape(MLA_KV_LORA, N_HEADS, MLA_NOPE + HEAD_DIM)
    mla = dict(
        wcq=bf(mla_w_in[:, :c1]), wckv=bf(mla_w_in[:, c1:c2]), wkr=bf(_pad_cols(wkr, 128)),
        wkrp=bf(_pad_cols(wkrp, 128)), wzt=bf(mla_w_in[:, c3:].T),
        gqa=row(mla_g_qa), gkva=row(mla_g_kva), gkr=row(jnp.pad(gkr, (0, 128 - MLA_ROPE))),
        gkrp=row(jnp.pad(gkrp, (0, 128 - MLA_ROPE))),
        wqt=bf(mla_w_qb.T), gqn=_col(mla_g_qn), gqr=_col(mla_g_qr),
        wkn=bf(kvb[:, :, :MLA_NOPE].reshape(MLA_KV_LORA, -1)),
        wvt=bf(kvb[:, :, MLA_NOPE:].reshape(MLA_KV_LORA, -1).T),
        gkn=_tile_gain(mla_g_kn, N_HEADS), bd=_block_diag(BD_LANES), wo=bf(mla_w_out))
    hq, hk = N_HEADS * HEAD_DIM, KV_HEADS * HEAD_DIM
    cuts = np.cumsum([hq, hk, hk, IDX_HEADS * IDX_DIM, IDX_DIM, IDX_HEADS]).tolist()
    dsa = dict(
        wqt=bf(dsa_w_in[:, :cuts[0]].T), wk=bf(dsa_w_in[:, cuts[0]:cuts[1]]),
        wv=bf(dsa_w_in[:, cuts[1]:cuts[2]]), wqit=bf(dsa_w_in[:, cuts[2]:cuts[3]].T),
        wki=bf(dsa_w_in[:, cuts[3]:cuts[4]]), wwit=bf(dsa_w_in[:, cuts[4]:cuts[5]].T),
        wzt=bf(dsa_w_in[:, cuts[5]:].T), gq=_col(dsa_g_q), gk=_tile_gain(dsa_g_k, KV_HEADS),
        bd=_block_diag(BD_LANES), wo=bf(dsa_w_out))
    swa = dict(
        wqt=bf(swa_w_in[:, :hq].T), wk=bf(swa_w_in[:, hq:hq + hk]), wv=bf(swa_w_in[:, hq + hk:hq + 2 * hk]),
        wzt=bf(swa_w_in[:, hq + 2 * hk:].T), gq=_col(swa_g_q), gk=_tile_gain(swa_g_k, KV_HEADS),
        bd=_block_diag(BD_LANES), wo=bf(swa_w_out), sinks=swa_sinks)
    fox = dict(
        wqt=bf(fox_w_in[:, :hq].T), wk=bf(fox_w_in[:, hq:2 * hq]), wv=bf(fox_w_in[:, 2 * hq:3 * hq]),
        wf=bf(_pad_cols(fox_w_in[:, 3 * hq:3 * hq + N_HEADS], 128)), wzt=bf(fox_w_in[:, 3 * hq + N_HEADS:].T),
        bf=row(jnp.pad(fox_b_f.astype(F32), (0, 128 - N_HEADS))), gq=_col(fox_g_q),
        gk=_tile_gain(fox_g_k, N_HEADS), bd=_block_diag(BD_LANES), wo=bf(fox_w_out))
    return [mla, dsa, swa, fox]


def kernel(x_prompt, x_sample, cache_mla_latent, cache_mla_krope, cache_dsa_k, cache_dsa_v, cache_dsa_kidx, state_swa_k, state_swa_v, cache_fox_k, cache_fox_v, cache_fox_logf, c_prompt, c_sample, norm_g, ada_w, ada_b, mla_w_in, mla_g_qa, mla_w_qb, mla_g_kva, mla_w_kvb, mla_g_qn, mla_g_qr, mla_g_kn, mla_g_kr, mla_w_out, dsa_w_in, dsa_g_q, dsa_g_k, dsa_w_out, swa_w_in, swa_g_q, swa_g_k, swa_sinks, swa_w_out, fox_w_in, fox_b_f, fox_g_q, fox_g_k, fox_w_out):
    bp, sp, _ = x_prompt.shape
    bs, ss, _ = x_sample.shape
    past_len = cache_mla_latent.shape[1]
    depth = norm_g.shape[0]
    assert bp == 1 and sp % TQ == 0 and sp % min(TQ_FLASH, sp) == 0 and sp % TS == 0 and (bs * ss) % 8 == 0 and ss <= TQ_DEC
    assert past_len % TK == 0 and past_len >= WINDOW

    weights = _prep_weights(mla_w_in, mla_g_qa, mla_w_qb, mla_g_kva, mla_w_kvb, mla_g_qn, mla_g_qr,
                            mla_g_kn, mla_g_kr, mla_w_out, dsa_w_in, dsa_g_q, dsa_g_k, dsa_w_out,
                            swa_w_in, swa_g_q, swa_g_k, swa_sinks, swa_w_out, fox_w_in, fox_b_f,
                            fox_g_q, fox_g_k, fox_w_out)
    rows = bp + bs
    rows_p = -(-rows // 8) * 8
    c_all = jnp.concatenate([c_prompt, c_sample, jnp.zeros((rows_p - rows, D_MODEL), F32)], axis=0)
    mod = _ada_mod(c_all, ada_w, ada_b)

    st_p = _Stream(bp, sp, 0)
    st_s = _Stream(bs, ss, past_len)
    pasts = ((cache_mla_latent, cache_mla_krope), (cache_dsa_k, cache_dsa_v, cache_dsa_kidx),
             (state_swa_k, state_swa_v), (cache_fox_k, cache_fox_v, cache_fox_logf))
    layers = (_mla_layer, _dsa_layer, _swa_layer, _fox_layer)
    xp = x_prompt.reshape(st_p.r, D_MODEL)
    xs = x_sample.reshape(st_s.r, D_MODEL)
    new_p, new_s = [], []
    for layer in range(depth):
        kind = layer % len(layers)
        g = norm_g[layer].astype(F32).reshape(1, -1)
        xp, n = layers[kind](st_p, xp, _mod_rows(st_p, mod[layer], 0), g, weights[kind], None)
        new_p.append(n)
        xs, n = layers[kind](st_s, xs, _mod_rows(st_s, mod[layer], bp), g, weights[kind], pasts[kind])
        new_s.append(n)

    def shaped(st, new):
        (lat, kr), (dk, dv, dki), (sk, sv), (fk, fv, flf) = new
        b, s = st.b, st.s
        return (lat.reshape(b, s, -1), kr.reshape(b, s, -1),
                dk.reshape(b, s, KV_HEADS, HEAD_DIM), dv.reshape(b, s, KV_HEADS, HEAD_DIM),
                dki.reshape(b, s, -1),
                sk.reshape(b, -1, KV_HEADS, HEAD_DIM), sv.reshape(b, -1, KV_HEADS, HEAD_DIM),
                fk.reshape(b, s, N_HEADS, HEAD_DIM), fv.reshape(b, s, N_HEADS, HEAD_DIM),
                flf.reshape(b, s, -1))

    return (xp.reshape(x_prompt.shape), xs.reshape(x_sample.shape)) + shaped(st_p, new_p) + shaped(st_s, new_s)
```

```python
import functools

import numpy as np
import jax
import jax.numpy as jnp
from jax import lax
from jax.experimental import pallas as pl
from jax.experimental.pallas import tpu as pltpu

F32 = jnp.float32
BF16 = jnp.bfloat16
I32 = jnp.int32

D_MODEL = 1024
HEAD_DIM = 64
N_HEADS = 16
KV_HEADS = 4
CHUNK = 64
CHUNK_SHIFT = 6
WINDOW = 128
WIN_CHUNKS = WINDOW // CHUNK
EPS = 1e-6
ROPE_BASE = 10000.0
MLA_NOPE, MLA_ROPE, MLA_Q_LORA, MLA_KV_LORA = 64, 32, 384, 256
IDX_HEADS, IDX_DIM, TOPK_MAX = 8, 64, 256
LOG2E = 1.4426950408889634
NEG = -1e30
INT_MIN = -(2 ** 31)
INT_MAX = 2 ** 31 - 1
I16 = jnp.int16
I16_MIN, I16_MAX = -(2 ** 15), 2 ** 15 - 1
HI_SPAN, HI_SPAN_BITS = 1024, 11

TS = 256
TQ = 256
TQ_FLASH = 2048
TK = 512
TQ_DEC = 128
BD_LANES = 256
SUM_ROWS = 16
KT_LANES = 256
VMEM_LIMIT = 56 * 1024 * 1024


def _row_tile(r):
    return TS if r % TS == 0 else r


def _cparams(sem, vmem=VMEM_LIMIT):
    return pltpu.CompilerParams(dimension_semantics=sem, vmem_limit_bytes=vmem)


def _dotf(a, b):
    return jnp.dot(a, b, preferred_element_type=F32)


def _dot_nt(a, b):
    return lax.dot_general(a, b, (((1,), (1,)), ((), ())), preferred_element_type=F32)


def _dot_tn(a, b):
    return lax.dot_general(a, b, (((0,), (0,)), ((), ())), preferred_element_type=F32)


def _split3(x):
    hi = x.astype(BF16)
    r = x - hi.astype(F32)
    mid = r.astype(BF16)
    lo = (r - mid.astype(F32)).astype(BF16)
    return hi, mid, lo


def _silu(x):
    return x / (1.0 + jnp.exp(-x))


def _full_spec(arr):
    nd = arr.ndim
    return pl.BlockSpec(arr.shape, lambda *_: (0,) * nd)


def _row_spec(arr, ts):
    if arr.shape[0] == 1:
        return pl.BlockSpec((1, arr.shape[1]), lambda i: (0, 0))
    return pl.BlockSpec((ts, arr.shape[1]), lambda i: (i, 0))


def _col_spec(arr, ts):
    return pl.BlockSpec((arr.shape[0], ts), lambda i: (0, i))


def _ada_body(c_ref, w_ref, b_ref, o_ref):
    a = _silu(c_ref[...])
    w = w_ref[0]
    a_hi = a.astype(BF16)
    a_lo = (a - a_hi.astype(F32)).astype(BF16)
    w_hi = w.astype(BF16)
    w_lo = (w - w_hi.astype(F32)).astype(BF16)
    o_ref[0] = _dotf(a_hi, w_hi) + _dotf(a_hi, w_lo) + _dotf(a_lo, w_hi) + b_ref[0]


def _ada_mod(c_all, ada_w, ada_b):
    depth, d, n3 = ada_w.shape
    bp = c_all.shape[0]
    tn = 768
    return pl.pallas_call(
        _ada_body,
        grid=(depth, n3 // tn),
        in_specs=[
            pl.BlockSpec((bp, d), lambda l, j: (0, 0)),
            pl.BlockSpec((1, d, tn), lambda l, j: (l, 0, j)),
            pl.BlockSpec((1, 1, tn), lambda l, j: (l, 0, j)),
        ],
        out_specs=pl.BlockSpec((1, bp, tn), lambda l, j: (l, 0, j)),
        out_shape=jax.ShapeDtypeStruct((depth, bp, n3), F32),
        compiler_params=_cparams(("arbitrary", "arbitrary")),
        name="ada_mod",
    )(c_all, ada_w, ada_b.reshape(depth, 1, n3))


def _prenorm(x_ref, g_ref, sc_ref, sh_ref):
    x = x_ref[...]
    ms = jnp.mean(x * x, axis=-1, keepdims=True)
    xn = x * lax.rsqrt(ms + EPS) * g_ref[...]
    return (xn * (1.0 + sc_ref[...]) + sh_ref[...]).astype(BF16)


def _group_sumsq(y, bd_ref):
    sq = y * y
    hi = sq.astype(BF16)
    lo = (sq - hi.astype(F32)).astype(BF16)
    bd = bd_ref[...]
    chunks = []
    for c in range(y.shape[1] // BD_LANES):
        sl = slice(c * BD_LANES, (c + 1) * BD_LANES)
        chunks.append(_dotf(hi[:, sl], bd) + _dotf(lo[:, sl], bd))
    return jnp.concatenate(chunks, axis=1) if len(chunks) > 1 else chunks[0]


def _head_rms_rows(q, g_col):
    ms = jnp.mean(q * q, axis=0, keepdims=True)
    return q * lax.rsqrt(ms + EPS) * g_col


def _place_rows(piece, row0, total):
    ts = piece.shape[1]
    parts = []
    if row0 > 0:
        parts.append(jnp.zeros((row0, ts), F32))
    parts.append(piece)
    rest = total - row0 - piece.shape[0]
    if rest > 0:
        parts.append(jnp.zeros((rest, ts), F32))
    return jnp.concatenate(parts, axis=0) if len(parts) > 1 else piece


def _proj_call(body, name, r, ts, row_in, col_in, const_in, outs):
    in_specs = ([_row_spec(a, ts) for a in row_in] + [_col_spec(a, ts) for a in col_in]
                + [_full_spec(a) for a in const_in])
    return pl.pallas_call(
        body,
        grid=(r // ts,),
        in_specs=in_specs,
        out_specs=[pl.BlockSpec(blk, im) for (_, _, blk, im) in outs],
        out_shape=[jax.ShapeDtypeStruct(s, dt) for (s, dt, _, _) in outs],
        compiler_params=_cparams(("arbitrary",)),
        name=name,
    )(*row_in, *col_in, *const_in)


def _rows_out(r, c, ts, dtype=F32):
    return ((r, c), dtype, (ts, c), lambda i: (i, 0))


def _cols_out(c, r, ts, dtype=F32):
    return ((c, r), dtype, (c, ts), lambda i: (0, i))


def _qp_out(r, ts):
    return ((N_HEADS, KT_LANES, r), BF16, (N_HEADS, KT_LANES, ts), lambda i: (0, 0, i))


def _mla_in_body(x_ref, sc_ref, sh_ref, cos_ref, sin_ref, g_ref, gqa, gkva, gkr, gkrp,
                 wcq, wckv, wkr, wkrp, wzt, cqn_o, lat_o, kr_o, zt_o):
    h = _prenorm(x_ref, g_ref, sc_ref, sh_ref)
    cq = _dotf(h, wcq[...])
    cqn_o[...] = (cq * lax.rsqrt(jnp.mean(cq * cq, axis=-1, keepdims=True) + EPS) * gqa[...]).astype(BF16)
    ckv = _dotf(h, wckv[...])
    lat_o[...] = ckv * lax.rsqrt(jnp.mean(ckv * ckv, axis=-1, keepdims=True) + EPS) * gkva[...]
    kr = _dotf(h, wkr[...])
    krp = _dotf(h, wkrp[...])
    inv = lax.rsqrt(jnp.sum(kr * kr, axis=-1, keepdims=True) * (1.0 / MLA_ROPE) + EPS)
    kr_o[...] = (kr * gkr[...] * cos_ref[...] + krp * gkrp[...] * sin_ref[...]) * inv
    zt_o[...] = _dot_nt(wzt[...], h)


def _mla_q_body(cqn_ref, cos_ref, sin_ref, wqt, gqn, gqr, qp_o):
    qt = _dot_nt(wqt[...], cqn_ref[...])
    ts = qt.shape[1]
    sc = (MLA_NOPE + MLA_ROPE) ** -0.5 * LOG2E
    c = cos_ref[...]
    s = sin_ref[...]
    half = MLA_ROPE // 2
    width = MLA_NOPE + MLA_ROPE
    for h in range(N_HEADS):
        qn = _head_rms_rows(qt[width * h:width * h + MLA_NOPE], gqn[...]) * sc
        qr = _head_rms_rows(qt[width * h + MLA_NOPE:width * (h + 1)], gqr[...]) * sc
        x1, x2 = qr[:half], qr[half:]
        o1 = x1 * c - x2 * s
        o2 = x2 * c + x1 * s
        a = h % 2
        pieces = [qn, jnp.zeros((HEAD_DIM, ts), F32)]
        if a:
            pieces = pieces[::-1]
        pieces += [o1, o2, jnp.zeros((KT_LANES - 2 * HEAD_DIM - MLA_ROPE, ts), F32)]
        qp_o[h] = jnp.concatenate(pieces, axis=0).astype(BF16)


def _mla_kv_body(lat_ref, kr_ref, gkn, wkn, wvt, bd, kt_o, vt_o):
    lat = lat_ref[...].astype(BF16)
    kn = _dotf(lat, wkn[...])
    ss = _group_sumsq(kn, bd)
    kn = kn * lax.rsqrt(ss * (1.0 / MLA_NOPE) + EPS) * gkn[...]
    kr = kr_ref[...].astype(BF16)
    for p in range(N_HEADS // 2):
        kt_o[p, :, 0:128] = kn[:, 128 * p:128 * (p + 1)].astype(BF16)
        kt_o[p, :, 128:256] = kr
    vt_o[0] = _dot_nt(wvt[...], lat).astype(BF16)


def _q_heads_to_qp(qt, gq, qp_o, row_of_head, extra_of_head=None):
    sc = HEAD_DIM ** -0.5 * LOG2E
    ts = qt.shape[1]
    for h in range(N_HEADS):
        qh = _head_rms_rows(qt[HEAD_DIM * h:HEAD_DIM * (h + 1)], gq[...]) * sc
        blk = _place_rows(qh, row_of_head(h), KT_LANES)
        if extra_of_head is not None:
            lo, hi = extra_of_head(h)
            rows = lax.broadcasted_iota(I32, (KT_LANES, ts), 0)
            blk = blk + jnp.where((rows >= lo) & (rows < hi), 1.0, 0.0)
        qp_o[h] = blk.astype(BF16)


def _dsa_in_body(x_ref, sc_ref, sh_ref, g_ref, gk, gq, wqt, wk, wv, wqit, wki, wwit, wzt, bd,
                 qp_o, k_o, v_o, qi_o, ki_o, wi_o, zt_o):
    h = _prenorm(x_ref, g_ref, sc_ref, sh_ref)
    _q_heads_to_qp(_dot_nt(wqt[...], h), gq, qp_o, lambda hh: HEAD_DIM * (hh // (N_HEADS // KV_HEADS)))
    k = _dotf(h, wk[...])
    k_o[...] = k * lax.rsqrt(_group_sumsq(k, bd) * (1.0 / HEAD_DIM) + EPS) * gk[...]
    v_o[...] = _dotf(h, wv[...])
    qit = _dot_nt(wqit[...], h)
    for ih in range(IDX_HEADS):
        qi_o[ih] = qit[IDX_DIM * ih:IDX_DIM * (ih + 1)].astype(BF16)
    ki_o[...] = _dotf(h, wki[...])
    wi_o[...] = _dot_nt(wwit[...], h)
    zt_o[...] = _dot_nt(wzt[...], h)


def _swa_in_body(x_ref, sc_ref, sh_ref, g_ref, gk, gq, wqt, wk, wv, wzt, bd, qp_o, k_o, v_o, zt_o):
    h = _prenorm(x_ref, g_ref, sc_ref, sh_ref)
    _q_heads_to_qp(_dot_nt(wqt[...], h), gq, qp_o, lambda hh: HEAD_DIM * (hh // (N_HEADS // KV_HEADS)))
    k = _dotf(h, wk[...])
    k_o[...] = k * lax.rsqrt(_group_sumsq(k, bd) * (1.0 / HEAD_DIM) + EPS) * gk[...]
    v_o[...] = _dotf(h, wv[...])
    zt_o[...] = _dot_nt(wzt[...], h)


def _fox_in_body(x_ref, sc_ref, sh_ref, g_ref, gk, bf, gq, wqt, wk, wv, wf, wzt, bd,
                 qp_o, k_o, v_o, lf_o, zt_o):
    h = _prenorm(x_ref, g_ref, sc_ref, sh_ref)
    _q_heads_to_qp(_dot_nt(wqt[...], h), gq, qp_o, lambda hh: HEAD_DIM * (hh % 2),
                   lambda hh: (128 + 3 * (hh % 2), 128 + 3 * (hh % 2) + 3))
    k = _dotf(h, wk[...])
    k_o[...] = k * lax.rsqrt(_group_sumsq(k, bd) * (1.0 / HEAD_DIM) + EPS) * gk[...]
    v_o[...] = _dotf(h, wv[...])
    f = _dotf(h, wf[...]) + bf[...]
    lf_o[...] = jnp.minimum(f, 0.0) - jnp.log1p(jnp.exp(-jnp.abs(f)))
    zt_o[...] = _dot_nt(wzt[...], h)


def _out_body(x_ref, gate_ref, ot_ref, zt_ref, wo, o_ref):
    u = (ot_ref[...] * _silu(zt_ref[...])).astype(BF16)
    o_ref[...] = x_ref[...] + gate_ref[...] * _dot_tn(u, wo[...])


def _out_proj(x, gate, ot, zt, wo):
    r = x.shape[0]
    ts = _row_tile(r)
    (out,) = _proj_call(_out_body, "out_proj", r, ts, [x, gate], [ot, zt], [wo],
                        [_rows_out(r, D_MODEL, ts)])
    return out


def _src_arrays(src):
    past, new = src
    return [new] if past is None else [past, new]


def _src_specs(src):
    past, new = src
    c = new.shape[2]
    if past is None:
        return [pl.BlockSpec((1, TK, c), lambda bb, j: (bb, j, 0))]
    npb = past.shape[1] // TK
    return [pl.BlockSpec((1, TK, c), lambda bb, j: (bb, jnp.minimum(j, npb - 1), 0)),
            pl.BlockSpec((1, TK, c), lambda bb, j: (bb, jnp.maximum(j - npb, 0), 0))]


def _src_rows(src):
    past, new = src
    return new.shape[1] + (0 if past is None else past.shape[1])


def _src_load(refs, src_past_blocks):
    if src_past_blocks is None:
        return refs[0][0]
    return jnp.where(pl.program_id(1) < src_past_blocks, refs[0][0], refs[1][0])


def _split_refs(refs, past_blocks):
    vals, pos = [], 0
    for npb in past_blocks:
        cnt = 1 if npb is None else 2
        vals.append(_src_load(refs[pos:pos + cnt], npb))
        pos += cnt
    return vals, refs[pos:]


def _past_blocks(srcs):
    return tuple(None if s[0] is None else s[0].shape[1] // TK for s in srcs)


def _gqa_prep_body(*refs, past_blocks):
    vals, outs = _split_refs(refs, past_blocks)
    outs[0][0] = vals[0].astype(BF16)
    outs[1][0, 0] = vals[1].T.astype(BF16)
    if len(vals) > 2:
        outs[2][0] = vals[2].astype(BF16)


def _gqa_prep(k, v, ki=None):
    srcs = [k, v] + ([ki] if ki is not None else [])
    b, c = k[1].shape[0], k[1].shape[2]
    lp = _src_rows(k)
    nkb = lp // TK
    out_specs = [pl.BlockSpec((1, TK, c), lambda bb, j: (bb, j, 0)),
                 pl.BlockSpec((1, 1, c, TK), lambda bb, j: (bb, j, 0, 0))]
    out_shape = [jax.ShapeDtypeStruct((b, lp, c), BF16), jax.ShapeDtypeStruct((b, nkb, c, TK), BF16)]
    if ki is not None:
        ci = ki[1].shape[2]
        out_specs.append(pl.BlockSpec((1, TK, ci), lambda bb, j: (bb, j, 0)))
        out_shape.append(jax.ShapeDtypeStruct((b, lp, ci), BF16))
    return pl.pallas_call(
        functools.partial(_gqa_prep_body, past_blocks=_past_blocks(srcs)),
        grid=(b, nkb), in_specs=[sp for s in srcs for sp in _src_specs(s)],
        out_specs=out_specs, out_shape=out_shape,
        compiler_params=_cparams(("arbitrary", "arbitrary")), name="gqa_prep",
    )(*[a for s in srcs for a in _src_arrays(s)])


def _fox_prep_body(*refs, past_blocks):
    (k, v, lf), (sel_ref, kt_o, vt_o, carry_ref) = _split_refs(refs, past_blocks)
    kb = pl.program_id(1)

    @pl.when(kb == 0)
    def _():
        carry_ref[...] = jnp.zeros_like(carry_ref)

    r = lax.broadcasted_iota(I32, (TK, TK), 0)
    c = lax.broadcasted_iota(I32, (TK, TK), 1)
    tri = jnp.where(c <= r, 1.0, 0.0).astype(BF16)
    hi, mid, lo = _split3(lf)
    cum = _dotf(tri, hi) + _dotf(tri, mid) + _dotf(tri, lo) + carry_ref[...]
    carry_ref[...] = cum[TK - 1:TK, :]
    a, b, d = _split3(-(cum * LOG2E))
    ex = _dotf(a, sel_ref[0]) + _dotf(b, sel_ref[1]) + _dotf(d, sel_ref[2])
    for p in range(N_HEADS // 2):
        kt_o[p, :, 0:128] = k[:, 128 * p:128 * (p + 1)].astype(BF16)
        kt_o[p, :, 128:256] = ex[:, 128 * p:128 * (p + 1)].astype(BF16)
    vt_o[0, 0] = v.T.astype(BF16)


def _fox_sel():
    sel = np.zeros((3, 128, N_HEADS * HEAD_DIM), np.float32)
    for h in range(N_HEADS):
        for j in range(3):
            sel[j, h, 128 * (h // 2) + 3 * (h % 2) + j] = 1.0
    return jnp.asarray(sel, BF16)


def _fox_prep(k, v, lf128):
    srcs = [k, v, lf128]
    b, c = k[1].shape[0], k[1].shape[2]
    lp = _src_rows(k)
    nkb = lp // TK
    npair = N_HEADS // 2
    sel = _fox_sel()
    return pl.pallas_call(
        functools.partial(_fox_prep_body, past_blocks=_past_blocks(srcs)),
        grid=(b, nkb),
        in_specs=[sp for s in srcs for sp in _src_specs(s)] + [_full_spec(sel)],
        out_specs=[pl.BlockSpec((npair, TK, KT_LANES), lambda bb, j: (bb, j, 0)),
                   pl.BlockSpec((1, 1, c, TK), lambda bb, j: (bb, j, 0, 0))],
        out_shape=[jax.ShapeDtypeStruct((b * npair, lp, KT_LANES), BF16),
                   jax.ShapeDtypeStruct((b, nkb, c, TK), BF16)],
        scratch_shapes=[pltpu.VMEM((1, 128), F32)],
        compiler_params=_cparams(("arbitrary", "arbitrary")), name="fox_prep",
    )(*[a for s in srcs for a in _src_arrays(s)], sel)


def _visible_end(qpos, mode, n_keys):
    if mode == "causal":
        end = qpos + 1
    else:
        end = ((qpos >> CHUNK_SHIFT) + 1) << CHUNK_SHIFT
    return jnp.minimum(end, n_keys)


def _softmax_update(s, mx8, m_ref, c8=None, lane0=0):
    m8 = m_ref[:, lane0:]
    m_new8 = jnp.maximum(m8, mx8)
    alpha8 = jnp.exp2(m8 - m_new8)
    shift = m_new8[0:1] if c8 is None else (m_new8 - c8)[0:1]
    m_ref[:, lane0:] = m_new8
    return jnp.exp2(s - shift).astype(BF16), alpha8[0:1]


def _pv_and_sum(vt_blk, pb):
    ones = jnp.ones((SUM_ROWS, vt_blk.shape[1]), BF16)
    return _dotf(jnp.concatenate([vt_blk, ones], axis=0), pb)


def _run_two_stage(n_full, n_masked, stage_a, stage_b, trim=lambda j: 0):
    odd = n_full % 2

    @pl.when(odd == 1)
    def _():
        stage_a(0, 0, False, 0)
        stage_b(0, 0, False, 0)

    n_pairs = (n_full - odd) // 2

    @pl.when(n_pairs >= 1)
    def _():
        stage_a(odd, 0, False, 0)

        def body(u, _):
            kb = odd + 2 * u
            stage_b(kb, 0, False, 0)
            stage_a(kb + 1, 1, False, 0)
            stage_b(kb + 1, 1, False, 0)
            stage_a(kb + 2, 0, False, 0)
            return 0

        lax.fori_loop(0, n_pairs - 1, body, 0)
        stage_b(n_full - 2, 0, False, 0)
        stage_a(n_full - 1, 1, False, 0)
        stage_b(n_full - 1, 1, False, 0)
        stage_a(n_full, 0, True, trim(0))

    @pl.when(n_pairs < 1)
    def _():
        stage_a(n_full, 0, True, trim(0))

    for j in range(1, n_masked):
        stage_b(n_full + j - 1, (j - 1) % 2, True, trim(j - 1))
        stage_a(n_full + j, j % 2, True, trim(j))
    stage_b(n_full + n_masked - 1, (n_masked - 1) % 2, True, trim(n_masked - 1))


def _flash_body(qp_ref, kt_ref, vt_ref, o_ref, s_buf, mx_buf, m_ref, acc_ref, *,
                tq, mode, q0, n_keys, paired, n_masked):
    i = pl.program_id(1)
    q_first = q0 + i * tq
    n_full = _visible_end(q_first, mode, n_keys) // TK
    ns = N_HEADS // 2 if paired else 1
    per = 2 if paired else 1
    ws, rows = per * tq, per * HEAD_DIM
    w = ns * ws
    ik = lax.broadcasted_iota(I32, (TK, ws), 0)
    iq = lax.broadcasted_iota(I32, (TK, tq), 1)
    qpos = q_first + (jnp.concatenate([iq] * per, axis=1) if paired else iq)
    qs = [jnp.concatenate([qp_ref[per * si + a] for a in range(per)], axis=1) if paired else qp_ref[si]
          for si in range(ns)]

    m_ref[...] = jnp.full((8, w), NEG, F32)
    acc_ref[...] = jnp.zeros((rows + SUM_ROWS, w), F32)

    def stage_a(kb, slot, masked, lane0):
        assert lane0 == 0 or ns == 1
        k0 = pl.multiple_of(kb * TK, TK)
        wl = ws - lane0
        if masked:
            if lane0:
                kpos = k0 + lax.broadcasted_iota(I32, (TK, wl), 0)
                qp = q_first + lane0 + lax.broadcasted_iota(I32, (TK, wl), 1)
            else:
                kpos, qp = k0 + ik, qpos
            valid = (kpos <= qp) if mode == "causal" else ((kpos >> CHUNK_SHIFT) <= (qp >> CHUNK_SHIFT))
            valid = valid & (kpos < n_keys)
        for si in range(ns):
            s = _dotf(kt_ref[si, pl.ds(k0, TK), :], qs[si][:, lane0:])
            if masked:
                s = jnp.where(valid, s, NEG)
            s_buf[slot, :, si * ws + lane0:(si + 1) * ws] = s
            mx_buf[slot, :, si * ws + lane0:(si + 1) * ws] = jnp.broadcast_to(
                jnp.max(s, axis=0, keepdims=True), (8, wl))

    def stage_b(kb, slot, masked, lane0):
        del masked
        pb, alpha = _softmax_update(s_buf[slot, :, lane0:], mx_buf[slot, :, lane0:], m_ref, lane0=lane0)
        for si in range(ns):
            loc = slice(si * ws, (si + 1) * ws - lane0)
            glob = slice(si * ws + lane0, (si + 1) * ws)
            acc_ref[:, glob] = alpha[:, loc] * acc_ref[:, glob] + _pv_and_sum(
                vt_ref[0, kb, rows * si:rows * (si + 1), :], pb[:, loc])

    trim = (lambda j: j * TK) if (not paired and tq > TK) else (lambda j: 0)
    _run_two_stage(n_full, n_masked, stage_a, stage_b, trim)
    out = acc_ref[0:rows, :] / acc_ref[rows:rows + 1, :]
    for si in range(ns):
        for a in range(per):
            o_ref[per * si + a] = out[HEAD_DIM * a:HEAD_DIM * (a + 1), si * ws + tq * a:si * ws + tq * (a + 1)]


def _visible_end_static(qpos, mode, n_keys):
    end = qpos + 1 if mode == "causal" else ((qpos >> CHUNK_SHIFT) + 1) << CHUNK_SHIFT
    return min(end, n_keys)


def _flash(qp, kt, vt, *, tq, mode, q0, n_real_q, n_keys, paired):
    bh, _, sq = qp.shape
    lp = kt.shape[1]
    nkb = vt.shape[1]
    assert (q0 % TK == 0) and (tq % TK == 0 or sq == tq)
    n_masked = (-(-_visible_end_static(q0 + n_real_q - 1, mode, n_keys) // TK)
                - _visible_end_static(q0, mode, n_keys) // TK)
    body = functools.partial(_flash_body, tq=tq, mode=mode, q0=q0, n_keys=n_keys, paired=paired,
                             n_masked=n_masked)
    if not paired:
        ns, w, rows = 1, tq, HEAD_DIM
        kt_spec = pl.BlockSpec((1, lp, KT_LANES), lambda g, i: (g // 2, 0, 0))
        vt_spec = pl.BlockSpec((1, nkb, HEAD_DIM, TK), lambda g, i: (g // N_HEADS, 0, g % N_HEADS, 0))
    else:
        ns, w, rows = N_HEADS, N_HEADS * tq, 2 * HEAD_DIM
        kt_spec = pl.BlockSpec((ns // 2, lp, KT_LANES), lambda g, i: (g, 0, 0))
        vt_spec = pl.BlockSpec((1, nkb, ns * HEAD_DIM, TK), lambda g, i: (g, 0, 0, 0))
    return pl.pallas_call(
        body,
        grid=(bh // ns, sq // tq),
        in_specs=[pl.BlockSpec((ns, KT_LANES, tq), lambda g, i: (g, 0, i)), kt_spec, vt_spec],
        out_specs=pl.BlockSpec((ns, HEAD_DIM, tq), lambda g, i: (g, 0, i)),
        out_shape=jax.ShapeDtypeStruct((bh, HEAD_DIM, sq), F32),
        scratch_shapes=[pltpu.VMEM((2, TK, w), F32), pltpu.VMEM((2, 8, w), F32), pltpu.VMEM((8, w), F32),
                        pltpu.VMEM((rows + SUM_ROWS, w), F32)],
        compiler_params=_cparams(("arbitrary", "arbitrary")),
        name="flash_" + mode,
    )(qp, kt, vt)


def _swa_body(hp_ref, qp_ref, *refs, tq, q0, n_pieces, n_rows):
    k_refs, v_refs, o_ref = refs[:n_pieces], refs[n_pieces:2 * n_pieces], refs[2 * n_pieces]
    i = pl.program_id(1)
    q_first = q0 + i * tq
    k = jnp.concatenate([r[0].astype(BF16) for r in k_refs], axis=0)
    v = jnp.concatenate([r[0].astype(BF16) for r in v_refs], axis=0)
    kw = k.shape[0]
    row = lax.broadcasted_iota(I32, (kw, tq), 0)
    kpos = (q_first - WINDOW) + row
    qpos = q_first + lax.broadcasted_iota(I32, (kw, tq), 1)
    qc = qpos >> CHUNK_SHIFT
    kc = kpos >> CHUNK_SHIFT
    valid = (kpos >= 0) & (kc <= qc) & (qc - kc <= WIN_CHUNKS) & (row < n_rows)
    dist = jnp.abs(qpos - kpos).astype(F32)
    group = N_HEADS // KV_HEADS

    def lanes(x):
        return jnp.concatenate([x] * group, axis=1)

    for n in range(KV_HEADS):
        h0 = n * group

        def per_head(col):
            return jnp.concatenate(
                [jnp.broadcast_to(hp_ref[h0 + g][:, col:col + 1], (1, tq)) for g in range(group)], axis=1)

        q = jnp.concatenate([qp_ref[h0 + g] for g in range(group)], axis=1)
        slope2, sink2 = per_head(0), per_head(1)
        s = _dotf(k, q) - slope2 * lanes(dist)
        s = jnp.where(lanes(valid), s, NEG)
        m = jnp.maximum(jnp.max(s, axis=0, keepdims=True), sink2)
        p = jnp.exp2(s - m)
        l = jnp.sum(p, axis=0, keepdims=True) + jnp.exp2(sink2 - m)
        acc = _dot_tn(v, p.astype(BF16))
        out = acc[HEAD_DIM * n:HEAD_DIM * (n + 1)] / l
        for g in range(group):
            o_ref[h0 + g] = out[:, g * tq:(g + 1) * tq]


def _swa_attend(hp, qp, k_pieces, v_pieces, *, tq, q0, n_rows):
    b = k_pieces[0][0].shape[0]
    sq = qp.shape[2]
    n_pieces = len(k_pieces)

    def spec(piece):
        _, rows, idx = piece
        return pl.BlockSpec((1, rows, KV_HEADS * HEAD_DIM), lambda bb, i: (bb, idx(i), 0))

    body = functools.partial(_swa_body, tq=tq, q0=q0, n_pieces=n_pieces, n_rows=n_rows)
    return pl.pallas_call(
        body,
        grid=(b, sq // tq),
        in_specs=[_full_spec(hp), pl.BlockSpec((N_HEADS, KT_LANES, tq), lambda bb, i: (bb, 0, i))]
        + [spec(p) for p in k_pieces] + [spec(p) for p in v_pieces],
        out_specs=pl.BlockSpec((N_HEADS, HEAD_DIM, tq), lambda bb, i: (bb, 0, i)),
        out_shape=jax.ShapeDtypeStruct((b * N_HEADS, HEAD_DIM, sq), F32),
        compiler_params=_cparams(("arbitrary", "arbitrary")),
        name="swa_attend",
    )(hp, qp, *[p[0] for p in k_pieces], *[p[0] for p in v_pieces])


def _dsa_body(hp_ref, qi_ref, wi_ref, qp_ref, ki_ref, kt_ref, vt_ref, o_ref, hi_ref, lo_ref,
              s_buf, mx_buf, m_ref, acc_ref, *, tq, q0, n_real_q, n_keys, topk):
    i = pl.program_id(1)
    q_first = q0 + i * tq
    q_last = q_first + (n_real_q - 1)
    n_tot = (_visible_end(q_last, "chunk", n_keys) + (TK - 1)) // TK
    n_past = jnp.minimum(q_first, n_keys) // TK
    ik = lax.broadcasted_iota(I32, (TK, tq), 0)
    iq = lax.broadcasted_iota(I32, (TK, tq), 1)
    qpos = q_first + iq
    tf = float(topk)

    def blk(kb):
        return pl.ds(pl.multiple_of(kb * TK, TK), TK)

    def to_key(x):
        bits = pltpu.bitcast(x, I32)
        return jnp.where(bits < 0, bits ^ INT_MAX, bits)

    def score_blk(kb, smax, diag):
        ki = ki_ref[0, blk(kb), :]
        acc = jnp.zeros((TK, tq), F32)
        for h in range(IDX_HEADS):
            acc = acc + wi_ref[0, h:h + 1, :] * jnp.maximum(_dotf(ki, qi_ref[0, h]), 0.0)
        key = to_key(acc)
        if diag:
            kpos = kb * TK + ik
            valid = ((kpos >> CHUNK_SHIFT) <= (qpos >> CHUNK_SHIFT)) & (kpos < n_keys)
            key = jnp.where(valid, key, INT_MIN)
            acc = jnp.where(valid, acc, -jnp.inf)
        hi_ref[blk(kb), :] = (key >> 16).astype(I16)
        lo_ref[blk(kb), :] = ((key & 0xFFFF) - 32768).astype(I16)
        return jnp.maximum(smax, jnp.max(acc, axis=0, keepdims=True))

    smax = lax.fori_loop(0, n_past, lambda kb, c: score_blk(kb, c, False), jnp.full((1, tq), -jnp.inf, F32))
    smax = lax.fori_loop(n_past, n_tot, lambda kb, c: score_blk(kb, c, True), smax)

    def count_ge(ref, mid):
        mid16 = mid.astype(I16)

        def body(kb, acc):
            ge = jnp.where(ref[blk(kb), :] >= mid16, jnp.int16(1), jnp.int16(0))
            parts = [ge[16 * r:16 * (r + 1)] for r in range(TK // 16)]
            while len(parts) > 1:
                parts = [parts[j] + parts[j + 1] for j in range(0, len(parts), 2)]
            return acc + parts[0]

        acc = lax.fori_loop(0, n_tot, body, jnp.zeros((16, tq), I16))
        return jnp.sum(acc.astype(I32), axis=0, keepdims=True).astype(F32)

    def bisect_step(ref, target, st):
        lo, hi, cl, ch = st
        mid = (lo + hi) >> 1
        cnt = count_ge(ref, mid)
        ge = cnt >= target
        return jnp.where(ge, mid, lo), jnp.where(ge, hi, mid), jnp.where(ge, cnt, cl), jnp.where(ge, ch, cnt)

    qrow = q_first + lax.broadcasted_iota(I32, (1, tq), 1)
    n_vis = _visible_end(qrow, "chunk", n_keys).astype(F32)
    zero = jnp.zeros((1, tq), F32)
    h_max = to_key(smax) >> 16
    lo_try = jnp.maximum(h_max - HI_SPAN, I16_MIN + 1)
    cnt_try = count_ge(hi_ref, lo_try)
    ok = cnt_try >= tf
    st1 = (jnp.where(ok, lo_try, I16_MIN + 1), h_max + 1, jnp.where(ok, cnt_try, n_vis), zero)
    n_pass = jnp.where(jnp.min(jnp.where(ok, 1.0, 0.0)) > 0.5, HI_SPAN_BITS, 16)
    h_thr, _, cl1, ch1 = lax.fori_loop(0, n_pass, lambda _, st: bisect_step(hi_ref, tf, st), st1)
    h16 = h_thr.astype(I16)

    def mask_lo(kb, _):
        lo_ref[blk(kb), :] = jnp.where(hi_ref[blk(kb), :] == h16, lo_ref[blk(kb), :], jnp.int16(I16_MIN))
        return 0

    lax.fori_loop(0, n_tot, mask_lo, 0)
    t2 = tf - ch1

    def cond(st):
        return (st[0] < 16) & (st[2] > 0.5)

    def body(st):
        lo, hi, cl, ch = bisect_step(lo_ref, t2, st[1])
        done = (cl <= t2) | (hi - lo <= 1)
        return st[0] + 1, (lo, hi, cl, ch), jnp.sum(jnp.where(done, 0.0, 1.0))

    cl2_0 = cl1 - ch1
    st2 = (jnp.full((1, tq), I16_MIN, I32), jnp.full((1, tq), I16_MAX + 1, I32), cl2_0, zero)
    _, (l_thr, _, cl2, ch2), _ = lax.while_loop(
        cond, body, (jnp.int32(0), st2, jnp.sum(jnp.where(cl2_0 > t2, 1.0, 0.0))))
    l16 = l_thr.astype(I16)

    need = t2 - ch2

    @pl.when(jnp.sum(jnp.where(cl2 > t2, 1.0, 0.0)) > 0.5)
    def _():
        r = lax.broadcasted_iota(I32, (TK, TK), 0)
        c = lax.broadcasted_iota(I32, (TK, TK), 1)
        tri = jnp.where(c < r, 1.0, 0.0).astype(BF16)

        def fix(kb, carry):
            hb = hi_ref[blk(kb), :]
            e16 = jnp.where(hb == h16, jnp.where(lo_ref[blk(kb), :] == l16, jnp.int16(1), jnp.int16(0)),
                            jnp.int16(0))
            e = e16.astype(I32).astype(F32)
            before = _dotf(tri, e.astype(BF16)) + carry
            drop = jnp.where((e > 0.5) & (before >= need), 1, 0).astype(I16)
            hi_ref[blk(kb), :] = jnp.where(drop == jnp.int16(1), jnp.int16(I16_MIN), hb)
            return carry + jnp.sum(e, axis=0, keepdims=True)

        lax.fori_loop(0, n_tot, fix, jnp.zeros((1, tq), F32))

    def to_bias(kb, _):
        hb = hi_ref[blk(kb), :]
        zero_b, neg_b = jnp.bfloat16(0.0), jnp.bfloat16(NEG)
        at_thr = jnp.where(lo_ref[blk(kb), :] >= l16, zero_b, neg_b)
        bias = jnp.where(hb > h16, zero_b, jnp.where(hb == h16, at_thr, neg_b))
        hi_ref[blk(kb), :] = pltpu.bitcast(bias, I16)
        return 0

    lax.fori_loop(0, n_tot, to_bias, 0)

    group = N_HEADS // KV_HEADS
    wide = group * tq

    def lanes(x):
        return jnp.concatenate([x] * group, axis=1)

    def group_body(n, _):
        h0 = n * group
        q = jnp.concatenate([qp_ref[h0 + g] for g in range(group)], axis=1)
        slope2 = jnp.concatenate(
            [jnp.broadcast_to(hp_ref[h0 + g][:, 0:1], (1, tq)) for g in range(group)], axis=1)
        slope8 = jnp.broadcast_to(slope2, (8, wide))
        a_tab = slope2 * lanes(ik.astype(F32))
        vrow = pl.ds(pl.multiple_of(n * HEAD_DIM, HEAD_DIM), HEAD_DIM)
        m_ref[...] = jnp.full((8, wide), NEG, F32)
        acc_ref[...] = jnp.zeros((HEAD_DIM + SUM_ROWS, wide), F32)

        def c8(kb):
            return slope8 * (kb * TK - q_first).astype(F32)

        def stage_a(kb, slot, diag, lane0):
            del lane0
            bias = pltpu.bitcast(hi_ref[blk(kb), :], jnp.bfloat16).astype(F32)
            s = _dotf(kt_ref[0, blk(kb), :], q) + lanes(bias)
            if diag:
                kpos = kb * TK + ik
                rel = iq.astype(F32) - jnp.abs(qpos - kpos).astype(F32)
                s = s + slope2 * lanes(rel)
            else:
                s = s + a_tab
            mx8 = jnp.broadcast_to(jnp.max(s, axis=0, keepdims=True), (8, wide))
            s_buf[slot] = s
            mx_buf[slot] = mx8 if diag else mx8 + c8(kb)

        def stage_b(kb, slot, diag, lane0):
            del lane0
            pb, alpha = _softmax_update(s_buf[slot], mx_buf[slot], m_ref, None if diag else c8(kb))
            acc_ref[...] = alpha * acc_ref[...] + _pv_and_sum(vt_ref[0, kb, vrow, :], pb)

        _run_two_stage(n_past, 1, stage_a, stage_b)
        out = acc_ref[0:HEAD_DIM, :] / acc_ref[HEAD_DIM:HEAD_DIM + 1, :]
        for g in range(group):
            o_ref[h0 + g] = out[:, g * tq:(g + 1) * tq]
        return 0

    lax.fori_loop(0, KV_HEADS, group_body, 0)


def _dsa_attend(hp, qi, wi, qp, ki, kt, vt, *, tq, q0, n_real_q, n_keys):
    b = ki.shape[0]
    sq = qp.shape[2]
    lp = kt.shape[1]
    nkb = vt.shape[1]
    topk = min(TOPK_MAX, n_keys // 4)
    assert TK % tq == 0 and q0 % TK == 0 and n_real_q <= tq
    wide = (N_HEADS // KV_HEADS) * tq
    body = functools.partial(_dsa_body, tq=tq, q0=q0, n_real_q=n_real_q, n_keys=n_keys, topk=topk)
    once = pl.Buffered(1)
    return pl.pallas_call(
        body,
        grid=(b, sq // tq),
        in_specs=[_full_spec(hp),
                  pl.BlockSpec((1, IDX_HEADS, IDX_DIM, tq), lambda bb, i: (bb, 0, 0, i)),
                  pl.BlockSpec((1, IDX_HEADS, tq), lambda bb, i: (bb, 0, i)),
                  pl.BlockSpec((N_HEADS, KT_LANES, tq), lambda bb, i: (bb, 0, i)),
                  pl.BlockSpec((1, lp, IDX_DIM), lambda bb, i: (bb, 0, 0), pipeline_mode=once),
                  pl.BlockSpec((1, lp, KT_LANES), lambda bb, i: (bb, 0, 0), pipeline_mode=once),
                  pl.BlockSpec((1, nkb, KV_HEADS * HEAD_DIM, TK), lambda bb, i: (bb, 0, 0, 0),
                               pipeline_mode=once)],
        out_specs=pl.BlockSpec((N_HEADS, HEAD_DIM, tq), lambda bb, i: (bb, 0, i)),
        out_shape=jax.ShapeDtypeStruct((b * N_HEADS, HEAD_DIM, sq), F32),
        scratch_shapes=[pltpu.VMEM((lp, tq), I16), pltpu.VMEM((lp, tq), I16),
                        pltpu.VMEM((2, TK, wide), F32), pltpu.VMEM((2, 8, wide), F32),
                        pltpu.VMEM((8, wide), F32), pltpu.VMEM((HEAD_DIM + SUM_ROWS, wide), F32)],
        compiler_params=_cparams(("arbitrary", "arbitrary")),
        name="dsa_attend",
    )(hp, qi, wi, qp, ki, kt, vt)


def _block_diag(c):
    g = np.arange(c) // HEAD_DIM
    return jnp.asarray(g[:, None] == g[None, :], BF16)


def _tile_gain(g, n):
    return jnp.tile(g.astype(F32), n).reshape(1, -1)


def _col(g):
    return g.astype(F32).reshape(-1, 1)


def _pad_cols(w, n):
    return jnp.pad(w, ((0, 0), (0, n - w.shape[1])))


def _rope_tables(pos):
    half = MLA_ROPE // 2
    inv = ROPE_BASE ** (-jnp.arange(half, dtype=F32) / half)
    ang = pos.astype(F32)[:, None] * inv[None, :]
    return jnp.cos(ang), jnp.sin(ang)


def _alibi_slopes():
    return np.asarray(2.0 ** (-8.0 * np.arange(1, N_HEADS + 1) / N_HEADS), dtype=np.float32)


def _head_params(sinks=None):
    hp = jnp.zeros((N_HEADS, 1, 128), F32)
    hp = hp.at[:, 0, 0].set(jnp.asarray(_alibi_slopes()) * LOG2E)
    if sinks is not None:
        hp = hp.at[:, 0, 1].set(sinks.astype(F32) * LOG2E)
    return hp


class _Stream:
    def __init__(self, batch, seq, past):
        self.b, self.s, self.p = batch, seq, past
        self.r = batch * seq
        self.decode = past > 0
        self.tq = TQ_DEC if self.decode else TQ
        self.tqf = TQ_DEC if self.decode else min(TQ_FLASH, seq)
        self.sq =self.tq if self.decode else seq
        self.n_keys = past + seq
        self.lp = -(-self.n_keys // TK) * TK
        self.pos = past + np.tile(np.arange(seq), batch)

    def pad_keys(self, past_arr, new_arr, n_keys=None, lp=None):
        n_keys = self.n_keys if n_keys is None else n_keys
        lp = self.lp if lp is None else lp
        new_arr = new_arr.reshape(self.b, self.s, -1)
        parts = [new_arr] if past_arr is None else [past_arr.astype(F32), new_arr]
        if lp > n_keys:
            parts.append(jnp.zeros((self.b, lp - n_keys, new_arr.shape[-1]), F32))
        return jnp.concatenate(parts, axis=1) if len(parts) > 1 else new_arr

    def key_source(self, past_arr, new_arr):
        new_arr = new_arr.reshape(self.b, self.s, -1)
        if past_arr is None:
            return (None, new_arr)
        assert past_arr.shape[1] == self.p and self.p % TK == 0
        pad = self.lp - self.n_keys
        return (past_arr.astype(F32), jnp.pad(new_arr, ((0, 0), (0, pad), (0, 0))))

    def qp_blocks(self, qp):
        if not self.decode:
            return qp
        x = qp.reshape(N_HEADS, KT_LANES, self.b, self.s).transpose(2, 0, 1, 3)
        x = jnp.pad(x, ((0, 0), (0, 0), (0, 0), (0, self.tq - self.s)))
        return x.reshape(self.b * N_HEADS, KT_LANES, self.tq)

    def lanes(self, x):
        n, c, _ = x.shape
        if not self.decode:
            return x[None]
        x = x.reshape(n, c, self.b, self.s).transpose(2, 0, 1, 3)
        return jnp.pad(x, ((0, 0), (0, 0), (0, 0), (0, self.tq - self.s)))

    def ot_cols(self, ot):
        if not self.decode:
            return ot.reshape(N_HEADS * HEAD_DIM, self.r)
        x = ot.reshape(self.b, N_HEADS, HEAD_DIM, self.tq)[..., :self.s]
        return x.transpose(1, 2, 0, 3).reshape(N_HEADS * HEAD_DIM, self.r)


def _mod_rows(st, mod_l, row0):
    m = mod_l[row0:row0 + st.b]
    if st.b > 1:
        m = jnp.repeat(m, st.s, axis=0)
    return m[:, :D_MODEL], m[:, D_MODEL:2 * D_MODEL], m[:, 2 * D_MODEL:]


def _mla_layer(st, x, mod, g, w, past):
    shift, scale, gate = mod
    r = st.r
    ts = _row_tile(r)
    cos, sin = _rope_tables(jnp.asarray(st.pos))
    zpad = jnp.zeros((r, 128 - MLA_ROPE), F32)
    cos_p = jnp.concatenate([cos, cos, zpad], axis=1)
    sin_p = jnp.concatenate([-sin, sin, zpad], axis=1)
    cqn, lat, kr, zt = _proj_call(
        _mla_in_body, "mla_in", r, ts,
        [x, scale, shift, cos_p, sin_p, g, w["gqa"], w["gkva"], w["gkr"], w["gkrp"]], [],
        [w["wcq"], w["wckv"], w["wkr"], w["wkrp"], w["wzt"]],
        [_rows_out(r, MLA_Q_LORA, ts, BF16), _rows_out(r, MLA_KV_LORA, ts), _rows_out(r, 128, ts),
         _cols_out(D_MODEL, r, ts)])
    (qp,) = _proj_call(_mla_q_body, "mla_q", r, ts, [cqn], [cos.T, sin.T], [w["wqt"], w["gqn"], w["gqr"]],
                       [_qp_out(r, ts)])
    past_lat, past_kr = (None, None) if past is None else past
    if past_kr is not None:
        past_kr = jnp.pad(past_kr.astype(F32), ((0, 0), (0, 0), (0, 128 - MLA_ROPE)))
    lat_all = st.pad_keys(past_lat, lat).reshape(st.b * st.lp, MLA_KV_LORA)
    kr_all = st.pad_keys(past_kr, kr).reshape(st.b * st.lp, 128)
    rk = st.b * st.lp
    npair = N_HEADS // 2
    nkb = st.lp // TK
    kt, vt = _proj_call(
        _mla_kv_body, "mla_kv", rk, TK, [lat_all, kr_all, w["gkn"]], [], [w["wkn"], w["wvt"], w["bd"]],
        [((npair * st.b, st.lp, KT_LANES), BF16, (npair, TK, KT_LANES), lambda i: (i // nkb, i % nkb, 0)),
         ((rk // TK, N_HEADS * HEAD_DIM, TK), BF16, (1, N_HEADS * HEAD_DIM, TK), lambda i: (i, 0, 0))])
    vt = vt.reshape(st.b, nkb, N_HEADS * HEAD_DIM, TK)
    ot = _flash(st.qp_blocks(qp), kt, vt, tq=st.tqf, mode="chunk", q0=st.p,
                n_real_q=st.tqf if not st.decode else st.s, n_keys=st.n_keys, paired=st.decode)
    x = _out_proj(x, gate, st.ot_cols(ot), zt, w["wo"])
    return x, (lat, kr[:, :MLA_ROPE])


def _dsa_layer(st, x, mod, g, w, past):
    shift, scale, gate = mod
    r = st.r
    ts = _row_tile(r)
    c = KV_HEADS * HEAD_DIM
    qp, k, v, qi, ki, wi, zt = _proj_call(
        _dsa_in_body, "dsa_in", r, ts, [x, scale, shift, g, w["gk"]], [],
        [w["gq"], w["wqt"], w["wk"], w["wv"], w["wqit"], w["wki"], w["wwit"], w["wzt"], w["bd"]],
        [_qp_out(r, ts), _rows_out(r, c, ts), _rows_out(r, c, ts),
         ((IDX_HEADS, IDX_DIM, r), BF16, (IDX_HEADS, IDX_DIM, ts), lambda i: (0, 0, i)),
         _rows_out(r, IDX_DIM, ts), _cols_out(IDX_HEADS, r, ts), _cols_out(D_MODEL, r, ts)])
    if past is None:
        pk = pv = pki = None
    else:
        pk, pv, pki = past[0].reshape(st.b, -1, c), past[1].reshape(st.b, -1, c), past[2]
    kt, vt, kib = _gqa_prep(st.key_source(pk, k), st.key_source(pv, v), st.key_source(pki, ki))
    qi_b = st.lanes(qi)
    wi_b = st.lanes(wi[None])[:, 0]
    ot = _dsa_attend(_head_params(), qi_b, wi_b, st.qp_blocks(qp), kib, kt, vt, tq=st.tq, q0=st.p,
                     n_real_q=st.tq if not st.decode else st.s, n_keys=st.n_keys)
    x = _out_proj(x, gate, st.ot_cols(ot), zt, w["wo"])
    return x, (k, v, ki)


def _swa_layer(st, x, mod, g, w, past):
    shift, scale, gate = mod
    r = st.r
    ts = _row_tile(r)
    c = KV_HEADS * HEAD_DIM
    qp, k, v, zt = _proj_call(
        _swa_in_body, "swa_in", r, ts, [x, scale, shift, g, w["gk"]], [],
        [w["gq"], w["wqt"], w["wk"], w["wv"], w["wzt"], w["bd"]],
        [_qp_out(r, ts), _rows_out(r, c, ts), _rows_out(r, c, ts), _cols_out(D_MODEL, r, ts)])
    k3, v3 = k.reshape(st.b, st.s, c), v.reshape(st.b, st.s, c)
    if past is None:
        per = st.tq // WINDOW
        idx = [lambda i: jnp.maximum(per * i - 1, 0), lambda i: per * i, lambda i: per * i + 1]
        k_pieces = [(k3, WINDOW, f) for f in idx]
        v_pieces = [(v3, WINDOW, f) for f in idx]
        n_rows = WINDOW + st.tq
        new = (k3[:, st.s - WINDOW:], v3[:, st.s - WINDOW:])
    else:
        win = past[0].shape[1]
        assert win == WINDOW and st.s <= WINDOW
        pad = ((0, 0), (0, WINDOW - st.s), (0, 0))
        pk, pv = past[0].reshape(st.b, win, c).astype(F32), past[1].reshape(st.b, win, c).astype(F32)
        zero = lambda i: 0
        k_pieces = [(pk, WINDOW, zero), (jnp.pad(k3, pad), WINDOW, zero)]
        v_pieces = [(pv, WINDOW, zero), (jnp.pad(v3, pad), WINDOW, zero)]
        n_rows = win + st.s
        new = (jnp.concatenate([pk, k3], axis=1)[:, st.s:], jnp.concatenate([pv, v3], axis=1)[:, st.s:])
    ot = _swa_attend(_head_params(w["sinks"]), st.qp_blocks(qp), k_pieces, v_pieces, tq=st.tq, q0=st.p,
                     n_rows=n_rows)
    x = _out_proj(x, gate, st.ot_cols(ot), zt, w["wo"])
    return x, new


def _fox_layer(st, x, mod, g, w, past):
    shift, scale, gate = mod
    r = st.r
    ts = _row_tile(r)
    c = N_HEADS * HEAD_DIM
    qp, k, v, lf, zt = _proj_call(
        _fox_in_body, "fox_in", r, ts, [x, scale, shift, g, w["gk"], w["bf"]], [],
        [w["gq"], w["wqt"], w["wk"], w["wv"], w["wf"], w["wzt"], w["bd"]],
        [_qp_out(r, ts), _rows_out(r, c, ts), _rows_out(r, c, ts), _rows_out(r, 128, ts),
         _cols_out(D_MODEL, r, ts)])
    if past is None:
        pk = pv = plf = None
    else:
        pk, pv = past[0].reshape(st.b, -1, c), past[1].reshape(st.b, -1, c)
        plf = jnp.pad(past[2].astype(F32), ((0, 0), (0, 0), (0, 128 - N_HEADS)))
    kt, vt = _fox_prep(st.key_source(pk, k), st.key_source(pv, v), st.key_source(plf, lf))
    ot = _flash(st.qp_blocks(qp), kt, vt, tq=st.tqf, mode="causal", q0=st.p,
                n_real_q=st.tqf if not st.decode else st.s, n_keys=st.n_keys, paired=st.decode)
    x = _out_proj(x, gate, st.ot_cols(ot), zt, w["wo"])
    return x, (k, v, lf[:, :N_HEADS])


def _prep_weights(mla_w_in, mla_g_qa, mla_w_qb, mla_g_kva, mla_w_kvb, mla_g_qn, mla_g_qr, mla_g_kn,
                  mla_g_kr, mla_w_out, dsa_w_in, dsa_g_q, dsa_g_k, dsa_w_out, swa_w_in, swa_g_q,
                  swa_g_k, swa_sinks, swa_w_out, fox_w_in, fox_b_f, fox_g_q, fox_g_k, fox_w_out):
    bf = lambda a: a.astype(BF16)
    row = lambda a: a.astype(F32).reshape(1, -1)
    half = MLA_ROPE // 2
    c1, c2, c3 = MLA_Q_LORA, MLA_Q_LORA + MLA_KV_LORA, MLA_Q_LORA + MLA_KV_LORA + MLA_ROPE
    wkr = mla_w_in[:, c2:c3]
    wkrp = jnp.concatenate([wkr[:, half:], wkr[:, :half]], axis=1)
    gkr = mla_g_kr.astype(F32)
    gkrp = jnp.concatenate([gkr[half:], gkr[:half]])
    kvb = mla_w_kvb.reshape(MLA_KV_LORA, N_HEADS, MLA_NOPE + HEAD_DIM)
    mla = dict(
        wcq=bf(mla_w_in[:, :c1]), wckv=bf(mla_w_in[:, c1:c2]), wkr=bf(_pad_cols(wkr, 128)),
        wkrp=bf(_pad_cols(wkrp, 128)), wzt=bf(mla_w_in[:, c3:].T),
        gqa=row(mla_g_qa), gkva=row(mla_g_kva), gkr=row(jnp.pad(gkr, (0, 128 - MLA_ROPE))),
        gkrp=row(jnp.pad(gkrp, (0, 128 - MLA_ROPE))),
        wqt=bf(mla_w_qb.T), gqn=_col(mla_g_qn), gqr=_col(mla_g_qr),
        wkn=bf(kvb[:, :, :MLA_NOPE].reshape(MLA_KV_LORA, -1)),
        wvt=bf(kvb[:, :, MLA_NOPE:].reshape(MLA_KV_LORA, -1).T),
        gkn=_tile_gain(mla_g_kn, N_HEADS), bd=_block_diag(BD_LANES), wo=bf(mla_w_out))
    hq, hk = N_HEADS * HEAD_DIM, KV_HEADS * HEAD_DIM
    cuts = np.cumsum([hq, hk, hk, IDX_HEADS * IDX_DIM, IDX_DIM, IDX_HEADS]).tolist()
    dsa = dict(
        wqt=bf(dsa_w_in[:, :cuts[0]].T), wk=bf(dsa_w_in[:, cuts[0]:cuts[1]]),
        wv=bf(dsa_w_in[:, cuts[1]:cuts[2]]), wqit=bf(dsa_w_in[:, cuts[2]:cuts[3]].T),
        wki=bf(dsa_w_in[:, cuts[3]:cuts[4]]), wwit=bf(dsa_w_in[:, cuts[4]:cuts[5]].T),
        wzt=bf(dsa_w_in[:, cuts[5]:].T), gq=_col(dsa_g_q), gk=_tile_gain(dsa_g_k, KV_HEADS),
        bd=_block_diag(BD_LANES), wo=bf(dsa_w_out))
    swa = dict(
        wqt=bf(swa_w_in[:, :hq].T), wk=bf(swa_w_in[:, hq:hq + hk]), wv=bf(swa_w_in[:, hq + hk:hq + 2 * hk]),
        wzt=bf(swa_w_in[:, hq + 2 * hk:].T), gq=_col(swa_g_q), gk=_tile_gain(swa_g_k, KV_HEADS),
        bd=_block_diag(BD_LANES), wo=bf(swa_w_out), sinks=swa_sinks)
    fox = dict(
        wqt=bf(fox_w_in[:, :hq].T), wk=bf(fox_w_in[:, hq:2 * hq]), wv=bf(fox_w_in[:, 2 * hq:3 * hq]),
        wf=bf(_pad_cols(fox_w_in[:, 3 * hq:3 * hq + N_HEADS], 128)), wzt=bf(fox_w_in[:, 3 * hq + N_HEADS:].T),
        bf=row(jnp.pad(fox_b_f.astype(F32), (0, 128 - N_HEADS))), gq=_col(fox_g_q),
        gk=_tile_gain(fox_g_k, N_HEADS), bd=_block_diag(BD_LANES), wo=bf(fox_w_out))
    return [mla, dsa, swa, fox]


def kernel(x_prompt, x_sample, cache_mla_latent, cache_mla_krope, cache_dsa_k, cache_dsa_v, cache_dsa_kidx, state_swa_k, state_swa_v, cache_fox_k, cache_fox_v, cache_fox_logf, c_prompt, c_sample, norm_g, ada_w, ada_b, mla_w_in, mla_g_qa, mla_w_qb, mla_g_kva, mla_w_kvb, mla_g_qn, mla_g_qr, mla_g_kn, mla_g_kr, mla_w_out, dsa_w_in, dsa_g_q, dsa_g_k, dsa_w_out, swa_w_in, swa_g_q, swa_g_k, swa_sinks, swa_w_out, fox_w_in, fox_b_f, fox_g_q, fox_g_k, fox_w_out):
    bp, sp, _ = x_prompt.shape
    bs, ss, _ = x_sample.shape
    past_len = cache_mla_latent.shape[1]
    depth = norm_g.shape[0]
    assert bp == 1 and sp % TQ == 0 and sp % min(TQ_FLASH, sp) == 0 and sp % TS == 0 and (bs * ss) % 8 == 0 and ss <= TQ_DEC
    assert past_len % TK == 0 and past_len >= WINDOW

    weights = _prep_weights(mla_w_in, mla_g_qa, mla_w_qb, mla_g_kva, mla_w_kvb, mla_g_qn, mla_g_qr,
                            mla_g_kn, mla_g_kr, mla_w_out, dsa_w_in, dsa_g_q, dsa_g_k, dsa_w_out,
                            swa_w_in, swa_g_q, swa_g_k, swa_sinks, swa_w_out, fox_w_in, fox_b_f,
                            fox_g_q, fox_g_k, fox_w_out)
    rows = bp + bs
    rows_p = -(-rows // 8) * 8
    c_all = jnp.concatenate([c_prompt, c_sample, jnp.zeros((rows_p - rows, D_MODEL), F32)], axis=0)
    mod = _ada_mod(c_all, ada_w, ada_b)

    st_p = _Stream(bp, sp, 0)
    st_s = _Stream(bs, ss, past_len)
    pasts = ((cache_mla_latent, cache_mla_krope), (cache_dsa_k, cache_dsa_v, cache_dsa_kidx),
             (state_swa_k, state_swa_v), (cache_fox_k, cache_fox_v, cache_fox_logf))
    layers = (_mla_layer, _dsa_layer, _swa_layer, _fox_layer)
    xp = x_prompt.reshape(st_p.r, D_MODEL)
    xs = x_sample.reshape(st_s.r, D_MODEL)
    new_p, new_s = [], []
    for layer in range(depth):
        kind = layer % len(layers)
        g = norm_g[layer].astype(F32).reshape(1, -1)
        xp, n = layers[kind](st_p, xp, _mod_rows(st_p, mod[layer], 0), g, weights[kind], None)
        new_p.append(n)
        xs, n = layers[kind](st_s, xs, _mod_rows(st_s, mod[layer], bp), g, weights[kind], pasts[kind])
        new_s.append(n)

    def shaped(st, new):
        (lat, kr), (dk, dv, dki), (sk, sv), (fk, fv, flf) = new
        b, s = st.b, st.s
        return (lat.reshape(b, s, -1), kr.reshape(b, s, -1),
                dk.reshape(b, s, KV_HEADS, HEAD_DIM), dv.reshape(b, s, KV_HEADS, HEAD_DIM),
                dki.reshape(b, s, -1),
                sk.reshape(b, -1, KV_HEADS, HEAD_DIM), sv.reshape(b, -1, KV_HEADS, HEAD_DIM),
                fk.reshape(b, s, N_HEADS, HEAD_DIM), fv.reshape(b, s, N_HEADS, HEAD_DIM),
                flf.reshape(b, s, -1))

    return (xp.reshape(x_prompt.shape), xs.reshape(x_sample.shape)) + shaped(st_p, new_p) + shaped(st_s, new_s)
```

```python
import functools

import numpy as np
import jax
import jax.numpy as jnp
from jax import lax
from jax.experimental import pallas as pl
from jax.experimental.pallas import tpu as pltpu

F32 = jnp.float32
BF16 = jnp.bfloat16
I32 = jnp.int32

D_MODEL = 1024
HEAD_DIM = 64
N_HEADS = 16
KV_HEADS = 4
CHUNK = 64
CHUNK_SHIFT = 6
WINDOW = 128
WIN_CHUNKS = WINDOW // CHUNK
EPS = 1e-6
ROPE_BASE = 10000.0
MLA_NOPE, MLA_ROPE, MLA_Q_LORA, MLA_KV_LORA = 64, 32, 384, 256
IDX_HEADS, IDX_DIM, TOPK_MAX = 8, 64, 256
LOG2E = 1.4426950408889634
NEG = -1e30
INT_MIN = -(2 ** 31)
INT_MAX = 2 ** 31 - 1
I16 = jnp.int16
I16_MIN, I16_MAX = -(2 ** 15), 2 ** 15 - 1

TS = 256
TQ = 256
TQ_FLASH = 2048
TK = 512
TQ_DEC = 128
BD_LANES = 256
SUM_ROWS = 16
KT_LANES = 256
VMEM_LIMIT = 56 * 1024 * 1024
DSA_GROUPS = 2
DSA_VMEM_LIMIT = 62 * 1024 * 1024


def _row_tile(r):
    return TS if r % TS == 0 else r


def _cparams(sem, vmem=VMEM_LIMIT):
    return pltpu.CompilerParams(dimension_semantics=sem, vmem_limit_bytes=vmem)


def _dotf(a, b):
    return jnp.dot(a, b, preferred_element_type=F32)


def _dot_nt(a, b):
    return lax.dot_general(a, b, (((1,), (1,)), ((), ())), preferred_element_type=F32)


def _dot_tn(a, b):
    return lax.dot_general(a, b, (((0,), (0,)), ((), ())), preferred_element_type=F32)


def _split3(x):
    hi = x.astype(BF16)
    r = x - hi.astype(F32)
    mid = r.astype(BF16)
    lo = (r - mid.astype(F32)).astype(BF16)
    return hi, mid, lo


def _silu(x):
    return x / (1.0 + jnp.exp(-x))


def _full_spec(arr):
    nd = arr.ndim
    return pl.BlockSpec(arr.shape, lambda *_: (0,) * nd)


def _row_spec(arr, ts):
    if arr.shape[0] == 1:
        return pl.BlockSpec((1, arr.shape[1]), lambda i: (0, 0))
    return pl.BlockSpec((ts, arr.shape[1]), lambda i: (i, 0))


def _col_spec(arr, ts):
    return pl.BlockSpec((arr.shape[0], ts), lambda i: (0, i))


def _ada_body(c_ref, w_ref, b_ref, o_ref):
    a = _silu(c_ref[...])
    w = w_ref[0]
    a_hi = a.astype(BF16)
    a_lo = (a - a_hi.astype(F32)).astype(BF16)
    w_hi = w.astype(BF16)
    w_lo = (w - w_hi.astype(F32)).astype(BF16)
    o_ref[0] = _dotf(a_hi, w_hi) + _dotf(a_hi, w_lo) + _dotf(a_lo, w_hi) + b_ref[0]


def _ada_mod(c_all, ada_w, ada_b):
    depth, d, n3 = ada_w.shape
    bp = c_all.shape[0]
    tn = 768
    return pl.pallas_call(
        _ada_body,
        grid=(depth, n3 // tn),
        in_specs=[
            pl.BlockSpec((bp, d), lambda l, j: (0, 0)),
            pl.BlockSpec((1, d, tn), lambda l, j: (l, 0, j)),
            pl.BlockSpec((1, 1, tn), lambda l, j: (l, 0, j)),
        ],
        out_specs=pl.BlockSpec((1, bp, tn), lambda l, j: (l, 0, j)),
        out_shape=jax.ShapeDtypeStruct((depth, bp, n3), F32),
        compiler_params=_cparams(("arbitrary", "arbitrary")),
        name="ada_mod",
    )(c_all, ada_w, ada_b.reshape(depth, 1, n3))


def _prenorm(x_ref, g_ref, sc_ref, sh_ref):
    x = x_ref[...]
    ms = jnp.mean(x * x, axis=-1, keepdims=True)
    xn = x * lax.rsqrt(ms + EPS) * g_ref[...]
    return (xn * (1.0 + sc_ref[...]) + sh_ref[...]).astype(BF16)


def _group_sumsq(y, bd_ref):
    sq = y * y
    hi = sq.astype(BF16)
    lo = (sq - hi.astype(F32)).astype(BF16)
    bd = bd_ref[...]
    chunks = []
    for c in range(y.shape[1] // BD_LANES):
        sl = slice(c * BD_LANES, (c + 1) * BD_LANES)
        chunks.append(_dotf(hi[:, sl], bd) + _dotf(lo[:, sl], bd))
    return jnp.concatenate(chunks, axis=1) if len(chunks) > 1 else chunks[0]


def _head_rms_rows(q, g_col):
    ms = jnp.mean(q * q, axis=0, keepdims=True)
    return q * lax.rsqrt(ms + EPS) * g_col


def _place_rows(piece, row0, total):
    ts = piece.shape[1]
    parts = []
    if row0 > 0:
        parts.append(jnp.zeros((row0, ts), F32))
    parts.append(piece)
    rest = total - row0 - piece.shape[0]
    if rest > 0:
        parts.append(jnp.zeros((rest, ts), F32))
    return jnp.concatenate(parts, axis=0) if len(parts) > 1 else piece


def _proj_call(body, name, r, ts, row_in, col_in, const_in, outs):
    in_specs = ([_row_spec(a, ts) for a in row_in] + [_col_spec(a, ts) for a in col_in]
                + [_full_spec(a) for a in const_in])
    return pl.pallas_call(
        body,
        grid=(r // ts,),
        in_specs=in_specs,
        out_specs=[pl.BlockSpec(blk, im) for (_, _, blk, im) in outs],
        out_shape=[jax.ShapeDtypeStruct(s, dt) for (s, dt, _, _) in outs],
        compiler_params=_cparams(("arbitrary",)),
        name=name,
    )(*row_in, *col_in, *const_in)


def _rows_out(r, c, ts, dtype=F32):
    return ((r, c), dtype, (ts, c), lambda i: (i, 0))


def _cols_out(c, r, ts, dtype=F32):
    return ((c, r), dtype, (c, ts), lambda i: (0, i))


def _qp_out(r, ts):
    return ((N_HEADS, KT_LANES, r), BF16, (N_HEADS, KT_LANES, ts), lambda i: (0, 0, i))


def _mla_in_body(x_ref, sc_ref, sh_ref, cos_ref, sin_ref, g_ref, gqa, gkva, gkr, gkrp,
                 wcq, wckv, wkr, wkrp, wzt, cqn_o, lat_o, kr_o, zt_o):
    h = _prenorm(x_ref, g_ref, sc_ref, sh_ref)
    cq = _dotf(h, wcq[...])
    cqn_o[...] = (cq * lax.rsqrt(jnp.mean(cq * cq, axis=-1, keepdims=True) + EPS) * gqa[...]).astype(BF16)
    ckv = _dotf(h, wckv[...])
    lat_o[...] = ckv * lax.rsqrt(jnp.mean(ckv * ckv, axis=-1, keepdims=True) + EPS) * gkva[...]
    kr = _dotf(h, wkr[...])
    krp = _dotf(h, wkrp[...])
    inv = lax.rsqrt(jnp.sum(kr * kr, axis=-1, keepdims=True) * (1.0 / MLA_ROPE) + EPS)
    kr_o[...] = (kr * gkr[...] * cos_ref[...] + krp * gkrp[...] * sin_ref[...]) * inv
    zt_o[...] = _dot_nt(wzt[...], h)


def _mla_q_body(cqn_ref, cos_ref, sin_ref, wqt, gqn, gqr, qp_o):
    qt = _dot_nt(wqt[...], cqn_ref[...])
    ts = qt.shape[1]
    sc = (MLA_NOPE + MLA_ROPE) ** -0.5 * LOG2E
    c = cos_ref[...]
    s = sin_ref[...]
    half = MLA_ROPE // 2
    width = MLA_NOPE + MLA_ROPE
    for h in range(N_HEADS):
        qn = _head_rms_rows(qt[width * h:width * h + MLA_NOPE], gqn[...]) * sc
        qr = _head_rms_rows(qt[width * h + MLA_NOPE:width * (h + 1)], gqr[...]) * sc
        x1, x2 = qr[:half], qr[half:]
        o1 = x1 * c - x2 * s
        o2 = x2 * c + x1 * s
        a = h % 2
        pieces = [qn, jnp.zeros((HEAD_DIM, ts), F32)]
        if a:
            pieces = pieces[::-1]
        pieces += [o1, o2, jnp.zeros((KT_LANES - 2 * HEAD_DIM - MLA_ROPE, ts), F32)]
        qp_o[h] = jnp.concatenate(pieces, axis=0).astype(BF16)


def _mla_kv_body(lat_ref, kr_ref, gkn, wkn, wvt, bd, kt_o, vt_o):
    lat = lat_ref[...].astype(BF16)
    kn = _dotf(lat, wkn[...])
    ss = _group_sumsq(kn, bd)
    kn = kn * lax.rsqrt(ss * (1.0 / MLA_NOPE) + EPS) * gkn[...]
    kr = kr_ref[...].astype(BF16)
    for p in range(N_HEADS // 2):
        kt_o[p, :, 0:128] = kn[:, 128 * p:128 * (p + 1)].astype(BF16)
        kt_o[p, :, 128:256] = kr
    vt_o[0] = _dot_nt(wvt[...], lat).astype(BF16)


def _q_heads_to_qp(qt, gq, qp_o, row_of_head, extra_of_head=None):
    sc = HEAD_DIM ** -0.5 * LOG2E
    ts = qt.shape[1]
    for h in range(N_HEADS):
        qh = _head_rms_rows(qt[HEAD_DIM * h:HEAD_DIM * (h + 1)], gq[...]) * sc
        blk = _place_rows(qh, row_of_head(h), KT_LANES)
        if extra_of_head is not None:
            lo, hi = extra_of_head(h)
            rows = lax.broadcasted_iota(I32, (KT_LANES, ts), 0)
            blk = blk + jnp.where((rows >= lo) & (rows < hi), 1.0, 0.0)
        qp_o[h] = blk.astype(BF16)


def _dsa_in_body(x_ref, sc_ref, sh_ref, g_ref, gk, gq, wqt, wk, wv, wqit, wki, wwit, wzt, bd,
                 qp_o, k_o, v_o, qi_o, ki_o, wi_o, zt_o):
    h = _prenorm(x_ref, g_ref, sc_ref, sh_ref)
    _q_heads_to_qp(_dot_nt(wqt[...], h), gq, qp_o, lambda hh: HEAD_DIM * (hh // (N_HEADS // KV_HEADS)))
    k = _dotf(h, wk[...])
    k_o[...] = k * lax.rsqrt(_group_sumsq(k, bd) * (1.0 / HEAD_DIM) + EPS) * gk[...]
    v_o[...] = _dotf(h, wv[...])
    qit = _dot_nt(wqit[...], h)
    for ih in range(IDX_HEADS):
        qi_o[ih] = qit[IDX_DIM * ih:IDX_DIM * (ih + 1)].astype(BF16)
    ki_o[...] = _dotf(h, wki[...])
    wi_o[...] = _dot_nt(wwit[...], h)
    zt_o[...] = _dot_nt(wzt[...], h)


def _swa_in_body(x_ref, sc_ref, sh_ref, g_ref, gk, gq, wqt, wk, wv, wzt, bd, qp_o, k_o, v_o, zt_o):
    h = _prenorm(x_ref, g_ref, sc_ref, sh_ref)
    _q_heads_to_qp(_dot_nt(wqt[...], h), gq, qp_o, lambda hh: HEAD_DIM * (hh // (N_HEADS // KV_HEADS)))
    k = _dotf(h, wk[...])
    k_o[...] = k * lax.rsqrt(_group_sumsq(k, bd) * (1.0 / HEAD_DIM) + EPS) * gk[...]
    v_o[...] = _dotf(h, wv[...])
    zt_o[...] = _dot_nt(wzt[...], h)


def _fox_in_body(x_ref, sc_ref, sh_ref, g_ref, gk, bf, gq, wqt, wk, wv, wf, wzt, bd,
                 qp_o, k_o, v_o, lf_o, zt_o):
    h = _prenorm(x_ref, g_ref, sc_ref, sh_ref)
    _q_heads_to_qp(_dot_nt(wqt[...], h), gq, qp_o, lambda hh: HEAD_DIM * (hh % 2),
                   lambda hh: (128 + 3 * (hh % 2), 128 + 3 * (hh % 2) + 3))
    k = _dotf(h, wk[...])
    k_o[...] = k * lax.rsqrt(_group_sumsq(k, bd) * (1.0 / HEAD_DIM) + EPS) * gk[...]
    v_o[...] = _dotf(h, wv[...])
    f = _dotf(h, wf[...]) + bf[...]
    lf_o[...] = jnp.minimum(f, 0.0) - jnp.log1p(jnp.exp(-jnp.abs(f)))
    zt_o[...] = _dot_nt(wzt[...], h)


def _out_body(x_ref, gate_ref, ot_ref, zt_ref, wo, o_ref):
    u = (ot_ref[...] * _silu(zt_ref[...])).astype(BF16)
    o_ref[...] = x_ref[...] + gate_ref[...] * _dot_tn(u, wo[...])


def _out_proj(x, gate, ot, zt, wo):
    r = x.shape[0]
    ts = _row_tile(r)
    (out,) = _proj_call(_out_body, "out_proj", r, ts, [x, gate], [ot, zt], [wo],
                        [_rows_out(r, D_MODEL, ts)])
    return out


def _src_arrays(src):
    past, new = src
    return [new] if past is None else [past, new]


def _src_specs(src):
    past, new = src
    c = new.shape[2]
    if past is None:
        return [pl.BlockSpec((1, TK, c), lambda bb, j: (bb, j, 0))]
    npb = past.shape[1] // TK
    return [pl.BlockSpec((1, TK, c), lambda bb, j: (bb, jnp.minimum(j, npb - 1), 0)),
            pl.BlockSpec((1, TK, c), lambda bb, j: (bb, jnp.maximum(j - npb, 0), 0))]


def _src_rows(src):
    past, new = src
    return new.shape[1] + (0 if past is None else past.shape[1])


def _src_load(refs, src_past_blocks):
    if src_past_blocks is None:
        return refs[0][0]
    return jnp.where(pl.program_id(1) < src_past_blocks, refs[0][0], refs[1][0])


def _split_refs(refs, past_blocks):
    vals, pos = [], 0
    for npb in past_blocks:
        cnt = 1 if npb is None else 2
        vals.append(_src_load(refs[pos:pos + cnt], npb))
        pos += cnt
    return vals, refs[pos:]


def _past_blocks(srcs):
    return tuple(None if s[0] is None else s[0].shape[1] // TK for s in srcs)


def _gqa_prep_body(*refs, past_blocks):
    vals, outs = _split_refs(refs, past_blocks)
    outs[0][0] = vals[0].astype(BF16)
    outs[1][0, 0] = vals[1].T.astype(BF16)
    if len(vals) > 2:
        outs[2][0] = vals[2].astype(BF16)


def _gqa_prep(k, v, ki=None):
    srcs = [k, v] + ([ki] if ki is not None else [])
    b, c = k[1].shape[0], k[1].shape[2]
    lp = _src_rows(k)
    nkb = lp // TK
    out_specs = [pl.BlockSpec((1, TK, c), lambda bb, j: (bb, j, 0)),
                 pl.BlockSpec((1, 1, c, TK), lambda bb, j: (bb, j, 0, 0))]
    out_shape = [jax.ShapeDtypeStruct((b, lp, c), BF16), jax.ShapeDtypeStruct((b, nkb, c, TK), BF16)]
    if ki is not None:
        ci = ki[1].shape[2]
        out_specs.append(pl.BlockSpec((1, TK, ci), lambda bb, j: (bb, j, 0)))
        out_shape.append(jax.ShapeDtypeStruct((b, lp, ci), BF16))
    return pl.pallas_call(
        functools.partial(_gqa_prep_body, past_blocks=_past_blocks(srcs)),
        grid=(b, nkb), in_specs=[sp for s in srcs for sp in _src_specs(s)],
        out_specs=out_specs, out_shape=out_shape,
        compiler_params=_cparams(("arbitrary", "arbitrary")), name="gqa_prep",
    )(*[a for s in srcs for a in _src_arrays(s)])


def _fox_prep_body(*refs, past_blocks):
    (k, v, lf), (sel_ref, kt_o, vt_o, carry_ref) = _split_refs(refs, past_blocks)
    kb = pl.program_id(1)

    @pl.when(kb == 0)
    def _():
        carry_ref[...] = jnp.zeros_like(carry_ref)

    r = lax.broadcasted_iota(I32, (TK, TK), 0)
    c = lax.broadcasted_iota(I32, (TK, TK), 1)
    tri = jnp.where(c <= r, 1.0, 0.0).astype(BF16)
    hi, mid, lo = _split3(lf)
    cum = _dotf(tri, hi) + _dotf(tri, mid) + _dotf(tri, lo) + carry_ref[...]
    carry_ref[...] = cum[TK - 1:TK, :]
    a, b, d = _split3(-(cum * LOG2E))
    ex = _dotf(a, sel_ref[0]) + _dotf(b, sel_ref[1]) + _dotf(d, sel_ref[2])
    for p in range(N_HEADS // 2):
        kt_o[p, :, 0:128] = k[:, 128 * p:128 * (p + 1)].astype(BF16)
        kt_o[p, :, 128:256] = ex[:, 128 * p:128 * (p + 1)].astype(BF16)
    vt_o[0, 0] = v.T.astype(BF16)


def _fox_sel():
    sel = np.zeros((3, 128, N_HEADS * HEAD_DIM), np.float32)
    for h in range(N_HEADS):
        for j in range(3):
            sel[j, h, 128 * (h // 2) + 3 * (h % 2) + j] = 1.0
    return jnp.asarray(sel, BF16)


def _fox_prep(k, v, lf128):
    srcs = [k, v, lf128]
    b, c = k[1].shape[0], k[1].shape[2]
    lp = _src_rows(k)
    nkb = lp // TK
    npair = N_HEADS // 2
    sel = _fox_sel()
    return pl.pallas_call(
        functools.partial(_fox_prep_body, past_blocks=_past_blocks(srcs)),
        grid=(b, nkb),
        in_specs=[sp for s in srcs for sp in _src_specs(s)] + [_full_spec(sel)],
        out_specs=[pl.BlockSpec((npair, TK, KT_LANES), lambda bb, j: (bb, j, 0)),
                   pl.BlockSpec((1, 1, c, TK), lambda bb, j: (bb, j, 0, 0))],
        out_shape=[jax.ShapeDtypeStruct((b * npair, lp, KT_LANES), BF16),
                   jax.ShapeDtypeStruct((b, nkb, c, TK), BF16)],
        scratch_shapes=[pltpu.VMEM((1, 128), F32)],
        compiler_params=_cparams(("arbitrary", "arbitrary")), name="fox_prep",
    )(*[a for s in srcs for a in _src_arrays(s)], sel)


def _visible_end(qpos, mode, n_keys):
    if mode == "causal":
        end = qpos + 1
    else:
        end = ((qpos >> CHUNK_SHIFT) + 1) << CHUNK_SHIFT
    return jnp.minimum(end, n_keys)


def _softmax_update(s, mx8, m_ref, c8=None, lane0=0):
    m8 = m_ref[:, lane0:]
    m_new8 = jnp.maximum(m8, mx8)
    alpha8 = jnp.exp2(m8 - m_new8)
    shift = m_new8[0:1] if c8 is None else (m_new8 - c8)[0:1]
    m_ref[:, lane0:] = m_new8
    return jnp.exp2(s - shift).astype(BF16), alpha8[0:1]


def _pv_and_sum(vt_blk, pb):
    ones = jnp.ones((SUM_ROWS, vt_blk.shape[1]), BF16)
    return _dotf(jnp.concatenate([vt_blk, ones], axis=0), pb)


def _run_two_stage(n_full, n_masked, stage_a, stage_b, trim=lambda j: 0):
    odd = n_full % 2

    @pl.when(odd == 1)
    def _():
        stage_a(0, 0, False, 0)
        stage_b(0, 0, False, 0)

    n_pairs = (n_full - odd) // 2

    @pl.when(n_pairs >= 1)
    def _():
        stage_a(odd, 0, False, 0)

        def body(u, _):
            kb = odd + 2 * u
            stage_b(kb, 0, False, 0)
            stage_a(kb + 1, 1, False, 0)
            stage_b(kb + 1, 1, False, 0)
            stage_a(kb + 2, 0, False, 0)
            return 0

        lax.fori_loop(0, n_pairs - 1, body, 0)
        stage_b(n_full - 2, 0, False, 0)
        stage_a(n_full - 1, 1, False, 0)
        stage_b(n_full - 1, 1, False, 0)
        stage_a(n_full, 0, True, trim(0))

    @pl.when(n_pairs < 1)
    def _():
        stage_a(n_full, 0, True, trim(0))

    for j in range(1, n_masked):
        stage_b(n_full + j - 1, (j - 1) % 2, True, trim(j - 1))
        stage_a(n_full + j, j % 2, True, trim(j))
    stage_b(n_full + n_masked - 1, (n_masked - 1) % 2, True, trim(n_masked - 1))


def _flash_body(qp_ref, kt_ref, vt_ref, o_ref, s_buf, mx_buf, m_ref, acc_ref, *,
                tq, mode, q0, n_keys, paired, n_masked):
    i = pl.program_id(1)
    q_first = q0 + i * tq
    n_full = _visible_end(q_first, mode, n_keys) // TK
    ns = N_HEADS // 2 if paired else 1
    per = 2 if paired else 1
    ws, rows = per * tq, per * HEAD_DIM
    w = ns * ws
    ik = lax.broadcasted_iota(I32, (TK, ws), 0)
    iq = lax.broadcasted_iota(I32, (TK, tq), 1)
    qpos = q_first + (jnp.concatenate([iq] * per, axis=1) if paired else iq)
    qs = [jnp.concatenate([qp_ref[per * si + a] for a in range(per)], axis=1) if paired else qp_ref[si]
          for si in range(ns)]

    m_ref[...] = jnp.full((8, w), NEG, F32)
    acc_ref[...] = jnp.zeros((rows + SUM_ROWS, w), F32)

    def stage_a(kb, slot, masked, lane0):
        assert lane0 == 0 or ns == 1
        k0 = pl.multiple_of(kb * TK, TK)
        wl = ws - lane0
        if masked:
            if lane0:
                kpos = k0 + lax.broadcasted_iota(I32, (TK, wl), 0)
                qp = q_first + lane0 + lax.broadcasted_iota(I32, (TK, wl), 1)
            else:
                kpos, qp = k0 + ik, qpos
            valid = (kpos <= qp) if mode == "causal" else ((kpos >> CHUNK_SHIFT) <= (qp >> CHUNK_SHIFT))
            valid = valid & (kpos < n_keys)
        for si in range(ns):
            s = _dotf(kt_ref[si, pl.ds(k0, TK), :], qs[si][:, lane0:])
            if masked:
                s = jnp.where(valid, s, NEG)
            s_buf[slot, :, si * ws + lane0:(si + 1) * ws] = s
            mx_buf[slot, :, si * ws + lane0:(si + 1) * ws] = jnp.broadcast_to(
                jnp.max(s, axis=0, keepdims=True), (8, wl))

    def stage_b(kb, slot, masked, lane0):
        del masked
        pb, alpha = _softmax_update(s_buf[slot, :, lane0:], mx_buf[slot, :, lane0:], m_ref, lane0=lane0)
        for si in range(ns):
            loc = slice(si * ws, (si + 1) * ws - lane0)
            glob = slice(si * ws + lane0, (si + 1) * ws)
            acc_ref[:, glob] = alpha[:, loc] * acc_ref[:, glob] + _pv_and_sum(
                vt_ref[0, kb, rows * si:rows * (si + 1), :], pb[:, loc])

    trim = (lambda j: j * TK) if (not paired and tq > TK) else (lambda j: 0)
    _run_two_stage(n_full, n_masked, stage_a, stage_b, trim)
    out = acc_ref[0:rows, :] / acc_ref[rows:rows + 1, :]
    for si in range(ns):
        for a in range(per):
            o_ref[per * si + a] = out[HEAD_DIM * a:HEAD_DIM * (a + 1), si * ws + tq * a:si * ws + tq * (a + 1)]


def _visible_end_static(qpos, mode, n_keys):
    end = qpos + 1 if mode == "causal" else ((qpos >> CHUNK_SHIFT) + 1) << CHUNK_SHIFT
    return min(end, n_keys)


def _flash(qp, kt, vt, *, tq, mode, q0, n_real_q, n_keys, paired):
    bh, _, sq = qp.shape
    lp = kt.shape[1]
    nkb = vt.shape[1]
    assert (q0 % TK == 0) and (tq % TK == 0 or sq == tq)
    n_masked = (-(-_visible_end_static(q0 + n_real_q - 1, mode, n_keys) // TK)
                - _visible_end_static(q0, mode, n_keys) // TK)
    body = functools.partial(_flash_body, tq=tq, mode=mode, q0=q0, n_keys=n_keys, paired=paired,
                             n_masked=n_masked)
    if not paired:
        ns, w, rows = 1, tq, HEAD_DIM
        kt_spec = pl.BlockSpec((1, lp, KT_LANES), lambda g, i: (g // 2, 0, 0))
        vt_spec = pl.BlockSpec((1, nkb, HEAD_DIM, TK), lambda g, i: (g // N_HEADS, 0, g % N_HEADS, 0))
    else:
        ns, w, rows = N_HEADS, N_HEADS * tq, 2 * HEAD_DIM
        kt_spec = pl.BlockSpec((ns // 2, lp, KT_LANES), lambda g, i: (g, 0, 0))
        vt_spec = pl.BlockSpec((1, nkb, ns * HEAD_DIM, TK), lambda g, i: (g, 0, 0, 0))
    return pl.pallas_call(
        body,
        grid=(bh // ns, sq // tq),
        in_specs=[pl.BlockSpec((ns, KT_LANES, tq), lambda g, i: (g, 0, i)), kt_spec, vt_spec],
        out_specs=pl.BlockSpec((ns, HEAD_DIM, tq), lambda g, i: (g, 0, i)),
        out_shape=jax.ShapeDtypeStruct((bh, HEAD_DIM, sq), F32),
        scratch_shapes=[pltpu.VMEM((2, TK, w), F32), pltpu.VMEM((2, 8, w), F32), pltpu.VMEM((8, w), F32),
                        pltpu.VMEM((rows + SUM_ROWS, w), F32)],
        compiler_params=_cparams(("arbitrary", "arbitrary")),
        name="flash_" + mode,
    )(qp, kt, vt)


def _swa_body(hp_ref, qp_ref, *refs, tq, q0, n_pieces, n_rows):
    k_refs, v_refs, o_ref = refs[:n_pieces], refs[n_pieces:2 * n_pieces], refs[2 * n_pieces]
    i = pl.program_id(1)
    q_first = q0 + i * tq
    k = jnp.concatenate([r[0].astype(BF16) for r in k_refs], axis=0)
    v = jnp.concatenate([r[0].astype(BF16) for r in v_refs], axis=0)
    kw = k.shape[0]
    row = lax.broadcasted_iota(I32, (kw, tq), 0)
    kpos = (q_first - WINDOW) + row
    qpos = q_first + lax.broadcasted_iota(I32, (kw, tq), 1)
    qc = qpos >> CHUNK_SHIFT
    kc = kpos >> CHUNK_SHIFT
    valid = (kpos >= 0) & (kc <= qc) & (qc - kc <= WIN_CHUNKS) & (row < n_rows)
    dist = jnp.abs(qpos - kpos).astype(F32)
    group = N_HEADS // KV_HEADS

    def lanes(x):
        return jnp.concatenate([x] * group, axis=1)

    for n in range(KV_HEADS):
        h0 = n * group

        def per_head(col):
            return jnp.concatenate(
                [jnp.broadcast_to(hp_ref[h0 + g][:, col:col + 1], (1, tq)) for g in range(group)], axis=1)

        q = jnp.concatenate([qp_ref[h0 + g] for g in range(group)], axis=1)
        slope2, sink2 = per_head(0), per_head(1)
        s = _dotf(k, q) - slope2 * lanes(dist)
        s = jnp.where(lanes(valid), s, NEG)
        m = jnp.maximum(jnp.max(s, axis=0, keepdims=True), sink2)
        p = jnp.exp2(s - m)
        l = jnp.sum(p, axis=0, keepdims=True) + jnp.exp2(sink2 - m)
        acc = _dot_tn(v, p.astype(BF16))
        out = acc[HEAD_DIM * n:HEAD_DIM * (n + 1)] / l
        for g in range(group):
            o_ref[h0 + g] = out[:, g * tq:(g + 1) * tq]


def _swa_attend(hp, qp, k_pieces, v_pieces, *, tq, q0, n_rows):
    b = k_pieces[0][0].shape[0]
    sq = qp.shape[2]
    n_pieces = len(k_pieces)

    def spec(piece):
        _, rows, idx = piece
        return pl.BlockSpec((1, rows, KV_HEADS * HEAD_DIM), lambda bb, i: (bb, idx(i), 0))

    body = functools.partial(_swa_body, tq=tq, q0=q0, n_pieces=n_pieces, n_rows=n_rows)
    return pl.pallas_call(
        body,
        grid=(b, sq // tq),
        in_specs=[_full_spec(hp), pl.BlockSpec((N_HEADS, KT_LANES, tq), lambda bb, i: (bb, 0, i))]
        + [spec(p) for p in k_pieces] + [spec(p) for p in v_pieces],
        out_specs=pl.BlockSpec((N_HEADS, HEAD_DIM, tq), lambda bb, i: (bb, 0, i)),
        out_shape=jax.ShapeDtypeStruct((b * N_HEADS, HEAD_DIM, sq), F32),
        compiler_params=_cparams(("arbitrary", "arbitrary")),
        name="swa_attend",
    )(hp, qp, *[p[0] for p in k_pieces], *[p[0] for p in v_pieces])


def _dsa_body(hp_ref, qi_ref, wi_ref, qp_ref, ki_ref, kt_ref, vt_ref, o_ref, hi_ref, lo_ref,
              s_buf, mx_buf, m_ref, acc_ref, *, tq, q0, n_real_q, n_keys, topk):
    i = pl.program_id(1)
    q_first = q0 + i * tq
    q_last = q_first + (n_real_q - 1)
    n_tot = (_visible_end(q_last, "chunk", n_keys) + (TK - 1)) // TK
    n_past = jnp.minimum(q_first, n_keys) // TK
    ik = lax.broadcasted_iota(I32, (TK, tq), 0)
    iq = lax.broadcasted_iota(I32, (TK, tq), 1)
    qpos = q_first + iq
    tf = float(topk)

    def blk(kb):
        return pl.ds(pl.multiple_of(kb * TK, TK), TK)

    def to_key(x):
        bits = pltpu.bitcast(x, I32)
        return jnp.where(bits < 0, bits ^ INT_MAX, bits)

    def score_blk(kb, diag):
        ki = ki_ref[0, blk(kb), :]
        acc = jnp.zeros((TK, tq), F32)
        for h in range(IDX_HEADS):
            acc = acc + wi_ref[0, h:h + 1, :] * jnp.maximum(_dotf(ki, qi_ref[0, h]), 0.0)
        key = to_key(acc)
        if diag:
            kpos = kb * TK + ik
            valid = ((kpos >> CHUNK_SHIFT) <= (qpos >> CHUNK_SHIFT)) & (kpos < n_keys)
            key = jnp.where(valid, key, INT_MIN)
        hi_ref[blk(kb), :] = (key >> 16).astype(I16)
        lo_ref[blk(kb), :] = ((key & 0xFFFF) - 32768).astype(I16)
        return 0

    lax.fori_loop(0, n_past, lambda kb, c: score_blk(kb, False), 0)
    lax.fori_loop(n_past, n_tot, lambda kb, c: score_blk(kb, True), 0)

    def count_ge(ref, mid):
        mid16 = mid.astype(I16)

        def body(kb, acc):
            ge = jnp.where(ref[blk(kb), :] >= mid16, jnp.int16(1), jnp.int16(0))
            parts = [ge[16 * r:16 * (r + 1)] for r in range(TK // 16)]
            while len(parts) > 1:
                parts = [parts[j] + parts[j + 1] for j in range(0, len(parts), 2)]
            return acc + parts[0]

        acc = lax.fori_loop(0, n_tot, body, jnp.zeros((16, tq), I16))
        return jnp.sum(acc.astype(I32), axis=0, keepdims=True).astype(F32)

    def bisect_step(ref, target, st):
        lo, hi, cl, ch = st
        mid = (lo + hi) >> 1
        cnt = count_ge(ref, mid)
        ge = cnt >= target
        return jnp.where(ge, mid, lo), jnp.where(ge, hi, mid), jnp.where(ge, cnt, cl), jnp.where(ge, ch, cnt)

    qrow = q_first + lax.broadcasted_iota(I32, (1, tq), 1)
    n_vis = _visible_end(qrow, "chunk", n_keys).astype(F32)
    zero = jnp.zeros((1, tq), F32)
    st1 = (jnp.full((1, tq), I16_MIN + 1, I32), jnp.full((1, tq), I16_MAX + 1, I32), n_vis, zero)
    h_thr, _, cl1, ch1 = lax.fori_loop(0, 16, lambda _, st: bisect_step(hi_ref, tf, st), st1)
    h16 = h_thr.astype(I16)

    def mask_lo(kb, _):
        lo_ref[blk(kb), :] = jnp.where(hi_ref[blk(kb), :] == h16, lo_ref[blk(kb), :], jnp.int16(I16_MIN))
        return 0

    lax.fori_loop(0, n_tot, mask_lo, 0)
    t2 = tf - ch1

    def cond(st):
        return (st[0] < 16) & (st[2] > 0.5)

    def body(st):
        lo, hi, cl, ch = bisect_step(lo_ref, t2, st[1])
        done = (cl <= t2) | (hi - lo <= 1)
        return st[0] + 1, (lo, hi, cl, ch), jnp.sum(jnp.where(done, 0.0, 1.0))

    cl2_0 = cl1 - ch1
    st2 = (jnp.full((1, tq), I16_MIN, I32), jnp.full((1, tq), I16_MAX + 1, I32), cl2_0, zero)
    _, (l_thr, _, cl2, ch2), _ = lax.while_loop(
        cond, body, (jnp.int32(0), st2, jnp.sum(jnp.where(cl2_0 > t2, 1.0, 0.0))))
    l16 = l_thr.astype(I16)

    need = t2 - ch2

    @pl.when(jnp.sum(jnp.where(cl2 > t2, 1.0, 0.0)) > 0.5)
    def _():
        r = lax.broadcasted_iota(I32, (TK, TK), 0)
        c = lax.broadcasted_iota(I32, (TK, TK), 1)
        tri = jnp.where(c < r, 1.0, 0.0).astype(BF16)

        def fix(kb, carry):
            hb = hi_ref[blk(kb), :]
            e16 = jnp.where(hb == h16, jnp.where(lo_ref[blk(kb), :] == l16, jnp.int16(1), jnp.int16(0)),
                            jnp.int16(0))
            e = e16.astype(I32).astype(F32)
            before = _dotf(tri, e.astype(BF16)) + carry
            drop = jnp.where((e > 0.5) & (before >= need), 1, 0).astype(I16)
            hi_ref[blk(kb), :] = jnp.where(drop == jnp.int16(1), jnp.int16(I16_MIN), hb)
            return carry + jnp.sum(e, axis=0, keepdims=True)

        lax.fori_loop(0, n_tot, fix, jnp.zeros((1, tq), F32))

    def to_bias(kb, _):
        hb = hi_ref[blk(kb), :]
        zero_b, neg_b = jnp.bfloat16(0.0), jnp.bfloat16(NEG)
        at_thr = jnp.where(lo_ref[blk(kb), :] >= l16, zero_b, neg_b)
        bias = jnp.where(hb > h16, zero_b, jnp.where(hb == h16, at_thr, neg_b))
        hi_ref[blk(kb), :] = pltpu.bitcast(bias, I16)
        return 0

    lax.fori_loop(0, n_tot, to_bias, 0)

    group = N_HEADS // KV_HEADS
    heads = DSA_GROUPS * group
    gw = group * tq
    wide = heads * tq

    def lanes(x):
        return jnp.concatenate([x] * heads, axis=1)

    def group_body(n, _):
        h0 = n * heads
        q = jnp.concatenate([qp_ref[h0 + g] for g in range(heads)], axis=1)
        slope2 = jnp.concatenate(
            [jnp.broadcast_to(hp_ref[h0 + g][:, 0:1], (1, tq)) for g in range(heads)], axis=1)
        slope8 = jnp.broadcast_to(slope2, (8, wide))
        a_tab = slope2 * lanes(ik.astype(F32))
        vrows = [pl.ds(pl.multiple_of((n * DSA_GROUPS + gi) * HEAD_DIM, HEAD_DIM), HEAD_DIM)
                 for gi in range(DSA_GROUPS)]
        m_ref[...] = jnp.full((8, wide), NEG, F32)
        acc_ref[...] = jnp.zeros((HEAD_DIM + SUM_ROWS, wide), F32)

        def c8(kb):
            return slope8 * (kb * TK - q_first).astype(F32)

        def stage_a(kb, slot, diag, lane0):
            del lane0
            bias = pltpu.bitcast(hi_ref[blk(kb), :], jnp.bfloat16).astype(F32)
            s = _dotf(kt_ref[0, blk(kb), :], q) + lanes(bias)
            if diag:
                kpos = kb * TK + ik
                rel = iq.astype(F32) - jnp.abs(qpos - kpos).astype(F32)
                s = s + slope2 * lanes(rel)
            else:
                s = s + a_tab
            mx8 = jnp.broadcast_to(jnp.max(s, axis=0, keepdims=True), (8, wide))
            s_buf[slot] = s
            mx_buf[slot] = mx8 if diag else mx8 + c8(kb)

        def stage_b(kb, slot, diag, lane0):
            del lane0
            pb, alpha = _softmax_update(s_buf[slot], mx_buf[slot], m_ref, None if diag else c8(kb))
            for gi in range(DSA_GROUPS):
                sl = slice(gi * gw, (gi + 1) * gw)
                acc_ref[:, sl] = alpha[:, sl] * acc_ref[:, sl] + _pv_and_sum(vt_ref[0, kb, vrows[gi], :], pb[:, sl])

        _run_two_stage(n_past, 1, stage_a, stage_b)
        out = acc_ref[0:HEAD_DIM, :] / acc_ref[HEAD_DIM:HEAD_DIM + 1, :]
        for g in range(heads):
            o_ref[h0 + g] = out[:, g * tq:(g + 1) * tq]
        return 0

    lax.fori_loop(0, KV_HEADS // DSA_GROUPS, group_body, 0)


def _dsa_attend(hp, qi, wi, qp, ki, kt, vt, *, tq, q0, n_real_q, n_keys):
    b = ki.shape[0]
    sq = qp.shape[2]
    lp = kt.shape[1]
    nkb = vt.shape[1]
    topk = min(TOPK_MAX, n_keys // 4)
    assert TK % tq == 0 and q0 % TK == 0 and n_real_q <= tq
    wide = DSA_GROUPS * (N_HEADS // KV_HEADS) * tq
    body = functools.partial(_dsa_body, tq=tq, q0=q0, n_real_q=n_real_q, n_keys=n_keys, topk=topk)
    once = pl.Buffered(1)
    return pl.pallas_call(
        body,
        grid=(b, sq // tq),
        in_specs=[_full_spec(hp),
                  pl.BlockSpec((1, IDX_HEADS, IDX_DIM, tq), lambda bb, i: (bb, 0, 0, i)),
                  pl.BlockSpec((1, IDX_HEADS, tq), lambda bb, i: (bb, 0, i)),
                  pl.BlockSpec((N_HEADS, KT_LANES, tq), lambda bb, i: (bb, 0, i)),
                  pl.BlockSpec((1, lp, IDX_DIM), lambda bb, i: (bb, 0, 0), pipeline_mode=once),
                  pl.BlockSpec((1, lp, KT_LANES), lambda bb, i: (bb, 0, 0), pipeline_mode=once),
                  pl.BlockSpec((1, nkb, KV_HEADS * HEAD_DIM, TK), lambda bb, i: (bb, 0, 0, 0),
                               pipeline_mode=once)],
        out_specs=pl.BlockSpec((N_HEADS, HEAD_DIM, tq), lambda bb, i: (bb, 0, i)),
        out_shape=jax.ShapeDtypeStruct((b * N_HEADS, HEAD_DIM, sq), F32),
        scratch_shapes=[pltpu.VMEM((lp, tq), I16), pltpu.VMEM((lp, tq), I16),
                        pltpu.VMEM((2, TK, wide), F32), pltpu.VMEM((2, 8, wide), F32),
                        pltpu.VMEM((8, wide), F32), pltpu.VMEM((HEAD_DIM + SUM_ROWS, wide), F32)],
        compiler_params=_cparams(("arbitrary", "arbitrary"), DSA_VMEM_LIMIT),
        name="dsa_attend",
    )(hp, qi, wi, qp, ki, kt, vt)


def _block_diag(c):
    g = np.arange(c) // HEAD_DIM
    return jnp.asarray(g[:, None] == g[None, :], BF16)


def _tile_gain(g, n):
    return jnp.tile(g.astype(F32), n).reshape(1, -1)


def _col(g):
    return g.astype(F32).reshape(-1, 1)


def _pad_cols(w, n):
    return jnp.pad(w, ((0, 0), (0, n - w.shape[1])))


def _rope_tables(pos):
    half = MLA_ROPE // 2
    inv = ROPE_BASE ** (-jnp.arange(half, dtype=F32) / half)
    ang = pos.astype(F32)[:, None] * inv[None, :]
    return jnp.cos(ang), jnp.sin(ang)


def _alibi_slopes():
    return np.asarray(2.0 ** (-8.0 * np.arange(1, N_HEADS + 1) / N_HEADS), dtype=np.float32)


def _head_params(sinks=None):
    hp = jnp.zeros((N_HEADS, 1, 128), F32)
    hp = hp.at[:, 0, 0].set(jnp.asarray(_alibi_slopes()) * LOG2E)
    if sinks is not None:
        hp = hp.at[:, 0, 1].set(sinks.astype(F32) * LOG2E)
    return hp


class _Stream:
    def __init__(self, batch, seq, past):
        self.b, self.s, self.p = batch, seq, past
        self.r = batch * seq
        self.decode = past > 0
        self.tq = TQ_DEC if self.decode else TQ
        self.tqf = TQ_DEC if self.decode else min(TQ_FLASH, seq)
        self.sq =self.tq if self.decode else seq
        self.n_keys = past + seq
        self.lp = -(-self.n_keys // TK) * TK
        self.pos = past + np.tile(np.arange(seq), batch)

    def pad_keys(self, past_arr, new_arr, n_keys=None, lp=None):
        n_keys = self.n_keys if n_keys is None else n_keys
        lp = self.lp if lp is None else lp
        new_arr = new_arr.reshape(self.b, self.s, -1)
        parts = [new_arr] if past_arr is None else [past_arr.astype(F32), new_arr]
        if lp > n_keys:
            parts.append(jnp.zeros((self.b, lp - n_keys, new_arr.shape[-1]), F32))
        return jnp.concatenate(parts, axis=1) if len(parts) > 1 else new_arr

    def key_source(self, past_arr, new_arr):
        new_arr = new_arr.reshape(self.b, self.s, -1)
        if past_arr is None:
            return (None, new_arr)
        assert past_arr.shape[1] == self.p and self.p % TK == 0
        pad = self.lp - self.n_keys
        return (past_arr.astype(F32), jnp.pad(new_arr, ((0, 0), (0, pad), (0, 0))))

    def qp_blocks(self, qp):
        if not self.decode:
            return qp
        x = qp.reshape(N_HEADS, KT_LANES, self.b, self.s).transpose(2, 0, 1, 3)
        x = jnp.pad(x, ((0, 0), (0, 0), (0, 0), (0, self.tq - self.s)))
        return x.reshape(self.b * N_HEADS, KT_LANES, self.tq)

    def lanes(self, x):
        n, c, _ = x.shape
        if not self.decode:
            return x[None]
        x = x.reshape(n, c, self.b, self.s).transpose(2, 0, 1, 3)
        return jnp.pad(x, ((0, 0), (0, 0), (0, 0), (0, self.tq - self.s)))

    def ot_cols(self, ot):
        if not self.decode:
            return ot.reshape(N_HEADS * HEAD_DIM, self.r)
        x = ot.reshape(self.b, N_HEADS, HEAD_DIM, self.tq)[..., :self.s]
        return x.transpose(1, 2, 0, 3).reshape(N_HEADS * HEAD_DIM, self.r)


def _mod_rows(st, mod_l, row0):
    m = mod_l[row0:row0 + st.b]
    if st.b > 1:
        m = jnp.repeat(m, st.s, axis=0)
    return m[:, :D_MODEL], m[:, D_MODEL:2 * D_MODEL], m[:, 2 * D_MODEL:]


def _mla_layer(st, x, mod, g, w, past):
    shift, scale, gate = mod
    r = st.r
    ts = _row_tile(r)
    cos, sin = _rope_tables(jnp.asarray(st.pos))
    zpad = jnp.zeros((r, 128 - MLA_ROPE), F32)
    cos_p = jnp.concatenate([cos, cos, zpad], axis=1)
    sin_p = jnp.concatenate([-sin, sin, zpad], axis=1)
    cqn, lat, kr, zt = _proj_call(
        _mla_in_body, "mla_in", r, ts,
        [x, scale, shift, cos_p, sin_p, g, w["gqa"], w["gkva"], w["gkr"], w["gkrp"]], [],
        [w["wcq"], w["wckv"], w["wkr"], w["wkrp"], w["wzt"]],
        [_rows_out(r, MLA_Q_LORA, ts, BF16), _rows_out(r, MLA_KV_LORA, ts), _rows_out(r, 128, ts),
         _cols_out(D_MODEL, r, ts)])
    (qp,) = _proj_call(_mla_q_body, "mla_q", r, ts, [cqn], [cos.T, sin.T], [w["wqt"], w["gqn"], w["gqr"]],
                       [_qp_out(r, ts)])
    past_lat, past_kr = (None, None) if past is None else past
    if past_kr is not None:
        past_kr = jnp.pad(past_kr.astype(F32), ((0, 0), (0, 0), (0, 128 - MLA_ROPE)))
    lat_all = st.pad_keys(past_lat, lat).reshape(st.b * st.lp, MLA_KV_LORA)
    kr_all = st.pad_keys(past_kr, kr).reshape(st.b * st.lp, 128)
    rk = st.b * st.lp
    npair = N_HEADS // 2
    nkb = st.lp // TK
    kt, vt = _proj_call(
        _mla_kv_body, "mla_kv", rk, TK, [lat_all, kr_all, w["gkn"]], [], [w["wkn"], w["wvt"], w["bd"]],
        [((npair * st.b, st.lp, KT_LANES), BF16, (npair, TK, KT_LANES), lambda i: (i // nkb, i % nkb, 0)),
         ((rk // TK, N_HEADS * HEAD_DIM, TK), BF16, (1, N_HEADS * HEAD_DIM, TK), lambda i: (i, 0, 0))])
    vt = vt.reshape(st.b, nkb, N_HEADS * HEAD_DIM, TK)
    ot = _flash(st.qp_blocks(qp), kt, vt, tq=st.tqf, mode="chunk", q0=st.p,
                n_real_q=st.tqf if not st.decode else st.s, n_keys=st.n_keys, paired=st.decode)
    x = _out_proj(x, gate, st.ot_cols(ot), zt, w["wo"])
    return x, (lat, kr[:, :MLA_ROPE])


def _dsa_layer(st, x, mod, g, w, past):
    shift, scale, gate = mod
    r = st.r
    ts = _row_tile(r)
    c = KV_HEADS * HEAD_DIM
    qp, k, v, qi, ki, wi, zt = _proj_call(
        _dsa_in_body, "dsa_in", r, ts, [x, scale, shift, g, w["gk"]], [],
        [w["gq"], w["wqt"], w["wk"], w["wv"], w["wqit"], w["wki"], w["wwit"], w["wzt"], w["bd"]],
        [_qp_out(r, ts), _rows_out(r, c, ts), _rows_out(r, c, ts),
         ((IDX_HEADS, IDX_DIM, r), BF16, (IDX_HEADS, IDX_DIM, ts), lambda i: (0, 0, i)),
         _rows_out(r, IDX_DIM, ts), _cols_out(IDX_HEADS, r, ts), _cols_out(D_MODEL, r, ts)])
    if past is None:
        pk = pv = pki = None
    else:
        pk, pv, pki = past[0].reshape(st.b, -1, c), past[1].reshape(st.b, -1, c), past[2]
    kt, vt, kib = _gqa_prep(st.key_source(pk, k), st.key_source(pv, v), st.key_source(pki, ki))
    qi_b = st.lanes(qi)
    wi_b = st.lanes(wi[None])[:, 0]
    ot = _dsa_attend(_head_params(), qi_b, wi_b, st.qp_blocks(qp), kib, kt, vt, tq=st.tq, q0=st.p,
                     n_real_q=st.tq if not st.decode else st.s, n_keys=st.n_keys)
    x = _out_proj(x, gate, st.ot_cols(ot), zt, w["wo"])
    return x, (k, v, ki)


def _swa_layer(st, x, mod, g, w, past):
    shift, scale, gate = mod
    r = st.r
    ts = _row_tile(r)
    c = KV_HEADS * HEAD_DIM
    qp, k, v, zt = _proj_call(
        _swa_in_body, "swa_in", r, ts, [x, scale, shift, g, w["gk"]], [],
        [w["gq"], w["wqt"], w["wk"], w["wv"], w["wzt"], w["bd"]],
        [_qp_out(r, ts), _rows_out(r, c, ts), _rows_out(r, c, ts), _cols_out(D_MODEL, r, ts)])
    k3, v3 = k.reshape(st.b, st.s, c), v.reshape(st.b, st.s, c)
    if past is None:
        per = st.tq // WINDOW
        idx = [lambda i: jnp.maximum(per * i - 1, 0), lambda i: per * i, lambda i: per * i + 1]
        k_pieces = [(k3, WINDOW, f) for f in idx]
        v_pieces = [(v3, WINDOW, f) for f in idx]
        n_rows = WINDOW + st.tq
        new = (k3[:, st.s - WINDOW:], v3[:, st.s - WINDOW:])
    else:
        win = past[0].shape[1]
        assert win == WINDOW and st.s <= WINDOW
        pad = ((0, 0), (0, WINDOW - st.s), (0, 0))
        pk, pv = past[0].reshape(st.b, win, c).astype(F32), past[1].reshape(st.b, win, c).astype(F32)
        zero = lambda i: 0
        k_pieces = [(pk, WINDOW, zero), (jnp.pad(k3, pad), WINDOW, zero)]
        v_pieces = [(pv, WINDOW, zero), (jnp.pad(v3, pad), WINDOW, zero)]
        n_rows = win + st.s
        new = (jnp.concatenate([pk, k3], axis=1)[:, st.s:], jnp.concatenate([pv, v3], axis=1)[:, st.s:])
    ot = _swa_attend(_head_params(w["sinks"]), st.qp_blocks(qp), k_pieces, v_pieces, tq=st.tq, q0=st.p,
                     n_rows=n_rows)
    x = _out_proj(x, gate, st.ot_cols(ot), zt, w["wo"])
    return x, new


def _fox_layer(st, x, mod, g, w, past):
    shift, scale, gate = mod
    r = st.r
    ts = _row_tile(r)
    c = N_HEADS * HEAD_DIM
    qp, k, v, lf, zt = _proj_call(
        _fox_in_body, "fox_in", r, ts, [x, scale, shift, g, w["gk"], w["bf"]], [],
        [w["gq"], w["wqt"], w["wk"], w["wv"], w["wf"], w["wzt"], w["bd"]],
        [_qp_out(r, ts), _rows_out(r, c, ts), _rows_out(r, c, ts), _rows_out(r, 128, ts),
         _cols_out(D_MODEL, r, ts)])
    if past is None:
        pk = pv = plf = None
    else:
        pk, pv = past[0].reshape(st.b, -1, c), past[1].reshape(st.b, -1, c)
        plf = jnp.pad(past[2].astype(F32), ((0, 0), (0, 0), (0, 128 - N_HEADS)))
    kt, vt = _fox_prep(st.key_source(pk, k), st.key_source(pv, v), st.key_source(plf, lf))
    ot = _flash(st.qp_blocks(qp), kt, vt, tq=st.tqf, mode="causal", q0=st.p,
                n_real_q=st.tqf if not st.decode else st.s, n_keys=st.n_keys, paired=st.decode)
    x = _out_proj(x, gate, st.ot_cols(ot), zt, w["wo"])
    return x, (k, v, lf[:, :N_HEADS])


def _prep_weights(mla_w_in, mla_g_qa, mla_w_qb, mla_g_kva, mla_w_kvb, mla_g_qn, mla_g_qr, mla_g_kn,
                  mla_g_kr, mla_w_out, dsa_w_in, dsa_g_q, dsa_g_k, dsa_w_out, swa_w_in, swa_g_q,
                  swa_g_k, swa_sinks, swa_w_out, fox_w_in, fox_b_f, fox_g_q, fox_g_k, fox_w_out):
    bf = lambda a: a.astype(BF16)
    row = lambda a: a.astype(F32).reshape(1, -1)
    half = MLA_ROPE // 2
    c1, c2, c3 = MLA_Q_LORA, MLA_Q_LORA + MLA_KV_LORA, MLA_Q_LORA + MLA_KV_LORA + MLA_ROPE
    wkr = mla_w_in[:, c2:c3]
    wkrp = jnp.concatenate([wkr[:, half:], wkr[:, :half]], axis=1)
    gkr = mla_g_kr.astype(F32)
    gkrp = jnp.concatenate([gkr[half:], gkr[:half]])
    kvb = mla_w_kvb.reshape(MLA_KV_LORA, N_HEADS, MLA_NOPE + HEAD_DIM)
    mla = dict(
        wcq=bf(mla_w_in[:, :c1]), wckv=bf(mla_w_in[:, c1:c2]), wkr=bf(_pad_cols(wkr, 128)),
        wkrp=bf(_pad_cols(wkrp, 128)), wzt=bf(mla_w_in[:, c3:].T),
        gqa=row(mla_g_qa), gkva=row(mla_g_kva), gkr=row(jnp.pad(gkr, (0, 128 - MLA_ROPE))),
        gkrp=row(jnp.pad(gkrp, (0, 128 - MLA_ROPE))),
        wqt=bf(mla_w_qb.T), gqn=_col(mla_g_qn), gqr=_col(mla_g_qr),
        wkn=bf(kvb[:, :, :MLA_NOPE].reshape(MLA_KV_LORA, -1)),
        wvt=bf(kvb[:, :, MLA_NOPE:].reshape(MLA_KV_LORA, -1).T),
        gkn=_tile_gain(mla_g_kn, N_HEADS), bd=_block_diag(BD_LANES), wo=bf(mla_w_out))
    hq, hk = N_HEADS * HEAD_DIM, KV_HEADS * HEAD_DIM
    cuts = np.cumsum([hq, hk, hk, IDX_HEADS * IDX_DIM, IDX_DIM, IDX_HEADS]).tolist()
    dsa = dict(
        wqt=bf(dsa_w_in[:, :cuts[0]].T), wk=bf(dsa_w_in[:, cuts[0]:cuts[1]]),
        wv=bf(dsa_w_in[:, cuts[1]:cuts[2]]), wqit=bf(dsa_w_in[:, cuts[2]:cuts[3]].T),
        wki=bf(dsa_w_in[:, cuts[3]:cuts[4]]), wwit=bf(dsa_w_in[:, cuts[4]:cuts[5]].T),
        wzt=bf(dsa_w_in[:, cuts[5]:].T), gq=_col(dsa_g_q), gk=_tile_gain(dsa_g_k, KV_HEADS),
        bd=_block_diag(BD_LANES), wo=bf(dsa_w_out))
    swa = dict(
        wqt=bf(swa_w_in[:, :hq].T), wk=bf(swa_w_in[:, hq:hq + hk]), wv=bf(swa_w_in[:, hq + hk:hq + 2 * hk]),
        wzt=bf(swa_w_in[:, hq + 2 * hk:].T), gq=_col(swa_g_q), gk=_tile_gain(swa_g_k, KV_HEADS),
        bd=_block_diag(BD_LANES), wo=bf(swa_w_out), sinks=swa_sinks)
    fox = dict(
        wqt=bf(fox_w_in[:, :hq].T), wk=bf(fox_w_in[:, hq:2 * hq]), wv=bf(fox_w_in[:, 2 * hq:3 * hq]),
        wf=bf(_pad_cols(fox_w_in[:, 3 * hq:3 * hq + N_HEADS], 128)), wzt=bf(fox_w_in[:, 3 * hq + N_HEADS:].T),
        bf=row(jnp.pad(fox_b_f.astype(F32), (0, 128 - N_HEADS))), gq=_col(fox_g_q),
        gk=_tile_gain(fox_g_k, N_HEADS), bd=_block_diag(BD_LANES), wo=bf(fox_w_out))
    return [mla, dsa, swa, fox]


def kernel(x_prompt, x_sample, cache_mla_latent, cache_mla_krope, cache_dsa_k, cache_dsa_v, cache_dsa_kidx, state_swa_k, state_swa_v, cache_fox_k, cache_fox_v, cache_fox_logf, c_prompt, c_sample, norm_g, ada_w, ada_b, mla_w_in, mla_g_qa, mla_w_qb, mla_g_kva, mla_w_kvb, mla_g_qn, mla_g_qr, mla_g_kn, mla_g_kr, mla_w_out, dsa_w_in, dsa_g_q, dsa_g_k, dsa_w_out, swa_w_in, swa_g_q, swa_g_k, swa_sinks, swa_w_out, fox_w_in, fox_b_f, fox_g_q, fox_g_k, fox_w_out):
    bp, sp, _ = x_prompt.shape
    bs, ss, _ = x_sample.shape
    past_len = cache_mla_latent.shape[1]
    depth = norm_g.shape[0]
    assert bp == 1 and sp % TQ == 0 and sp % min(TQ_FLASH, sp) == 0 and sp % TS == 0 and (bs * ss) % 8 == 0 and ss <= TQ_DEC
    assert past_len % TK == 0 and past_len >= WINDOW

    weights = _prep_weights(mla_w_in, mla_g_qa, mla_w_qb, mla_g_kva, mla_w_kvb, mla_g_qn, mla_g_qr,
                            mla_g_kn, mla_g_kr, mla_w_out, dsa_w_in, dsa_g_q, dsa_g_k, dsa_w_out,
                            swa_w_in, swa_g_q, swa_g_k, swa_sinks, swa_w_out, fox_w_in, fox_b_f,
                            fox_g_q, fox_g_k, fox_w_out)
    rows = bp + bs
    rows_p = -(-rows // 8) * 8
    c_all = jnp.concatenate([c_prompt, c_sample, jnp.zeros((rows_p - rows, D_MODEL), F32)], axis=0)
    mod = _ada_mod(c_all, ada_w, ada_b)

    st_p = _Stream(bp, sp, 0)
    st_s = _Stream(bs, ss, past_len)
    pasts = ((cache_mla_latent, cache_mla_krope), (cache_dsa_k, cache_dsa_v, cache_dsa_kidx),
             (state_swa_k, state_swa_v), (cache_fox_k, cache_fox_v, cache_fox_logf))
    layers = (_mla_layer, _dsa_layer, _swa_layer, _fox_layer)
    xp = x_prompt.reshape(st_p.r, D_MODEL)
    xs = x_sample.reshape(st_s.r, D_MODEL)
    new_p, new_s = [], []
    for layer in range(depth):
        kind = layer % len(layers)
        g = norm_g[layer].astype(F32).reshape(1, -1)
        xp, n = layers[kind](st_p, xp, _mod_rows(st_p, mod[layer], 0), g, weights[kind], None)
        new_p.append(n)
        xs, n = layers[kind](st_s, xs, _mod_rows(st_s, mod[layer], bp), g, weights[kind], pasts[kind])
        new_s.append(n)

    def shaped(st, new):
        (lat, kr), (dk, dv, dki), (sk, sv), (fk, fv, flf) = new
        b, s = st.b, st.s
        return (lat.reshape(b, s, -1), kr.reshape(b, s, -1),
                dk.reshape(b, s, KV_HEADS, HEAD_DIM), dv.reshape(b, s, KV_HEADS, HEAD_DIM),
                dki.reshape(b, s, -1),
                sk.reshape(b, -1, KV_HEADS, HEAD_DIM), sv.reshape(b, -1, KV_HEADS, HEAD_DIM),
                fk.reshape(b, s, N_HEADS, HEAD_DIM), fv.reshape(b, s, N_HEADS, HEAD_DIM),
                flf.reshape(b, s, -1))

    return (xp.reshape(x_prompt.shape), xs.reshape(x_sample.shape)) + shaped(st_p, new_p) + shaped(st_s, new_s)
```

```python
import functools

import numpy as np
import jax
import jax.numpy as jnp
from jax import lax
from jax.experimental import pallas as pl
from jax.experimental.pallas import tpu as pltpu

F32 = jnp.float32
BF16 = jnp.bfloat16
I32 = jnp.int32

D_MODEL = 1024
HEAD_DIM = 64
N_HEADS = 16
KV_HEADS = 4
CHUNK = 64
CHUNK_SHIFT = 6
WINDOW = 128
WIN_CHUNKS = WINDOW // CHUNK
EPS = 1e-6
ROPE_BASE = 10000.0
MLA_NOPE, MLA_ROPE, MLA_Q_LORA, MLA_KV_LORA = 64, 32, 384, 256
IDX_HEADS, IDX_DIM, TOPK_MAX = 8, 64, 256
LOG2E = 1.4426950408889634
NEG = -1e30
INT_MIN = -(2 ** 31)
INT_MAX = 2 ** 31 - 1
I16 = jnp.int16
I16_MIN, I16_MAX = -(2 ** 15), 2 ** 15 - 1

TS = 512
TQ = 256
TQ_FLASH = 2048
TK = 512
TQ_DEC = 128
BD_LANES = 256
SUM_ROWS = 16
KT_LANES = 256
VMEM_LIMIT = 56 * 1024 * 1024
DSA_GROUPS = 2
DSA_VMEM_LIMIT = 62 * 1024 * 1024


def _row_tile(r):
    return TS if r % TS == 0 else r


def _cparams(sem, vmem=VMEM_LIMIT):
    return pltpu.CompilerParams(dimension_semantics=sem, vmem_limit_bytes=vmem)


def _dotf(a, b):
    return jnp.dot(a, b, preferred_element_type=F32)


def _dot_nt(a, b):
    return lax.dot_general(a, b, (((1,), (1,)), ((), ())), preferred_element_type=F32)


def _dot_tn(a, b):
    return lax.dot_general(a, b, (((0,), (0,)), ((), ())), preferred_element_type=F32)


def _split3(x):
    hi = x.astype(BF16)
    r = x - hi.astype(F32)
    mid = r.astype(BF16)
    lo = (r - mid.astype(F32)).astype(BF16)
    return hi, mid, lo


def _silu(x):
    return x / (1.0 + jnp.exp(-x))


def _full_spec(arr):
    nd = arr.ndim
    return pl.BlockSpec(arr.shape, lambda *_: (0,) * nd)


def _row_spec(arr, ts):
    if arr.shape[0] == 1:
        return pl.BlockSpec((1, arr.shape[1]), lambda i: (0, 0))
    return pl.BlockSpec((ts, arr.shape[1]), lambda i: (i, 0))


def _col_spec(arr, ts):
    return pl.BlockSpec((arr.shape[0], ts), lambda i: (0, i))


def _ada_body(c_ref, w_ref, b_ref, o_ref):
    a = _silu(c_ref[...])
    w = w_ref[0]
    a_hi = a.astype(BF16)
    a_lo = (a - a_hi.astype(F32)).astype(BF16)
    w_hi = w.astype(BF16)
    w_lo = (w - w_hi.astype(F32)).astype(BF16)
    o_ref[0] = _dotf(a_hi, w_hi) + _dotf(a_hi, w_lo) + _dotf(a_lo, w_hi) + b_ref[0]


def _ada_mod(c_all, ada_w, ada_b):
    depth, d, n3 = ada_w.shape
    bp = c_all.shape[0]
    tn = 768
    return pl.pallas_call(
        _ada_body,
        grid=(depth, n3 // tn),
        in_specs=[
            pl.BlockSpec((bp, d), lambda l, j: (0, 0)),
            pl.BlockSpec((1, d, tn), lambda l, j: (l, 0, j)),
            pl.BlockSpec((1, 1, tn), lambda l, j: (l, 0, j)),
        ],
        out_specs=pl.BlockSpec((1, bp, tn), lambda l, j: (l, 0, j)),
        out_shape=jax.ShapeDtypeStruct((depth, bp, n3), F32),
        compiler_params=_cparams(("arbitrary", "arbitrary")),
        name="ada_mod",
    )(c_all, ada_w, ada_b.reshape(depth, 1, n3))


def _prenorm(x_ref, g_ref, sc_ref, sh_ref):
    x = x_ref[...]
    ms = jnp.mean(x * x, axis=-1, keepdims=True)
    xn = x * lax.rsqrt(ms + EPS) * g_ref[...]
    return (xn * (1.0 + sc_ref[...]) + sh_ref[...]).astype(BF16)


def _group_sumsq(y, bd_ref):
    sq = y * y
    hi = sq.astype(BF16)
    lo = (sq - hi.astype(F32)).astype(BF16)
    bd = bd_ref[...]
    chunks = []
    for c in range(y.shape[1] // BD_LANES):
        sl = slice(c * BD_LANES, (c + 1) * BD_LANES)
        chunks.append(_dotf(hi[:, sl], bd) + _dotf(lo[:, sl], bd))
    return jnp.concatenate(chunks, axis=1) if len(chunks) > 1 else chunks[0]


def _head_rms_rows(q, g_col):
    ms = jnp.mean(q * q, axis=0, keepdims=True)
    return q * lax.rsqrt(ms + EPS) * g_col


def _place_rows(piece, row0, total):
    ts = piece.shape[1]
    parts = []
    if row0 > 0:
        parts.append(jnp.zeros((row0, ts), F32))
    parts.append(piece)
    rest = total - row0 - piece.shape[0]
    if rest > 0:
        parts.append(jnp.zeros((rest, ts), F32))
    return jnp.concatenate(parts, axis=0) if len(parts) > 1 else piece


def _proj_call(body, name, r, ts, row_in, col_in, const_in, outs):
    in_specs = ([_row_spec(a, ts) for a in row_in] + [_col_spec(a, ts) for a in col_in]
                + [_full_spec(a) for a in const_in])
    return pl.pallas_call(
        body,
        grid=(r // ts,),
        in_specs=in_specs,
        out_specs=[pl.BlockSpec(blk, im) for (_, _, blk, im) in outs],
        out_shape=[jax.ShapeDtypeStruct(s, dt) for (s, dt, _, _) in outs],
        compiler_params=_cparams(("arbitrary",)),
        name=name,
    )(*row_in, *col_in, *const_in)


def _rows_out(r, c, ts, dtype=F32):
    return ((r, c), dtype, (ts, c), lambda i: (i, 0))


def _cols_out(c, r, ts, dtype=F32):
    return ((c, r), dtype, (c, ts), lambda i: (0, i))


def _qp_out(r, ts):
    return ((N_HEADS, KT_LANES, r), BF16, (N_HEADS, KT_LANES, ts), lambda i: (0, 0, i))


def _mla_in_body(x_ref, sc_ref, sh_ref, cos_ref, sin_ref, g_ref, gqa, gkva, gkr, gkrp,
                 wcq, wckv, wkr, wkrp, wzt, cqn_o, lat_o, kr_o, zt_o):
    h = _prenorm(x_ref, g_ref, sc_ref, sh_ref)
    cq = _dotf(h, wcq[...])
    cqn_o[...] = (cq * lax.rsqrt(jnp.mean(cq * cq, axis=-1, keepdims=True) + EPS) * gqa[...]).astype(BF16)
    ckv = _dotf(h, wckv[...])
    lat_o[...] = ckv * lax.rsqrt(jnp.mean(ckv * ckv, axis=-1, keepdims=True) + EPS) * gkva[...]
    kr = _dotf(h, wkr[...])
    krp = _dotf(h, wkrp[...])
    inv = lax.rsqrt(jnp.sum(kr * kr, axis=-1, keepdims=True) * (1.0 / MLA_ROPE) + EPS)
    kr_o[...] = (kr * gkr[...] * cos_ref[...] + krp * gkrp[...] * sin_ref[...]) * inv
    zt_o[...] = _dot_nt(wzt[...], h)


def _mla_q_body(cqn_ref, cos_ref, sin_ref, wqt, gqn, gqr, qp_o):
    qt = _dot_nt(wqt[...], cqn_ref[...])
    ts = qt.shape[1]
    sc = (MLA_NOPE + MLA_ROPE) ** -0.5 * LOG2E
    c = cos_ref[...]
    s = sin_ref[...]
    half = MLA_ROPE // 2
    width = MLA_NOPE + MLA_ROPE
    for h in range(N_HEADS):
        qn = _head_rms_rows(qt[width * h:width * h + MLA_NOPE], gqn[...]) * sc
        qr = _head_rms_rows(qt[width * h + MLA_NOPE:width * (h + 1)], gqr[...]) * sc
        x1, x2 = qr[:half], qr[half:]
        o1 = x1 * c - x2 * s
        o2 = x2 * c + x1 * s
        a = h % 2
        pieces = [qn, jnp.zeros((HEAD_DIM, ts), F32)]
        if a:
            pieces = pieces[::-1]
        pieces += [o1, o2, jnp.zeros((KT_LANES - 2 * HEAD_DIM - MLA_ROPE, ts), F32)]
        qp_o[h] = jnp.concatenate(pieces, axis=0).astype(BF16)


def _mla_kv_body(lat_ref, kr_ref, gkn, wkn, wvt, bd, kt_o, vt_o):
    lat = lat_ref[...].astype(BF16)
    kn = _dotf(lat, wkn[...])
    ss = _group_sumsq(kn, bd)
    kn = kn * lax.rsqrt(ss * (1.0 / MLA_NOPE) + EPS) * gkn[...]
    kr = kr_ref[...].astype(BF16)
    for p in range(N_HEADS // 2):
        kt_o[p, :, 0:128] = kn[:, 128 * p:128 * (p + 1)].astype(BF16)
        kt_o[p, :, 128:256] = kr
    vt_o[0] = _dot_nt(wvt[...], lat).astype(BF16)


def _q_heads_to_qp(qt, gq, qp_o, row_of_head, extra_of_head=None):
    sc = HEAD_DIM ** -0.5 * LOG2E
    ts = qt.shape[1]
    for h in range(N_HEADS):
        qh = _head_rms_rows(qt[HEAD_DIM * h:HEAD_DIM * (h + 1)], gq[...]) * sc
        blk = _place_rows(qh, row_of_head(h), KT_LANES)
        if extra_of_head is not None:
            lo, hi = extra_of_head(h)
            rows = lax.broadcasted_iota(I32, (KT_LANES, ts), 0)
            blk = blk + jnp.where((rows >= lo) & (rows < hi), 1.0, 0.0)
        qp_o[h] = blk.astype(BF16)


def _dsa_in_body(x_ref, sc_ref, sh_ref, g_ref, gk, gq, wqt, wk, wv, wqit, wki, wwit, wzt, bd,
                 qp_o, k_o, v_o, qi_o, ki_o, wi_o, zt_o):
    h = _prenorm(x_ref, g_ref, sc_ref, sh_ref)
    _q_heads_to_qp(_dot_nt(wqt[...], h), gq, qp_o, lambda hh: HEAD_DIM * (hh // (N_HEADS // KV_HEADS)))
    k = _dotf(h, wk[...])
    k_o[...] = k * lax.rsqrt(_group_sumsq(k, bd) * (1.0 / HEAD_DIM) + EPS) * gk[...]
    v_o[...] = _dotf(h, wv[...])
    qit = _dot_nt(wqit[...], h)
    for ih in range(IDX_HEADS):
        qi_o[ih] = qit[IDX_DIM * ih:IDX_DIM * (ih + 1)].astype(BF16)
    ki_o[...] = _dotf(h, wki[...])
    wi_o[...] = _dot_nt(wwit[...], h)
    zt_o[...] = _dot_nt(wzt[...], h)


def _swa_in_body(x_ref, sc_ref, sh_ref, g_ref, gk, gq, wqt, wk, wv, wzt, bd, qp_o, k_o, v_o, zt_o):
    h = _prenorm(x_ref, g_ref, sc_ref, sh_ref)
    _q_heads_to_qp(_dot_nt(wqt[...], h), gq, qp_o, lambda hh: HEAD_DIM * (hh // (N_HEADS // KV_HEADS)))
    k = _dotf(h, wk[...])
    k_o[...] = k * lax.rsqrt(_group_sumsq(k, bd) * (1.0 / HEAD_DIM) + EPS) * gk[...]
    v_o[...] = _dotf(h, wv[...])
    zt_o[...] = _dot_nt(wzt[...], h)


def _fox_in_body(x_ref, sc_ref, sh_ref, g_ref, gk, bf, gq, wqt, wk, wv, wf, wzt, bd,
                 qp_o, k_o, v_o, lf_o, zt_o):
    h = _prenorm(x_ref, g_ref, sc_ref, sh_ref)
    _q_heads_to_qp(_dot_nt(wqt[...], h), gq, qp_o, lambda hh: HEAD_DIM * (hh % 2),
                   lambda hh: (128 + 3 * (hh % 2), 128 + 3 * (hh % 2) + 3))
    k = _dotf(h, wk[...])
    k_o[...] = k * lax.rsqrt(_group_sumsq(k, bd) * (1.0 / HEAD_DIM) + EPS) * gk[...]
    v_o[...] = _dotf(h, wv[...])
    f = _dotf(h, wf[...]) + bf[...]
    lf_o[...] = jnp.minimum(f, 0.0) - jnp.log1p(jnp.exp(-jnp.abs(f)))
    zt_o[...] = _dot_nt(wzt[...], h)


def _out_body(x_ref, gate_ref, ot_ref, zt_ref, wo, o_ref):
    u = (ot_ref[...] * _silu(zt_ref[...])).astype(BF16)
    o_ref[...] = x_ref[...] + gate_ref[...] * _dot_tn(u, wo[...])


def _out_proj(x, gate, ot, zt, wo):
    r = x.shape[0]
    ts = _row_tile(r)
    (out,) = _proj_call(_out_body, "out_proj", r, ts, [x, gate], [ot, zt], [wo],
                        [_rows_out(r, D_MODEL, ts)])
    return out


def _src_arrays(src):
    past, new = src
    return [new] if past is None else [past, new]


def _src_specs(src):
    past, new = src
    c = new.shape[2]
    if past is None:
        return [pl.BlockSpec((1, TK, c), lambda bb, j: (bb, j, 0))]
    npb = past.shape[1] // TK
    return [pl.BlockSpec((1, TK, c), lambda bb, j: (bb, jnp.minimum(j, npb - 1), 0)),
            pl.BlockSpec((1, TK, c), lambda bb, j: (bb, jnp.maximum(j - npb, 0), 0))]


def _src_rows(src):
    past, new = src
    return new.shape[1] + (0 if past is None else past.shape[1])


def _src_load(refs, src_past_blocks):
    if src_past_blocks is None:
        return refs[0][0]
    return jnp.where(pl.program_id(1) < src_past_blocks, refs[0][0], refs[1][0])


def _split_refs(refs, past_blocks):
    vals, pos = [], 0
    for npb in past_blocks:
        cnt = 1 if npb is None else 2
        vals.append(_src_load(refs[pos:pos + cnt], npb))
        pos += cnt
    return vals, refs[pos:]


def _past_blocks(srcs):
    return tuple(None if s[0] is None else s[0].shape[1] // TK for s in srcs)


def _gqa_prep_body(*refs, past_blocks):
    vals, outs = _split_refs(refs, past_blocks)
    outs[0][0] = vals[0].astype(BF16)
    outs[1][0, 0] = vals[1].T.astype(BF16)
    if len(vals) > 2:
        outs[2][0] = vals[2].astype(BF16)


def _gqa_prep(k, v, ki=None):
    srcs = [k, v] + ([ki] if ki is not None else [])
    b, c = k[1].shape[0], k[1].shape[2]
    lp = _src_rows(k)
    nkb = lp // TK
    out_specs = [pl.BlockSpec((1, TK, c), lambda bb, j: (bb, j, 0)),
                 pl.BlockSpec((1, 1, c, TK), lambda bb, j: (bb, j, 0, 0))]
    out_shape = [jax.ShapeDtypeStruct((b, lp, c), BF16), jax.ShapeDtypeStruct((b, nkb, c, TK), BF16)]
    if ki is not None:
        ci = ki[1].shape[2]
        out_specs.append(pl.BlockSpec((1, TK, ci), lambda bb, j: (bb, j, 0)))
        out_shape.append(jax.ShapeDtypeStruct((b, lp, ci), BF16))
    return pl.pallas_call(
        functools.partial(_gqa_prep_body, past_blocks=_past_blocks(srcs)),
        grid=(b, nkb), in_specs=[sp for s in srcs for sp in _src_specs(s)],
        out_specs=out_specs, out_shape=out_shape,
        compiler_params=_cparams(("arbitrary", "arbitrary")), name="gqa_prep",
    )(*[a for s in srcs for a in _src_arrays(s)])


def _fox_prep_body(*refs, past_blocks):
    (k, v, lf), (sel_ref, kt_o, vt_o, carry_ref) = _split_refs(refs, past_blocks)
    kb = pl.program_id(1)

    @pl.when(kb == 0)
    def _():
        carry_ref[...] = jnp.zeros_like(carry_ref)

    r = lax.broadcasted_iota(I32, (TK, TK), 0)
    c = lax.broadcasted_iota(I32, (TK, TK), 1)
    tri = jnp.where(c <= r, 1.0, 0.0).astype(BF16)
    hi, mid, lo = _split3(lf)
    cum = _dotf(tri, hi) + _dotf(tri, mid) + _dotf(tri, lo) + carry_ref[...]
    carry_ref[...] = cum[TK - 1:TK, :]
    a, b, d = _split3(-(cum * LOG2E))
    ex = _dotf(a, sel_ref[0]) + _dotf(b, sel_ref[1]) + _dotf(d, sel_ref[2])
    for p in range(N_HEADS // 2):
        kt_o[p, :, 0:128] = k[:, 128 * p:128 * (p + 1)].astype(BF16)
        kt_o[p, :, 128:256] = ex[:, 128 * p:128 * (p + 1)].astype(BF16)
    vt_o[0, 0] = v.T.astype(BF16)


def _fox_sel():
    sel = np.zeros((3, 128, N_HEADS * HEAD_DIM), np.float32)
    for h in range(N_HEADS):
        for j in range(3):
            sel[j, h, 128 * (h // 2) + 3 * (h % 2) + j] = 1.0
    return jnp.asarray(sel, BF16)


def _fox_prep(k, v, lf128):
    srcs = [k, v, lf128]
    b, c = k[1].shape[0], k[1].shape[2]
    lp = _src_rows(k)
    nkb = lp // TK
    npair = N_HEADS // 2
    sel = _fox_sel()
    return pl.pallas_call(
        functools.partial(_fox_prep_body, past_blocks=_past_blocks(srcs)),
        grid=(b, nkb),
        in_specs=[sp for s in srcs for sp in _src_specs(s)] + [_full_spec(sel)],
        out_specs=[pl.BlockSpec((npair, TK, KT_LANES), lambda bb, j: (bb, j, 0)),
                   pl.BlockSpec((1, 1, c, TK), lambda bb, j: (bb, j, 0, 0))],
        out_shape=[jax.ShapeDtypeStruct((b * npair, lp, KT_LANES), BF16),
                   jax.ShapeDtypeStruct((b, nkb, c, TK), BF16)],
        scratch_shapes=[pltpu.VMEM((1, 128), F32)],
        compiler_params=_cparams(("arbitrary", "arbitrary")), name="fox_prep",
    )(*[a for s in srcs for a in _src_arrays(s)], sel)


def _visible_end(qpos, mode, n_keys):
    if mode == "causal":
        end = qpos + 1
    else:
        end = ((qpos >> CHUNK_SHIFT) + 1) << CHUNK_SHIFT
    return jnp.minimum(end, n_keys)


def _softmax_update(s, mx8, m_ref, c8=None, lane0=0):
    m8 = m_ref[:, lane0:]
    m_new8 = jnp.maximum(m8, mx8)
    alpha8 = jnp.exp2(m8 - m_new8)
    shift = m_new8[0:1] if c8 is None else (m_new8 - c8)[0:1]
    m_ref[:, lane0:] = m_new8
    return jnp.exp2(s - shift).astype(BF16), alpha8[0:1]


def _pv_and_sum(vt_blk, pb):
    ones = jnp.ones((SUM_ROWS, vt_blk.shape[1]), BF16)
    return _dotf(jnp.concatenate([vt_blk, ones], axis=0), pb)


def _run_two_stage(n_full, n_masked, stage_a, stage_b, trim=lambda j: 0):
    odd = n_full % 2

    @pl.when(odd == 1)
    def _():
        stage_a(0, 0, False, 0)
        stage_b(0, 0, False, 0)

    n_pairs = (n_full - odd) // 2

    @pl.when(n_pairs >= 1)
    def _():
        stage_a(odd, 0, False, 0)

        def body(u, _):
            kb = odd + 2 * u
            stage_b(kb, 0, False, 0)
            stage_a(kb + 1, 1, False, 0)
            stage_b(kb + 1, 1, False, 0)
            stage_a(kb + 2, 0, False, 0)
            return 0

        lax.fori_loop(0, n_pairs - 1, body, 0)
        stage_b(n_full - 2, 0, False, 0)
        stage_a(n_full - 1, 1, False, 0)
        stage_b(n_full - 1, 1, False, 0)
        stage_a(n_full, 0, True, trim(0))

    @pl.when(n_pairs < 1)
    def _():
        stage_a(n_full, 0, True, trim(0))

    for j in range(1, n_masked):
        stage_b(n_full + j - 1, (j - 1) % 2, True, trim(j - 1))
        stage_a(n_full + j, j % 2, True, trim(j))
    stage_b(n_full + n_masked - 1, (n_masked - 1) % 2, True, trim(n_masked - 1))


def _flash_body(qp_ref, kt_ref, vt_ref, o_ref, s_buf, mx_buf, m_ref, acc_ref, *,
                tq, mode, q0, n_keys, paired, n_masked):
    i = pl.program_id(1)
    q_first = q0 + i * tq
    n_full = _visible_end(q_first, mode, n_keys) // TK
    ns = N_HEADS // 2 if paired else 1
    per = 2 if paired else 1
    ws, rows = per * tq, per * HEAD_DIM
    w = ns * ws
    ik = lax.broadcasted_iota(I32, (TK, ws), 0)
    iq = lax.broadcasted_iota(I32, (TK, tq), 1)
    qpos = q_first + (jnp.concatenate([iq] * per, axis=1) if paired else iq)
    qs = [jnp.concatenate([qp_ref[per * si + a] for a in range(per)], axis=1) if paired else qp_ref[si]
          for si in range(ns)]

    m_ref[...] = jnp.full((8, w), NEG, F32)
    acc_ref[...] = jnp.zeros((rows + SUM_ROWS, w), F32)

    def stage_a(kb, slot, masked, lane0):
        assert lane0 == 0 or ns == 1
        k0 = pl.multiple_of(kb * TK, TK)
        wl = ws - lane0
        if masked:
            if lane0:
                kpos = k0 + lax.broadcasted_iota(I32, (TK, wl), 0)
                qp = q_first + lane0 + lax.broadcasted_iota(I32, (TK, wl), 1)
            else:
                kpos, qp = k0 + ik, qpos
            valid = (kpos <= qp) if mode == "causal" else ((kpos >> CHUNK_SHIFT) <= (qp >> CHUNK_SHIFT))
            valid = valid & (kpos < n_keys)
        for si in range(ns):
            s = _dotf(kt_ref[si, pl.ds(k0, TK), :], qs[si][:, lane0:])
            if masked:
                s = jnp.where(valid, s, NEG)
            s_buf[slot, :, si * ws + lane0:(si + 1) * ws] = s
            mx_buf[slot, :, si * ws + lane0:(si + 1) * ws] = jnp.broadcast_to(
                jnp.max(s, axis=0, keepdims=True), (8, wl))

    def stage_b(kb, slot, masked, lane0):
        del masked
        pb, alpha = _softmax_update(s_buf[slot, :, lane0:], mx_buf[slot, :, lane0:], m_ref, lane0=lane0)
        for si in range(ns):
            loc = slice(si * ws, (si + 1) * ws - lane0)
            glob = slice(si * ws + lane0, (si + 1) * ws)
            acc_ref[:, glob] = alpha[:, loc] * acc_ref[:, glob] + _pv_and_sum(
                vt_ref[0, kb, rows * si:rows * (si + 1), :], pb[:, loc])

    trim = (lambda j: j * TK) if (not paired and tq > TK) else (lambda j: 0)
    _run_two_stage(n_full, n_masked, stage_a, stage_b, trim)
    out = acc_ref[0:rows, :] / acc_ref[rows:rows + 1, :]
    for si in range(ns):
        for a in range(per):
            o_ref[per * si + a] = out[HEAD_DIM * a:HEAD_DIM * (a + 1), si * ws + tq * a:si * ws + tq * (a + 1)]


def _visible_end_static(qpos, mode, n_keys):
    end = qpos + 1 if mode == "causal" else ((qpos >> CHUNK_SHIFT) + 1) << CHUNK_SHIFT
    return min(end, n_keys)


def _flash(qp, kt, vt, *, tq, mode, q0, n_real_q, n_keys, paired):
    bh, _, sq = qp.shape
    lp = kt.shape[1]
    nkb = vt.shape[1]
    assert (q0 % TK == 0) and (tq % TK == 0 or sq == tq)
    n_masked = (-(-_visible_end_static(q0 + n_real_q - 1, mode, n_keys) // TK)
                - _visible_end_static(q0, mode, n_keys) // TK)
    body = functools.partial(_flash_body, tq=tq, mode=mode, q0=q0, n_keys=n_keys, paired=paired,
                             n_masked=n_masked)
    if not paired:
        ns, w, rows = 1, tq, HEAD_DIM
        kt_spec = pl.BlockSpec((1, lp, KT_LANES), lambda g, i: (g // 2, 0, 0))
        vt_spec = pl.BlockSpec((1, nkb, HEAD_DIM, TK), lambda g, i: (g // N_HEADS, 0, g % N_HEADS, 0))
    else:
        ns, w, rows = N_HEADS, N_HEADS * tq, 2 * HEAD_DIM
        kt_spec = pl.BlockSpec((ns // 2, lp, KT_LANES), lambda g, i: (g, 0, 0))
        vt_spec = pl.BlockSpec((1, nkb, ns * HEAD_DIM, TK), lambda g, i: (g, 0, 0, 0))
    return pl.pallas_call(
        body,
        grid=(bh // ns, sq // tq),
        in_specs=[pl.BlockSpec((ns, KT_LANES, tq), lambda g, i: (g, 0, i)), kt_spec, vt_spec],
        out_specs=pl.BlockSpec((ns, HEAD_DIM, tq), lambda g, i: (g, 0, i)),
        out_shape=jax.ShapeDtypeStruct((bh, HEAD_DIM, sq), F32),
        scratch_shapes=[pltpu.VMEM((2, TK, w), F32), pltpu.VMEM((2, 8, w), F32), pltpu.VMEM((8, w), F32),
                        pltpu.VMEM((rows + SUM_ROWS, w), F32)],
        compiler_params=_cparams(("arbitrary", "arbitrary")),
        name="flash_" + mode,
    )(qp, kt, vt)


def _swa_body(hp_ref, qp_ref, *refs, tq, q0, n_pieces, n_rows):
    k_refs, v_refs, o_ref = refs[:n_pieces], refs[n_pieces:2 * n_pieces], refs[2 * n_pieces]
    i = pl.program_id(1)
    q_first = q0 + i * tq
    k = jnp.concatenate([r[0].astype(BF16) for r in k_refs], axis=0)
    v = jnp.concatenate([r[0].astype(BF16) for r in v_refs], axis=0)
    kw = k.shape[0]
    row = lax.broadcasted_iota(I32, (kw, tq), 0)
    kpos = (q_first - WINDOW) + row
    qpos = q_first + lax.broadcasted_iota(I32, (kw, tq), 1)
    qc = qpos >> CHUNK_SHIFT
    kc = kpos >> CHUNK_SHIFT
    valid = (kpos >= 0) & (kc <= qc) & (qc - kc <= WIN_CHUNKS) & (row < n_rows)
    dist = jnp.abs(qpos - kpos).astype(F32)
    group = N_HEADS // KV_HEADS

    def lanes(x):
        return jnp.concatenate([x] * group, axis=1)

    for n in range(KV_HEADS):
        h0 = n * group

        def per_head(col):
            return jnp.concatenate(
                [jnp.broadcast_to(hp_ref[h0 + g][:, col:col + 1], (1, tq)) for g in range(group)], axis=1)

        q = jnp.concatenate([qp_ref[h0 + g] for g in range(group)], axis=1)
        slope2, sink2 = per_head(0), per_head(1)
        s = _dotf(k, q) - slope2 * lanes(dist)
        s = jnp.where(lanes(valid), s, NEG)
        m = jnp.maximum(jnp.max(s, axis=0, keepdims=True), sink2)
        p = jnp.exp2(s - m)
        l = jnp.sum(p, axis=0, keepdims=True) + jnp.exp2(sink2 - m)
        acc = _dot_tn(v, p.astype(BF16))
        out = acc[HEAD_DIM * n:HEAD_DIM * (n + 1)] / l
        for g in range(group):
            o_ref[h0 + g] = out[:, g * tq:(g + 1) * tq]


def _swa_attend(hp, qp, k_pieces, v_pieces, *, tq, q0, n_rows):
    b = k_pieces[0][0].shape[0]
    sq = qp.shape[2]
    n_pieces = len(k_pieces)

    def spec(piece):
        _, rows, idx = piece
        return pl.BlockSpec((1, rows, KV_HEADS * HEAD_DIM), lambda bb, i: (bb, idx(i), 0))

    body = functools.partial(_swa_body, tq=tq, q0=q0, n_pieces=n_pieces, n_rows=n_rows)
    return pl.pallas_call(
        body,
        grid=(b, sq // tq),
        in_specs=[_full_spec(hp), pl.BlockSpec((N_HEADS, KT_LANES, tq), lambda bb, i: (bb, 0, i))]
        + [spec(p) for p in k_pieces] + [spec(p) for p in v_pieces],
        out_specs=pl.BlockSpec((N_HEADS, HEAD_DIM, tq), lambda bb, i: (bb, 0, i)),
        out_shape=jax.ShapeDtypeStruct((b * N_HEADS, HEAD_DIM, sq), F32),
        compiler_params=_cparams(("arbitrary", "arbitrary")),
        name="swa_attend",
    )(hp, qp, *[p[0] for p in k_pieces], *[p[0] for p in v_pieces])


def _dsa_body(hp_ref, qi_ref, wi_ref, qp_ref, ki_ref, kt_ref, vt_ref, o_ref, hi_ref, lo_ref,
              s_buf, mx_buf, m_ref, acc_ref, *, tq, q0, n_real_q, n_keys, topk):
    i = pl.program_id(1)
    q_first = q0 + i * tq
    q_last = q_first + (n_real_q - 1)
    n_tot = (_visible_end(q_last, "chunk", n_keys) + (TK - 1)) // TK
    n_past = jnp.minimum(q_first, n_keys) // TK
    ik = lax.broadcasted_iota(I32, (TK, tq), 0)
    iq = lax.broadcasted_iota(I32, (TK, tq), 1)
    qpos = q_first + iq
    tf = float(topk)

    def blk(kb):
        return pl.ds(pl.multiple_of(kb * TK, TK), TK)

    def to_key(x):
        bits = pltpu.bitcast(x, I32)
        return jnp.where(bits < 0, bits ^ INT_MAX, bits)

    def score_blk(kb, diag):
        ki = ki_ref[0, blk(kb), :]
        acc = jnp.zeros((TK, tq), F32)
        for h in range(IDX_HEADS):
            acc = acc + wi_ref[0, h:h + 1, :] * jnp.maximum(_dotf(ki, qi_ref[0, h]), 0.0)
        key = to_key(acc)
        if diag:
            kpos = kb * TK + ik
            valid = ((kpos >> CHUNK_SHIFT) <= (qpos >> CHUNK_SHIFT)) & (kpos < n_keys)
            key = jnp.where(valid, key, INT_MIN)
        hi_ref[blk(kb), :] = (key >> 16).astype(I16)
        lo_ref[blk(kb), :] = ((key & 0xFFFF) - 32768).astype(I16)
        return 0

    lax.fori_loop(0, n_past, lambda kb, c: score_blk(kb, False), 0)
    lax.fori_loop(n_past, n_tot, lambda kb, c: score_blk(kb, True), 0)

    def count_ge(ref, mid):
        mid16 = mid.astype(I16)

        def body(kb, acc):
            ge = jnp.where(ref[blk(kb), :] >= mid16, jnp.int16(1), jnp.int16(0))
            parts = [ge[16 * r:16 * (r + 1)] for r in range(TK // 16)]
            while len(parts) > 1:
                parts = [parts[j] + parts[j + 1] for j in range(0, len(parts), 2)]
            return acc + parts[0]

        acc = lax.fori_loop(0, n_tot, body, jnp.zeros((16, tq), I16))
        return jnp.sum(acc.astype(I32), axis=0, keepdims=True).astype(F32)

    def bisect_step(ref, target, st):
        lo, hi, cl, ch = st
        mid = (lo + hi) >> 1
        cnt = count_ge(ref, mid)
        ge = cnt >= target
        return jnp.where(ge, mid, lo), jnp.where(ge, hi, mid), jnp.where(ge, cnt, cl), jnp.where(ge, ch, cnt)

    qrow = q_first + lax.broadcasted_iota(I32, (1, tq), 1)
    n_vis = _visible_end(qrow, "chunk", n_keys).astype(F32)
    zero = jnp.zeros((1, tq), F32)
    st1 = (jnp.full((1, tq), I16_MIN + 1, I32), jnp.full((1, tq), I16_MAX + 1, I32), n_vis, zero)
    h_thr, _, cl1, ch1 = lax.fori_loop(0, 16, lambda _, st: bisect_step(hi_ref, tf, st), st1)
    h16 = h_thr.astype(I16)

    def mask_lo(kb, _):
        lo_ref[blk(kb), :] = jnp.where(hi_ref[blk(kb), :] == h16, lo_ref[blk(kb), :], jnp.int16(I16_MIN))
        return 0

    lax.fori_loop(0, n_tot, mask_lo, 0)
    t2 = tf - ch1

    def cond(st):
        return (st[0] < 16) & (st[2] > 0.5)

    def body(st):
        lo, hi, cl, ch = bisect_step(lo_ref, t2, st[1])
        done = (cl <= t2) | (hi - lo <= 1)
        return st[0] + 1, (lo, hi, cl, ch), jnp.sum(jnp.where(done, 0.0, 1.0))

    cl2_0 = cl1 - ch1
    st2 = (jnp.full((1, tq), I16_MIN, I32), jnp.full((1, tq), I16_MAX + 1, I32), cl2_0, zero)
    _, (l_thr, _, cl2, ch2), _ = lax.while_loop(
        cond, body, (jnp.int32(0), st2, jnp.sum(jnp.where(cl2_0 > t2, 1.0, 0.0))))
    l16 = l_thr.astype(I16)

    need = t2 - ch2

    @pl.when(jnp.sum(jnp.where(cl2 > t2, 1.0, 0.0)) > 0.5)
    def _():
        r = lax.broadcasted_iota(I32, (TK, TK), 0)
        c = lax.broadcasted_iota(I32, (TK, TK), 1)
        tri = jnp.where(c < r, 1.0, 0.0).astype(BF16)

        def fix(kb, carry):
            hb = hi_ref[blk(kb), :]
            e16 = jnp.where(hb == h16, jnp.where(lo_ref[blk(kb), :] == l16, jnp.int16(1), jnp.int16(0)),
                            jnp.int16(0))
            e = e16.astype(I32).astype(F32)
            before = _dotf(tri, e.astype(BF16)) + carry
            drop = jnp.where((e > 0.5) & (before >= need), 1, 0).astype(I16)
            hi_ref[blk(kb), :] = jnp.where(drop == jnp.int16(1), jnp.int16(I16_MIN), hb)
            return carry + jnp.sum(e, axis=0, keepdims=True)

        lax.fori_loop(0, n_tot, fix, jnp.zeros((1, tq), F32))

    def to_bias(kb, _):
        hb = hi_ref[blk(kb), :]
        zero_b, neg_b = jnp.bfloat16(0.0), jnp.bfloat16(NEG)
        at_thr = jnp.where(lo_ref[blk(kb), :] >= l16, zero_b, neg_b)
        bias = jnp.where(hb > h16, zero_b, jnp.where(hb == h16, at_thr, neg_b))
        hi_ref[blk(kb), :] = pltpu.bitcast(bias, I16)
        return 0

    lax.fori_loop(0, n_tot, to_bias, 0)

    group = N_HEADS // KV_HEADS
    heads = DSA_GROUPS * group
    gw = group * tq
    wide = heads * tq

    def lanes(x):
        return jnp.concatenate([x] * heads, axis=1)

    def group_body(n, _):
        h0 = n * heads
        q = jnp.concatenate([qp_ref[h0 + g] for g in range(heads)], axis=1)
        slope2 = jnp.concatenate(
            [jnp.broadcast_to(hp_ref[h0 + g][:, 0:1], (1, tq)) for g in range(heads)], axis=1)
        slope8 = jnp.broadcast_to(slope2, (8, wide))
        a_tab = slope2 * lanes(ik.astype(F32))
        vrows = [pl.ds(pl.multiple_of((n * DSA_GROUPS + gi) * HEAD_DIM, HEAD_DIM), HEAD_DIM)
                 for gi in range(DSA_GROUPS)]
        m_ref[...] = jnp.full((8, wide), NEG, F32)
        acc_ref[...] = jnp.zeros((HEAD_DIM + SUM_ROWS, wide), F32)

        def c8(kb):
            return slope8 * (kb * TK - q_first).astype(F32)

        def stage_a(kb, slot, diag, lane0):
            del lane0
            bias = pltpu.bitcast(hi_ref[blk(kb), :], jnp.bfloat16).astype(F32)
            s = _dotf(kt_ref[0, blk(kb), :], q) + lanes(bias)
            if diag:
                kpos = kb * TK + ik
                rel = iq.astype(F32) - jnp.abs(qpos - kpos).astype(F32)
                s = s + slope2 * lanes(rel)
            else:
                s = s + a_tab
            mx8 = jnp.broadcast_to(jnp.max(s, axis=0, keepdims=True), (8, wide))
            s_buf[slot] = s
            mx_buf[slot] = mx8 if diag else mx8 + c8(kb)

        def stage_b(kb, slot, diag, lane0):
            del lane0
            pb, alpha = _softmax_update(s_buf[slot], mx_buf[slot], m_ref, None if diag else c8(kb))
            for gi in range(DSA_GROUPS):
                sl = slice(gi * gw, (gi + 1) * gw)
                acc_ref[:, sl] = alpha[:, sl] * acc_ref[:, sl] + _pv_and_sum(vt_ref[0, kb, vrows[gi], :], pb[:, sl])

        _run_two_stage(n_past, 1, stage_a, stage_b)
        out = acc_ref[0:HEAD_DIM, :] / acc_ref[HEAD_DIM:HEAD_DIM + 1, :]
        for g in range(heads):
            o_ref[h0 + g] = out[:, g * tq:(g + 1) * tq]
        return 0

    lax.fori_loop(0, KV_HEADS // DSA_GROUPS, group_body, 0)


def _dsa_attend(hp, qi, wi, qp, ki, kt, vt, *, tq, q0, n_real_q, n_keys):
    b = ki.shape[0]
    sq = qp.shape[2]
    lp = kt.shape[1]
    nkb = vt.shape[1]
    topk = min(TOPK_MAX, n_keys // 4)
    assert TK % tq == 0 and q0 % TK == 0 and n_real_q <= tq
    wide = DSA_GROUPS * (N_HEADS // KV_HEADS) * tq
    body = functools.partial(_dsa_body, tq=tq, q0=q0, n_real_q=n_real_q, n_keys=n_keys, topk=topk)
    once = pl.Buffered(1)
    return pl.pallas_call(
        body,
        grid=(b, sq // tq),
        in_specs=[_full_spec(hp),
                  pl.BlockSpec((1, IDX_HEADS, IDX_DIM, tq), lambda bb, i: (bb, 0, 0, i)),
                  pl.BlockSpec((1, IDX_HEADS, tq), lambda bb, i: (bb, 0, i)),
                  pl.BlockSpec((N_HEADS, KT_LANES, tq), lambda bb, i: (bb, 0, i)),
                  pl.BlockSpec((1, lp, IDX_DIM), lambda bb, i: (bb, 0, 0), pipeline_mode=once),
                  pl.BlockSpec((1, lp, KT_LANES), lambda bb, i: (bb, 0, 0), pipeline_mode=once),
                  pl.BlockSpec((1, nkb, KV_HEADS * HEAD_DIM, TK), lambda bb, i: (bb, 0, 0, 0),
                               pipeline_mode=once)],
        out_specs=pl.BlockSpec((N_HEADS, HEAD_DIM, tq), lambda bb, i: (bb, 0, i)),
        out_shape=jax.ShapeDtypeStruct((b * N_HEADS, HEAD_DIM, sq), F32),
        scratch_shapes=[pltpu.VMEM((lp, tq), I16), pltpu.VMEM((lp, tq), I16),
                        pltpu.VMEM((2, TK, wide), F32), pltpu.VMEM((2, 8, wide), F32),
                        pltpu.VMEM((8, wide), F32), pltpu.VMEM((HEAD_DIM + SUM_ROWS, wide), F32)],
        compiler_params=_cparams(("arbitrary", "arbitrary"), DSA_VMEM_LIMIT),
        name="dsa_attend",
    )(hp, qi, wi, qp, ki, kt, vt)


def _block_diag(c):
    g = np.arange(c) // HEAD_DIM
    return jnp.asarray(g[:, None] == g[None, :], BF16)


def _tile_gain(g, n):
    return jnp.tile(g.astype(F32), n).reshape(1, -1)


def _col(g):
    return g.astype(F32).reshape(-1, 1)


def _pad_cols(w, n):
    return jnp.pad(w, ((0, 0), (0, n - w.shape[1])))


def _rope_tables(pos):
    half = MLA_ROPE // 2
    inv = ROPE_BASE ** (-jnp.arange(half, dtype=F32) / half)
    ang = pos.astype(F32)[:, None] * inv[None, :]
    return jnp.cos(ang), jnp.sin(ang)


def _alibi_slopes():
    return np.asarray(2.0 ** (-8.0 * np.arange(1, N_HEADS + 1) / N_HEADS), dtype=np.float32)


def _head_params(sinks=None):
    hp = jnp.zeros((N_HEADS, 1, 128), F32)
    hp = hp.at[:, 0, 0].set(jnp.asarray(_alibi_slopes()) * LOG2E)
    if sinks is not None:
        hp = hp.at[:, 0, 1].set(sinks.astype(F32) * LOG2E)
    return hp


class _Stream:
    def __init__(self, batch, seq, past):
        self.b, self.s, self.p = batch, seq, past
        self.r = batch * seq
        self.decode = past > 0
        self.tq = TQ_DEC if self.decode else TQ
        self.tqf = TQ_DEC if self.decode else min(TQ_FLASH, seq)
        self.sq =self.tq if self.decode else seq
        self.n_keys = past + seq
        self.lp = -(-self.n_keys // TK) * TK
        self.pos = past + np.tile(np.arange(seq), batch)

    def pad_keys(self, past_arr, new_arr, n_keys=None, lp=None):
        n_keys = self.n_keys if n_keys is None else n_keys
        lp = self.lp if lp is None else lp
        new_arr = new_arr.reshape(self.b, self.s, -1)
        parts = [new_arr] if past_arr is None else [past_arr.astype(F32), new_arr]
        if lp > n_keys:
            parts.append(jnp.zeros((self.b, lp - n_keys, new_arr.shape[-1]), F32))
        return jnp.concatenate(parts, axis=1) if len(parts) > 1 else new_arr

    def key_source(self, past_arr, new_arr):
        new_arr = new_arr.reshape(self.b, self.s, -1)
        if past_arr is None:
            return (None, new_arr)
        assert past_arr.shape[1] == self.p and self.p % TK == 0
        pad = self.lp - self.n_keys
        return (past_arr.astype(F32), jnp.pad(new_arr, ((0, 0), (0, pad), (0, 0))))

    def qp_blocks(self, qp):
        if not self.decode:
            return qp
        x = qp.reshape(N_HEADS, KT_LANES, self.b, self.s).transpose(2, 0, 1, 3)
        x = jnp.pad(x, ((0, 0), (0, 0), (0, 0), (0, self.tq - self.s)))
        return x.reshape(self.b * N_HEADS, KT_LANES, self.tq)

    def lanes(self, x):
        n, c, _ = x.shape
        if not self.decode:
            return x[None]
        x = x.reshape(n, c, self.b, self.s).transpose(2, 0, 1, 3)
        return jnp.pad(x, ((0, 0), (0, 0), (0, 0), (0, self.tq - self.s)))

    def ot_cols(self, ot):
        if not self.decode:
            return ot.reshape(N_HEADS * HEAD_DIM, self.r)
        x = ot.reshape(self.b, N_HEADS, HEAD_DIM, self.tq)[..., :self.s]
        return x.transpose(1, 2, 0, 3).reshape(N_HEADS * HEAD_DIM, self.r)


def _mod_rows(st, mod_l, row0):
    m = mod_l[row0:row0 + st.b]
    if st.b > 1:
        m = jnp.repeat(m, st.s, axis=0)
    return m[:, :D_MODEL], m[:, D_MODEL:2 * D_MODEL], m[:, 2 * D_MODEL:]


def _mla_layer(st, x, mod, g, w, past):
    shift, scale, gate = mod
    r = st.r
    ts = _row_tile(r)
    cos, sin = _rope_tables(jnp.asarray(st.pos))
    zpad = jnp.zeros((r, 128 - MLA_ROPE), F32)
    cos_p = jnp.concatenate([cos, cos, zpad], axis=1)
    sin_p = jnp.concatenate([-sin, sin, zpad], axis=1)
    cqn, lat, kr, zt = _proj_call(
        _mla_in_body, "mla_in", r, ts,
        [x, scale, shift, cos_p, sin_p, g, w["gqa"], w["gkva"], w["gkr"], w["gkrp"]], [],
        [w["wcq"], w["wckv"], w["wkr"], w["wkrp"], w["wzt"]],
        [_rows_out(r, MLA_Q_LORA, ts, BF16), _rows_out(r, MLA_KV_LORA, ts), _rows_out(r, 128, ts),
         _cols_out(D_MODEL, r, ts)])
    (qp,) = _proj_call(_mla_q_body, "mla_q", r, ts, [cqn], [cos.T, sin.T], [w["wqt"], w["gqn"], w["gqr"]],
                       [_qp_out(r, ts)])
    past_lat, past_kr = (None, None) if past is None else past
    if past_kr is not None:
        past_kr = jnp.pad(past_kr.astype(F32), ((0, 0), (0, 0), (0, 128 - MLA_ROPE)))
    lat_all = st.pad_keys(past_lat, lat).reshape(st.b * st.lp, MLA_KV_LORA)
    kr_all = st.pad_keys(past_kr, kr).reshape(st.b * st.lp, 128)
    rk = st.b * st.lp
    npair = N_HEADS // 2
    nkb = st.lp // TK
    kt, vt = _proj_call(
        _mla_kv_body, "mla_kv", rk, TK, [lat_all, kr_all, w["gkn"]], [], [w["wkn"], w["wvt"], w["bd"]],
        [((npair * st.b, st.lp, KT_LANES), BF16, (npair, TK, KT_LANES), lambda i: (i // nkb, i % nkb, 0)),
         ((rk // TK, N_HEADS * HEAD_DIM, TK), BF16, (1, N_HEADS * HEAD_DIM, TK), lambda i: (i, 0, 0))])
    vt = vt.reshape(st.b, nkb, N_HEADS * HEAD_DIM, TK)
    ot = _flash(st.qp_blocks(qp), kt, vt, tq=st.tqf, mode="chunk", q0=st.p,
                n_real_q=st.tqf if not st.decode else st.s, n_keys=st.n_keys, paired=st.decode)
    x = _out_proj(x, gate, st.ot_cols(ot), zt, w["wo"])
    return x, (lat, kr[:, :MLA_ROPE])


def _dsa_layer(st, x, mod, g, w, past):
    shift, scale, gate = mod
    r = st.r
    ts = _row_tile(r)
    c = KV_HEADS * HEAD_DIM
    qp, k, v, qi, ki, wi, zt = _proj_call(
        _dsa_in_body, "dsa_in", r, ts, [x, scale, shift, g, w["gk"]], [],
        [w["gq"], w["wqt"], w["wk"], w["wv"], w["wqit"], w["wki"], w["wwit"], w["wzt"], w["bd"]],
        [_qp_out(r, ts), _rows_out(r, c, ts), _rows_out(r, c, ts),
         ((IDX_HEADS, IDX_DIM, r), BF16, (IDX_HEADS, IDX_DIM, ts), lambda i: (0, 0, i)),
         _rows_out(r, IDX_DIM, ts), _cols_out(IDX_HEADS, r, ts), _cols_out(D_MODEL, r, ts)])
    if past is None:
        pk = pv = pki = None
    else:
        pk, pv, pki = past[0].reshape(st.b, -1, c), past[1].reshape(st.b, -1, c), past[2]
    kt, vt, kib = _gqa_prep(st.key_source(pk, k), st.key_source(pv, v), st.key_source(pki, ki))
    qi_b = st.lanes(qi)
    wi_b = st.lanes(wi[None])[:, 0]
    ot = _dsa_attend(_head_params(), qi_b, wi_b, st.qp_blocks(qp), kib, kt, vt, tq=st.tq, q0=st.p,
                     n_real_q=st.tq if not st.decode else st.s, n_keys=st.n_keys)
    x = _out_proj(x, gate, st.ot_cols(ot), zt, w["wo"])
    return x, (k, v, ki)


def _swa_layer(st, x, mod, g, w, past):
    shift, scale, gate = mod
    r = st.r
    ts = _row_tile(r)
    c = KV_HEADS * HEAD_DIM
    qp, k, v, zt = _proj_call(
        _swa_in_body, "swa_in", r, ts, [x, scale, shift, g, w["gk"]], [],
        [w["gq"], w["wqt"], w["wk"], w["wv"], w["wzt"], w["bd"]],
        [_qp_out(r, ts), _rows_out(r, c, ts), _rows_out(r, c, ts), _cols_out(D_MODEL, r, ts)])
    k3, v3 = k.reshape(st.b, st.s, c), v.reshape(st.b, st.s, c)
    if past is None:
        per = st.tq // WINDOW
        idx = [lambda i: jnp.maximum(per * i - 1, 0), lambda i: per * i, lambda i: per * i + 1]
        k_pieces = [(k3, WINDOW, f) for f in idx]
        v_pieces = [(v3, WINDOW, f) for f in idx]
        n_rows = WINDOW + st.tq
        new = (k3[:, st.s - WINDOW:], v3[:, st.s - WINDOW:])
    else:
        win = past[0].shape[1]
        assert win == WINDOW and st.s <= WINDOW
        pad = ((0, 0), (0, WINDOW - st.s), (0, 0))
        pk, pv = past[0].reshape(st.b, win, c).astype(F32), past[1].reshape(st.b, win, c).astype(F32)
        zero = lambda i: 0
        k_pieces = [(pk, WINDOW, zero), (jnp.pad(k3, pad), WINDOW, zero)]
        v_pieces = [(pv, WINDOW, zero), (jnp.pad(v3, pad), WINDOW, zero)]
        n_rows = win + st.s
        new = (jnp.concatenate([pk, k3], axis=1)[:, st.s:], jnp.concatenate([pv, v3], axis=1)[:, st.s:])
    ot = _swa_attend(_head_params(w["sinks"]), st.qp_blocks(qp), k_pieces, v_pieces, tq=st.tq, q0=st.p,
                     n_rows=n_rows)
    x = _out_proj(x, gate, st.ot_cols(ot), zt, w["wo"])
    return x, new


def _fox_layer(st, x, mod, g, w, past):
    shift, scale, gate = mod
    r = st.r
    ts = _row_tile(r)
    c = N_HEADS * HEAD_DIM
    qp, k, v, lf, zt = _proj_call(
        _fox_in_body, "fox_in", r, ts, [x, scale, shift, g, w["gk"], w["bf"]], [],
        [w["gq"], w["wqt"], w["wk"], w["wv"], w["wf"], w["wzt"], w["bd"]],
        [_qp_out(r, ts), _rows_out(r, c, ts), _rows_out(r, c, ts), _rows_out(r, 128, ts),
         _cols_out(D_MODEL, r, ts)])
    if past is None:
        pk = pv = plf = None
    else:
        pk, pv = past[0].reshape(st.b, -1, c), past[1].reshape(st.b, -1, c)
        plf = jnp.pad(past[2].astype(F32), ((0, 0), (0, 0), (0, 128 - N_HEADS)))
    kt, vt = _fox_prep(st.key_source(pk, k), st.key_source(pv, v), st.key_source(plf, lf))
    ot = _flash(st.qp_blocks(qp), kt, vt, tq=st.tqf, mode="causal", q0=st.p,
                n_real_q=st.tqf if not st.decode else st.s, n_keys=st.n_keys, paired=st.decode)
    x = _out_proj(x, gate, st.ot_cols(ot), zt, w["wo"])
    return x, (k, v, lf[:, :N_HEADS])


def _prep_weights(mla_w_in, mla_g_qa, mla_w_qb, mla_g_kva, mla_w_kvb, mla_g_qn, mla_g_qr, mla_g_kn,
                  mla_g_kr, mla_w_out, dsa_w_in, dsa_g_q, dsa_g_k, dsa_w_out, swa_w_in, swa_g_q,
                  swa_g_k, swa_sinks, swa_w_out, fox_w_in, fox_b_f, fox_g_q, fox_g_k, fox_w_out):
    bf = lambda a: a.astype(BF16)
    row = lambda a: a.astype(F32).reshape(1, -1)
    half = MLA_ROPE // 2
    c1, c2, c3 = MLA_Q_LORA, MLA_Q_LORA + MLA_KV_LORA, MLA_Q_LORA + MLA_KV_LORA + MLA_ROPE
    wkr = mla_w_in[:, c2:c3]
    wkrp = jnp.concatenate([wkr[:, half:], wkr[:, :half]], axis=1)
    gkr = mla_g_kr.astype(F32)
    gkrp = jnp.concatenate([gkr[half:], gkr[:half]])
    kvb = mla_w_kvb.reshape(MLA_KV_LORA, N_HEADS, MLA_NOPE + HEAD_DIM)
    mla = dict(
        wcq=bf(mla_w_in[:, :c1]), wckv=bf(mla_w_in[:, c1:c2]), wkr=bf(_pad_cols(wkr, 128)),
        wkrp=bf(_pad_cols(wkrp, 128)), wzt=bf(mla_w_in[:, c3:].T),
        gqa=row(mla_g_qa), gkva=row(mla_g_kva), gkr=row(jnp.pad(gkr, (0, 128 - MLA_ROPE))),
        gkrp=row(jnp.pad(gkrp, (0, 128 - MLA_ROPE))),
        wqt=bf(mla_w_qb.T), gqn=_col(mla_g_qn), gqr=_col(mla_g_qr),
        wkn=bf(kvb[:, :, :MLA_NOPE].reshape(MLA_KV_LORA, -1)),
        wvt=bf(kvb[:, :, MLA_NOPE:].reshape(MLA_KV_LORA, -1).T),
        gkn=_tile_gain(mla_g_kn, N_HEADS), bd=_block_diag(BD_LANES), wo=bf(mla_w_out))
    hq, hk = N_HEADS * HEAD_DIM, KV_HEADS * HEAD_DIM
    cuts = np.cumsum([hq, hk, hk, IDX_HEADS * IDX_DIM, IDX_DIM, IDX_HEADS]).tolist()
    dsa = dict(
        wqt=bf(dsa_w_in[:, :cuts[0]].T), wk=bf(dsa_w_in[:, cuts[0]:cuts[1]]),
        wv=bf(dsa_w_in[:, cuts[1]:cuts[2]]), wqit=bf(dsa_w_in[:, cuts[2]:cuts[3]].T),
        wki=bf(dsa_w_in[:, cuts[3]:cuts[4]]), wwit=bf(dsa_w_in[:, cuts[4]:cuts[5]].T),
        wzt=bf(dsa_w_in[:, cuts[5]:].T), gq=_col(dsa_g_q), gk=_tile_gain(dsa_g_k, KV_HEADS),
        bd=_block_diag(BD_LANES), wo=bf(dsa_w_out))
    swa = dict(
        wqt=bf(swa_w_in[:, :hq].T), wk=bf(swa_w_in[:, hq:hq + hk]), wv=bf(swa_w_in[:, hq + hk:hq + 2 * hk]),
        wzt=bf(swa_w_in[:, hq + 2 * hk:].T), gq=_col(swa_g_q), gk=_tile_gain(swa_g_k, KV_HEADS),
        bd=_block_diag(BD_LANES), wo=bf(swa_w_out), sinks=swa_sinks)
    fox = dict(
        wqt=bf(fox_w_in[:, :hq].T), wk=bf(fox_w_in[:, hq:2 * hq]), wv=bf(fox_w_in[:, 2 * hq:3 * hq]),
        wf=bf(_pad_cols(fox_w_in[:, 3 * hq:3 * hq + N_HEADS], 128)), wzt=bf(fox_w_in[:, 3 * hq + N_HEADS:].T),
        bf=row(jnp.pad(fox_b_f.astype(F32), (0, 128 - N_HEADS))), gq=_col(fox_g_q),
        gk=_tile_gain(fox_g_k, N_HEADS), bd=_block_diag(BD_LANES), wo=bf(fox_w_out))
    return [mla, dsa, swa, fox]


def kernel(x_prompt, x_sample, cache_mla_latent, cache_mla_krope, cache_dsa_k, cache_dsa_v, cache_dsa_kidx, state_swa_k, state_swa_v, cache_fox_k, cache_fox_v, cache_fox_logf, c_prompt, c_sample, norm_g, ada_w, ada_b, mla_w_in, mla_g_qa, mla_w_qb, mla_g_kva, mla_w_kvb, mla_g_qn, mla_g_qr, mla_g_kn, mla_g_kr, mla_w_out, dsa_w_in, dsa_g_q, dsa_g_k, dsa_w_out, swa_w_in, swa_g_q, swa_g_k, swa_sinks, swa_w_out, fox_w_in, fox_b_f, fox_g_q, fox_g_k, fox_w_out):
    bp, sp, _ = x_prompt.shape
    bs, ss, _ = x_sample.shape
    past_len = cache_mla_latent.shape[1]
    depth = norm_g.shape[0]
    assert bp == 1 and sp % TQ == 0 and sp % min(TQ_FLASH, sp) == 0 and sp % TS == 0 and (bs * ss) % 8 == 0 and ss <= TQ_DEC
    assert past_len % TK == 0 and past_len >= WINDOW

    weights = _prep_weights(mla_w_in, mla_g_qa, mla_w_qb, mla_g_kva, mla_w_kvb, mla_g_qn, mla_g_qr,
                            mla_g_kn, mla_g_kr, mla_w_out, dsa_w_in, dsa_g_q, dsa_g_k, dsa_w_out,
                            swa_w_in, swa_g_q, swa_g_k, swa_sinks, swa_w_out, fox_w_in, fox_b_f,
                            fox_g_q, fox_g_k, fox_w_out)
    rows = bp + bs
    rows_p = -(-rows // 8) * 8
    c_all = jnp.concatenate([c_prompt, c_sample, jnp.zeros((rows_p - rows, D_MODEL), F32)], axis=0)
    mod = _ada_mod(c_all, ada_w, ada_b)

    st_p = _Stream(bp, sp, 0)
    st_s = _Stream(bs, ss, past_len)
    pasts = ((cache_mla_latent, cache_mla_krope), (cache_dsa_k, cache_dsa_v, cache_dsa_kidx),
             (state_swa_k, state_swa_v), (cache_fox_k, cache_fox_v, cache_fox_logf))
    layers = (_mla_layer, _dsa_layer, _swa_layer, _fox_layer)
    xp = x_prompt.reshape(st_p.r, D_MODEL)
    xs = x_sample.reshape(st_s.r, D_MODEL)
    new_p, new_s = [], []
    for layer in range(depth):
        kind = layer % len(layers)
        g = norm_g[layer].astype(F32).reshape(1, -1)
        xp, n = layers[kind](st_p, xp, _mod_rows(st_p, mod[layer], 0), g, weights[kind], None)
        new_p.append(n)
        xs, n = layers[kind](st_s, xs, _mod_rows(st_s, mod[layer], bp), g, weights[kind], pasts[kind])
        new_s.append(n)

    def shaped(st, new):
        (lat, kr), (dk, dv, dki), (sk, sv), (fk, fv, flf) = new
        b, s = st.b, st.s
        return (lat.reshape(b, s, -1), kr.reshape(b, s, -1),
                dk.reshape(b, s, KV_HEADS, HEAD_DIM), dv.reshape(b, s, KV_HEADS, HEAD_DIM),
                dki.reshape(b, s, -1),
                sk.reshape(b, -1, KV_HEADS, HEAD_DIM), sv.reshape(b, -1, KV_HEADS, HEAD_DIM),
                fk.reshape(b, s, N_HEADS, HEAD_DIM), fv.reshape(b, s, N_HEADS, HEAD_DIM),
                flf.reshape(b, s, -1))

    return (xp.reshape(x_prompt.shape), xs.reshape(x_sample.shape)) + shaped(st_p, new_p) + shaped(st_s, new_s)
```

```python
import functools

import numpy as np
import jax
import jax.numpy as jnp
from jax import lax
from jax.experimental import pallas as pl
from jax.experimental.pallas import tpu as pltpu

F32 = jnp.float32
BF16 = jnp.bfloat16
I32 = jnp.int32

D_MODEL = 1024
HEAD_DIM = 64
N_HEADS = 16
KV_HEADS = 4
CHUNK = 64
CHUNK_SHIFT = 6
WINDOW = 128
WIN_CHUNKS = WINDOW // CHUNK
EPS = 1e-6
ROPE_BASE = 10000.0
MLA_NOPE, MLA_ROPE, MLA_Q_LORA, MLA_KV_LORA = 64, 32, 384, 256
IDX_HEADS, IDX_DIM, TOPK_MAX = 8, 64, 256
LOG2E = 1.4426950408889634
NEG = -1e30
INT_MIN = -(2 ** 31)
INT_MAX = 2 ** 31 - 1
I16 = jnp.int16
I16_MIN, I16_MAX = -(2 ** 15), 2 ** 15 - 1

TS = 512
TQ = 256
TQ_FLASH = 2048
TK = 512
TQ_DEC = 128
BD_LANES = 256
SUM_ROWS = 16
KT_LANES = 256
VMEM_LIMIT = 56 * 1024 * 1024
DSA_GROUPS = 2
DSA_VMEM_LIMIT = 62 * 1024 * 1024


def _row_tile(r):
    return TS if r % TS == 0 else r


def _cparams(sem, vmem=VMEM_LIMIT):
    return pltpu.CompilerParams(dimension_semantics=sem, vmem_limit_bytes=vmem)


def _dotf(a, b):
    return jnp.dot(a, b, preferred_element_type=F32)


def _dot_nt(a, b):
    return lax.dot_general(a, b, (((1,), (1,)), ((), ())), preferred_element_type=F32)


def _dot_tn(a, b):
    return lax.dot_general(a, b, (((0,), (0,)), ((), ())), preferred_element_type=F32)


def _split3(x):
    hi = x.astype(BF16)
    r = x - hi.astype(F32)
    mid = r.astype(BF16)
    lo = (r - mid.astype(F32)).astype(BF16)
    return hi, mid, lo


def _silu(x):
    return x / (1.0 + jnp.exp(-x))


def _full_spec(arr):
    nd = arr.ndim
    return pl.BlockSpec(arr.shape, lambda *_: (0,) * nd)


def _row_spec(arr, ts):
    if arr.shape[0] == 1:
        return pl.BlockSpec((1, arr.shape[1]), lambda i: (0, 0))
    return pl.BlockSpec((ts, arr.shape[1]), lambda i: (i, 0))


def _col_spec(arr, ts):
    return pl.BlockSpec((arr.shape[0], ts), lambda i: (0, i))


def _ada_body(c_ref, w_ref, b_ref, o_ref):
    a = _silu(c_ref[...])
    w = w_ref[0]
    a_hi = a.astype(BF16)
    a_lo = (a - a_hi.astype(F32)).astype(BF16)
    w_hi = w.astype(BF16)
    w_lo = (w - w_hi.astype(F32)).astype(BF16)
    o_ref[0] = _dotf(a_hi, w_hi) + _dotf(a_hi, w_lo) + _dotf(a_lo, w_hi) + b_ref[0]


def _ada_mod(c_all, ada_w, ada_b):
    depth, d, n3 = ada_w.shape
    bp = c_all.shape[0]
    tn = 768
    return pl.pallas_call(
        _ada_body,
        grid=(depth, n3 // tn),
        in_specs=[
            pl.BlockSpec((bp, d), lambda l, j: (0, 0)),
            pl.BlockSpec((1, d, tn), lambda l, j: (l, 0, j)),
            pl.BlockSpec((1, 1, tn), lambda l, j: (l, 0, j)),
        ],
        out_specs=pl.BlockSpec((1, bp, tn), lambda l, j: (l, 0, j)),
        out_shape=jax.ShapeDtypeStruct((depth, bp, n3), F32),
        compiler_params=_cparams(("arbitrary", "arbitrary")),
        name="ada_mod",
    )(c_all, ada_w, ada_b.reshape(depth, 1, n3))


def _prenorm(x_ref, g_ref, sc_ref, sh_ref):
    x = x_ref[...]
    ms = jnp.mean(x * x, axis=-1, keepdims=True)
    xn = x * lax.rsqrt(ms + EPS) * g_ref[...]
    return (xn * (1.0 + sc_ref[...]) + sh_ref[...]).astype(BF16)


def _group_sumsq(y, bd_ref):
    sq = y * y
    hi = sq.astype(BF16)
    lo = (sq - hi.astype(F32)).astype(BF16)
    bd = bd_ref[...]
    chunks = []
    for c in range(y.shape[1] // BD_LANES):
        sl = slice(c * BD_LANES, (c + 1) * BD_LANES)
        chunks.append(_dotf(hi[:, sl], bd) + _dotf(lo[:, sl], bd))
    return jnp.concatenate(chunks, axis=1) if len(chunks) > 1 else chunks[0]


def _head_rms_rows(q, g_col):
    ms = jnp.mean(q * q, axis=0, keepdims=True)
    return q * lax.rsqrt(ms + EPS) * g_col


def _place_rows(piece, row0, total):
    ts = piece.shape[1]
    parts = []
    if row0 > 0:
        parts.append(jnp.zeros((row0, ts), F32))
    parts.append(piece)
    rest = total - row0 - piece.shape[0]
    if rest > 0:
        parts.append(jnp.zeros((rest, ts), F32))
    return jnp.concatenate(parts, axis=0) if len(parts) > 1 else piece


def _proj_call(body, name, r, ts, row_in, col_in, const_in, outs):
    in_specs = ([_row_spec(a, ts) for a in row_in] + [_col_spec(a, ts) for a in col_in]
                + [_full_spec(a) for a in const_in])
    return pl.pallas_call(
        body,
        grid=(r // ts,),
        in_specs=in_specs,
        out_specs=[pl.BlockSpec(blk, im) for (_, _, blk, im) in outs],
        out_shape=[jax.ShapeDtypeStruct(s, dt) for (s, dt, _, _) in outs],
        compiler_params=_cparams(("arbitrary",)),
        name=name,
    )(*row_in, *col_in, *const_in)


def _rows_out(r, c, ts, dtype=F32):
    return ((r, c), dtype, (ts, c), lambda i: (i, 0))


def _cols_out(c, r, ts, dtype=F32):
    return ((c, r), dtype, (c, ts), lambda i: (0, i))


def _qp_out(r, ts):
    return ((N_HEADS, KT_LANES, r), BF16, (N_HEADS, KT_LANES, ts), lambda i: (0, 0, i))


def _mla_in_body(x_ref, sc_ref, sh_ref, cos_ref, sin_ref, g_ref, gqa, gkva, gkr, gkrp,
                 wcq, wckv, wkr, wkrp, wzt, cqn_o, lat_o, kr_o, zt_o):
    h = _prenorm(x_ref, g_ref, sc_ref, sh_ref)
    cq = _dotf(h, wcq[...])
    cqn_o[...] = (cq * lax.rsqrt(jnp.mean(cq * cq, axis=-1, keepdims=True) + EPS) * gqa[...]).astype(BF16)
    ckv = _dotf(h, wckv[...])
    lat_o[...] = ckv * lax.rsqrt(jnp.mean(ckv * ckv, axis=-1, keepdims=True) + EPS) * gkva[...]
    kr = _dotf(h, wkr[...])
    krp = _dotf(h, wkrp[...])
    inv = lax.rsqrt(jnp.sum(kr * kr, axis=-1, keepdims=True) * (1.0 / MLA_ROPE) + EPS)
    kr_o[...] = (kr * gkr[...] * cos_ref[...] + krp * gkrp[...] * sin_ref[...]) * inv
    zt_o[...] = _dot_nt(wzt[...], h)


def _mla_q_body(cqn_ref, cos_ref, sin_ref, wqt, gqn, gqr, qp_o):
    qt = _dot_nt(wqt[...], cqn_ref[...])
    ts = qt.shape[1]
    sc = (MLA_NOPE + MLA_ROPE) ** -0.5 * LOG2E
    c = cos_ref[...]
    s = sin_ref[...]
    half = MLA_ROPE // 2
    width = MLA_NOPE + MLA_ROPE
    for h in range(N_HEADS):
        qn = _head_rms_rows(qt[width * h:width * h + MLA_NOPE], gqn[...]) * sc
        qr = _head_rms_rows(qt[width * h + MLA_NOPE:width * (h + 1)], gqr[...]) * sc
        x1, x2 = qr[:half], qr[half:]
        o1 = x1 * c - x2 * s
        o2 = x2 * c + x1 * s
        a = h % 2
        pieces = [qn, jnp.zeros((HEAD_DIM, ts), F32)]
        if a:
            pieces = pieces[::-1]
        pieces += [o1, o2, jnp.zeros((KT_LANES - 2 * HEAD_DIM - MLA_ROPE, ts), F32)]
        qp_o[h] = jnp.concatenate(pieces, axis=0).astype(BF16)


def _mla_kv_body(lat_ref, kr_ref, gkn, wkn, wvt, bd, kt_o, vt_o):
    lat = lat_ref[...].astype(BF16)
    kn = _dotf(lat, wkn[...])
    ss = _group_sumsq(kn, bd)
    kn = kn * lax.rsqrt(ss * (1.0 / MLA_NOPE) + EPS) * gkn[...]
    kr = kr_ref[...].astype(BF16)
    for p in range(N_HEADS // 2):
        kt_o[p, :, 0:128] = kn[:, 128 * p:128 * (p + 1)].astype(BF16)
        kt_o[p, :, 128:256] = kr
    vt_o[0] = _dot_nt(wvt[...], lat).astype(BF16)


def _q_heads_to_qp(qt, gq, qp_o, row_of_head, extra_of_head=None):
    sc = HEAD_DIM ** -0.5 * LOG2E
    ts = qt.shape[1]
    for h in range(N_HEADS):
        qh = _head_rms_rows(qt[HEAD_DIM * h:HEAD_DIM * (h + 1)], gq[...]) * sc
        blk = _place_rows(qh, row_of_head(h), KT_LANES)
        if extra_of_head is not None:
            lo, hi = extra_of_head(h)
            rows = lax.broadcasted_iota(I32, (KT_LANES, ts), 0)
            blk = blk + jnp.where((rows >= lo) & (rows < hi), 1.0, 0.0)
        qp_o[h] = blk.astype(BF16)


def _dsa_in_body(x_ref, sc_ref, sh_ref, g_ref, gk, gq, wqt, wk, wv, wqit, wki, wwit, wzt, bd,
                 qp_o, k_o, v_o, qi_o, ki_o, wi_o, zt_o, *key_side):
    h = _prenorm(x_ref, g_ref, sc_ref, sh_ref)
    _q_heads_to_qp(_dot_nt(wqt[...], h), gq, qp_o, lambda hh: HEAD_DIM * (hh // (N_HEADS // KV_HEADS)))
    k = _dotf(h, wk[...])
    k = k * lax.rsqrt(_group_sumsq(k, bd) * (1.0 / HEAD_DIM) + EPS) * gk[...]
    v = _dotf(h, wv[...])
    ki = _dotf(h, wki[...])
    k_o[...] = k
    v_o[...] = v
    ki_o[...] = ki
    qit = _dot_nt(wqit[...], h)
    for ih in range(IDX_HEADS):
        qi_o[ih] = qit[IDX_DIM * ih:IDX_DIM * (ih + 1)].astype(BF16)
    wi_o[...] = _dot_nt(wwit[...], h)
    zt_o[...] = _dot_nt(wzt[...], h)
    if key_side:
        kt_o, vt_o, kib_o = key_side
        kt_o[...] = k.astype(BF16)
        vt_o[0] = v.T.astype(BF16)
        kib_o[...] = ki.astype(BF16)


def _swa_in_body(x_ref, sc_ref, sh_ref, g_ref, gk, gq, wqt, wk, wv, wzt, bd, qp_o, k_o, v_o, zt_o):
    h = _prenorm(x_ref, g_ref, sc_ref, sh_ref)
    _q_heads_to_qp(_dot_nt(wqt[...], h), gq, qp_o, lambda hh: HEAD_DIM * (hh // (N_HEADS // KV_HEADS)))
    k = _dotf(h, wk[...])
    k_o[...] = k * lax.rsqrt(_group_sumsq(k, bd) * (1.0 / HEAD_DIM) + EPS) * gk[...]
    v_o[...] = _dotf(h, wv[...])
    zt_o[...] = _dot_nt(wzt[...], h)


def _fox_in_body(x_ref, sc_ref, sh_ref, g_ref, gk, bf, gq, wqt, wk, wv, wf, wzt, bd,
                 qp_o, k_o, v_o, lf_o, zt_o):
    h = _prenorm(x_ref, g_ref, sc_ref, sh_ref)
    _q_heads_to_qp(_dot_nt(wqt[...], h), gq, qp_o, lambda hh: HEAD_DIM * (hh % 2),
                   lambda hh: (128 + 3 * (hh % 2), 128 + 3 * (hh % 2) + 3))
    k = _dotf(h, wk[...])
    k_o[...] = k * lax.rsqrt(_group_sumsq(k, bd) * (1.0 / HEAD_DIM) + EPS) * gk[...]
    v_o[...] = _dotf(h, wv[...])
    f = _dotf(h, wf[...]) + bf[...]
    lf_o[...] = jnp.minimum(f, 0.0) - jnp.log1p(jnp.exp(-jnp.abs(f)))
    zt_o[...] = _dot_nt(wzt[...], h)


def _out_body(x_ref, gate_ref, ot_ref, zt_ref, wo, o_ref):
    u = (ot_ref[...] * _silu(zt_ref[...])).astype(BF16)
    o_ref[...] = x_ref[...] + gate_ref[...] * _dot_tn(u, wo[...])


def _out_proj(x, gate, ot, zt, wo):
    r = x.shape[0]
    ts = _row_tile(r)
    (out,) = _proj_call(_out_body, "out_proj", r, ts, [x, gate], [ot, zt], [wo],
                        [_rows_out(r, D_MODEL, ts)])
    return out


def _src_arrays(src):
    past, new = src
    return [new] if past is None else [past, new]


def _src_specs(src):
    past, new = src
    c = new.shape[2]
    if past is None:
        return [pl.BlockSpec((1, TK, c), lambda bb, j: (bb, j, 0))]
    npb = past.shape[1] // TK
    return [pl.BlockSpec((1, TK, c), lambda bb, j: (bb, jnp.minimum(j, npb - 1), 0)),
            pl.BlockSpec((1, TK, c), lambda bb, j: (bb, jnp.maximum(j - npb, 0), 0))]


def _src_rows(src):
    past, new = src
    return new.shape[1] + (0 if past is None else past.shape[1])


def _src_load(refs, src_past_blocks):
    if src_past_blocks is None:
        return refs[0][0]
    return jnp.where(pl.program_id(1) < src_past_blocks, refs[0][0], refs[1][0])


def _split_refs(refs, past_blocks):
    vals, pos = [], 0
    for npb in past_blocks:
        cnt = 1 if npb is None else 2
        vals.append(_src_load(refs[pos:pos + cnt], npb))
        pos += cnt
    return vals, refs[pos:]


def _past_blocks(srcs):
    return tuple(None if s[0] is None else s[0].shape[1] // TK for s in srcs)


def _gqa_prep_body(*refs, past_blocks):
    vals, outs = _split_refs(refs, past_blocks)
    outs[0][0] = vals[0].astype(BF16)
    outs[1][0, 0] = vals[1].T.astype(BF16)
    if len(vals) > 2:
        outs[2][0] = vals[2].astype(BF16)


def _gqa_prep(k, v, ki=None):
    srcs = [k, v] + ([ki] if ki is not None else [])
    b, c = k[1].shape[0], k[1].shape[2]
    lp = _src_rows(k)
    nkb = lp // TK
    out_specs = [pl.BlockSpec((1, TK, c), lambda bb, j: (bb, j, 0)),
                 pl.BlockSpec((1, 1, c, TK), lambda bb, j: (bb, j, 0, 0))]
    out_shape = [jax.ShapeDtypeStruct((b, lp, c), BF16), jax.ShapeDtypeStruct((b, nkb, c, TK), BF16)]
    if ki is not None:
        ci = ki[1].shape[2]
        out_specs.append(pl.BlockSpec((1, TK, ci), lambda bb, j: (bb, j, 0)))
        out_shape.append(jax.ShapeDtypeStruct((b, lp, ci), BF16))
    return pl.pallas_call(
        functools.partial(_gqa_prep_body, past_blocks=_past_blocks(srcs)),
        grid=(b, nkb), in_specs=[sp for s in srcs for sp in _src_specs(s)],
        out_specs=out_specs, out_shape=out_shape,
        compiler_params=_cparams(("arbitrary", "arbitrary")), name="gqa_prep",
    )(*[a for s in srcs for a in _src_arrays(s)])


def _fox_prep_body(*refs, past_blocks):
    (k, v, lf), (sel_ref, kt_o, vt_o, carry_ref) = _split_refs(refs, past_blocks)
    kb = pl.program_id(1)

    @pl.when(kb == 0)
    def _():
        carry_ref[...] = jnp.zeros_like(carry_ref)

    r = lax.broadcasted_iota(I32, (TK, TK), 0)
    c = lax.broadcasted_iota(I32, (TK, TK), 1)
    tri = jnp.where(c <= r, 1.0, 0.0).astype(BF16)
    cum3 = _dotf(tri, jnp.concatenate(_split3(lf), axis=1))
    cum = cum3[:, 0:128] + cum3[:, 128:256] + cum3[:, 256:384] + carry_ref[...]
    carry_ref[...] = cum[TK - 1:TK, :]
    ex = _dotf(jnp.concatenate(_split3(-(cum * LOG2E)), axis=1), sel_ref[...])
    for p in range(N_HEADS // 2):
        kt_o[p, :, 0:128] = k[:, 128 * p:128 * (p + 1)].astype(BF16)
        kt_o[p, :, 128:256] = ex[:, 128 * p:128 * (p + 1)].astype(BF16)
    vt_o[0, 0] = v.T.astype(BF16)


def _fox_sel():
    sel = np.zeros((3, 128, N_HEADS * HEAD_DIM), np.float32)
    for h in range(N_HEADS):
        for j in range(3):
            sel[j, h, 128 * (h // 2) + 3 * (h % 2) + j] = 1.0
    return jnp.asarray(sel.reshape(3 * 128, N_HEADS * HEAD_DIM), BF16)


def _fox_prep(k, v, lf128):
    srcs = [k, v, lf128]
    b, c = k[1].shape[0], k[1].shape[2]
    lp = _src_rows(k)
    nkb = lp // TK
    npair = N_HEADS // 2
    sel = _fox_sel()
    return pl.pallas_call(
        functools.partial(_fox_prep_body, past_blocks=_past_blocks(srcs)),
        grid=(b, nkb),
        in_specs=[sp for s in srcs for sp in _src_specs(s)] + [_full_spec(sel)],
        out_specs=[pl.BlockSpec((npair, TK, KT_LANES), lambda bb, j: (bb, j, 0)),
                   pl.BlockSpec((1, 1, c, TK), lambda bb, j: (bb, j, 0, 0))],
        out_shape=[jax.ShapeDtypeStruct((b * npair, lp, KT_LANES), BF16),
                   jax.ShapeDtypeStruct((b, nkb, c, TK), BF16)],
        scratch_shapes=[pltpu.VMEM((1, 128), F32)],
        compiler_params=_cparams(("arbitrary", "arbitrary")), name="fox_prep",
    )(*[a for s in srcs for a in _src_arrays(s)], sel)


def _visible_end(qpos, mode, n_keys):
    if mode == "causal":
        end = qpos + 1
    else:
        end = ((qpos >> CHUNK_SHIFT) + 1) << CHUNK_SHIFT
    return jnp.minimum(end, n_keys)


def _softmax_update(s, mx8, m_ref, c8=None, lane0=0):
    m8 = m_ref[:, lane0:]
    m_new8 = jnp.maximum(m8, mx8)
    alpha8 = jnp.exp2(m8 - m_new8)
    shift = m_new8[0:1] if c8 is None else (m_new8 - c8)[0:1]
    m_ref[:, lane0:] = m_new8
    return jnp.exp2(s - shift).astype(BF16), alpha8[0:1]


def _pv_and_sum(vt_blk, pb):
    ones = jnp.ones((SUM_ROWS, vt_blk.shape[1]), BF16)
    return _dotf(jnp.concatenate([vt_blk, ones], axis=0), pb)


def _run_two_stage(n_full, n_masked, stage_a, stage_b, trim=lambda j: 0):
    odd = n_full % 2

    @pl.when(odd == 1)
    def _():
        stage_a(0, 0, False, 0)
        stage_b(0, 0, False, 0)

    n_pairs = (n_full - odd) // 2

    @pl.when(n_pairs >= 1)
    def _():
        stage_a(odd, 0, False, 0)

        def body(u, _):
            kb = odd + 2 * u
            stage_b(kb, 0, False, 0)
            stage_a(kb + 1, 1, False, 0)
            stage_b(kb + 1, 1, False, 0)
            stage_a(kb + 2, 0, False, 0)
            return 0

        lax.fori_loop(0, n_pairs - 1, body, 0)
        stage_b(n_full - 2, 0, False, 0)
        stage_a(n_full - 1, 1, False, 0)
        stage_b(n_full - 1, 1, False, 0)
        stage_a(n_full, 0, True, trim(0))

    @pl.when(n_pairs < 1)
    def _():
        stage_a(n_full, 0, True, trim(0))

    for j in range(1, n_masked):
        stage_b(n_full + j - 1, (j - 1) % 2, True, trim(j - 1))
        stage_a(n_full + j, j % 2, True, trim(j))
    stage_b(n_full + n_masked - 1, (n_masked - 1) % 2, True, trim(n_masked - 1))


def _flash_body(qp_ref, kt_ref, vt_ref, o_ref, s_buf, mx_buf, m_ref, acc_ref, *,
                tq, mode, q0, n_keys, paired, n_masked):
    i = pl.program_id(1)
    q_first = q0 + i * tq
    n_full = _visible_end(q_first, mode, n_keys) // TK
    ns = N_HEADS // 2 if paired else 1
    per = 2 if paired else 1
    ws, rows = per * tq, per * HEAD_DIM
    w = ns * ws
    ik = lax.broadcasted_iota(I32, (TK, ws), 0)
    iq = lax.broadcasted_iota(I32, (TK, tq), 1)
    qpos = q_first + (jnp.concatenate([iq] * per, axis=1) if paired else iq)
    qs = [jnp.concatenate([qp_ref[per * si + a] for a in range(per)], axis=1) if paired else qp_ref[si]
          for si in range(ns)]

    m_ref[...] = jnp.full((8, w), NEG, F32)
    acc_ref[...] = jnp.zeros((rows + SUM_ROWS, w), F32)

    def stage_a(kb, slot, masked, lane0):
        assert lane0 == 0 or ns == 1
        k0 = pl.multiple_of(kb * TK, TK)
        wl = ws - lane0
        if masked:
            if lane0:
                kpos = k0 + lax.broadcasted_iota(I32, (TK, wl), 0)
                qp = q_first + lane0 + lax.broadcasted_iota(I32, (TK, wl), 1)
            else:
                kpos, qp = k0 + ik, qpos
            valid = (kpos <= qp) if mode == "causal" else ((kpos >> CHUNK_SHIFT) <= (qp >> CHUNK_SHIFT))
            valid = valid & (kpos < n_keys)
        for si in range(ns):
            s = _dotf(kt_ref[si, pl.ds(k0, TK), :], qs[si][:, lane0:])
            if masked:
                s = jnp.where(valid, s, NEG)
            s_buf[slot, :, si * ws + lane0:(si + 1) * ws] = s
            mx_buf[slot, :, si * ws + lane0:(si + 1) * ws] = jnp.broadcast_to(
                jnp.max(s, axis=0, keepdims=True), (8, wl))

    def stage_b(kb, slot, masked, lane0):
        del masked
        pb, alpha = _softmax_update(s_buf[slot, :, lane0:], mx_buf[slot, :, lane0:], m_ref, lane0=lane0)
        for si in range(ns):
            loc = slice(si * ws, (si + 1) * ws - lane0)
            glob = slice(si * ws + lane0, (si + 1) * ws)
            acc_ref[:, glob] = alpha[:, loc] * acc_ref[:, glob] + _pv_and_sum(
                vt_ref[0, kb, rows * si:rows * (si + 1), :], pb[:, loc])

    trim = (lambda j: j * TK) if (not paired and tq > TK) else (lambda j: 0)
    _run_two_stage(n_full, n_masked, stage_a, stage_b, trim)
    out = acc_ref[0:rows, :] / acc_ref[rows:rows + 1, :]
    for si in range(ns):
        for a in range(per):
            o_ref[per * si + a] = out[HEAD_DIM * a:HEAD_DIM * (a + 1), si * ws + tq * a:si * ws + tq * (a + 1)]


def _visible_end_static(qpos, mode, n_keys):
    end = qpos + 1 if mode == "causal" else ((qpos >> CHUNK_SHIFT) + 1) << CHUNK_SHIFT
    return min(end, n_keys)


def _flash(qp, kt, vt, *, tq, mode, q0, n_real_q, n_keys, paired):
    bh, _, sq = qp.shape
    lp = kt.shape[1]
    nkb = vt.shape[1]
    assert (q0 % TK == 0) and (tq % TK == 0 or sq == tq)
    n_masked = (-(-_visible_end_static(q0 + n_real_q - 1, mode, n_keys) // TK)
                - _visible_end_static(q0, mode, n_keys) // TK)
    body = functools.partial(_flash_body, tq=tq, mode=mode, q0=q0, n_keys=n_keys, paired=paired,
                             n_masked=n_masked)
    if not paired:
        ns, w, rows = 1, tq, HEAD_DIM
        kt_spec = pl.BlockSpec((1, lp, KT_LANES), lambda g, i: (g // 2, 0, 0))
        vt_spec = pl.BlockSpec((1, nkb, HEAD_DIM, TK), lambda g, i: (g // N_HEADS, 0, g % N_HEADS, 0))
    else:
        ns, w, rows = N_HEADS, N_HEADS * tq, 2 * HEAD_DIM
        kt_spec = pl.BlockSpec((ns // 2, lp, KT_LANES), lambda g, i: (g, 0, 0))
        vt_spec = pl.BlockSpec((1, nkb, ns * HEAD_DIM, TK), lambda g, i: (g, 0, 0, 0))
    return pl.pallas_call(
        body,
        grid=(bh // ns, sq // tq),
        in_specs=[pl.BlockSpec((ns, KT_LANES, tq), lambda g, i: (g, 0, i)), kt_spec, vt_spec],
        out_specs=pl.BlockSpec((ns, HEAD_DIM, tq), lambda g, i: (g, 0, i)),
        out_shape=jax.ShapeDtypeStruct((bh, HEAD_DIM, sq), F32),
        scratch_shapes=[pltpu.VMEM((2, TK, w), F32), pltpu.VMEM((2, 8, w), F32), pltpu.VMEM((8, w), F32),
                        pltpu.VMEM((rows + SUM_ROWS, w), F32)],
        compiler_params=_cparams(("arbitrary", "arbitrary")),
        name="flash_" + mode,
    )(qp, kt, vt)


def _swa_body(hp_ref, qp_ref, *refs, tq, q0, n_pieces, n_rows):
    k_refs, v_refs, o_ref = refs[:n_pieces], refs[n_pieces:2 * n_pieces], refs[2 * n_pieces]
    i = pl.program_id(1)
    q_first = q0 + i * tq
    k = jnp.concatenate([r[0].astype(BF16) for r in k_refs], axis=0)
    v = jnp.concatenate([r[0].astype(BF16) for r in v_refs], axis=0)
    kw = k.shape[0]
    row = lax.broadcasted_iota(I32, (kw, tq), 0)
    kpos = (q_first - WINDOW) + row
    qpos = q_first + lax.broadcasted_iota(I32, (kw, tq), 1)
    qc = qpos >> CHUNK_SHIFT
    kc = kpos >> CHUNK_SHIFT
    valid = (kpos >= 0) & (kc <= qc) & (qc - kc <= WIN_CHUNKS) & (row < n_rows)
    dist = jnp.abs(qpos - kpos).astype(F32)
    group = N_HEADS // KV_HEADS

    def lanes(x):
        return jnp.concatenate([x] * group, axis=1)

    for n in range(KV_HEADS):
        h0 = n * group

        def per_head(col):
            return jnp.concatenate(
                [jnp.broadcast_to(hp_ref[h0 + g][:, col:col + 1], (1, tq)) for g in range(group)], axis=1)

        q = jnp.concatenate([qp_ref[h0 + g] for g in range(group)], axis=1)
        slope2, sink2 = per_head(0), per_head(1)
        s = _dotf(k, q) - slope2 * lanes(dist)
        s = jnp.where(lanes(valid), s, NEG)
        m = jnp.maximum(jnp.max(s, axis=0, keepdims=True), sink2)
        p = jnp.exp2(s - m)
        l = jnp.sum(p, axis=0, keepdims=True) + jnp.exp2(sink2 - m)
        acc = _dot_tn(v, p.astype(BF16))
        out = acc[HEAD_DIM * n:HEAD_DIM * (n + 1)] / l
        for g in range(group):
            o_ref[h0 + g] = out[:, g * tq:(g + 1) * tq]


def _swa_attend(hp, qp, k_pieces, v_pieces, *, tq, q0, n_rows):
    b = k_pieces[0][0].shape[0]
    sq = qp.shape[2]
    n_pieces = len(k_pieces)

    def spec(piece):
        _, rows, idx = piece
        return pl.BlockSpec((1, rows, KV_HEADS * HEAD_DIM), lambda bb, i: (bb, idx(i), 0))

    body = functools.partial(_swa_body, tq=tq, q0=q0, n_pieces=n_pieces, n_rows=n_rows)
    return pl.pallas_call(
        body,
        grid=(b, sq // tq),
        in_specs=[_full_spec(hp), pl.BlockSpec((N_HEADS, KT_LANES, tq), lambda bb, i: (bb, 0, i))]
        + [spec(p) for p in k_pieces] + [spec(p) for p in v_pieces],
        out_specs=pl.BlockSpec((N_HEADS, HEAD_DIM, tq), lambda bb, i: (bb, 0, i)),
        out_shape=jax.ShapeDtypeStruct((b * N_HEADS, HEAD_DIM, sq), F32),
        compiler_params=_cparams(("arbitrary", "arbitrary")),
        name="swa_attend",
    )(hp, qp, *[p[0] for p in k_pieces], *[p[0] for p in v_pieces])


def _dsa_body(hp_ref, qi_ref, wi_ref, qp_ref, ki_ref, kt_ref, vt_ref, o_ref, hi_ref, lo_ref,
              s_buf, mx_buf, m_ref, acc_ref, *, tq, q0, n_real_q, n_keys, topk):
    i = pl.program_id(1)
    q_first = q0 + i * tq
    q_last = q_first + (n_real_q - 1)
    n_tot = (_visible_end(q_last, "chunk", n_keys) + (TK - 1)) // TK
    n_past = jnp.minimum(q_first, n_keys) // TK
    ik = lax.broadcasted_iota(I32, (TK, tq), 0)
    iq = lax.broadcasted_iota(I32, (TK, tq), 1)
    qpos = q_first + iq
    tf = float(topk)

    def blk(kb):
        return pl.ds(pl.multiple_of(kb * TK, TK), TK)

    def to_key(x):
        bits = pltpu.bitcast(x, I32)
        return jnp.where(bits < 0, bits ^ INT_MAX, bits)

    def score_blk(kb, diag):
        ki = ki_ref[0, blk(kb), :]
        acc = jnp.zeros((TK, tq), F32)
        for h in range(IDX_HEADS):
            acc = acc + wi_ref[0, h:h + 1, :] * jnp.maximum(_dotf(ki, qi_ref[0, h]), 0.0)
        key = to_key(acc)
        if diag:
            kpos = kb * TK + ik
            valid = ((kpos >> CHUNK_SHIFT) <= (qpos >> CHUNK_SHIFT)) & (kpos < n_keys)
            key = jnp.where(valid, key, INT_MIN)
        hi_ref[blk(kb), :] = (key >> 16).astype(I16)
        lo_ref[blk(kb), :] = ((key & 0xFFFF) - 32768).astype(I16)
        return 0

    lax.fori_loop(0, n_past, lambda kb, c: score_blk(kb, False), 0)
    lax.fori_loop(n_past, n_tot, lambda kb, c: score_blk(kb, True), 0)

    def count_ge(ref, mid):
        mid16 = mid.astype(I16)

        def body(kb, acc):
            ge = jnp.where(ref[blk(kb), :] >= mid16, jnp.int16(1), jnp.int16(0))
            parts = [ge[16 * r:16 * (r + 1)] for r in range(TK // 16)]
            while len(parts) > 1:
                parts = [parts[j] + parts[j + 1] for j in range(0, len(parts), 2)]
            return acc + parts[0]

        acc = lax.fori_loop(0, n_tot, body, jnp.zeros((16, tq), I16))
        return jnp.sum(acc.astype(I32), axis=0, keepdims=True).astype(F32)

    def bisect_step(ref, target, st):
        lo, hi, cl, ch = st
        mid = (lo + hi) >> 1
        cnt = count_ge(ref, mid)
        ge = cnt >= target
        return jnp.where(ge, mid, lo), jnp.where(ge, hi, mid), jnp.where(ge, cnt, cl), jnp.where(ge, ch, cnt)

    qrow = q_first + lax.broadcasted_iota(I32, (1, tq), 1)
    n_vis = _visible_end(qrow, "chunk", n_keys).astype(F32)
    zero = jnp.zeros((1, tq), F32)
    st1 = (jnp.full((1, tq), I16_MIN + 1, I32), jnp.full((1, tq), I16_MAX + 1, I32), n_vis, zero)
    h_thr, _, cl1, ch1 = lax.fori_loop(0, 16, lambda _, st: bisect_step(hi_ref, tf, st), st1)
    h16 = h_thr.astype(I16)

    def mask_lo(kb, _):
        lo_ref[blk(kb), :] = jnp.where(hi_ref[blk(kb), :] == h16, lo_ref[blk(kb), :], jnp.int16(I16_MIN))
        return 0

    lax.fori_loop(0, n_tot, mask_lo, 0)
    t2 = tf - ch1

    def cond(st):
        return (st[0] < 16) & (st[2] > 0.5)

    def body(st):
        lo, hi, cl, ch = bisect_step(lo_ref, t2, st[1])
        done = (cl <= t2) | (hi - lo <= 1)
        return st[0] + 1, (lo, hi, cl, ch), jnp.sum(jnp.where(done, 0.0, 1.0))

    cl2_0 = cl1 - ch1
    st2 = (jnp.full((1, tq), I16_MIN, I32), jnp.full((1, tq), I16_MAX + 1, I32), cl2_0, zero)
    _, (l_thr, _, cl2, ch2), _ = lax.while_loop(
        cond, body, (jnp.int32(0), st2, jnp.sum(jnp.where(cl2_0 > t2, 1.0, 0.0))))
    l16 = l_thr.astype(I16)

    need = t2 - ch2

    @pl.when(jnp.sum(jnp.where(cl2 > t2, 1.0, 0.0)) > 0.5)
    def _():
        r = lax.broadcasted_iota(I32, (TK, TK), 0)
        c = lax.broadcasted_iota(I32, (TK, TK), 1)
        tri = jnp.where(c < r, 1.0, 0.0).astype(BF16)

        def fix(kb, carry):
            hb = hi_ref[blk(kb), :]
            e16 = jnp.where(hb == h16, jnp.where(lo_ref[blk(kb), :] == l16, jnp.int16(1), jnp.int16(0)),
                            jnp.int16(0))
            e = e16.astype(I32).astype(F32)
            before = _dotf(tri, e.astype(BF16)) + carry
            drop = jnp.where((e > 0.5) & (before >= need), 1, 0).astype(I16)
            hi_ref[blk(kb), :] = jnp.where(drop == jnp.int16(1), jnp.int16(I16_MIN), hb)
            return carry + jnp.sum(e, axis=0, keepdims=True)

        lax.fori_loop(0, n_tot, fix, jnp.zeros((1, tq), F32))

    def to_bias(kb, _):
        hb = hi_ref[blk(kb), :]
        zero_b, neg_b = jnp.bfloat16(0.0), jnp.bfloat16(NEG)
        at_thr = jnp.where(lo_ref[blk(kb), :] >= l16, zero_b, neg_b)
        bias = jnp.where(hb > h16, zero_b, jnp.where(hb == h16, at_thr, neg_b))
        hi_ref[blk(kb), :] = pltpu.bitcast(bias, I16)
        return 0

    lax.fori_loop(0, n_tot, to_bias, 0)

    group = N_HEADS // KV_HEADS
    heads = DSA_GROUPS * group
    gw = group * tq
    wide = heads * tq

    def lanes(x):
        return jnp.concatenate([x] * heads, axis=1)

    def group_body(n, _):
        h0 = n * heads
        q = jnp.concatenate([qp_ref[h0 + g] for g in range(heads)], axis=1)
        slope2 = jnp.concatenate(
            [jnp.broadcast_to(hp_ref[h0 + g][:, 0:1], (1, tq)) for g in range(heads)], axis=1)
        slope8 = jnp.broadcast_to(slope2, (8, wide))
        a_tab = slope2 * lanes(ik.astype(F32))
        vrows = [pl.ds(pl.multiple_of((n * DSA_GROUPS + gi) * HEAD_DIM, HEAD_DIM), HEAD_DIM)
                 for gi in range(DSA_GROUPS)]
        m_ref[...] = jnp.full((8, wide), NEG, F32)
        acc_ref[...] = jnp.zeros((HEAD_DIM + SUM_ROWS, wide), F32)

        def c8(kb):
            return slope8 * (kb * TK - q_first).astype(F32)

        def stage_a(kb, slot, diag, lane0):
            del lane0
            bias = pltpu.bitcast(hi_ref[blk(kb), :], jnp.bfloat16).astype(F32)
            s = _dotf(kt_ref[0, blk(kb), :], q) + lanes(bias)
            if diag:
                kpos = kb * TK + ik
                rel = iq.astype(F32) - jnp.abs(qpos - kpos).astype(F32)
                s = s + slope2 * lanes(rel)
            else:
                s = s + a_tab
            mx8 = jnp.broadcast_to(jnp.max(s, axis=0, keepdims=True), (8, wide))
            s_buf[slot] = s
            mx_buf[slot] = mx8 if diag else mx8 + c8(kb)

        def stage_b(kb, slot, diag, lane0):
            del lane0
            pb, alpha = _softmax_update(s_buf[slot], mx_buf[slot], m_ref, None if diag else c8(kb))
            for gi in range(DSA_GROUPS):
                sl = slice(gi * gw, (gi + 1) * gw)
                acc_ref[:, sl] = alpha[:, sl] * acc_ref[:, sl] + _pv_and_sum(vt_ref[0, kb, vrows[gi], :], pb[:, sl])

        _run_two_stage(n_past, 1, stage_a, stage_b)
        out = acc_ref[0:HEAD_DIM, :] / acc_ref[HEAD_DIM:HEAD_DIM + 1, :]
        for g in range(heads):
            o_ref[h0 + g] = out[:, g * tq:(g + 1) * tq]
        return 0

    lax.fori_loop(0, KV_HEADS // DSA_GROUPS, group_body, 0)


def _dsa_attend(hp, qi, wi, qp, ki, kt, vt, *, tq, q0, n_real_q, n_keys):
    b = ki.shape[0]
    sq = qp.shape[2]
    lp = kt.shape[1]
    nkb = vt.shape[1]
    topk = min(TOPK_MAX, n_keys // 4)
    assert TK % tq == 0 and q0 % TK == 0 and n_real_q <= tq
    wide = DSA_GROUPS * (N_HEADS // KV_HEADS) * tq
    body = functools.partial(_dsa_body, tq=tq, q0=q0, n_real_q=n_real_q, n_keys=n_keys, topk=topk)
    once = pl.Buffered(1)
    return pl.pallas_call(
        body,
        grid=(b, sq // tq),
        in_specs=[_full_spec(hp),
                  pl.BlockSpec((1, IDX_HEADS, IDX_DIM, tq), lambda bb, i: (bb, 0, 0, i)),
                  pl.BlockSpec((1, IDX_HEADS, tq), lambda bb, i: (bb, 0, i)),
                  pl.BlockSpec((N_HEADS, KT_LANES, tq), lambda bb, i: (bb, 0, i)),
                  pl.BlockSpec((1, lp, IDX_DIM), lambda bb, i: (bb, 0, 0), pipeline_mode=once),
                  pl.BlockSpec((1, lp, KT_LANES), lambda bb, i: (bb, 0, 0), pipeline_mode=once),
                  pl.BlockSpec((1, nkb, KV_HEADS * HEAD_DIM, TK), lambda bb, i: (bb, 0, 0, 0),
                               pipeline_mode=once)],
        out_specs=pl.BlockSpec((N_HEADS, HEAD_DIM, tq), lambda bb, i: (bb, 0, i)),
        out_shape=jax.ShapeDtypeStruct((b * N_HEADS, HEAD_DIM, sq), F32),
        scratch_shapes=[pltpu.VMEM((lp, tq), I16), pltpu.VMEM((lp, tq), I16),
                        pltpu.VMEM((2, TK, wide), F32), pltpu.VMEM((2, 8, wide), F32),
                        pltpu.VMEM((8, wide), F32), pltpu.VMEM((HEAD_DIM + SUM_ROWS, wide), F32)],
        compiler_params=_cparams(("arbitrary", "arbitrary"), DSA_VMEM_LIMIT),
        name="dsa_attend",
    )(hp, qi, wi, qp, ki, kt, vt)


def _block_diag(c):
    g = np.arange(c) // HEAD_DIM
    return jnp.asarray(g[:, None] == g[None, :], BF16)


def _tile_gain(g, n):
    return jnp.tile(g.astype(F32), n).reshape(1, -1)


def _col(g):
    return g.astype(F32).reshape(-1, 1)


def _pad_cols(w, n):
    return jnp.pad(w, ((0, 0), (0, n - w.shape[1])))


def _rope_tables(pos):
    half = MLA_ROPE // 2
    inv = ROPE_BASE ** (-jnp.arange(half, dtype=F32) / half)
    ang = pos.astype(F32)[:, None] * inv[None, :]
    return jnp.cos(ang), jnp.sin(ang)


def _alibi_slopes():
    return np.asarray(2.0 ** (-8.0 * np.arange(1, N_HEADS + 1) / N_HEADS), dtype=np.float32)


def _head_params(sinks=None):
    hp = jnp.zeros((N_HEADS, 1, 128), F32)
    hp = hp.at[:, 0, 0].set(jnp.asarray(_alibi_slopes()) * LOG2E)
    if sinks is not None:
        hp = hp.at[:, 0, 1].set(sinks.astype(F32) * LOG2E)
    return hp


class _Stream:
    def __init__(self, batch, seq, past):
        self.b, self.s, self.p = batch, seq, past
        self.r = batch * seq
        self.decode = past > 0
        self.tq = TQ_DEC if self.decode else TQ
        self.tqf = TQ_DEC if self.decode else min(TQ_FLASH, seq)
        self.sq =self.tq if self.decode else seq
        self.n_keys = past + seq
        self.lp = -(-self.n_keys // TK) * TK
        self.pos = past + np.tile(np.arange(seq), batch)

    def pad_keys(self, past_arr, new_arr, n_keys=None, lp=None):
        n_keys = self.n_keys if n_keys is None else n_keys
        lp = self.lp if lp is None else lp
        new_arr = new_arr.reshape(self.b, self.s, -1)
        parts = [new_arr] if past_arr is None else [past_arr.astype(F32), new_arr]
        if lp > n_keys:
            parts.append(jnp.zeros((self.b, lp - n_keys, new_arr.shape[-1]), F32))
        return jnp.concatenate(parts, axis=1) if len(parts) > 1 else new_arr

    def key_source(self, past_arr, new_arr):
        new_arr = new_arr.reshape(self.b, self.s, -1)
        if past_arr is None:
            return (None, new_arr)
        assert past_arr.shape[1] == self.p and self.p % TK == 0
        pad = self.lp - self.n_keys
        return (past_arr.astype(F32), jnp.pad(new_arr, ((0, 0), (0, pad), (0, 0))))

    def qp_blocks(self, qp):
        if not self.decode:
            return qp
        x = qp.reshape(N_HEADS, KT_LANES, self.b, self.s).transpose(2, 0, 1, 3)
        x = jnp.pad(x, ((0, 0), (0, 0), (0, 0), (0, self.tq - self.s)))
        return x.reshape(self.b * N_HEADS, KT_LANES, self.tq)

    def lanes(self, x):
        n, c, _ = x.shape
        if not self.decode:
            return x[None]
        x = x.reshape(n, c, self.b, self.s).transpose(2, 0, 1, 3)
        return jnp.pad(x, ((0, 0), (0, 0), (0, 0), (0, self.tq - self.s)))

    def ot_cols(self, ot):
        if not self.decode:
            return ot.reshape(N_HEADS * HEAD_DIM, self.r)
        x = ot.reshape(self.b, N_HEADS, HEAD_DIM, self.tq)[..., :self.s]
        return x.transpose(1, 2, 0, 3).reshape(N_HEADS * HEAD_DIM, self.r)


def _mod_rows(st, mod_l, row0):
    m = mod_l[row0:row0 + st.b]
    if st.b > 1:
        m = jnp.repeat(m, st.s, axis=0)
    return m[:, :D_MODEL], m[:, D_MODEL:2 * D_MODEL], m[:, 2 * D_MODEL:]


def _mla_layer(st, x, mod, g, w, past):
    shift, scale, gate = mod
    r = st.r
    ts = _row_tile(r)
    cos, sin = _rope_tables(jnp.asarray(st.pos))
    zpad = jnp.zeros((r, 128 - MLA_ROPE), F32)
    cos_p = jnp.concatenate([cos, cos, zpad], axis=1)
    sin_p = jnp.concatenate([-sin, sin, zpad], axis=1)
    cqn, lat, kr, zt = _proj_call(
        _mla_in_body, "mla_in", r, ts,
        [x, scale, shift, cos_p, sin_p, g, w["gqa"], w["gkva"], w["gkr"], w["gkrp"]], [],
        [w["wcq"], w["wckv"], w["wkr"], w["wkrp"], w["wzt"]],
        [_rows_out(r, MLA_Q_LORA, ts, BF16), _rows_out(r, MLA_KV_LORA, ts), _rows_out(r, 128, ts),
         _cols_out(D_MODEL, r, ts)])
    (qp,) = _proj_call(_mla_q_body, "mla_q", r, ts, [cqn], [cos.T, sin.T], [w["wqt"], w["gqn"], w["gqr"]],
                       [_qp_out(r, ts)])
    past_lat, past_kr = (None, None) if past is None else past
    if past_kr is not None:
        past_kr = jnp.pad(past_kr.astype(F32), ((0, 0), (0, 0), (0, 128 - MLA_ROPE)))
    lat_all = st.pad_keys(past_lat, lat).reshape(st.b * st.lp, MLA_KV_LORA)
    kr_all = st.pad_keys(past_kr, kr).reshape(st.b * st.lp, 128)
    rk = st.b * st.lp
    npair = N_HEADS // 2
    nkb = st.lp // TK
    kt, vt = _proj_call(
        _mla_kv_body, "mla_kv", rk, TK, [lat_all, kr_all, w["gkn"]], [], [w["wkn"], w["wvt"], w["bd"]],
        [((npair * st.b, st.lp, KT_LANES), BF16, (npair, TK, KT_LANES), lambda i: (i // nkb, i % nkb, 0)),
         ((rk // TK, N_HEADS * HEAD_DIM, TK), BF16, (1, N_HEADS * HEAD_DIM, TK), lambda i: (i, 0, 0))])
    vt = vt.reshape(st.b, nkb, N_HEADS * HEAD_DIM, TK)
    ot = _flash(st.qp_blocks(qp), kt, vt, tq=st.tqf, mode="chunk", q0=st.p,
                n_real_q=st.tqf if not st.decode else st.s, n_keys=st.n_keys, paired=st.decode)
    x = _out_proj(x, gate, st.ot_cols(ot), zt, w["wo"])
    return x, (lat, kr[:, :MLA_ROPE])


def _dsa_layer(st, x, mod, g, w, past):
    shift, scale, gate = mod
    r = st.r
    ts = _row_tile(r)
    c = KV_HEADS * HEAD_DIM
    fused = past is None and ts == TK and st.b == 1
    outs = [_qp_out(r, ts), _rows_out(r, c, ts), _rows_out(r, c, ts),
            ((IDX_HEADS, IDX_DIM, r), BF16, (IDX_HEADS, IDX_DIM, ts), lambda i: (0, 0, i)),
            _rows_out(r, IDX_DIM, ts), _cols_out(IDX_HEADS, r, ts), _cols_out(D_MODEL, r, ts)]
    if fused:
        outs += [_rows_out(r, c, ts, BF16), ((r // TK, c, TK), BF16, (1, c, TK), lambda i: (i, 0, 0)),
                 _rows_out(r, IDX_DIM, ts, BF16)]
    res = _proj_call(
        _dsa_in_body, "dsa_in", r, ts, [x, scale, shift, g, w["gk"]], [],
        [w["gq"], w["wqt"], w["wk"], w["wv"], w["wqit"], w["wki"], w["wwit"], w["wzt"], w["bd"]], outs)
    qp, k, v, qi, ki, wi, zt = res[:7]
    if fused:
        kt, vt, kib = res[7][None], res[8][None], res[9][None]
    else:
        if past is None:
            pk = pv = pki = None
        else:
            pk, pv, pki = past[0].reshape(st.b, -1, c), past[1].reshape(st.b, -1, c), past[2]
        kt, vt, kib = _gqa_prep(st.key_source(pk, k), st.key_source(pv, v), st.key_source(pki, ki))
    qi_b = st.lanes(qi)
    wi_b = st.lanes(wi[None])[:, 0]
    ot = _dsa_attend(_head_params(), qi_b, wi_b, st.qp_blocks(qp), kib, kt, vt, tq=st.tq, q0=st.p,
                     n_real_q=st.tq if not st.decode else st.s, n_keys=st.n_keys)
    x = _out_proj(x, gate, st.ot_cols(ot), zt, w["wo"])
    return x, (k, v, ki)


def _swa_layer(st, x, mod, g, w, past):
    shift, scale, gate = mod
    r = st.r
    ts = _row_tile(r)
    c = KV_HEADS * HEAD_DIM
    qp, k, v, zt = _proj_call(
        _swa_in_body, "swa_in", r, ts, [x, scale, shift, g, w["gk"]], [],
        [w["gq"], w["wqt"], w["wk"], w["wv"], w["wzt"], w["bd"]],
        [_qp_out(r, ts), _rows_out(r, c, ts), _rows_out(r, c, ts), _cols_out(D_MODEL, r, ts)])
    k3, v3 = k.reshape(st.b, st.s, c), v.reshape(st.b, st.s, c)
    if past is None:
        per = st.tq // WINDOW
        idx = [lambda i: jnp.maximum(per * i - 1, 0), lambda i: per * i, lambda i: per * i + 1]
        k_pieces = [(k3, WINDOW, f) for f in idx]
        v_pieces = [(v3, WINDOW, f) for f in idx]
        n_rows = WINDOW + st.tq
        new = (k3[:, st.s - WINDOW:], v3[:, st.s - WINDOW:])
    else:
        win = past[0].shape[1]
        assert win == WINDOW and st.s <= WINDOW
        pad = ((0, 0), (0, WINDOW - st.s), (0, 0))
        pk, pv = past[0].reshape(st.b, win, c).astype(F32), past[1].reshape(st.b, win, c).astype(F32)
        zero = lambda i: 0
        k_pieces = [(pk, WINDOW, zero), (jnp.pad(k3, pad), WINDOW, zero)]
        v_pieces = [(pv, WINDOW, zero), (jnp.pad(v3, pad), WINDOW, zero)]
        n_rows = win + st.s
        new = (jnp.concatenate([pk, k3], axis=1)[:, st.s:], jnp.concatenate([pv, v3], axis=1)[:, st.s:])
    ot = _swa_attend(_head_params(w["sinks"]), st.qp_blocks(qp), k_pieces, v_pieces, tq=st.tq, q0=st.p,
                     n_rows=n_rows)
    x = _out_proj(x, gate, st.ot_cols(ot), zt, w["wo"])
    return x, new


def _fox_layer(st, x, mod, g, w, past):
    shift, scale, gate = mod
    r = st.r
    ts = _row_tile(r)
    c = N_HEADS * HEAD_DIM
    qp, k, v, lf, zt = _proj_call(
        _fox_in_body, "fox_in", r, ts, [x, scale, shift, g, w["gk"], w["bf"]], [],
        [w["gq"], w["wqt"], w["wk"], w["wv"], w["wf"], w["wzt"], w["bd"]],
        [_qp_out(r, ts), _rows_out(r, c, ts), _rows_out(r, c, ts), _rows_out(r, 128, ts),
         _cols_out(D_MODEL, r, ts)])
    if past is None:
        pk = pv = plf = None
    else:
        pk, pv = past[0].reshape(st.b, -1, c), past[1].reshape(st.b, -1, c)
        plf = jnp.pad(past[2].astype(F32), ((0, 0), (0, 0), (0, 128 - N_HEADS)))
    kt, vt = _fox_prep(st.key_source(pk, k), st.key_source(pv, v), st.key_source(plf, lf))
    ot = _flash(st.qp_blocks(qp), kt, vt, tq=st.tqf, mode="causal", q0=st.p,
                n_real_q=st.tqf if not st.decode else st.s, n_keys=st.n_keys, paired=st.decode)
    x = _out_proj(x, gate, st.ot_cols(ot), zt, w["wo"])
    return x, (k, v, lf[:, :N_HEADS])


def _prep_weights(mla_w_in, mla_g_qa, mla_w_qb, mla_g_kva, mla_w_kvb, mla_g_qn, mla_g_qr, mla_g_kn,
                  mla_g_kr, mla_w_out, dsa_w_in, dsa_g_q, dsa_g_k, dsa_w_out, swa_w_in, swa_g_q,
                  swa_g_k, swa_sinks, swa_w_out, fox_w_in, fox_b_f, fox_g_q, fox_g_k, fox_w_out):
    bf = lambda a: a.astype(BF16)
    row = lambda a: a.astype(F32).reshape(1, -1)
    half = MLA_ROPE // 2
    c1, c2, c3 = MLA_Q_LORA, MLA_Q_LORA + MLA_KV_LORA, MLA_Q_LORA + MLA_KV_LORA + MLA_ROPE
    wkr = mla_w_in[:, c2:c3]
    wkrp = jnp.concatenate([wkr[:, half:], wkr[:, :half]], axis=1)
    gkr = mla_g_kr.astype(F32)
    gkrp = jnp.concatenate([gkr[half:], gkr[:half]])
    kvb = mla_w_kvb.reshape(MLA_KV_LORA, N_HEADS, MLA_NOPE + HEAD_DIM)
    mla = dict(
        wcq=bf(mla_w_in[:, :c1]), wckv=bf(mla_w_in[:, c1:c2]), wkr=bf(_pad_cols(wkr, 128)),
        wkrp=bf(_pad_cols(wkrp, 128)), wzt=bf(mla_w_in[:, c3:].T),
        gqa=row(mla_g_qa), gkva=row(mla_g_kva), gkr=row(jnp.pad(gkr, (0, 128 - MLA_ROPE))),
        gkrp=row(jnp.pad(gkrp, (0, 128 - MLA_ROPE))),
        wqt=bf(mla_w_qb.T), gqn=_col(mla_g_qn), gqr=_col(mla_g_qr),
        wkn=bf(kvb[:, :, :MLA_NOPE].reshape(MLA_KV_LORA, -1)),
        wvt=bf(kvb[:, :, MLA_NOPE:].reshape(MLA_KV_LORA, -1).T),
        gkn=_tile_gain(mla_g_kn, N_HEADS), bd=_block_diag(BD_LANES), wo=bf(mla_w_out))
    hq, hk = N_HEADS * HEAD_DIM, KV_HEADS * HEAD_DIM
    cuts = np.cumsum([hq, hk, hk, IDX_HEADS * IDX_DIM, IDX_DIM, IDX_HEADS]).tolist()
    dsa = dict(
        wqt=bf(dsa_w_in[:, :cuts[0]].T), wk=bf(dsa_w_in[:, cuts[0]:cuts[1]]),
        wv=bf(dsa_w_in[:, cuts[1]:cuts[2]]), wqit=bf(dsa_w_in[:, cuts[2]:cuts[3]].T),
        wki=bf(dsa_w_in[:, cuts[3]:cuts[4]]), wwit=bf(dsa_w_in[:, cuts[4]:cuts[5]].T),
        wzt=bf(dsa_w_in[:, cuts[5]:].T), gq=_col(dsa_g_q), gk=_tile_gain(dsa_g_k, KV_HEADS),
        bd=_block_diag(BD_LANES), wo=bf(dsa_w_out))
    swa = dict(
        wqt=bf(swa_w_in[:, :hq].T), wk=bf(swa_w_in[:, hq:hq + hk]), wv=bf(swa_w_in[:, hq + hk:hq + 2 * hk]),
        wzt=bf(swa_w_in[:, hq + 2 * hk:].T), gq=_col(swa_g_q), gk=_tile_gain(swa_g_k, KV_HEADS),
        bd=_block_diag(BD_LANES), wo=bf(swa_w_out), sinks=swa_sinks)
    fox = dict(
        wqt=bf(fox_w_in[:, :hq].T), wk=bf(fox_w_in[:, hq:2 * hq]), wv=bf(fox_w_in[:, 2 * hq:3 * hq]),
        wf=bf(_pad_cols(fox_w_in[:, 3 * hq:3 * hq + N_HEADS], 128)), wzt=bf(fox_w_in[:, 3 * hq + N_HEADS:].T),
        bf=row(jnp.pad(fox_b_f.astype(F32), (0, 128 - N_HEADS))), gq=_col(fox_g_q),
        gk=_tile_gain(fox_g_k, N_HEADS), bd=_block_diag(BD_LANES), wo=bf(fox_w_out))
    return [mla, dsa, swa, fox]


def kernel(x_prompt, x_sample, cache_mla_latent, cache_mla_krope, cache_dsa_k, cache_dsa_v, cache_dsa_kidx, state_swa_k, state_swa_v, cache_fox_k, cache_fox_v, cache_fox_logf, c_prompt, c_sample, norm_g, ada_w, ada_b, mla_w_in, mla_g_qa, mla_w_qb, mla_g_kva, mla_w_kvb, mla_g_qn, mla_g_qr, mla_g_kn, mla_g_kr, mla_w_out, dsa_w_in, dsa_g_q, dsa_g_k, dsa_w_out, swa_w_in, swa_g_q, swa_g_k, swa_sinks, swa_w_out, fox_w_in, fox_b_f, fox_g_q, fox_g_k, fox_w_out):
    bp, sp, _ = x_prompt.shape
    bs, ss, _ = x_sample.shape
    past_len = cache_mla_latent.shape[1]
    depth = norm_g.shape[0]
    assert bp == 1 and sp % TQ == 0 and sp % min(TQ_FLASH, sp) == 0 and sp % TS == 0 and (bs * ss) % 8 == 0 and ss <= TQ_DEC
    assert past_len % TK == 0 and past_len >= WINDOW

    weights = _prep_weights(mla_w_in, mla_g_qa, mla_w_qb, mla_g_kva, mla_w_kvb, mla_g_qn, mla_g_qr,
                            mla_g_kn, mla_g_kr, mla_w_out, dsa_w_in, dsa_g_q, dsa_g_k, dsa_w_out,
                            swa_w_in, swa_g_q, swa_g_k, swa_sinks, swa_w_out, fox_w_in, fox_b_f,
                            fox_g_q, fox_g_k, fox_w_out)
    rows = bp + bs
    rows_p = -(-rows // 8) * 8
    c_all = jnp.concatenate([c_prompt, c_sample, jnp.zeros((rows_p - rows, D_MODEL), F32)], axis=0)
    mod = _ada_mod(c_all, ada_w, ada_b)

    st_p = _Stream(bp, sp, 0)
    st_s = _Stream(bs, ss, past_len)
    pasts = ((cache_mla_latent, cache_mla_krope), (cache_dsa_k, cache_dsa_v, cache_dsa_kidx),
             (state_swa_k, state_swa_v), (cache_fox_k, cache_fox_v, cache_fox_logf))
    layers = (_mla_layer, _dsa_layer, _swa_layer, _fox_layer)
    xp = x_prompt.reshape(st_p.r, D_MODEL)
    xs = x_sample.reshape(st_s.r, D_MODEL)
    new_p, new_s = [], []
    for layer in range(depth):
        kind = layer % len(layers)
        g = norm_g[layer].astype(F32).reshape(1, -1)
        xp, n = layers[kind](st_p, xp, _mod_rows(st_p, mod[layer], 0), g, weights[kind], None)
        new_p.append(n)
        xs, n = layers[kind](st_s, xs, _mod_rows(st_s, mod[layer], bp), g, weights[kind], pasts[kind])
        new_s.append(n)

    def shaped(st, new):
        (lat, kr), (dk, dv, dki), (sk, sv), (fk, fv, flf) = new
        b, s = st.b, st.s
        return (lat.reshape(b, s, -1), kr.reshape(b, s, -1),
                dk.reshape(b, s, KV_HEADS, HEAD_DIM), dv.reshape(b, s, KV_HEADS, HEAD_DIM),
                dki.reshape(b, s, -1),
                sk.reshape(b, -1, KV_HEADS, HEAD_DIM), sv.reshape(b, -1, KV_HEADS, HEAD_DIM),
                fk.reshape(b, s, N_HEADS, HEAD_DIM), fv.reshape(b, s, N_HEADS, HEAD_DIM),
                flf.reshape(b, s, -1))

    return (xp.reshape(x_prompt.shape), xs.reshape(x_sample.shape)) + shaped(st_p, new_p) + shaped(st_s, new_s)
```

```python
import functools

import numpy as np
import jax
import jax.numpy as jnp
from jax import lax
from jax.experimental import pallas as pl
from jax.experimental.pallas import tpu as pltpu

F32 = jnp.float32
BF16 = jnp.bfloat16
I32 = jnp.int32

D_MODEL = 1024
HEAD_DIM = 64
N_HEADS = 16
KV_HEADS = 4
CHUNK = 64
CHUNK_SHIFT = 6
WINDOW = 128
WIN_CHUNKS = WINDOW // CHUNK
EPS = 1e-6
ROPE_BASE = 10000.0
MLA_NOPE, MLA_ROPE, MLA_Q_LORA, MLA_KV_LORA = 64, 32, 384, 256
IDX_HEADS, IDX_DIM, TOPK_MAX = 8, 64, 256
LOG2E = 1.4426950408889634
NEG = -1e30
INT_MIN = -(2 ** 31)
INT_MAX = 2 ** 31 - 1
I16 = jnp.int16
I16_MIN, I16_MAX = -(2 ** 15), 2 ** 15 - 1

TS = 512
TQ = 256
TQ_FLASH = 2048
TK = 512
TQ_DEC = 128
BD_LANES = 256
SUM_ROWS = 16
KT_LANES = 256
VMEM_LIMIT = 56 * 1024 * 1024
DSA_GROUPS = 2
DSA_VMEM_LIMIT = 62 * 1024 * 1024


def _row_tile(r):
    return TS if r % TS == 0 else r


def _cparams(sem, vmem=VMEM_LIMIT):
    return pltpu.CompilerParams(dimension_semantics=sem, vmem_limit_bytes=vmem)


def _dotf(a, b):
    return jnp.dot(a, b, preferred_element_type=F32)


def _dot_nt(a, b):
    return lax.dot_general(a, b, (((1,), (1,)), ((), ())), preferred_element_type=F32)


def _dot_tn(a, b):
    return lax.dot_general(a, b, (((0,), (0,)), ((), ())), preferred_element_type=F32)


def _split3(x):
    hi = x.astype(BF16)
    r = x - hi.astype(F32)
    mid = r.astype(BF16)
    lo = (r - mid.astype(F32)).astype(BF16)
    return hi, mid, lo


def _silu(x):
    return x / (1.0 + jnp.exp(-x))


def _full_spec(arr):
    nd = arr.ndim
    return pl.BlockSpec(arr.shape, lambda *_: (0,) * nd)


def _row_spec(arr, ts):
    if arr.shape[0] == 1:
        return pl.BlockSpec((1, arr.shape[1]), lambda i: (0, 0))
    return pl.BlockSpec((ts, arr.shape[1]), lambda i: (i, 0))


def _col_spec(arr, ts):
    return pl.BlockSpec((arr.shape[0], ts), lambda i: (0, i))


def _ada_body(c_ref, w_ref, b_ref, o_ref):
    a = _silu(c_ref[...])
    w = w_ref[0]
    a_hi = a.astype(BF16)
    a_lo = (a - a_hi.astype(F32)).astype(BF16)
    w_hi = w.astype(BF16)
    w_lo = (w - w_hi.astype(F32)).astype(BF16)
    o_ref[0] = _dotf(a_hi, w_hi) + _dotf(a_hi, w_lo) + _dotf(a_lo, w_hi) + b_ref[0]


def _ada_mod(c_all, ada_w, ada_b):
    depth, d, n3 = ada_w.shape
    bp = c_all.shape[0]
    tn = 768
    return pl.pallas_call(
        _ada_body,
        grid=(depth, n3 // tn),
        in_specs=[
            pl.BlockSpec((bp, d), lambda l, j: (0, 0)),
            pl.BlockSpec((1, d, tn), lambda l, j: (l, 0, j)),
            pl.BlockSpec((1, 1, tn), lambda l, j: (l, 0, j)),
        ],
        out_specs=pl.BlockSpec((1, bp, tn), lambda l, j: (l, 0, j)),
        out_shape=jax.ShapeDtypeStruct((depth, bp, n3), F32),
        compiler_params=_cparams(("arbitrary", "arbitrary")),
        name="ada_mod",
    )(c_all, ada_w, ada_b.reshape(depth, 1, n3))


def _prenorm(x_ref, g_ref, sc_ref, sh_ref):
    x = x_ref[...]
    ms = jnp.mean(x * x, axis=-1, keepdims=True)
    xn = x * lax.rsqrt(ms + EPS) * g_ref[...]
    return (xn * (1.0 + sc_ref[...]) + sh_ref[...]).astype(BF16)


def _group_sumsq(y, bd_ref):
    sq = y * y
    hi = sq.astype(BF16)
    lo = (sq - hi.astype(F32)).astype(BF16)
    bd = bd_ref[...]
    chunks = []
    for c in range(y.shape[1] // BD_LANES):
        sl = slice(c * BD_LANES, (c + 1) * BD_LANES)
        chunks.append(_dotf(hi[:, sl], bd) + _dotf(lo[:, sl], bd))
    return jnp.concatenate(chunks, axis=1) if len(chunks) > 1 else chunks[0]


def _head_rms_rows(q, g_col):
    ms = jnp.mean(q * q, axis=0, keepdims=True)
    return q * lax.rsqrt(ms + EPS) * g_col


def _place_rows(piece, row0, total):
    ts = piece.shape[1]
    parts = []
    if row0 > 0:
        parts.append(jnp.zeros((row0, ts), F32))
    parts.append(piece)
    rest = total - row0 - piece.shape[0]
    if rest > 0:
        parts.append(jnp.zeros((rest, ts), F32))
    return jnp.concatenate(parts, axis=0) if len(parts) > 1 else piece


def _proj_call(body, name, r, ts, row_in, col_in, const_in, outs):
    in_specs = ([_row_spec(a, ts) for a in row_in] + [_col_spec(a, ts) for a in col_in]
                + [_full_spec(a) for a in const_in])
    return pl.pallas_call(
        body,
        grid=(r // ts,),
        in_specs=in_specs,
        out_specs=[pl.BlockSpec(blk, im) for (_, _, blk, im) in outs],
        out_shape=[jax.ShapeDtypeStruct(s, dt) for (s, dt, _, _) in outs],
        compiler_params=_cparams(("arbitrary",)),
        name=name,
    )(*row_in, *col_in, *const_in)


def _rows_out(r, c, ts, dtype=F32):
    return ((r, c), dtype, (ts, c), lambda i: (i, 0))


def _cols_out(c, r, ts, dtype=F32):
    return ((c, r), dtype, (c, ts), lambda i: (0, i))


def _qp_out(r, ts):
    return ((N_HEADS, KT_LANES, r), BF16, (N_HEADS, KT_LANES, ts), lambda i: (0, 0, i))


def _mla_in_body(x_ref, sc_ref, sh_ref, cos_ref, sin_ref, g_ref, gqa, gkva, gkr, gkrp,
                 wcq, wckv, wkr, wkrp, wzt, cqn_o, lat_o, kr_o, zt_o):
    h = _prenorm(x_ref, g_ref, sc_ref, sh_ref)
    cq = _dotf(h, wcq[...])
    cqn_o[...] = (cq * lax.rsqrt(jnp.mean(cq * cq, axis=-1, keepdims=True) + EPS) * gqa[...]).astype(BF16)
    ckv = _dotf(h, wckv[...])
    lat_o[...] = ckv * lax.rsqrt(jnp.mean(ckv * ckv, axis=-1, keepdims=True) + EPS) * gkva[...]
    kr = _dotf(h, wkr[...])
    krp = _dotf(h, wkrp[...])
    inv = lax.rsqrt(jnp.sum(kr * kr, axis=-1, keepdims=True) * (1.0 / MLA_ROPE) + EPS)
    kr_o[...] = (kr * gkr[...] * cos_ref[...] + krp * gkrp[...] * sin_ref[...]) * inv
    zt_o[...] = _dot_nt(wzt[...], h)


def _mla_q_body(cqn_ref, cos_ref, sin_ref, wqt, gqn, gqr, qp_o):
    qt = _dot_nt(wqt[...], cqn_ref[...])
    ts = qt.shape[1]
    sc = (MLA_NOPE + MLA_ROPE) ** -0.5 * LOG2E
    c = cos_ref[...]
    s = sin_ref[...]
    half = MLA_ROPE // 2
    width = MLA_NOPE + MLA_ROPE
    for h in range(N_HEADS):
        qn = _head_rms_rows(qt[width * h:width * h + MLA_NOPE], gqn[...]) * sc
        qr = _head_rms_rows(qt[width * h + MLA_NOPE:width * (h + 1)], gqr[...]) * sc
        x1, x2 = qr[:half], qr[half:]
        o1 = x1 * c - x2 * s
        o2 = x2 * c + x1 * s
        a = h % 2
        pieces = [qn, jnp.zeros((HEAD_DIM, ts), F32)]
        if a:
            pieces = pieces[::-1]
        pieces += [o1, o2, jnp.zeros((KT_LANES - 2 * HEAD_DIM - MLA_ROPE, ts), F32)]
        qp_o[h] = jnp.concatenate(pieces, axis=0).astype(BF16)


def _mla_kv_body(lat_ref, kr_ref, gkn, wkn, wvt, bd, kt_o, vt_o):
    lat = lat_ref[...].astype(BF16)
    kn = _dotf(lat, wkn[...])
    ss = _group_sumsq(kn, bd)
    kn = kn * lax.rsqrt(ss * (1.0 / MLA_NOPE) + EPS) * gkn[...]
    kr = kr_ref[...].astype(BF16)
    for p in range(N_HEADS // 2):
        kt_o[p, :, 0:128] = kn[:, 128 * p:128 * (p + 1)].astype(BF16)
        kt_o[p, :, 128:256] = kr
    vt_o[0] = _dot_nt(wvt[...], lat).astype(BF16)


def _q_heads_to_qp(qt, gq, qp_o, row_of_head, extra_of_head=None):
    sc = HEAD_DIM ** -0.5 * LOG2E
    ts = qt.shape[1]
    for h in range(N_HEADS):
        qh = _head_rms_rows(qt[HEAD_DIM * h:HEAD_DIM * (h + 1)], gq[...]) * sc
        blk = _place_rows(qh, row_of_head(h), KT_LANES)
        if extra_of_head is not None:
            lo, hi = extra_of_head(h)
            rows = lax.broadcasted_iota(I32, (KT_LANES, ts), 0)
            blk = blk + jnp.where((rows >= lo) & (rows < hi), 1.0, 0.0)
        qp_o[h] = blk.astype(BF16)


def _dsa_in_body(x_ref, sc_ref, sh_ref, g_ref, gk, gq, wqt, wk, wv, wqit, wki, wwit, wzt, bd,
                 qp_o, k_o, v_o, qi_o, ki_o, wi_o, zt_o, *key_side):
    h = _prenorm(x_ref, g_ref, sc_ref, sh_ref)
    _q_heads_to_qp(_dot_nt(wqt[...], h), gq, qp_o, lambda hh: HEAD_DIM * (hh // (N_HEADS // KV_HEADS)))
    k = _dotf(h, wk[...])
    k = k * lax.rsqrt(_group_sumsq(k, bd) * (1.0 / HEAD_DIM) + EPS) * gk[...]
    v = _dotf(h, wv[...])
    ki = _dotf(h, wki[...])
    k_o[...] = k
    v_o[...] = v
    ki_o[...] = ki
    qit = _dot_nt(wqit[...], h)
    for ih in range(IDX_HEADS):
        qi_o[ih] = qit[IDX_DIM * ih:IDX_DIM * (ih + 1)].astype(BF16)
    wi_o[...] = _dot_nt(wwit[...], h)
    zt_o[...] = _dot_nt(wzt[...], h)
    if key_side:
        kt_o, vt_o, kib_o = key_side
        kt_o[...] = k.astype(BF16)
        vt_o[0] = v.T.astype(BF16)
        kib_o[...] = ki.astype(BF16)


def _swa_in_body(x_ref, sc_ref, sh_ref, g_ref, gk, gq, wqt, wk, wv, wzt, bd, qp_o, k_o, v_o, zt_o):
    h = _prenorm(x_ref, g_ref, sc_ref, sh_ref)
    _q_heads_to_qp(_dot_nt(wqt[...], h), gq, qp_o, lambda hh: HEAD_DIM * (hh // (N_HEADS // KV_HEADS)))
    k = _dotf(h, wk[...])
    k_o[...] = k * lax.rsqrt(_group_sumsq(k, bd) * (1.0 / HEAD_DIM) + EPS) * gk[...]
    v_o[...] = _dotf(h, wv[...])
    zt_o[...] = _dot_nt(wzt[...], h)


def _fox_in_body(x_ref, sc_ref, sh_ref, g_ref, gk, bf, gq, wqt, wk, wv, wf, wzt, bd,
                 qp_o, k_o, v_o, lf_o, zt_o):
    h = _prenorm(x_ref, g_ref, sc_ref, sh_ref)
    _q_heads_to_qp(_dot_nt(wqt[...], h), gq, qp_o, lambda hh: HEAD_DIM * (hh % 2),
                   lambda hh: (128 + 3 * (hh % 2), 128 + 3 * (hh % 2) + 3))
    k = _dotf(h, wk[...])
    k_o[...] = k * lax.rsqrt(_group_sumsq(k, bd) * (1.0 / HEAD_DIM) + EPS) * gk[...]
    v_o[...] = _dotf(h, wv[...])
    f = _dotf(h, wf[...]) + bf[...]
    lf_o[...] = jnp.minimum(f, 0.0) - jnp.log1p(jnp.exp(-jnp.abs(f)))
    zt_o[...] = _dot_nt(wzt[...], h)


def _out_body(x_ref, gate_ref, ot_ref, zt_ref, wo, o_ref):
    u = (ot_ref[...] * _silu(zt_ref[...])).astype(BF16)
    o_ref[...] = x_ref[...] + gate_ref[...] * _dot_tn(u, wo[...])


def _out_proj(x, gate, ot, zt, wo):
    r = x.shape[0]
    ts = _row_tile(r)
    (out,) = _proj_call(_out_body, "out_proj", r, ts, [x, gate], [ot, zt], [wo],
                        [_rows_out(r, D_MODEL, ts)])
    return out


def _src_arrays(src):
    past, new = src
    return [new] if past is None else [past, new]


def _src_specs(src):
    past, new = src
    c = new.shape[2]
    if past is None:
        return [pl.BlockSpec((1, TK, c), lambda bb, j: (bb, j, 0))]
    npb = past.shape[1] // TK
    return [pl.BlockSpec((1, TK, c), lambda bb, j: (bb, jnp.minimum(j, npb - 1), 0)),
            pl.BlockSpec((1, TK, c), lambda bb, j: (bb, jnp.maximum(j - npb, 0), 0))]


def _src_rows(src):
    past, new = src
    return new.shape[1] + (0 if past is None else past.shape[1])


def _src_load(refs, src_past_blocks):
    if src_past_blocks is None:
        return refs[0][0]
    return jnp.where(pl.program_id(1) < src_past_blocks, refs[0][0], refs[1][0])


def _split_refs(refs, past_blocks):
    vals, pos = [], 0
    for npb in past_blocks:
        cnt = 1 if npb is None else 2
        vals.append(_src_load(refs[pos:pos + cnt], npb))
        pos += cnt
    return vals, refs[pos:]


def _past_blocks(srcs):
    return tuple(None if s[0] is None else s[0].shape[1] // TK for s in srcs)


def _gqa_prep_body(*refs, past_blocks):
    vals, outs = _split_refs(refs, past_blocks)
    outs[0][0] = vals[0].astype(BF16)
    outs[1][0, 0] = vals[1].T.astype(BF16)
    if len(vals) > 2:
        outs[2][0] = vals[2].astype(BF16)


def _gqa_prep(k, v, ki=None):
    srcs = [k, v] + ([ki] if ki is not None else [])
    b, c = k[1].shape[0], k[1].shape[2]
    lp = _src_rows(k)
    nkb = lp // TK
    out_specs = [pl.BlockSpec((1, TK, c), lambda bb, j: (bb, j, 0)),
                 pl.BlockSpec((1, 1, c, TK), lambda bb, j: (bb, j, 0, 0))]
    out_shape = [jax.ShapeDtypeStruct((b, lp, c), BF16), jax.ShapeDtypeStruct((b, nkb, c, TK), BF16)]
    if ki is not None:
        ci = ki[1].shape[2]
        out_specs.append(pl.BlockSpec((1, TK, ci), lambda bb, j: (bb, j, 0)))
        out_shape.append(jax.ShapeDtypeStruct((b, lp, ci), BF16))
    return pl.pallas_call(
        functools.partial(_gqa_prep_body, past_blocks=_past_blocks(srcs)),
        grid=(b, nkb), in_specs=[sp for s in srcs for sp in _src_specs(s)],
        out_specs=out_specs, out_shape=out_shape,
        compiler_params=_cparams(("arbitrary", "arbitrary")), name="gqa_prep",
    )(*[a for s in srcs for a in _src_arrays(s)])


def _fox_prep_body(*refs, past_blocks):
    (k, v, lf), (sel_ref, kt_o, vt_o, carry_ref) = _split_refs(refs, past_blocks)
    kb = pl.program_id(1)

    @pl.when(kb == 0)
    def _():
        carry_ref[...] = jnp.zeros_like(carry_ref)

    r = lax.broadcasted_iota(I32, (TK, TK), 0)
    c = lax.broadcasted_iota(I32, (TK, TK), 1)
    tri = jnp.where(c <= r, 1.0, 0.0).astype(BF16)
    cum3 = _dotf(tri, jnp.concatenate(_split3(lf), axis=1))
    cum = cum3[:, 0:128] + cum3[:, 128:256] + cum3[:, 256:384] + carry_ref[...]
    carry_ref[...] = cum[TK - 1:TK, :]
    ex = _dotf(jnp.concatenate(_split3(-(cum * LOG2E)), axis=1), sel_ref[...])
    for p in range(N_HEADS // 2):
        kt_o[p, :, 0:128] = k[:, 128 * p:128 * (p + 1)].astype(BF16)
        kt_o[p, :, 128:256] = ex[:, 128 * p:128 * (p + 1)].astype(BF16)
    vt_o[0, 0] = v.T.astype(BF16)


def _fox_sel():
    sel = np.zeros((3, 128, N_HEADS * HEAD_DIM), np.float32)
    for h in range(N_HEADS):
        for j in range(3):
            sel[j, h, 128 * (h // 2) + 3 * (h % 2) + j] = 1.0
    return jnp.asarray(sel.reshape(3 * 128, N_HEADS * HEAD_DIM), BF16)


def _fox_prep(k, v, lf128):
    srcs = [k, v, lf128]
    b, c = k[1].shape[0], k[1].shape[2]
    lp = _src_rows(k)
    nkb = lp // TK
    npair = N_HEADS // 2
    sel = _fox_sel()
    return pl.pallas_call(
        functools.partial(_fox_prep_body, past_blocks=_past_blocks(srcs)),
        grid=(b, nkb),
        in_specs=[sp for s in srcs for sp in _src_specs(s)] + [_full_spec(sel)],
        out_specs=[pl.BlockSpec((npair, TK, KT_LANES), lambda bb, j: (bb, j, 0)),
                   pl.BlockSpec((1, 1, c, TK), lambda bb, j: (bb, j, 0, 0))],
        out_shape=[jax.ShapeDtypeStruct((b * npair, lp, KT_LANES), BF16),
                   jax.ShapeDtypeStruct((b, nkb, c, TK), BF16)],
        scratch_shapes=[pltpu.VMEM((1, 128), F32)],
        compiler_params=_cparams(("arbitrary", "arbitrary")), name="fox_prep",
    )(*[a for s in srcs for a in _src_arrays(s)], sel)


def _visible_end(qpos, mode, n_keys):
    if mode == "causal":
        end = qpos + 1
    else:
        end = ((qpos >> CHUNK_SHIFT) + 1) << CHUNK_SHIFT
    return jnp.minimum(end, n_keys)


def _softmax_update(s, mx8, m_ref, c8=None, lane0=0):
    m8 = m_ref[:, lane0:]
    m_new8 = jnp.maximum(m8, mx8)
    alpha8 = jnp.exp2(m8 - m_new8)
    shift = m_new8[0:1] if c8 is None else (m_new8 - c8)[0:1]
    m_ref[:, lane0:] = m_new8
    return jnp.exp2(s - shift).astype(BF16), alpha8[0:1]


def _pv_and_sum(vt_blk, pb):
    ones = jnp.ones((SUM_ROWS, vt_blk.shape[1]), BF16)
    return _dotf(jnp.concatenate([vt_blk, ones], axis=0), pb)


def _run_two_stage(n_full, n_masked, stage_a, stage_b, trim=lambda j: 0):
    odd = n_full % 2

    @pl.when(odd == 1)
    def _():
        stage_a(0, 0, False, 0)
        stage_b(0, 0, False, 0)

    n_pairs = (n_full - odd) // 2

    @pl.when(n_pairs >= 1)
    def _():
        stage_a(odd, 0, False, 0)

        def body(u, _):
            kb = odd + 2 * u
            stage_b(kb, 0, False, 0)
            stage_a(kb + 1, 1, False, 0)
            stage_b(kb + 1, 1, False, 0)
            stage_a(kb + 2, 0, False, 0)
            return 0

        lax.fori_loop(0, n_pairs - 1, body, 0)
        stage_b(n_full - 2, 0, False, 0)
        stage_a(n_full - 1, 1, False, 0)
        stage_b(n_full - 1, 1, False, 0)
        stage_a(n_full, 0, True, trim(0))

    @pl.when(n_pairs < 1)
    def _():
        stage_a(n_full, 0, True, trim(0))

    for j in range(1, n_masked):
        stage_b(n_full + j - 1, (j - 1) % 2, True, trim(j - 1))
        stage_a(n_full + j, j % 2, True, trim(j))
    stage_b(n_full + n_masked - 1, (n_masked - 1) % 2, True, trim(n_masked - 1))


def _flash_body(qp_ref, kt_ref, vt_ref, o_ref, s_buf, mx_buf, m_ref, acc_ref, *,
                tq, mode, q0, n_keys, paired, n_masked):
    i = pl.program_id(1)
    q_first = q0 + i * tq
    n_full = _visible_end(q_first, mode, n_keys) // TK
    ns = N_HEADS // 2 if paired else 1
    per = 2 if paired else 1
    ws, rows = per * tq, per * HEAD_DIM
    w = ns * ws
    ik = lax.broadcasted_iota(I32, (TK, ws), 0)
    iq = lax.broadcasted_iota(I32, (TK, tq), 1)
    qpos = q_first + (jnp.concatenate([iq] * per, axis=1) if paired else iq)
    qs = [jnp.concatenate([qp_ref[per * si + a] for a in range(per)], axis=1) if paired else qp_ref[si]
          for si in range(ns)]

    m_ref[...] = jnp.full((8, w), NEG, F32)
    acc_ref[...] = jnp.zeros((rows + SUM_ROWS, w), F32)

    def stage_a(kb, slot, masked, lane0):
        assert lane0 == 0 or ns == 1
        k0 = pl.multiple_of(kb * TK, TK)
        wl = ws - lane0
        if masked:
            if lane0:
                kpos = k0 + lax.broadcasted_iota(I32, (TK, wl), 0)
                qp = q_first + lane0 + lax.broadcasted_iota(I32, (TK, wl), 1)
            else:
                kpos, qp = k0 + ik, qpos
            valid = (kpos <= qp) if mode == "causal" else ((kpos >> CHUNK_SHIFT) <= (qp >> CHUNK_SHIFT))
            valid = valid & (kpos < n_keys)
        for si in range(ns):
            s = _dotf(kt_ref[si, pl.ds(k0, TK), :], qs[si][:, lane0:])
            if masked:
                s = jnp.where(valid, s, NEG)
            s_buf[slot, :, si * ws + lane0:(si + 1) * ws] = s
            mx_buf[slot, :, si * ws + lane0:(si + 1) * ws] = jnp.broadcast_to(
                jnp.max(s, axis=0, keepdims=True), (8, wl))

    def stage_b(kb, slot, masked, lane0):
        del masked
        pb, alpha = _softmax_update(s_buf[slot, :, lane0:], mx_buf[slot, :, lane0:], m_ref, lane0=lane0)
        for si in range(ns):
            loc = slice(si * ws, (si + 1) * ws - lane0)
            glob = slice(si * ws + lane0, (si + 1) * ws)
            acc_ref[:, glob] = alpha[:, loc] * acc_ref[:, glob] + _pv_and_sum(
                vt_ref[0, kb, rows * si:rows * (si + 1), :], pb[:, loc])

    trim = (lambda j: j * TK) if (not paired and tq > TK) else (lambda j: 0)
    _run_two_stage(n_full, n_masked, stage_a, stage_b, trim)
    out = acc_ref[0:rows, :] / acc_ref[rows:rows + 1, :]
    for si in range(ns):
        for a in range(per):
            o_ref[per * si + a] = out[HEAD_DIM * a:HEAD_DIM * (a + 1), si * ws + tq * a:si * ws + tq * (a + 1)]


def _visible_end_static(qpos, mode, n_keys):
    end = qpos + 1 if mode == "causal" else ((qpos >> CHUNK_SHIFT) + 1) << CHUNK_SHIFT
    return min(end, n_keys)


def _flash(qp, kt, vt, *, tq, mode, q0, n_real_q, n_keys, paired):
    bh, _, sq = qp.shape
    lp = kt.shape[1]
    nkb = vt.shape[1]
    assert (q0 % TK == 0) and (tq % TK == 0 or sq == tq)
    n_masked = (-(-_visible_end_static(q0 + n_real_q - 1, mode, n_keys) // TK)
                - _visible_end_static(q0, mode, n_keys) // TK)
    body = functools.partial(_flash_body, tq=tq, mode=mode, q0=q0, n_keys=n_keys, paired=paired,
                             n_masked=n_masked)
    if not paired:
        ns, w, rows = 1, tq, HEAD_DIM
        kt_spec = pl.BlockSpec((1, lp, KT_LANES), lambda g, i: (g // 2, 0, 0))
        vt_spec = pl.BlockSpec((1, nkb, HEAD_DIM, TK), lambda g, i: (g // N_HEADS, 0, g % N_HEADS, 0))
    else:
        ns, w, rows = N_HEADS, N_HEADS * tq, 2 * HEAD_DIM
        kt_spec = pl.BlockSpec((ns // 2, lp, KT_LANES), lambda g, i: (g, 0, 0))
        vt_spec = pl.BlockSpec((1, nkb, ns * HEAD_DIM, TK), lambda g, i: (g, 0, 0, 0))
    return pl.pallas_call(
        body,
        grid=(bh // ns, sq // tq),
        in_specs=[pl.BlockSpec((ns, KT_LANES, tq), lambda g, i: (g, 0, i)), kt_spec, vt_spec],
        out_specs=pl.BlockSpec((ns, HEAD_DIM, tq), lambda g, i: (g, 0, i)),
        out_shape=jax.ShapeDtypeStruct((bh, HEAD_DIM, sq), F32),
        scratch_shapes=[pltpu.VMEM((2, TK, w), F32), pltpu.VMEM((2, 8, w), F32), pltpu.VMEM((8, w), F32),
                        pltpu.VMEM((rows + SUM_ROWS, w), F32)],
        compiler_params=_cparams(("arbitrary", "arbitrary")),
        name="flash_" + mode,
    )(qp, kt, vt)


def _swa_body(hp_ref, qp_ref, *refs, tq, q0, n_pieces, n_rows):
    k_refs, v_refs, o_ref = refs[:n_pieces], refs[n_pieces:2 * n_pieces], refs[2 * n_pieces]
    i = pl.program_id(1)
    q_first = q0 + i * tq
    k = jnp.concatenate([r[0].astype(BF16) for r in k_refs], axis=0)
    v = jnp.concatenate([r[0].astype(BF16) for r in v_refs], axis=0)
    kw = k.shape[0]
    row = lax.broadcasted_iota(I32, (kw, tq), 0)
    kpos = (q_first - WINDOW) + row
    qpos = q_first + lax.broadcasted_iota(I32, (kw, tq), 1)
    qc = qpos >> CHUNK_SHIFT
    kc = kpos >> CHUNK_SHIFT
    valid = (kpos >= 0) & (kc <= qc) & (qc - kc <= WIN_CHUNKS) & (row < n_rows)
    dist = jnp.abs(qpos - kpos).astype(F32)
    group = N_HEADS // KV_HEADS

    def lanes(x):
        return jnp.concatenate([x] * group, axis=1)

    for n in range(KV_HEADS):
        h0 = n * group

        def per_head(col):
            return jnp.concatenate(
                [jnp.broadcast_to(hp_ref[h0 + g][:, col:col + 1], (1, tq)) for g in range(group)], axis=1)

        q = jnp.concatenate([qp_ref[h0 + g] for g in range(group)], axis=1)
        slope2, sink2 = per_head(0), per_head(1)
        s = _dotf(k, q) - slope2 * lanes(dist)
        s = jnp.where(lanes(valid), s, NEG)
        m = jnp.maximum(jnp.max(s, axis=0, keepdims=True), sink2)
        p = jnp.exp2(s - m)
        l = jnp.sum(p, axis=0, keepdims=True) + jnp.exp2(sink2 - m)
        acc = _dot_tn(v, p.astype(BF16))
        out = acc[HEAD_DIM * n:HEAD_DIM * (n + 1)] / l
        for g in range(group):
            o_ref[h0 + g] = out[:, g * tq:(g + 1) * tq]


def _swa_attend(hp, qp, k_pieces, v_pieces, *, tq, q0, n_rows):
    b = k_pieces[0][0].shape[0]
    sq = qp.shape[2]
    n_pieces = len(k_pieces)

    def spec(piece):
        _, rows, idx = piece
        return pl.BlockSpec((1, rows, KV_HEADS * HEAD_DIM), lambda bb, i: (bb, idx(i), 0))

    body = functools.partial(_swa_body, tq=tq, q0=q0, n_pieces=n_pieces, n_rows=n_rows)
    return pl.pallas_call(
        body,
        grid=(b, sq // tq),
        in_specs=[_full_spec(hp), pl.BlockSpec((N_HEADS, KT_LANES, tq), lambda bb, i: (bb, 0, i))]
        + [spec(p) for p in k_pieces] + [spec(p) for p in v_pieces],
        out_specs=pl.BlockSpec((N_HEADS, HEAD_DIM, tq), lambda bb, i: (bb, 0, i)),
        out_shape=jax.ShapeDtypeStruct((b * N_HEADS, HEAD_DIM, sq), F32),
        compiler_params=_cparams(("arbitrary", "arbitrary")),
        name="swa_attend",
    )(hp, qp, *[p[0] for p in k_pieces], *[p[0] for p in v_pieces])


def _dsa_body(hp_ref, qi_ref, wi_ref, qp_ref, ki_ref, kt_ref, vt_ref, o_ref, hi_ref, lo_ref,
              s_buf, mx_buf, m_ref, acc_ref, *, tq, q0, n_real_q, n_keys, topk, q_period):
    packed = q_period < tq
    i = pl.program_id(1)
    q_first = q0 + i * tq
    q_last = q_first + (n_real_q - 1)
    n_tot = (_visible_end(q_last, "chunk", n_keys) + (TK - 1)) // TK
    n_past = jnp.minimum(q_first, n_keys) // TK
    ik = lax.broadcasted_iota(I32, (TK, tq), 0)
    iq = lax.broadcasted_iota(I32, (TK, tq), 1)
    if packed:
        iq = iq & (q_period - 1)
    qpos = q_first + iq
    tf = float(topk)

    def blk(kb):
        return pl.ds(pl.multiple_of(kb * TK, TK), TK)

    def to_key(x):
        bits = pltpu.bitcast(x, I32)
        return jnp.where(bits < 0, bits ^ INT_MAX, bits)

    def score_blk(kb, diag):
        ki = ki_ref[0, blk(kb), :]
        acc = jnp.zeros((TK, tq), F32)
        for h in range(IDX_HEADS):
            acc = acc + wi_ref[0, h:h + 1, :] * jnp.maximum(_dotf(ki, qi_ref[0, h]), 0.0)
        key = to_key(acc)
        if diag:
            kpos = kb * TK + ik
            valid = ((kpos >> CHUNK_SHIFT) <= (qpos >> CHUNK_SHIFT)) & (kpos < n_keys)
            key = jnp.where(valid, key, INT_MIN)
        hi_ref[blk(kb), :] = (key >> 16).astype(I16)
        lo_ref[blk(kb), :] = ((key & 0xFFFF) - 32768).astype(I16)
        return 0

    lax.fori_loop(0, n_past, lambda kb, c: score_blk(kb, False), 0)
    lax.fori_loop(n_past, n_tot, lambda kb, c: score_blk(kb, True), 0)

    def count_ge(ref, mid):
        mid16 = mid.astype(I16)

        def body(kb, acc):
            ge = jnp.where(ref[blk(kb), :] >= mid16, jnp.int16(1), jnp.int16(0))
            parts = [ge[16 * r:16 * (r + 1)] for r in range(TK // 16)]
            while len(parts) > 1:
                parts = [parts[j] + parts[j + 1] for j in range(0, len(parts), 2)]
            return acc + parts[0]

        acc = lax.fori_loop(0, n_tot, body, jnp.zeros((16, tq), I16))
        return jnp.sum(acc.astype(I32), axis=0, keepdims=True).astype(F32)

    def bisect_step(ref, target, st):
        lo, hi, cl, ch = st
        mid = (lo + hi) >> 1
        cnt = count_ge(ref, mid)
        ge = cnt >= target
        return jnp.where(ge, mid, lo), jnp.where(ge, hi, mid), jnp.where(ge, cnt, cl), jnp.where(ge, ch, cnt)

    qrow = q_first + iq[0:1]
    n_vis = _visible_end(qrow, "chunk", n_keys).astype(F32)
    zero = jnp.zeros((1, tq), F32)
    st1 = (jnp.full((1, tq), I16_MIN + 1, I32), jnp.full((1, tq), I16_MAX + 1, I32), n_vis, zero)
    h_thr, _, cl1, ch1 = lax.fori_loop(0, 16, lambda _, st: bisect_step(hi_ref, tf, st), st1)
    h16 = h_thr.astype(I16)

    def mask_lo(kb, _):
        lo_ref[blk(kb), :] = jnp.where(hi_ref[blk(kb), :] == h16, lo_ref[blk(kb), :], jnp.int16(I16_MIN))
        return 0

    lax.fori_loop(0, n_tot, mask_lo, 0)
    t2 = tf - ch1

    def cond(st):
        return (st[0] < 16) & (st[2] > 0.5)

    def body(st):
        lo, hi, cl, ch = bisect_step(lo_ref, t2, st[1])
        done = (cl <= t2) | (hi - lo <= 1)
        return st[0] + 1, (lo, hi, cl, ch), jnp.sum(jnp.where(done, 0.0, 1.0))

    cl2_0 = cl1 - ch1
    st2 = (jnp.full((1, tq), I16_MIN, I32), jnp.full((1, tq), I16_MAX + 1, I32), cl2_0, zero)
    _, (l_thr, _, cl2, ch2), _ = lax.while_loop(
        cond, body, (jnp.int32(0), st2, jnp.sum(jnp.where(cl2_0 > t2, 1.0, 0.0))))
    l16 = l_thr.astype(I16)

    need = t2 - ch2

    @pl.when(jnp.sum(jnp.where(cl2 > t2, 1.0, 0.0)) > 0.5)
    def _():
        r = lax.broadcasted_iota(I32, (TK, TK), 0)
        c = lax.broadcasted_iota(I32, (TK, TK), 1)
        tri = jnp.where(c < r, 1.0, 0.0).astype(BF16)

        def fix(kb, carry):
            hb = hi_ref[blk(kb), :]
            e16 = jnp.where(hb == h16, jnp.where(lo_ref[blk(kb), :] == l16, jnp.int16(1), jnp.int16(0)),
                            jnp.int16(0))
            e = e16.astype(I32).astype(F32)
            before = _dotf(tri, e.astype(BF16)) + carry
            drop = jnp.where((e > 0.5) & (before >= need), 1, 0).astype(I16)
            hi_ref[blk(kb), :] = jnp.where(drop == jnp.int16(1), jnp.int16(I16_MIN), hb)
            return carry + jnp.sum(e, axis=0, keepdims=True)

        lax.fori_loop(0, n_tot, fix, jnp.zeros((1, tq), F32))

    def to_bias(kb, _):
        hb = hi_ref[blk(kb), :]
        zero_b, neg_b = jnp.bfloat16(0.0), jnp.bfloat16(NEG)
        at_thr = jnp.where(lo_ref[blk(kb), :] >= l16, zero_b, neg_b)
        bias = jnp.where(hb > h16, zero_b, jnp.where(hb == h16, at_thr, neg_b))
        hi_ref[blk(kb), :] = pltpu.bitcast(bias, I16)
        return 0

    lax.fori_loop(0, n_tot, to_bias, 0)

    group = N_HEADS // KV_HEADS
    heads = DSA_GROUPS * group
    wide = tq if packed else heads * tq
    gw = wide // DSA_GROUPS
    vr = DSA_GROUPS * HEAD_DIM if packed else HEAD_DIM

    def lanes(x):
        return x if packed else jnp.concatenate([x] * heads, axis=1)

    def group_body(n, _):
        h0 = n * heads
        if packed:
            q, slope2 = qp_ref[n], hp_ref[n]
        else:
            q = jnp.concatenate([qp_ref[h0 + g] for g in range(heads)], axis=1)
            slope2 = jnp.concatenate(
                [jnp.broadcast_to(hp_ref[h0 + g][:, 0:1], (1, tq)) for g in range(heads)], axis=1)
        slope8 = jnp.broadcast_to(slope2, (8, wide))
        a_tab = slope2 * lanes(ik.astype(F32))
        vrows = [pl.ds(pl.multiple_of((n * DSA_GROUPS + gi) * HEAD_DIM, HEAD_DIM), HEAD_DIM)
                 for gi in range(DSA_GROUPS)]
        m_ref[...] = jnp.full((8, wide), NEG, F32)
        acc_ref[...] = jnp.zeros((vr + SUM_ROWS, wide), F32)

        def c8(kb):
            return slope8 * (kb * TK - q_first).astype(F32)

        def stage_a(kb, slot, diag, lane0):
            del lane0
            bias = pltpu.bitcast(hi_ref[blk(kb), :], jnp.bfloat16).astype(F32)
            s = _dotf(kt_ref[0, blk(kb), :], q) + lanes(bias)
            if diag:
                kpos = kb * TK + ik
                rel = iq.astype(F32) - jnp.abs(qpos - kpos).astype(F32)
                s = s + slope2 * lanes(rel)
            else:
                s = s + a_tab
            mx8 = jnp.broadcast_to(jnp.max(s, axis=0, keepdims=True), (8, wide))
            s_buf[slot] = s
            mx_buf[slot] = mx8 if diag else mx8 + c8(kb)

        def stage_b(kb, slot, diag, lane0):
            del lane0
            pb, alpha = _softmax_update(s_buf[slot], mx_buf[slot], m_ref, None if diag else c8(kb))
            if packed:
                vt2 = jnp.concatenate([vt_ref[0, kb, vrows[gi], :] for gi in range(DSA_GROUPS)], axis=0)
                acc_ref[...] = alpha * acc_ref[...] + _pv_and_sum(vt2, pb)
            else:
                for gi in range(DSA_GROUPS):
                    sl = slice(gi * gw, (gi + 1) * gw)
                    acc_ref[:, sl] = alpha[:, sl] * acc_ref[:, sl] + _pv_and_sum(
                        vt_ref[0, kb, vrows[gi], :], pb[:, sl])

        _run_two_stage(n_past, 1, stage_a, stage_b)
        out = acc_ref[0:vr, :] / acc_ref[vr:vr + 1, :]
        if packed:
            lane = lax.broadcasted_iota(I32, (HEAD_DIM, wide), 1)
            sel = out[0:HEAD_DIM]
            for gi in range(1, DSA_GROUPS):
                sel = jnp.where(lane >= gi * gw, out[gi * HEAD_DIM:(gi + 1) * HEAD_DIM], sel)
            o_ref[n] = sel
        else:
            for g in range(heads):
                o_ref[h0 + g] = out[:, g * tq:(g + 1) * tq]
        return 0

    lax.fori_loop(0, KV_HEADS // DSA_GROUPS, group_body, 0)


def _dsa_attend(hp, qi, wi, qp, ki, kt, vt, *, tq, q0, n_real_q, n_keys, q_period):
    b = ki.shape[0]
    sq = qp.shape[2]
    lp = kt.shape[1]
    nkb = vt.shape[1]
    topk = min(TOPK_MAX, n_keys // 4)
    heads = DSA_GROUPS * (N_HEADS // KV_HEADS)
    steps = KV_HEADS // DSA_GROUPS
    packed = q_period < tq
    assert TK % tq == 0 and q0 % TK == 0 and n_real_q <= q_period
    assert (not packed) or (heads * q_period == tq and sq == tq and q_period & (q_period - 1) == 0)
    wide = tq if packed else heads * tq
    vr = DSA_GROUPS * HEAD_DIM if packed else HEAD_DIM
    nq = steps if packed else N_HEADS
    body = functools.partial(_dsa_body, tq=tq, q0=q0, n_real_q=n_real_q, n_keys=n_keys, topk=topk,
                             q_period=q_period)
    once = pl.Buffered(1)
    return pl.pallas_call(
        body,
        grid=(b, sq // tq),
        in_specs=[_full_spec(hp),
                  pl.BlockSpec((1, IDX_HEADS, IDX_DIM, tq), lambda bb, i: (bb, 0, 0, i)),
                  pl.BlockSpec((1, IDX_HEADS, tq), lambda bb, i: (bb, 0, i)),
                  pl.BlockSpec((nq, KT_LANES, tq), lambda bb, i: (bb, 0, i)),
                  pl.BlockSpec((1, lp, IDX_DIM), lambda bb, i: (bb, 0, 0), pipeline_mode=once),
                  pl.BlockSpec((1, lp, KT_LANES), lambda bb, i: (bb, 0, 0), pipeline_mode=once),
                  pl.BlockSpec((1, nkb, KV_HEADS * HEAD_DIM, TK), lambda bb, i: (bb, 0, 0, 0),
                               pipeline_mode=once)],
        out_specs=pl.BlockSpec((nq, HEAD_DIM, tq), lambda bb, i: (bb, 0, i)),
        out_shape=jax.ShapeDtypeStruct((b * nq, HEAD_DIM, sq), F32),
        scratch_shapes=[pltpu.VMEM((lp, tq), I16), pltpu.VMEM((lp, tq), I16),
                        pltpu.VMEM((2, TK, wide), F32), pltpu.VMEM((2, 8, wide), F32),
                        pltpu.VMEM((8, wide), F32), pltpu.VMEM((vr + SUM_ROWS, wide), F32)],
        compiler_params=_cparams(("arbitrary", "arbitrary"), DSA_VMEM_LIMIT),
        name="dsa_attend",
    )(hp, qi, wi, qp, ki, kt, vt)


def _block_diag(c):
    g = np.arange(c) // HEAD_DIM
    return jnp.asarray(g[:, None] == g[None, :], BF16)


def _tile_gain(g, n):
    return jnp.tile(g.astype(F32), n).reshape(1, -1)


def _col(g):
    return g.astype(F32).reshape(-1, 1)


def _pad_cols(w, n):
    return jnp.pad(w, ((0, 0), (0, n - w.shape[1])))


def _rope_tables(pos):
    half = MLA_ROPE // 2
    inv = ROPE_BASE ** (-jnp.arange(half, dtype=F32) / half)
    ang = pos.astype(F32)[:, None] * inv[None, :]
    return jnp.cos(ang), jnp.sin(ang)


def _alibi_slopes():
    return np.asarray(2.0 ** (-8.0 * np.arange(1, N_HEADS + 1) / N_HEADS), dtype=np.float32)


def _head_params(sinks=None):
    hp = jnp.zeros((N_HEADS, 1, 128), F32)
    hp = hp.at[:, 0, 0].set(jnp.asarray(_alibi_slopes()) * LOG2E)
    if sinks is not None:
        hp = hp.at[:, 0, 1].set(sinks.astype(F32) * LOG2E)
    return hp


class _Stream:
    def __init__(self, batch, seq, past):
        self.b, self.s, self.p = batch, seq, past
        self.r = batch * seq
        self.decode = past > 0
        self.tq = TQ_DEC if self.decode else TQ
        self.tqf = TQ_DEC if self.decode else min(TQ_FLASH, seq)
        self.sq =self.tq if self.decode else seq
        self.n_keys = past + seq
        self.lp = -(-self.n_keys // TK) * TK
        self.pos = past + np.tile(np.arange(seq), batch)

    def pad_keys(self, past_arr, new_arr, n_keys=None, lp=None):
        n_keys = self.n_keys if n_keys is None else n_keys
        lp = self.lp if lp is None else lp
        new_arr = new_arr.reshape(self.b, self.s, -1)
        parts = [new_arr] if past_arr is None else [past_arr.astype(F32), new_arr]
        if lp > n_keys:
            parts.append(jnp.zeros((self.b, lp - n_keys, new_arr.shape[-1]), F32))
        return jnp.concatenate(parts, axis=1) if len(parts) > 1 else new_arr

    def key_source(self, past_arr, new_arr):
        new_arr = new_arr.reshape(self.b, self.s, -1)
        if past_arr is None:
            return (None, new_arr)
        assert past_arr.shape[1] == self.p and self.p % TK == 0
        pad = self.lp - self.n_keys
        return (past_arr.astype(F32), jnp.pad(new_arr, ((0, 0), (0, pad), (0, 0))))

    def qp_blocks(self, qp):
        if not self.decode:
            return qp
        x = qp.reshape(N_HEADS, KT_LANES, self.b, self.s).transpose(2, 0, 1, 3)
        x = jnp.pad(x, ((0, 0), (0, 0), (0, 0), (0, self.tq - self.s)))
        return x.reshape(self.b * N_HEADS, KT_LANES, self.tq)

    def ot_cols(self, ot):
        if not self.decode:
            return ot.reshape(N_HEADS * HEAD_DIM, self.r)
        x = ot.reshape(self.b, N_HEADS, HEAD_DIM, self.tq)[..., :self.s]
        return x.transpose(1, 2, 0, 3).reshape(N_HEADS * HEAD_DIM, self.r)


def _mod_rows(st, mod_l, row0):
    m = mod_l[row0:row0 + st.b]
    if st.b > 1:
        m = jnp.repeat(m, st.s, axis=0)
    return m[:, :D_MODEL], m[:, D_MODEL:2 * D_MODEL], m[:, 2 * D_MODEL:]


def _mla_layer(st, x, mod, g, w, past):
    shift, scale, gate = mod
    r = st.r
    ts = _row_tile(r)
    cos, sin = _rope_tables(jnp.asarray(st.pos))
    zpad = jnp.zeros((r, 128 - MLA_ROPE), F32)
    cos_p = jnp.concatenate([cos, cos, zpad], axis=1)
    sin_p = jnp.concatenate([-sin, sin, zpad], axis=1)
    cqn, lat, kr, zt = _proj_call(
        _mla_in_body, "mla_in", r, ts,
        [x, scale, shift, cos_p, sin_p, g, w["gqa"], w["gkva"], w["gkr"], w["gkrp"]], [],
        [w["wcq"], w["wckv"], w["wkr"], w["wkrp"], w["wzt"]],
        [_rows_out(r, MLA_Q_LORA, ts, BF16), _rows_out(r, MLA_KV_LORA, ts), _rows_out(r, 128, ts),
         _cols_out(D_MODEL, r, ts)])
    (qp,) = _proj_call(_mla_q_body, "mla_q", r, ts, [cqn], [cos.T, sin.T], [w["wqt"], w["gqn"], w["gqr"]],
                       [_qp_out(r, ts)])
    past_lat, past_kr = (None, None) if past is None else past
    if past_kr is not None:
        past_kr = jnp.pad(past_kr.astype(F32), ((0, 0), (0, 0), (0, 128 - MLA_ROPE)))
    lat_all = st.pad_keys(past_lat, lat).reshape(st.b * st.lp, MLA_KV_LORA)
    kr_all = st.pad_keys(past_kr, kr).reshape(st.b * st.lp, 128)
    rk = st.b * st.lp
    npair = N_HEADS // 2
    nkb = st.lp // TK
    kt, vt = _proj_call(
        _mla_kv_body, "mla_kv", rk, TK, [lat_all, kr_all, w["gkn"]], [], [w["wkn"], w["wvt"], w["bd"]],
        [((npair * st.b, st.lp, KT_LANES), BF16, (npair, TK, KT_LANES), lambda i: (i // nkb, i % nkb, 0)),
         ((rk // TK, N_HEADS * HEAD_DIM, TK), BF16, (1, N_HEADS * HEAD_DIM, TK), lambda i: (i, 0, 0))])
    vt = vt.reshape(st.b, nkb, N_HEADS * HEAD_DIM, TK)
    ot = _flash(st.qp_blocks(qp), kt, vt, tq=st.tqf, mode="chunk", q0=st.p,
                n_real_q=st.tqf if not st.decode else st.s, n_keys=st.n_keys, paired=st.decode)
    x = _out_proj(x, gate, st.ot_cols(ot), zt, w["wo"])
    return x, (lat, kr[:, :MLA_ROPE])


def _dsa_layer(st, x, mod, g, w, past):
    shift, scale, gate = mod
    r = st.r
    ts = _row_tile(r)
    c = KV_HEADS * HEAD_DIM
    fused = past is None and ts == TK and st.b == 1
    outs = [_qp_out(r, ts), _rows_out(r, c, ts), _rows_out(r, c, ts),
            ((IDX_HEADS, IDX_DIM, r), BF16, (IDX_HEADS, IDX_DIM, ts), lambda i: (0, 0, i)),
            _rows_out(r, IDX_DIM, ts), _cols_out(IDX_HEADS, r, ts), _cols_out(D_MODEL, r, ts)]
    if fused:
        outs += [_rows_out(r, c, ts, BF16), ((r // TK, c, TK), BF16, (1, c, TK), lambda i: (i, 0, 0)),
                 _rows_out(r, IDX_DIM, ts, BF16)]
    res = _proj_call(
        _dsa_in_body, "dsa_in", r, ts, [x, scale, shift, g, w["gk"]], [],
        [w["gq"], w["wqt"], w["wk"], w["wv"], w["wqit"], w["wki"], w["wwit"], w["wzt"], w["bd"]], outs)
    qp, k, v, qi, ki, wi, zt = res[:7]
    if fused:
        kt, vt, kib = res[7][None], res[8][None], res[9][None]
    else:
        if past is None:
            pk = pv = pki = None
        else:
            pk, pv, pki = past[0].reshape(st.b, -1, c), past[1].reshape(st.b, -1, c), past[2]
        kt, vt, kib = _gqa_prep(st.key_source(pk, k), st.key_source(pv, v), st.key_source(pki, ki))
    if not st.decode:
        ot = _dsa_attend(_head_params(), qi[None], wi[None], qp, kib, kt, vt, tq=st.tq, q0=st.p,
                         n_real_q=st.tq, n_keys=st.n_keys, q_period=st.tq)
        ot = st.ot_cols(ot)
    else:
        heads = DSA_GROUPS * (N_HEADS // KV_HEADS)
        steps = KV_HEADS // DSA_GROUPS
        assert heads * st.s == st.tq

        def rows(a):
            return a.reshape(a.shape[0], a.shape[1], st.b, st.s).transpose(2, 0, 1, 3)

        qi_b = jnp.tile(rows(qi), (1, 1, 1, heads))
        wi_b = jnp.tile(rows(wi[None])[:, 0], (1, 1, heads))
        qp_b = rows(qp).reshape(st.b, steps, heads, KT_LANES, st.s).transpose(0, 1, 3, 2, 4)
        qp_b = qp_b.reshape(st.b * steps, KT_LANES, st.tq)
        slopes = (jnp.asarray(_alibi_slopes()) * LOG2E).reshape(steps, heads)
        hp = jnp.repeat(slopes, st.s, axis=1).reshape(steps, 1, st.tq)
        ot = _dsa_attend(hp, qi_b, wi_b, qp_b, kib, kt, vt, tq=st.tq, q0=st.p, n_real_q=st.s,
                         n_keys=st.n_keys, q_period=st.s)
        ot = ot.reshape(st.b, steps, HEAD_DIM, heads, st.s).transpose(1, 3, 2, 0, 4)
        ot = ot.reshape(N_HEADS * HEAD_DIM, st.r)
    x = _out_proj(x, gate, ot, zt, w["wo"])
    return x, (k, v, ki)


def _swa_layer(st, x, mod, g, w, past):
    shift, scale, gate = mod
    r = st.r
    ts = _row_tile(r)
    c = KV_HEADS * HEAD_DIM
    qp, k, v, zt = _proj_call(
        _swa_in_body, "swa_in", r, ts, [x, scale, shift, g, w["gk"]], [],
        [w["gq"], w["wqt"], w["wk"], w["wv"], w["wzt"], w["bd"]],
        [_qp_out(r, ts), _rows_out(r, c, ts), _rows_out(r, c, ts), _cols_out(D_MODEL, r, ts)])
    k3, v3 = k.reshape(st.b, st.s, c), v.reshape(st.b, st.s, c)
    if past is None:
        per = st.tq // WINDOW
        idx = [lambda i: jnp.maximum(per * i - 1, 0), lambda i: per * i, lambda i: per * i + 1]
        k_pieces = [(k3, WINDOW, f) for f in idx]
        v_pieces = [(v3, WINDOW, f) for f in idx]
        n_rows = WINDOW + st.tq
        new = (k3[:, st.s - WINDOW:], v3[:, st.s - WINDOW:])
    else:
        win = past[0].shape[1]
        assert win == WINDOW and st.s <= WINDOW
        pad = ((0, 0), (0, WINDOW - st.s), (0, 0))
        pk, pv = past[0].reshape(st.b, win, c).astype(F32), past[1].reshape(st.b, win, c).astype(F32)
        zero = lambda i: 0
        k_pieces = [(pk, WINDOW, zero), (jnp.pad(k3, pad), WINDOW, zero)]
        v_pieces = [(pv, WINDOW, zero), (jnp.pad(v3, pad), WINDOW, zero)]
        n_rows = win + st.s
        new = (jnp.concatenate([pk, k3], axis=1)[:, st.s:], jnp.concatenate([pv, v3], axis=1)[:, st.s:])
    ot = _swa_attend(_head_params(w["sinks"]), st.qp_blocks(qp), k_pieces, v_pieces, tq=st.tq, q0=st.p,
                     n_rows=n_rows)
    x = _out_proj(x, gate, st.ot_cols(ot), zt, w["wo"])
    return x, new


def _fox_layer(st, x, mod, g, w, past):
    shift, scale, gate = mod
    r = st.r
    ts = _row_tile(r)
    c = N_HEADS * HEAD_DIM
    qp, k, v, lf, zt = _proj_call(
        _fox_in_body, "fox_in", r, ts, [x, scale, shift, g, w["gk"], w["bf"]], [],
        [w["gq"], w["wqt"], w["wk"], w["wv"], w["wf"], w["wzt"], w["bd"]],
        [_qp_out(r, ts), _rows_out(r, c, ts), _rows_out(r, c, ts), _rows_out(r, 128, ts),
         _cols_out(D_MODEL, r, ts)])
    if past is None:
        pk = pv = plf = None
    else:
        pk, pv = past[0].reshape(st.b, -1, c), past[1].reshape(st.b, -1, c)
        plf = jnp.pad(past[2].astype(F32), ((0, 0), (0, 0), (0, 128 - N_HEADS)))
    kt, vt = _fox_prep(st.key_source(pk, k), st.key_source(pv, v), st.key_source(plf, lf))
    ot = _flash(st.qp_blocks(qp), kt, vt, tq=st.tqf, mode="causal", q0=st.p,
                n_real_q=st.tqf if not st.decode else st.s, n_keys=st.n_keys, paired=st.decode)
    x = _out_proj(x, gate, st.ot_cols(ot), zt, w["wo"])
    return x, (k, v, lf[:, :N_HEADS])


def _prep_weights(mla_w_in, mla_g_qa, mla_w_qb, mla_g_kva, mla_w_kvb, mla_g_qn, mla_g_qr, mla_g_kn,
                  mla_g_kr, mla_w_out, dsa_w_in, dsa_g_q, dsa_g_k, dsa_w_out, swa_w_in, swa_g_q,
                  swa_g_k, swa_sinks, swa_w_out, fox_w_in, fox_b_f, fox_g_q, fox_g_k, fox_w_out):
    bf = lambda a: a.astype(BF16)
    row = lambda a: a.astype(F32).reshape(1, -1)
    half = MLA_ROPE // 2
    c1, c2, c3 = MLA_Q_LORA, MLA_Q_LORA + MLA_KV_LORA, MLA_Q_LORA + MLA_KV_LORA + MLA_ROPE
    wkr = mla_w_in[:, c2:c3]
    wkrp = jnp.concatenate([wkr[:, half:], wkr[:, :half]], axis=1)
    gkr = mla_g_kr.astype(F32)
    gkrp = jnp.concatenate([gkr[half:], gkr[:half]])
    kvb = mla_w_kvb.reshape(MLA_KV_LORA, N_HEADS, MLA_NOPE + HEAD_DIM)
    mla = dict(
        wcq=bf(mla_w_in[:, :c1]), wckv=bf(mla_w_in[:, c1:c2]), wkr=bf(_pad_cols(wkr, 128)),
        wkrp=bf(_pad_cols(wkrp, 128)), wzt=bf(mla_w_in[:, c3:].T),
        gqa=row(mla_g_qa), gkva=row(mla_g_kva), gkr=row(jnp.pad(gkr, (0, 128 - MLA_ROPE))),
        gkrp=row(jnp.pad(gkrp, (0, 128 - MLA_ROPE))),
        wqt=bf(mla_w_qb.T), gqn=_col(mla_g_qn), gqr=_col(mla_g_qr),
        wkn=bf(kvb[:, :, :MLA_NOPE].reshape(MLA_KV_LORA, -1)),
        wvt=bf(kvb[:, :, MLA_NOPE:].reshape(MLA_KV_LORA, -1).T),
        gkn=_tile_gain(mla_g_kn, N_HEADS), bd=_block_diag(BD_LANES), wo=bf(mla_w_out))
    hq, hk = N_HEADS * HEAD_DIM, KV_HEADS * HEAD_DIM
    cuts = np.cumsum([hq, hk, hk, IDX_HEADS * IDX_DIM, IDX_DIM, IDX_HEADS]).tolist()
    dsa = dict(
        wqt=bf(dsa_w_in[:, :cuts[0]].T), wk=bf(dsa_w_in[:, cuts[0]:cuts[1]]),
        wv=bf(dsa_w_in[:, cuts[1]:cuts[2]]), wqit=bf(dsa_w_in[:, cuts[2]:cuts[3]].T),
        wki=bf(dsa_w_in[:, cuts[3]:cuts[4]]), wwit=bf(dsa_w_in[:, cuts[4]:cuts[5]].T),
        wzt=bf(dsa_w_in[:, cuts[5]:].T), gq=_col(dsa_g_q), gk=_tile_gain(dsa_g_k, KV_HEADS),
        bd=_block_diag(BD_LANES), wo=bf(dsa_w_out))
    swa = dict(
        wqt=bf(swa_w_in[:, :hq].T), wk=bf(swa_w_in[:, hq:hq + hk]), wv=bf(swa_w_in[:, hq + hk:hq + 2 * hk]),
        wzt=bf(swa_w_in[:, hq + 2 * hk:].T), gq=_col(swa_g_q), gk=_tile_gain(swa_g_k, KV_HEADS),
        bd=_block_diag(BD_LANES), wo=bf(swa_w_out), sinks=swa_sinks)
    fox = dict(
        wqt=bf(fox_w_in[:, :hq].T), wk=bf(fox_w_in[:, hq:2 * hq]), wv=bf(fox_w_in[:, 2 * hq:3 * hq]),
        wf=bf(_pad_cols(fox_w_in[:, 3 * hq:3 * hq + N_HEADS], 128)), wzt=bf(fox_w_in[:, 3 * hq + N_HEADS:].T),
        bf=row(jnp.pad(fox_b_f.astype(F32), (0, 128 - N_HEADS))), gq=_col(fox_g_q),
        gk=_tile_gain(fox_g_k, N_HEADS), bd=_block_diag(BD_LANES), wo=bf(fox_w_out))
    return [mla, dsa, swa, fox]


def kernel(x_prompt, x_sample, cache_mla_latent, cache_mla_krope, cache_dsa_k, cache_dsa_v, cache_dsa_kidx, state_swa_k, state_swa_v, cache_fox_k, cache_fox_v, cache_fox_logf, c_prompt, c_sample, norm_g, ada_w, ada_b, mla_w_in, mla_g_qa, mla_w_qb, mla_g_kva, mla_w_kvb, mla_g_qn, mla_g_qr, mla_g_kn, mla_g_kr, mla_w_out, dsa_w_in, dsa_g_q, dsa_g_k, dsa_w_out, swa_w_in, swa_g_q, swa_g_k, swa_sinks, swa_w_out, fox_w_in, fox_b_f, fox_g_q, fox_g_k, fox_w_out):
    bp, sp, _ = x_prompt.shape
    bs, ss, _ = x_sample.shape
    past_len = cache_mla_latent.shape[1]
    depth = norm_g.shape[0]
    assert bp == 1 and sp % TQ == 0 and sp % min(TQ_FLASH, sp) == 0 and sp % TS == 0 and (bs * ss) % 8 == 0 and ss <= TQ_DEC
    assert past_len % TK == 0 and past_len >= WINDOW

    weights = _prep_weights(mla_w_in, mla_g_qa, mla_w_qb, mla_g_kva, mla_w_kvb, mla_g_qn, mla_g_qr,
                            mla_g_kn, mla_g_kr, mla_w_out, dsa_w_in, dsa_g_q, dsa_g_k, dsa_w_out,
                            swa_w_in, swa_g_q, swa_g_k, swa_sinks, swa_w_out, fox_w_in, fox_b_f,
                            fox_g_q, fox_g_k, fox_w_out)
    rows = bp + bs
    rows_p = -(-rows // 8) * 8
    c_all = jnp.concatenate([c_prompt, c_sample, jnp.zeros((rows_p - rows, D_MODEL), F32)], axis=0)
    mod = _ada_mod(c_all, ada_w, ada_b)

    st_p = _Stream(bp, sp, 0)
    st_s = _Stream(bs, ss, past_len)
    pasts = ((cache_mla_latent, cache_mla_krope), (cache_dsa_k, cache_dsa_v, cache_dsa_kidx),
             (state_swa_k, state_swa_v), (cache_fox_k, cache_fox_v, cache_fox_logf))
    layers = (_mla_layer, _dsa_layer, _swa_layer, _fox_layer)
    xp = x_prompt.reshape(st_p.r, D_MODEL)
    xs = x_sample.reshape(st_s.r, D_MODEL)
    new_p, new_s = [], []
    for layer in range(depth):
        kind = layer % len(layers)
        g = norm_g[layer].astype(F32).reshape(1, -1)
        xp, n = layers[kind](st_p, xp, _mod_rows(st_p, mod[layer], 0), g, weights[kind], None)
        new_p.append(n)
        xs, n = layers[kind](st_s, xs, _mod_rows(st_s, mod[layer], bp), g, weights[kind], pasts[kind])
        new_s.append(n)

    def shaped(st, new):
        (lat, kr), (dk, dv, dki), (sk, sv), (fk, fv, flf) = new
        b, s = st.b, st.s
        return (lat.reshape(b, s, -1), kr.reshape(b, s, -1),
                dk.reshape(b, s, KV_HEADS, HEAD_DIM), dv.reshape(b, s, KV_HEADS, HEAD_DIM),
                dki.reshape(b, s, -1),
                sk.reshape(b, -1, KV_HEADS, HEAD_DIM), sv.reshape(b, -1, KV_HEADS, HEAD_DIM),
                fk.reshape(b, s, N_HEADS, HEAD_DIM), fv.reshape(b, s, N_HEADS, HEAD_DIM),
                flf.reshape(b, s, -1))

    return (xp.reshape(x_prompt.shape), xs.reshape(x_sample.shape)) + shaped(st_p, new_p) + shaped(st_s, new_s)
```

```python
import functools

import numpy as np
import jax
import jax.numpy as jnp
from jax import lax
from jax.experimental import pallas as pl
from jax.experimental.pallas import tpu as pltpu

F32 = jnp.float32
BF16 = jnp.bfloat16
I32 = jnp.int32

D_MODEL = 1024
HEAD_DIM = 64
N_HEADS = 16
KV_HEADS = 4
CHUNK = 64
CHUNK_SHIFT = 6
WINDOW = 128
WIN_CHUNKS = WINDOW // CHUNK
EPS = 1e-6
ROPE_BASE = 10000.0
MLA_NOPE, MLA_ROPE, MLA_Q_LORA, MLA_KV_LORA = 64, 32, 384, 256
IDX_HEADS, IDX_DIM, TOPK_MAX = 8, 64, 256
LOG2E = 1.4426950408889634
NEG = -1e30
INT_MIN = -(2 ** 31)
INT_MAX = 2 ** 31 - 1
I16 = jnp.int16
I16_MIN, I16_MAX = -(2 ** 15), 2 ** 15 - 1

TS = 512
TQ = 256
TQ_FLASH = 2048
TK = 512
TQ_DEC = 128
BD_LANES = 256
SUM_ROWS = 16
KT_LANES = 256
VMEM_LIMIT = 56 * 1024 * 1024
DSA_GROUPS = 2
DSA_VMEM_LIMIT = 62 * 1024 * 1024


def _row_tile(r):
    return TS if r % TS == 0 else r


def _cparams(sem, vmem=VMEM_LIMIT):
    return pltpu.CompilerParams(dimension_semantics=sem, vmem_limit_bytes=vmem)


def _dotf(a, b):
    return jnp.dot(a, b, preferred_element_type=F32)


def _dot_nt(a, b):
    return lax.dot_general(a, b, (((1,), (1,)), ((), ())), preferred_element_type=F32)


def _dot_tn(a, b):
    return lax.dot_general(a, b, (((0,), (0,)), ((), ())), preferred_element_type=F32)


def _split3(x):
    hi = x.astype(BF16)
    r = x - hi.astype(F32)
    mid = r.astype(BF16)
    lo = (r - mid.astype(F32)).astype(BF16)
    return hi, mid, lo


def _silu(x):
    return x / (1.0 + jnp.exp(-x))


def _full_spec(arr):
    nd = arr.ndim
    return pl.BlockSpec(arr.shape, lambda *_: (0,) * nd)


def _row_spec(arr, ts):
    if arr.shape[0] == 1:
        return pl.BlockSpec((1, arr.shape[1]), lambda i: (0, 0))
    return pl.BlockSpec((ts, arr.shape[1]), lambda i: (i, 0))


def _col_spec(arr, ts):
    return pl.BlockSpec((arr.shape[0], ts), lambda i: (0, i))


def _ada_body(c_ref, w_ref, b_ref, o_ref):
    a = _silu(c_ref[...])
    w = w_ref[0]
    a_hi = a.astype(BF16)
    a_lo = (a - a_hi.astype(F32)).astype(BF16)
    w_hi = w.astype(BF16)
    w_lo = (w - w_hi.astype(F32)).astype(BF16)
    o_ref[0] = _dotf(a_hi, w_hi) + _dotf(a_hi, w_lo) + _dotf(a_lo, w_hi) + b_ref[0]


def _ada_mod(c_all, ada_w, ada_b):
    depth, d, n3 = ada_w.shape
    bp = c_all.shape[0]
    tn = 768
    return pl.pallas_call(
        _ada_body,
        grid=(depth, n3 // tn),
        in_specs=[
            pl.BlockSpec((bp, d), lambda l, j: (0, 0)),
            pl.BlockSpec((1, d, tn), lambda l, j: (l, 0, j)),
            pl.BlockSpec((1, 1, tn), lambda l, j: (l, 0, j)),
        ],
        out_specs=pl.BlockSpec((1, bp, tn), lambda l, j: (l, 0, j)),
        out_shape=jax.ShapeDtypeStruct((depth, bp, n3), F32),
        compiler_params=_cparams(("arbitrary", "arbitrary")),
        name="ada_mod",
    )(c_all, ada_w, ada_b.reshape(depth, 1, n3))


def _prenorm(x_ref, g_ref, sc_ref, sh_ref):
    x = x_ref[...]
    ms = jnp.mean(x * x, axis=-1, keepdims=True)
    xn = x * lax.rsqrt(ms + EPS) * g_ref[...]
    return (xn * (1.0 + sc_ref[...]) + sh_ref[...]).astype(BF16)


def _group_sumsq(y, bd_ref):
    sq = (y * y).astype(BF16)
    bd = bd_ref[...]
    chunks = []
    for c in range(y.shape[1] // BD_LANES):
        sl = slice(c * BD_LANES, (c + 1) * BD_LANES)
        chunks.append(_dotf(sq[:, sl], bd))
    return jnp.concatenate(chunks, axis=1) if len(chunks) > 1 else chunks[0]


def _head_rms_rows(q, g_col):
    ms = jnp.mean(q * q, axis=0, keepdims=True)
    return q * lax.rsqrt(ms + EPS) * g_col


def _place_rows(piece, row0, total):
    ts = piece.shape[1]
    parts = []
    if row0 > 0:
        parts.append(jnp.zeros((row0, ts), F32))
    parts.append(piece)
    rest = total - row0 - piece.shape[0]
    if rest > 0:
        parts.append(jnp.zeros((rest, ts), F32))
    return jnp.concatenate(parts, axis=0) if len(parts) > 1 else piece


def _proj_call(body, name, r, ts, row_in, col_in, const_in, outs):
    in_specs = ([_row_spec(a, ts) for a in row_in] + [_col_spec(a, ts) for a in col_in]
                + [_full_spec(a) for a in const_in])
    return pl.pallas_call(
        body,
        grid=(r // ts,),
        in_specs=in_specs,
        out_specs=[pl.BlockSpec(blk, im) for (_, _, blk, im) in outs],
        out_shape=[jax.ShapeDtypeStruct(s, dt) for (s, dt, _, _) in outs],
        compiler_params=_cparams(("arbitrary",)),
        name=name,
    )(*row_in, *col_in, *const_in)


def _rows_out(r, c, ts, dtype=F32):
    return ((r, c), dtype, (ts, c), lambda i: (i, 0))


def _cols_out(c, r, ts, dtype=F32):
    return ((c, r), dtype, (c, ts), lambda i: (0, i))


def _qp_out(r, ts):
    return ((N_HEADS, KT_LANES, r), BF16, (N_HEADS, KT_LANES, ts), lambda i: (0, 0, i))


def _mla_in_body(x_ref, sc_ref, sh_ref, cos_ref, sin_ref, g_ref, gqa, gkva, gkr, gkrp,
                 wcq, wckv, wkr, wkrp, wzt, cqn_o, lat_o, kr_o, zt_o):
    h = _prenorm(x_ref, g_ref, sc_ref, sh_ref)
    cq = _dotf(h, wcq[...])
    cqn_o[...] = (cq * lax.rsqrt(jnp.mean(cq * cq, axis=-1, keepdims=True) + EPS) * gqa[...]).astype(BF16)
    ckv = _dotf(h, wckv[...])
    lat_o[...] = ckv * lax.rsqrt(jnp.mean(ckv * ckv, axis=-1, keepdims=True) + EPS) * gkva[...]
    kr = _dotf(h, wkr[...])
    krp = _dotf(h, wkrp[...])
    inv = lax.rsqrt(jnp.sum(kr * kr, axis=-1, keepdims=True) * (1.0 / MLA_ROPE) + EPS)
    kr_o[...] = (kr * gkr[...] * cos_ref[...] + krp * gkrp[...] * sin_ref[...]) * inv
    zt_o[...] = _dot_nt(wzt[...], h)


def _mla_q_body(cqn_ref, cos_ref, sin_ref, wqt, gqn, gqr, qp_o):
    qt = _dot_nt(wqt[...], cqn_ref[...])
    ts = qt.shape[1]
    sc = (MLA_NOPE + MLA_ROPE) ** -0.5 * LOG2E
    c = cos_ref[...]
    s = sin_ref[...]
    half = MLA_ROPE // 2
    width = MLA_NOPE + MLA_ROPE
    for h in range(N_HEADS):
        qn = _head_rms_rows(qt[width * h:width * h + MLA_NOPE], gqn[...]) * sc
        qr = _head_rms_rows(qt[width * h + MLA_NOPE:width * (h + 1)], gqr[...]) * sc
        x1, x2 = qr[:half], qr[half:]
        o1 = x1 * c - x2 * s
        o2 = x2 * c + x1 * s
        a = h % 2
        pieces = [qn, jnp.zeros((HEAD_DIM, ts), F32)]
        if a:
            pieces = pieces[::-1]
        pieces += [o1, o2, jnp.zeros((KT_LANES - 2 * HEAD_DIM - MLA_ROPE, ts), F32)]
        qp_o[h] = jnp.concatenate(pieces, axis=0).astype(BF16)


def _mla_kv_body(lat_ref, kr_ref, gkn, wkn, wvt, bd, kt_o, vt_o):
    lat = lat_ref[...].astype(BF16)
    kn = _dotf(lat, wkn[...])
    ss = _group_sumsq(kn, bd)
    kn = kn * lax.rsqrt(ss * (1.0 / MLA_NOPE) + EPS) * gkn[...]
    kr = kr_ref[...].astype(BF16)
    for p in range(N_HEADS // 2):
        kt_o[p, :, 0:128] = kn[:, 128 * p:128 * (p + 1)].astype(BF16)
        kt_o[p, :, 128:256] = kr
    vt_o[0] = _dot_nt(wvt[...], lat).astype(BF16)


def _q_heads_to_qp(qt, gq, qp_o, row_of_head, extra_of_head=None):
    sc = HEAD_DIM ** -0.5 * LOG2E
    ts = qt.shape[1]
    for h in range(N_HEADS):
        qh = _head_rms_rows(qt[HEAD_DIM * h:HEAD_DIM * (h + 1)], gq[...]) * sc
        blk = _place_rows(qh, row_of_head(h), KT_LANES)
        if extra_of_head is not None:
            lo, hi = extra_of_head(h)
            rows = lax.broadcasted_iota(I32, (KT_LANES, ts), 0)
            blk = blk + jnp.where((rows >= lo) & (rows < hi), 1.0, 0.0)
        qp_o[h] = blk.astype(BF16)


def _dsa_in_body(x_ref, sc_ref, sh_ref, g_ref, gk, gq, wqt, wk, wv, wqit, wki, wwit, wzt, bd,
                 qp_o, k_o, v_o, qi_o, ki_o, wi_o, zt_o, *key_side):
    h = _prenorm(x_ref, g_ref, sc_ref, sh_ref)
    _q_heads_to_qp(_dot_nt(wqt[...], h), gq, qp_o, lambda hh: HEAD_DIM * (hh // (N_HEADS // KV_HEADS)))
    k = _dotf(h, wk[...])
    k = k * lax.rsqrt(_group_sumsq(k, bd) * (1.0 / HEAD_DIM) + EPS) * gk[...]
    v = _dotf(h, wv[...])
    ki = _dotf(h, wki[...])
    k_o[...] = k
    v_o[...] = v
    ki_o[...] = ki
    qit = _dot_nt(wqit[...], h)
    for ih in range(IDX_HEADS):
        qi_o[ih] = qit[IDX_DIM * ih:IDX_DIM * (ih + 1)].astype(BF16)
    wi_o[...] = _dot_nt(wwit[...], h)
    zt_o[...] = _dot_nt(wzt[...], h)
    if key_side:
        kt_o, vt_o, kib_o = key_side
        kt_o[...] = k.astype(BF16)
        vt_o[0] = v.T.astype(BF16)
        kib_o[...] = ki.astype(BF16)


def _swa_in_body(x_ref, sc_ref, sh_ref, g_ref, gk, gq, wqt, wk, wv, wzt, bd, qp_o, k_o, v_o, zt_o):
    h = _prenorm(x_ref, g_ref, sc_ref, sh_ref)
    _q_heads_to_qp(_dot_nt(wqt[...], h), gq, qp_o, lambda hh: HEAD_DIM * (hh // (N_HEADS // KV_HEADS)))
    k = _dotf(h, wk[...])
    k_o[...] = k * lax.rsqrt(_group_sumsq(k, bd) * (1.0 / HEAD_DIM) + EPS) * gk[...]
    v_o[...] = _dotf(h, wv[...])
    zt_o[...] = _dot_nt(wzt[...], h)


def _fox_in_body(x_ref, sc_ref, sh_ref, g_ref, gk, bf, gq, wqt, wk, wv, wf, wzt, bd,
                 qp_o, k_o, v_o, lf_o, zt_o):
    h = _prenorm(x_ref, g_ref, sc_ref, sh_ref)
    _q_heads_to_qp(_dot_nt(wqt[...], h), gq, qp_o, lambda hh: HEAD_DIM * (hh % 2),
                   lambda hh: (128 + 3 * (hh % 2), 128 + 3 * (hh % 2) + 3))
    k = _dotf(h, wk[...])
    k_o[...] = k * lax.rsqrt(_group_sumsq(k, bd) * (1.0 / HEAD_DIM) + EPS) * gk[...]
    v_o[...] = _dotf(h, wv[...])
    f = _dotf(h, wf[...]) + bf[...]
    lf_o[...] = jnp.minimum(f, 0.0) - jnp.log1p(jnp.exp(-jnp.abs(f)))
    zt_o[...] = _dot_nt(wzt[...], h)


def _out_body(x_ref, gate_ref, ot_ref, zt_ref, wo, o_ref):
    u = (ot_ref[...] * _silu(zt_ref[...])).astype(BF16)
    o_ref[...] = x_ref[...] + gate_ref[...] * _dot_tn(u, wo[...])


def _out_proj(x, gate, ot, zt, wo):
    r = x.shape[0]
    ts = _row_tile(r)
    (out,) = _proj_call(_out_body, "out_proj", r, ts, [x, gate], [ot, zt], [wo],
                        [_rows_out(r, D_MODEL, ts)])
    return out


def _src_arrays(src):
    past, new = src
    return [new] if past is None else [past, new]


def _src_specs(src):
    past, new = src
    c = new.shape[2]
    if past is None:
        return [pl.BlockSpec((1, TK, c), lambda bb, j: (bb, j, 0))]
    npb = past.shape[1] // TK
    return [pl.BlockSpec((1, TK, c), lambda bb, j: (bb, jnp.minimum(j, npb - 1), 0)),
            pl.BlockSpec((1, TK, c), lambda bb, j: (bb, jnp.maximum(j - npb, 0), 0))]


def _src_rows(src):
    past, new = src
    return new.shape[1] + (0 if past is None else past.shape[1])


def _src_load(refs, src_past_blocks):
    if src_past_blocks is None:
        return refs[0][0]
    return jnp.where(pl.program_id(1) < src_past_blocks, refs[0][0], refs[1][0])


def _split_refs(refs, past_blocks):
    vals, pos = [], 0
    for npb in past_blocks:
        cnt = 1 if npb is None else 2
        vals.append(_src_load(refs[pos:pos + cnt], npb))
        pos += cnt
    return vals, refs[pos:]


def _past_blocks(srcs):
    return tuple(None if s[0] is None else s[0].shape[1] // TK for s in srcs)


def _gqa_prep_body(*refs, past_blocks):
    vals, outs = _split_refs(refs, past_blocks)
    outs[0][0] = vals[0].astype(BF16)
    outs[1][0, 0] = vals[1].T.astype(BF16)
    if len(vals) > 2:
        outs[2][0] = vals[2].astype(BF16)


def _gqa_prep(k, v, ki=None):
    srcs = [k, v] + ([ki] if ki is not None else [])
    b, c = k[1].shape[0], k[1].shape[2]
    lp = _src_rows(k)
    nkb = lp // TK
    out_specs = [pl.BlockSpec((1, TK, c), lambda bb, j: (bb, j, 0)),
                 pl.BlockSpec((1, 1, c, TK), lambda bb, j: (bb, j, 0, 0))]
    out_shape = [jax.ShapeDtypeStruct((b, lp, c), BF16), jax.ShapeDtypeStruct((b, nkb, c, TK), BF16)]
    if ki is not None:
        ci = ki[1].shape[2]
        out_specs.append(pl.BlockSpec((1, TK, ci), lambda bb, j: (bb, j, 0)))
        out_shape.append(jax.ShapeDtypeStruct((b, lp, ci), BF16))
    return pl.pallas_call(
        functools.partial(_gqa_prep_body, past_blocks=_past_blocks(srcs)),
        grid=(b, nkb), in_specs=[sp for s in srcs for sp in _src_specs(s)],
        out_specs=out_specs, out_shape=out_shape,
        compiler_params=_cparams(("arbitrary", "arbitrary")), name="gqa_prep",
    )(*[a for s in srcs for a in _src_arrays(s)])


def _fox_prep_body(*refs, past_blocks):
    (k, v, lf), (sel_ref, kt_o, vt_o, carry_ref) = _split_refs(refs, past_blocks)
    kb = pl.program_id(1)

    @pl.when(kb == 0)
    def _():
        carry_ref[...] = jnp.zeros_like(carry_ref)

    r = lax.broadcasted_iota(I32, (TK, TK), 0)
    c = lax.broadcasted_iota(I32, (TK, TK), 1)
    tri = jnp.where(c <= r, 1.0, 0.0).astype(BF16)
    cum3 = _dotf(tri, jnp.concatenate(_split3(lf), axis=1))
    cum = cum3[:, 0:128] + cum3[:, 128:256] + cum3[:, 256:384] + carry_ref[...]
    carry_ref[...] = cum[TK - 1:TK, :]
    ex = _dotf(jnp.concatenate(_split3(-(cum * LOG2E)), axis=1), sel_ref[...])
    for p in range(N_HEADS // 2):
        kt_o[p, :, 0:128] = k[:, 128 * p:128 * (p + 1)].astype(BF16)
        kt_o[p, :, 128:256] = ex[:, 128 * p:128 * (p + 1)].astype(BF16)
    vt_o[0, 0] = v.T.astype(BF16)


def _fox_sel():
    sel = np.zeros((3, 128, N_HEADS * HEAD_DIM), np.float32)
    for h in range(N_HEADS):
        for j in range(3):
            sel[j, h, 128 * (h // 2) + 3 * (h % 2) + j] = 1.0
    return jnp.asarray(sel.reshape(3 * 128, N_HEADS * HEAD_DIM), BF16)


def _fox_prep(k, v, lf128):
    srcs = [k, v, lf128]
    b, c = k[1].shape[0], k[1].shape[2]
    lp = _src_rows(k)
    nkb = lp // TK
    npair = N_HEADS // 2
    sel = _fox_sel()
    return pl.pallas_call(
        functools.partial(_fox_prep_body, past_blocks=_past_blocks(srcs)),
        grid=(b, nkb),
        in_specs=[sp for s in srcs for sp in _src_specs(s)] + [_full_spec(sel)],
        out_specs=[pl.BlockSpec((npair, TK, KT_LANES), lambda bb, j: (bb, j, 0)),
                   pl.BlockSpec((1, 1, c, TK), lambda bb, j: (bb, j, 0, 0))],
        out_shape=[jax.ShapeDtypeStruct((b * npair, lp, KT_LANES), BF16),
                   jax.ShapeDtypeStruct((b, nkb, c, TK), BF16)],
        scratch_shapes=[pltpu.VMEM((1, 128), F32)],
        compiler_params=_cparams(("arbitrary", "arbitrary")), name="fox_prep",
    )(*[a for s in srcs for a in _src_arrays(s)], sel)


def _visible_end(qpos, mode, n_keys):
    if mode == "causal":
        end = qpos + 1
    else:
        end = ((qpos >> CHUNK_SHIFT) + 1) << CHUNK_SHIFT
    return jnp.minimum(end, n_keys)


def _softmax_update(s, mx8, m_ref, c8=None, lane0=0):
    m8 = m_ref[:, lane0:]
    m_new8 = jnp.maximum(m8, mx8)
    alpha8 = jnp.exp2(m8 - m_new8)
    shift = m_new8[0:1] if c8 is None else (m_new8 - c8)[0:1]
    m_ref[:, lane0:] = m_new8
    return jnp.exp2(s - shift).astype(BF16), alpha8[0:1]


def _pv_and_sum(vt_blk, pb):
    ones = jnp.ones((SUM_ROWS, vt_blk.shape[1]), BF16)
    return _dotf(jnp.concatenate([vt_blk, ones], axis=0), pb)


def _run_two_stage(n_full, n_masked, stage_a, stage_b, trim=lambda j: 0):
    odd = n_full % 2

    @pl.when(odd == 1)
    def _():
        stage_a(0, 0, False, 0)
        stage_b(0, 0, False, 0)

    n_pairs = (n_full - odd) // 2

    @pl.when(n_pairs >= 1)
    def _():
        stage_a(odd, 0, False, 0)

        def body(u, _):
            kb = odd + 2 * u
            stage_b(kb, 0, False, 0)
            stage_a(kb + 1, 1, False, 0)
            stage_b(kb + 1, 1, False, 0)
            stage_a(kb + 2, 0, False, 0)
            return 0

        lax.fori_loop(0, n_pairs - 1, body, 0)
        stage_b(n_full - 2, 0, False, 0)
        stage_a(n_full - 1, 1, False, 0)
        stage_b(n_full - 1, 1, False, 0)
        stage_a(n_full, 0, True, trim(0))

    @pl.when(n_pairs < 1)
    def _():
        stage_a(n_full, 0, True, trim(0))

    for j in range(1, n_masked):
        stage_b(n_full + j - 1, (j - 1) % 2, True, trim(j - 1))
        stage_a(n_full + j, j % 2, True, trim(j))
    stage_b(n_full + n_masked - 1, (n_masked - 1) % 2, True, trim(n_masked - 1))


def _flash_body(qp_ref, kt_ref, vt_ref, o_ref, s_buf, mx_buf, m_ref, acc_ref, *,
                tq, mode, q0, n_keys, paired, n_masked):
    i = pl.program_id(1)
    q_first = q0 + i * tq
    n_full = _visible_end(q_first, mode, n_keys) // TK
    ns = N_HEADS // 2 if paired else 1
    per = 2 if paired else 1
    ws, rows = per * tq, per * HEAD_DIM
    w = ns * ws
    ik = lax.broadcasted_iota(I32, (TK, ws), 0)
    iq = lax.broadcasted_iota(I32, (TK, tq), 1)
    qpos = q_first + (jnp.concatenate([iq] * per, axis=1) if paired else iq)
    qs = [jnp.concatenate([qp_ref[per * si + a] for a in range(per)], axis=1) if paired else qp_ref[si]
          for si in range(ns)]

    m_ref[...] = jnp.full((8, w), NEG, F32)
    acc_ref[...] = jnp.zeros((rows + SUM_ROWS, w), F32)

    def stage_a(kb, slot, masked, lane0):
        assert lane0 == 0 or ns == 1
        k0 = pl.multiple_of(kb * TK, TK)
        wl = ws - lane0
        if masked:
            if lane0:
                kpos = k0 + lax.broadcasted_iota(I32, (TK, wl), 0)
                qp = q_first + lane0 + lax.broadcasted_iota(I32, (TK, wl), 1)
            else:
                kpos, qp = k0 + ik, qpos
            valid = (kpos <= qp) if mode == "causal" else ((kpos >> CHUNK_SHIFT) <= (qp >> CHUNK_SHIFT))
            valid = valid & (kpos < n_keys)
        for si in range(ns):
            s = _dotf(kt_ref[si, pl.ds(k0, TK), :], qs[si][:, lane0:])
            if masked:
                s = jnp.where(valid, s, NEG)
            s_buf[slot, :, si * ws + lane0:(si + 1) * ws] = s
            mx_buf[slot, :, si * ws + lane0:(si + 1) * ws] = jnp.broadcast_to(
                jnp.max(s, axis=0, keepdims=True), (8, wl))

    def stage_b(kb, slot, masked, lane0):
        del masked
        pb, alpha = _softmax_update(s_buf[slot, :, lane0:], mx_buf[slot, :, lane0:], m_ref, lane0=lane0)
        for si in range(ns):
            loc = slice(si * ws, (si + 1) * ws - lane0)
            glob = slice(si * ws + lane0, (si + 1) * ws)
            acc_ref[:, glob] = alpha[:, loc] * acc_ref[:, glob] + _pv_and_sum(
                vt_ref[0, kb, rows * si:rows * (si + 1), :], pb[:, loc])

    trim = (lambda j: j * TK) if (not paired and tq > TK) else (lambda j: 0)
    _run_two_stage(n_full, n_masked, stage_a, stage_b, trim)
    out = acc_ref[0:rows, :] / acc_ref[rows:rows + 1, :]
    for si in range(ns):
        for a in range(per):
            o_ref[per * si + a] = out[HEAD_DIM * a:HEAD_DIM * (a + 1), si * ws + tq * a:si * ws + tq * (a + 1)]


def _visible_end_static(qpos, mode, n_keys):
    end = qpos + 1 if mode == "causal" else ((qpos >> CHUNK_SHIFT) + 1) << CHUNK_SHIFT
    return min(end, n_keys)


def _flash(qp, kt, vt, *, tq, mode, q0, n_real_q, n_keys, paired):
    bh, _, sq = qp.shape
    lp = kt.shape[1]
    nkb = vt.shape[1]
    assert (q0 % TK == 0) and (tq % TK == 0 or sq == tq)
    n_masked = (-(-_visible_end_static(q0 + n_real_q - 1, mode, n_keys) // TK)
                - _visible_end_static(q0, mode, n_keys) // TK)
    body = functools.partial(_flash_body, tq=tq, mode=mode, q0=q0, n_keys=n_keys, paired=paired,
                             n_masked=n_masked)
    if not paired:
        ns, w, rows = 1, tq, HEAD_DIM
        kt_spec = pl.BlockSpec((1, lp, KT_LANES), lambda g, i: (g // 2, 0, 0))
        vt_spec = pl.BlockSpec((1, nkb, HEAD_DIM, TK), lambda g, i: (g // N_HEADS, 0, g % N_HEADS, 0))
    else:
        ns, w, rows = N_HEADS, N_HEADS * tq, 2 * HEAD_DIM
        kt_spec = pl.BlockSpec((ns // 2, lp, KT_LANES), lambda g, i: (g, 0, 0))
        vt_spec = pl.BlockSpec((1, nkb, ns * HEAD_DIM, TK), lambda g, i: (g, 0, 0, 0))
    return pl.pallas_call(
        body,
        grid=(bh // ns, sq // tq),
        in_specs=[pl.BlockSpec((ns, KT_LANES, tq), lambda g, i: (g, 0, i)), kt_spec, vt_spec],
        out_specs=pl.BlockSpec((ns, HEAD_DIM, tq), lambda g, i: (g, 0, i)),
        out_shape=jax.ShapeDtypeStruct((bh, HEAD_DIM, sq), F32),
        scratch_shapes=[pltpu.VMEM((2, TK, w), F32), pltpu.VMEM((2, 8, w), F32), pltpu.VMEM((8, w), F32),
                        pltpu.VMEM((rows + SUM_ROWS, w), F32)],
        compiler_params=_cparams(("arbitrary", "arbitrary")),
        name="flash_" + mode,
    )(qp, kt, vt)


def _swa_body(hp_ref, qp_ref, *refs, tq, q0, n_pieces, n_rows):
    k_refs, v_refs, o_ref = refs[:n_pieces], refs[n_pieces:2 * n_pieces], refs[2 * n_pieces]
    i = pl.program_id(1)
    q_first = q0 + i * tq
    k = jnp.concatenate([r[0].astype(BF16) for r in k_refs], axis=0)
    v = jnp.concatenate([r[0].astype(BF16) for r in v_refs], axis=0)
    kw = k.shape[0]
    row = lax.broadcasted_iota(I32, (kw, tq), 0)
    kpos = (q_first - WINDOW) + row
    qpos = q_first + lax.broadcasted_iota(I32, (kw, tq), 1)
    qc = qpos >> CHUNK_SHIFT
    kc = kpos >> CHUNK_SHIFT
    valid = (kpos >= 0) & (kc <= qc) & (qc - kc <= WIN_CHUNKS) & (row < n_rows)
    dist = jnp.abs(qpos - kpos).astype(F32)
    group = N_HEADS // KV_HEADS

    def lanes(x):
        return jnp.concatenate([x] * group, axis=1)

    for n in range(KV_HEADS):
        h0 = n * group

        def per_head(col):
            return jnp.concatenate(
                [jnp.broadcast_to(hp_ref[h0 + g][:, col:col + 1], (1, tq)) for g in range(group)], axis=1)

        q = jnp.concatenate([qp_ref[h0 + g] for g in range(group)], axis=1)
        slope2, sink2 = per_head(0), per_head(1)
        s = _dotf(k, q) - slope2 * lanes(dist)
        s = jnp.where(lanes(valid), s, NEG)
        m = jnp.maximum(jnp.max(s, axis=0, keepdims=True), sink2)
        p = jnp.exp2(s - m)
        l = jnp.sum(p, axis=0, keepdims=True) + jnp.exp2(sink2 - m)
        acc = _dot_tn(v, p.astype(BF16))
        out = acc[HEAD_DIM * n:HEAD_DIM * (n + 1)] / l
        for g in range(group):
            o_ref[h0 + g] = out[:, g * tq:(g + 1) * tq]


def _swa_attend(hp, qp, k_pieces, v_pieces, *, tq, q0, n_rows):
    b = k_pieces[0][0].shape[0]
    sq = qp.shape[2]
    n_pieces = len(k_pieces)

    def spec(piece):
        _, rows, idx = piece
        return pl.BlockSpec((1, rows, KV_HEADS * HEAD_DIM), lambda bb, i: (bb, idx(i), 0))

    body = functools.partial(_swa_body, tq=tq, q0=q0, n_pieces=n_pieces, n_rows=n_rows)
    return pl.pallas_call(
        body,
        grid=(b, sq // tq),
        in_specs=[_full_spec(hp), pl.BlockSpec((N_HEADS, KT_LANES, tq), lambda bb, i: (bb, 0, i))]
        + [spec(p) for p in k_pieces] + [spec(p) for p in v_pieces],
        out_specs=pl.BlockSpec((N_HEADS, HEAD_DIM, tq), lambda bb, i: (bb, 0, i)),
        out_shape=jax.ShapeDtypeStruct((b * N_HEADS, HEAD_DIM, sq), F32),
        compiler_params=_cparams(("arbitrary", "arbitrary")),
        name="swa_attend",
    )(hp, qp, *[p[0] for p in k_pieces], *[p[0] for p in v_pieces])


def _dsa_body(hp_ref, qi_ref, wi_ref, qp_ref, ki_ref, kt_ref, vt_ref, o_ref, hi_ref, lo_ref,
              s_buf, mx_buf, m_ref, acc_ref, *, tq, q0, n_real_q, n_keys, topk, q_period):
    packed = q_period < tq
    i = pl.program_id(1)
    q_first = q0 + i * tq
    q_last = q_first + (n_real_q - 1)
    n_tot = (_visible_end(q_last, "chunk", n_keys) + (TK - 1)) // TK
    n_past = jnp.minimum(q_first, n_keys) // TK
    ik = lax.broadcasted_iota(I32, (TK, tq), 0)
    iq = lax.broadcasted_iota(I32, (TK, tq), 1)
    if packed:
        iq = iq & (q_period - 1)
    qpos = q_first + iq
    tf = float(topk)

    def blk(kb):
        return pl.ds(pl.multiple_of(kb * TK, TK), TK)

    def to_key(x):
        bits = pltpu.bitcast(x, I32)
        return jnp.where(bits < 0, bits ^ INT_MAX, bits)

    def score_blk(kb, diag):
        ki = ki_ref[0, blk(kb), :]
        acc = jnp.zeros((TK, tq), F32)
        for h in range(IDX_HEADS):
            acc = acc + wi_ref[0, h:h + 1, :] * jnp.maximum(_dotf(ki, qi_ref[0, h]), 0.0)
        key = to_key(acc)
        if diag:
            kpos = kb * TK + ik
            valid = ((kpos >> CHUNK_SHIFT) <= (qpos >> CHUNK_SHIFT)) & (kpos < n_keys)
            key = jnp.where(valid, key, INT_MIN)
        hi_ref[blk(kb), :] = (key >> 16).astype(I16)
        lo_ref[blk(kb), :] = ((key & 0xFFFF) - 32768).astype(I16)
        return 0

    lax.fori_loop(0, n_past, lambda kb, c: score_blk(kb, False), 0)
    lax.fori_loop(n_past, n_tot, lambda kb, c: score_blk(kb, True), 0)

    def count_ge(ref, mid):
        mid16 = mid.astype(I16)

        def body(kb, acc):
            ge = jnp.where(ref[blk(kb), :] >= mid16, jnp.int16(1), jnp.int16(0))
            parts = [ge[16 * r:16 * (r + 1)] for r in range(TK // 16)]
            while len(parts) > 1:
                parts = [parts[j] + parts[j + 1] for j in range(0, len(parts), 2)]
            return acc + parts[0]

        acc = lax.fori_loop(0, n_tot, body, jnp.zeros((16, tq), I16))
        return jnp.sum(acc.astype(I32), axis=0, keepdims=True).astype(F32)

    def bisect_step(ref, target, st):
        lo, hi, cl, ch = st
        mid = (lo + hi) >> 1
        cnt = count_ge(ref, mid)
        ge = cnt >= target
        return jnp.where(ge, mid, lo), jnp.where(ge, hi, mid), jnp.where(ge, cnt, cl), jnp.where(ge, ch, cnt)

    qrow = q_first + iq[0:1]
    n_vis = _visible_end(qrow, "chunk", n_keys).astype(F32)
    zero = jnp.zeros((1, tq), F32)
    st1 = (jnp.full((1, tq), I16_MIN + 1, I32), jnp.full((1, tq), I16_MAX + 1, I32), n_vis, zero)
    h_thr, _, cl1, ch1 = lax.fori_loop(0, 16, lambda _, st: bisect_step(hi_ref, tf, st), st1)
    h16 = h_thr.astype(I16)

    def mask_lo(kb, _):
        lo_ref[blk(kb), :] = jnp.where(hi_ref[blk(kb), :] == h16, lo_ref[blk(kb), :], jnp.int16(I16_MIN))
        return 0

    lax.fori_loop(0, n_tot, mask_lo, 0)
    t2 = tf - ch1

    def cond(st):
        return (st[0] < 16) & (st[2] > 0.5)

    def body(st):
        lo, hi, cl, ch = bisect_step(lo_ref, t2, st[1])
        done = (cl <= t2) | (hi - lo <= 1)
        return st[0] + 1, (lo, hi, cl, ch), jnp.sum(jnp.where(done, 0.0, 1.0))

    cl2_0 = cl1 - ch1
    st2 = (jnp.full((1, tq), I16_MIN, I32), jnp.full((1, tq), I16_MAX + 1, I32), cl2_0, zero)
    _, (l_thr, _, cl2, ch2), _ = lax.while_loop(
        cond, body, (jnp.int32(0), st2, jnp.sum(jnp.where(cl2_0 > t2, 1.0, 0.0))))
    l16 = l_thr.astype(I16)

    need = t2 - ch2

    @pl.when(jnp.sum(jnp.where(cl2 > t2, 1.0, 0.0)) > 0.5)
    def _():
        r = lax.broadcasted_iota(I32, (TK, TK), 0)
        c = lax.broadcasted_iota(I32, (TK, TK), 1)
        tri = jnp.where(c < r, 1.0, 0.0).astype(BF16)

        def fix(kb, carry):
            hb = hi_ref[blk(kb), :]
            e16 = jnp.where(hb == h16, jnp.where(lo_ref[blk(kb), :] == l16, jnp.int16(1), jnp.int16(0)),
                            jnp.int16(0))
            e = e16.astype(I32).astype(F32)
            before = _dotf(tri, e.astype(BF16)) + carry
            drop = jnp.where((e > 0.5) & (before >= need), 1, 0).astype(I16)
            hi_ref[blk(kb), :] = jnp.where(drop == jnp.int16(1), jnp.int16(I16_MIN), hb)
            return carry + jnp.sum(e, axis=0, keepdims=True)

        lax.fori_loop(0, n_tot, fix, jnp.zeros((1, tq), F32))

    def to_bias(kb, _):
        hb = hi_ref[blk(kb), :]
        zero_b, neg_b = jnp.bfloat16(0.0), jnp.bfloat16(NEG)
        at_thr = jnp.where(lo_ref[blk(kb), :] >= l16, zero_b, neg_b)
        bias = jnp.where(hb > h16, zero_b, jnp.where(hb == h16, at_thr, neg_b))
        hi_ref[blk(kb), :] = pltpu.bitcast(bias, I16)
        return 0

    lax.fori_loop(0, n_tot, to_bias, 0)

    group = N_HEADS // KV_HEADS
    heads = DSA_GROUPS * group
    wide = tq if packed else heads * tq
    gw = wide // DSA_GROUPS
    vr = DSA_GROUPS * HEAD_DIM if packed else HEAD_DIM

    def lanes(x):
        return x if packed else jnp.concatenate([x] * heads, axis=1)

    def group_body(n, _):
        h0 = n * heads
        if packed:
            q, slope2 = qp_ref[n], hp_ref[n]
        else:
            q = jnp.concatenate([qp_ref[h0 + g] for g in range(heads)], axis=1)
            slope2 = jnp.concatenate(
                [jnp.broadcast_to(hp_ref[h0 + g][:, 0:1], (1, tq)) for g in range(heads)], axis=1)
        slope8 = jnp.broadcast_to(slope2, (8, wide))
        a_tab = slope2 * lanes(ik.astype(F32))
        vrows = [pl.ds(pl.multiple_of((n * DSA_GROUPS + gi) * HEAD_DIM, HEAD_DIM), HEAD_DIM)
                 for gi in range(DSA_GROUPS)]
        m_ref[...] = jnp.full((8, wide), NEG, F32)
        acc_ref[...] = jnp.zeros((vr + SUM_ROWS, wide), F32)

        def c8(kb):
            return slope8 * (kb * TK - q_first).astype(F32)

        def stage_a(kb, slot, diag, lane0):
            del lane0
            bias = pltpu.bitcast(hi_ref[blk(kb), :], jnp.bfloat16).astype(F32)
            s = _dotf(kt_ref[0, blk(kb), :], q) + lanes(bias)
            if diag:
                kpos = kb * TK + ik
                rel = iq.astype(F32) - jnp.abs(qpos - kpos).astype(F32)
                s = s + slope2 * lanes(rel)
            else:
                s = s + a_tab
            mx8 = jnp.broadcast_to(jnp.max(s, axis=0, keepdims=True), (8, wide))
            s_buf[slot] = s
            mx_buf[slot] = mx8 if diag else mx8 + c8(kb)

        def stage_b(kb, slot, diag, lane0):
            del lane0
            pb, alpha = _softmax_update(s_buf[slot], mx_buf[slot], m_ref, None if diag else c8(kb))
            if packed:
                vt2 = jnp.concatenate([vt_ref[0, kb, vrows[gi], :] for gi in range(DSA_GROUPS)], axis=0)
                acc_ref[...] = alpha * acc_ref[...] + _pv_and_sum(vt2, pb)
            else:
                for gi in range(DSA_GROUPS):
                    sl = slice(gi * gw, (gi + 1) * gw)
                    acc_ref[:, sl] = alpha[:, sl] * acc_ref[:, sl] + _pv_and_sum(
                        vt_ref[0, kb, vrows[gi], :], pb[:, sl])

        _run_two_stage(n_past, 1, stage_a, stage_b)
        out = acc_ref[0:vr, :] / acc_ref[vr:vr + 1, :]
        if packed:
            lane = lax.broadcasted_iota(I32, (HEAD_DIM, wide), 1)
            sel = out[0:HEAD_DIM]
            for gi in range(1, DSA_GROUPS):
                sel = jnp.where(lane >= gi * gw, out[gi * HEAD_DIM:(gi + 1) * HEAD_DIM], sel)
            o_ref[n] = sel
        else:
            for g in range(heads):
                o_ref[h0 + g] = out[:, g * tq:(g + 1) * tq]
        return 0

    lax.fori_loop(0, KV_HEADS // DSA_GROUPS, group_body, 0)


def _dsa_attend(hp, qi, wi, qp, ki, kt, vt, *, tq, q0, n_real_q, n_keys, q_period):
    b = ki.shape[0]
    sq = qp.shape[2]
    lp = kt.shape[1]
    nkb = vt.shape[1]
    topk = min(TOPK_MAX, n_keys // 4)
    heads = DSA_GROUPS * (N_HEADS // KV_HEADS)
    steps = KV_HEADS // DSA_GROUPS
    packed = q_period < tq
    assert TK % tq == 0 and q0 % TK == 0 and n_real_q <= q_period
    assert (not packed) or (heads * q_period == tq and sq == tq and q_period & (q_period - 1) == 0)
    wide = tq if packed else heads * tq
    vr = DSA_GROUPS * HEAD_DIM if packed else HEAD_DIM
    nq = steps if packed else N_HEADS
    body = functools.partial(_dsa_body, tq=tq, q0=q0, n_real_q=n_real_q, n_keys=n_keys, topk=topk,
                             q_period=q_period)
    once = pl.Buffered(1)
    return pl.pallas_call(
        body,
        grid=(b, sq // tq),
        in_specs=[_full_spec(hp),
                  pl.BlockSpec((1, IDX_HEADS, IDX_DIM, tq), lambda bb, i: (bb, 0, 0, i)),
                  pl.BlockSpec((1, IDX_HEADS, tq), lambda bb, i: (bb, 0, i)),
                  pl.BlockSpec((nq, KT_LANES, tq), lambda bb, i: (bb, 0, i)),
                  pl.BlockSpec((1, lp, IDX_DIM), lambda bb, i: (bb, 0, 0), pipeline_mode=once),
                  pl.BlockSpec((1, lp, KT_LANES), lambda bb, i: (bb, 0, 0), pipeline_mode=once),
                  pl.BlockSpec((1, nkb, KV_HEADS * HEAD_DIM, TK), lambda bb, i: (bb, 0, 0, 0),
                               pipeline_mode=once)],
        out_specs=pl.BlockSpec((nq, HEAD_DIM, tq), lambda bb, i: (bb, 0, i)),
        out_shape=jax.ShapeDtypeStruct((b * nq, HEAD_DIM, sq), F32),
        scratch_shapes=[pltpu.VMEM((lp, tq), I16), pltpu.VMEM((lp, tq), I16),
                        pltpu.VMEM((2, TK, wide), F32), pltpu.VMEM((2, 8, wide), F32),
                        pltpu.VMEM((8, wide), F32), pltpu.VMEM((vr + SUM_ROWS, wide), F32)],
        compiler_params=_cparams(("arbitrary", "arbitrary"), DSA_VMEM_LIMIT),
        name="dsa_attend",
    )(hp, qi, wi, qp, ki, kt, vt)


def _block_diag(c):
    g = np.arange(c) // HEAD_DIM
    return jnp.asarray(g[:, None] == g[None, :], BF16)


def _tile_gain(g, n):
    return jnp.tile(g.astype(F32), n).reshape(1, -1)


def _col(g):
    return g.astype(F32).reshape(-1, 1)


def _pad_cols(w, n):
    return jnp.pad(w, ((0, 0), (0, n - w.shape[1])))


def _rope_tables(pos):
    half = MLA_ROPE // 2
    inv = ROPE_BASE ** (-jnp.arange(half, dtype=F32) / half)
    ang = pos.astype(F32)[:, None] * inv[None, :]
    return jnp.cos(ang), jnp.sin(ang)


def _alibi_slopes():
    return np.asarray(2.0 ** (-8.0 * np.arange(1, N_HEADS + 1) / N_HEADS), dtype=np.float32)


def _head_params(sinks=None):
    hp = jnp.zeros((N_HEADS, 1, 128), F32)
    hp = hp.at[:, 0, 0].set(jnp.asarray(_alibi_slopes()) * LOG2E)
    if sinks is not None:
        hp = hp.at[:, 0, 1].set(sinks.astype(F32) * LOG2E)
    return hp


class _Stream:
    def __init__(self, batch, seq, past):
        self.b, self.s, self.p = batch, seq, past
        self.r = batch * seq
        self.decode = past > 0
        self.tq = TQ_DEC if self.decode else TQ
        self.tqf = TQ_DEC if self.decode else min(TQ_FLASH, seq)
        self.sq =self.tq if self.decode else seq
        self.n_keys = past + seq
        self.lp = -(-self.n_keys // TK) * TK
        self.pos = past + np.tile(np.arange(seq), batch)

    def pad_keys(self, past_arr, new_arr, n_keys=None, lp=None):
        n_keys = self.n_keys if n_keys is None else n_keys
        lp = self.lp if lp is None else lp
        new_arr = new_arr.reshape(self.b, self.s, -1)
        parts = [new_arr] if past_arr is None else [past_arr.astype(F32), new_arr]
        if lp > n_keys:
            parts.append(jnp.zeros((self.b, lp - n_keys, new_arr.shape[-1]), F32))
        return jnp.concatenate(parts, axis=1) if len(parts) > 1 else new_arr

    def key_source(self, past_arr, new_arr):
        new_arr = new_arr.reshape(self.b, self.s, -1)
        if past_arr is None:
            return (None, new_arr)
        assert past_arr.shape[1] == self.p and self.p % TK == 0
        pad = self.lp - self.n_keys
        return (past_arr.astype(F32), jnp.pad(new_arr, ((0, 0), (0, pad), (0, 0))))

    def qp_blocks(self, qp):
        if not self.decode:
            return qp
        x = qp.reshape(N_HEADS, KT_LANES, self.b, self.s).transpose(2, 0, 1, 3)
        x = jnp.pad(x, ((0, 0), (0, 0), (0, 0), (0, self.tq - self.s)))
        return x.reshape(self.b * N_HEADS, KT_LANES, self.tq)

    def ot_cols(self, ot):
        if not self.decode:
            return ot.reshape(N_HEADS * HEAD_DIM, self.r)
        x = ot.reshape(self.b, N_HEADS, HEAD_DIM, self.tq)[..., :self.s]
        return x.transpose(1, 2, 0, 3).reshape(N_HEADS * HEAD_DIM, self.r)


def _mod_rows(st, mod_l, row0):
    m = mod_l[row0:row0 + st.b]
    if st.b > 1:
        m = jnp.repeat(m, st.s, axis=0)
    return m[:, :D_MODEL], m[:, D_MODEL:2 * D_MODEL], m[:, 2 * D_MODEL:]


def _mla_layer(st, x, mod, g, w, past):
    shift, scale, gate = mod
    r = st.r
    ts = _row_tile(r)
    cos, sin = _rope_tables(jnp.asarray(st.pos))
    zpad = jnp.zeros((r, 128 - MLA_ROPE), F32)
    cos_p = jnp.concatenate([cos, cos, zpad], axis=1)
    sin_p = jnp.concatenate([-sin, sin, zpad], axis=1)
    cqn, lat, kr, zt = _proj_call(
        _mla_in_body, "mla_in", r, ts,
        [x, scale, shift, cos_p, sin_p, g, w["gqa"], w["gkva"], w["gkr"], w["gkrp"]], [],
        [w["wcq"], w["wckv"], w["wkr"], w["wkrp"], w["wzt"]],
        [_rows_out(r, MLA_Q_LORA, ts, BF16), _rows_out(r, MLA_KV_LORA, ts), _rows_out(r, 128, ts),
         _cols_out(D_MODEL, r, ts)])
    (qp,) = _proj_call(_mla_q_body, "mla_q", r, ts, [cqn], [cos.T, sin.T], [w["wqt"], w["gqn"], w["gqr"]],
                       [_qp_out(r, ts)])
    past_lat, past_kr = (None, None) if past is None else past
    if past_kr is not None:
        past_kr = jnp.pad(past_kr.astype(F32), ((0, 0), (0, 0), (0, 128 - MLA_ROPE)))
    lat_all = st.pad_keys(past_lat, lat).reshape(st.b * st.lp, MLA_KV_LORA)
    kr_all = st.pad_keys(past_kr, kr).reshape(st.b * st.lp, 128)
    rk = st.b * st.lp
    npair = N_HEADS // 2
    nkb = st.lp // TK
    kt, vt = _proj_call(
        _mla_kv_body, "mla_kv", rk, TK, [lat_all, kr_all, w["gkn"]], [], [w["wkn"], w["wvt"], w["bd"]],
        [((npair * st.b, st.lp, KT_LANES), BF16, (npair, TK, KT_LANES), lambda i: (i // nkb, i % nkb, 0)),
         ((rk // TK, N_HEADS * HEAD_DIM, TK), BF16, (1, N_HEADS * HEAD_DIM, TK), lambda i: (i, 0, 0))])
    vt = vt.reshape(st.b, nkb, N_HEADS * HEAD_DIM, TK)
    ot = _flash(st.qp_blocks(qp), kt, vt, tq=st.tqf, mode="chunk", q0=st.p,
                n_real_q=st.tqf if not st.decode else st.s, n_keys=st.n_keys, paired=st.decode)
    x = _out_proj(x, gate, st.ot_cols(ot), zt, w["wo"])
    return x, (lat, kr[:, :MLA_ROPE])


def _dsa_layer(st, x, mod, g, w, past):
    shift, scale, gate = mod
    r = st.r
    ts = _row_tile(r)
    c = KV_HEADS * HEAD_DIM
    fused = past is None and ts == TK and st.b == 1
    outs = [_qp_out(r, ts), _rows_out(r, c, ts), _rows_out(r, c, ts),
            ((IDX_HEADS, IDX_DIM, r), BF16, (IDX_HEADS, IDX_DIM, ts), lambda i: (0, 0, i)),
            _rows_out(r, IDX_DIM, ts), _cols_out(IDX_HEADS, r, ts), _cols_out(D_MODEL, r, ts)]
    if fused:
        outs += [_rows_out(r, c, ts, BF16), ((r // TK, c, TK), BF16, (1, c, TK), lambda i: (i, 0, 0)),
                 _rows_out(r, IDX_DIM, ts, BF16)]
    res = _proj_call(
        _dsa_in_body, "dsa_in", r, ts, [x, scale, shift, g, w["gk"]], [],
        [w["gq"], w["wqt"], w["wk"], w["wv"], w["wqit"], w["wki"], w["wwit"], w["wzt"], w["bd"]], outs)
    qp, k, v, qi, ki, wi, zt = res[:7]
    if fused:
        kt, vt, kib = res[7][None], res[8][None], res[9][None]
    else:
        if past is None:
            pk = pv = pki = None
        else:
            pk, pv, pki = past[0].reshape(st.b, -1, c), past[1].reshape(st.b, -1, c), past[2]
        kt, vt, kib = _gqa_prep(st.key_source(pk, k), st.key_source(pv, v), st.key_source(pki, ki))
    if not st.decode:
        ot = _dsa_attend(_head_params(), qi[None], wi[None], qp, kib, kt, vt, tq=st.tq, q0=st.p,
                         n_real_q=st.tq, n_keys=st.n_keys, q_period=st.tq)
        ot = st.ot_cols(ot)
    else:
        heads = DSA_GROUPS * (N_HEADS // KV_HEADS)
        steps = KV_HEADS // DSA_GROUPS
        assert heads * st.s == st.tq

        def rows(a):
            return a.reshape(a.shape[0], a.shape[1], st.b, st.s).transpose(2, 0, 1, 3)

        qi_b = jnp.tile(rows(qi), (1, 1, 1, heads))
        wi_b = jnp.tile(rows(wi[None])[:, 0], (1, 1, heads))
        qp_b = rows(qp).reshape(st.b, steps, heads, KT_LANES, st.s).transpose(0, 1, 3, 2, 4)
        qp_b = qp_b.reshape(st.b * steps, KT_LANES, st.tq)
        slopes = (jnp.asarray(_alibi_slopes()) * LOG2E).reshape(steps, heads)
        hp = jnp.repeat(slopes, st.s, axis=1).reshape(steps, 1, st.tq)
        ot = _dsa_attend(hp, qi_b, wi_b, qp_b, kib, kt, vt, tq=st.tq, q0=st.p, n_real_q=st.s,
                         n_keys=st.n_keys, q_period=st.s)
        ot = ot.reshape(st.b, steps, HEAD_DIM, heads, st.s).transpose(1, 3, 2, 0, 4)
        ot = ot.reshape(N_HEADS * HEAD_DIM, st.r)
    x = _out_proj(x, gate, ot, zt, w["wo"])
    return x, (k, v, ki)


def _swa_layer(st, x, mod, g, w, past):
    shift, scale, gate = mod
    r = st.r
    ts = _row_tile(r)
    c = KV_HEADS * HEAD_DIM
    qp, k, v, zt = _proj_call(
        _swa_in_body, "swa_in", r, ts, [x, scale, shift, g, w["gk"]], [],
        [w["gq"], w["wqt"], w["wk"], w["wv"], w["wzt"], w["bd"]],
        [_qp_out(r, ts), _rows_out(r, c, ts), _rows_out(r, c, ts), _cols_out(D_MODEL, r, ts)])
    k3, v3 = k.reshape(st.b, st.s, c), v.reshape(st.b, st.s, c)
    if past is None:
        per = st.tq // WINDOW
        idx = [lambda i: jnp.maximum(per * i - 1, 0), lambda i: per * i, lambda i: per * i + 1]
        k_pieces = [(k3, WINDOW, f) for f in idx]
        v_pieces = [(v3, WINDOW, f) for f in idx]
        n_rows = WINDOW + st.tq
        new = (k3[:, st.s - WINDOW:], v3[:, st.s - WINDOW:])
    else:
        win = past[0].shape[1]
        assert win == WINDOW and st.s <= WINDOW
        pad = ((0, 0), (0, WINDOW - st.s), (0, 0))
        pk, pv = past[0].reshape(st.b, win, c).astype(F32), past[1].reshape(st.b, win, c).astype(F32)
        zero = lambda i: 0
        k_pieces = [(pk, WINDOW, zero), (jnp.pad(k3, pad), WINDOW, zero)]
        v_pieces = [(pv, WINDOW, zero), (jnp.pad(v3, pad), WINDOW, zero)]
        n_rows = win + st.s
        new = (jnp.concatenate([pk, k3], axis=1)[:, st.s:], jnp.concatenate([pv, v3], axis=1)[:, st.s:])
    ot = _swa_attend(_head_params(w["sinks"]), st.qp_blocks(qp), k_pieces, v_pieces, tq=st.tq, q0=st.p,
                     n_rows=n_rows)
    x = _out_proj(x, gate, st.ot_cols(ot), zt, w["wo"])
    return x, new


def _fox_layer(st, x, mod, g, w, past):
    shift, scale, gate = mod
    r = st.r
    ts = _row_tile(r)
    c = N_HEADS * HEAD_DIM
    qp, k, v, lf, zt = _proj_call(
        _fox_in_body, "fox_in", r, ts, [x, scale, shift, g, w["gk"], w["bf"]], [],
        [w["gq"], w["wqt"], w["wk"], w["wv"], w["wf"], w["wzt"], w["bd"]],
        [_qp_out(r, ts), _rows_out(r, c, ts), _rows_out(r, c, ts), _rows_out(r, 128, ts),
         _cols_out(D_MODEL, r, ts)])
    if past is None:
        pk = pv = plf = None
    else:
        pk, pv = past[0].reshape(st.b, -1, c), past[1].reshape(st.b, -1, c)
        plf = jnp.pad(past[2].astype(F32), ((0, 0), (0, 0), (0, 128 - N_HEADS)))
    kt, vt = _fox_prep(st.key_source(pk, k), st.key_source(pv, v), st.key_source(plf, lf))
    ot = _flash(st.qp_blocks(qp), kt, vt, tq=st.tqf, mode="causal", q0=st.p,
                n_real_q=st.tqf if not st.decode else st.s, n_keys=st.n_keys, paired=st.decode)
    x = _out_proj(x, gate, st.ot_cols(ot), zt, w["wo"])
    return x, (k, v, lf[:, :N_HEADS])


def _prep_weights(mla_w_in, mla_g_qa, mla_w_qb, mla_g_kva, mla_w_kvb, mla_g_qn, mla_g_qr, mla_g_kn,
                  mla_g_kr, mla_w_out, dsa_w_in, dsa_g_q, dsa_g_k, dsa_w_out, swa_w_in, swa_g_q,
                  swa_g_k, swa_sinks, swa_w_out, fox_w_in, fox_b_f, fox_g_q, fox_g_k, fox_w_out):
    bf = lambda a: a.astype(BF16)
    row = lambda a: a.astype(F32).reshape(1, -1)
    half = MLA_ROPE // 2
    c1, c2, c3 = MLA_Q_LORA, MLA_Q_LORA + MLA_KV_LORA, MLA_Q_LORA + MLA_KV_LORA + MLA_ROPE
    wkr = mla_w_in[:, c2:c3]
    wkrp = jnp.concatenate([wkr[:, half:], wkr[:, :half]], axis=1)
    gkr = mla_g_kr.astype(F32)
    gkrp = jnp.concatenate([gkr[half:], gkr[:half]])
    kvb = mla_w_kvb.reshape(MLA_KV_LORA, N_HEADS, MLA_NOPE + HEAD_DIM)
    mla = dict(
        wcq=bf(mla_w_in[:, :c1]), wckv=bf(mla_w_in[:, c1:c2]), wkr=bf(_pad_cols(wkr, 128)),
        wkrp=bf(_pad_cols(wkrp, 128)), wzt=bf(mla_w_in[:, c3:].T),
        gqa=row(mla_g_qa), gkva=row(mla_g_kva), gkr=row(jnp.pad(gkr, (0, 128 - MLA_ROPE))),
        gkrp=row(jnp.pad(gkrp, (0, 128 - MLA_ROPE))),
        wqt=bf(mla_w_qb.T), gqn=_col(mla_g_qn), gqr=_col(mla_g_qr),
        wkn=bf(kvb[:, :, :MLA_NOPE].reshape(MLA_KV_LORA, -1)),
        wvt=bf(kvb[:, :, MLA_NOPE:].reshape(MLA_KV_LORA, -1).T),
        gkn=_tile_gain(mla_g_kn, N_HEADS), bd=_block_diag(BD_LANES), wo=bf(mla_w_out))
    hq, hk = N_HEADS * HEAD_DIM, KV_HEADS * HEAD_DIM
    cuts = np.cumsum([hq, hk, hk, IDX_HEADS * IDX_DIM, IDX_DIM, IDX_HEADS]).tolist()
    dsa = dict(
        wqt=bf(dsa_w_in[:, :cuts[0]].T), wk=bf(dsa_w_in[:, cuts[0]:cuts[1]]),
        wv=bf(dsa_w_in[:, cuts[1]:cuts[2]]), wqit=bf(dsa_w_in[:, cuts[2]:cuts[3]].T),
        wki=bf(dsa_w_in[:, cuts[3]:cuts[4]]), wwit=bf(dsa_w_in[:, cuts[4]:cuts[5]].T),
        wzt=bf(dsa_w_in[:, cuts[5]:].T), gq=_col(dsa_g_q), gk=_tile_gain(dsa_g_k, KV_HEADS),
        bd=_block_diag(BD_LANES), wo=bf(dsa_w_out))
    swa = dict(
        wqt=bf(swa_w_in[:, :hq].T), wk=bf(swa_w_in[:, hq:hq + hk]), wv=bf(swa_w_in[:, hq + hk:hq + 2 * hk]),
        wzt=bf(swa_w_in[:, hq + 2 * hk:].T), gq=_col(swa_g_q), gk=_tile_gain(swa_g_k, KV_HEADS),
        bd=_block_diag(BD_LANES), wo=bf(swa_w_out), sinks=swa_sinks)
    fox = dict(
        wqt=bf(fox_w_in[:, :hq].T), wk=bf(fox_w_in[:, hq:2 * hq]), wv=bf(fox_w_in[:, 2 * hq:3 * hq]),
        wf=bf(_pad_cols(fox_w_in[:, 3 * hq:3 * hq + N_HEADS], 128)), wzt=bf(fox_w_in[:, 3 * hq + N_HEADS:].T),
        bf=row(jnp.pad(fox_b_f.astype(F32), (0, 128 - N_HEADS))), gq=_col(fox_g_q),
        gk=_tile_gain(fox_g_k, N_HEADS), bd=_block_diag(BD_LANES), wo=bf(fox_w_out))
    return [mla, dsa, swa, fox]


def kernel(x_prompt, x_sample, cache_mla_latent, cache_mla_krope, cache_dsa_k, cache_dsa_v, cache_dsa_kidx, state_swa_k, state_swa_v, cache_fox_k, cache_fox_v, cache_fox_logf, c_prompt, c_sample, norm_g, ada_w, ada_b, mla_w_in, mla_g_qa, mla_w_qb, mla_g_kva, mla_w_kvb, mla_g_qn, mla_g_qr, mla_g_kn, mla_g_kr, mla_w_out, dsa_w_in, dsa_g_q, dsa_g_k, dsa_w_out, swa_w_in, swa_g_q, swa_g_k, swa_sinks, swa_w_out, fox_w_in, fox_b_f, fox_g_q, fox_g_k, fox_w_out):
    bp, sp, _ = x_prompt.shape
    bs, ss, _ = x_sample.shape
    past_len = cache_mla_latent.shape[1]
    depth = norm_g.shape[0]
    assert bp == 1 and sp % TQ == 0 and sp % min(TQ_FLASH, sp) == 0 and sp % TS == 0 and (bs * ss) % 8 == 0 and ss <= TQ_DEC
    assert past_len % TK == 0 and past_len >= WINDOW

    weights = _prep_weights(mla_w_in, mla_g_qa, mla_w_qb, mla_g_kva, mla_w_kvb, mla_g_qn, mla_g_qr,
                            mla_g_kn, mla_g_kr, mla_w_out, dsa_w_in, dsa_g_q, dsa_g_k, dsa_w_out,
                            swa_w_in, swa_g_q, swa_g_k, swa_sinks, swa_w_out, fox_w_in, fox_b_f,
                            fox_g_q, fox_g_k, fox_w_out)
    rows = bp + bs
    rows_p = -(-rows // 8) * 8
    c_all = jnp.concatenate([c_prompt, c_sample, jnp.zeros((rows_p - rows, D_MODEL), F32)], axis=0)
    mod = _ada_mod(c_all, ada_w, ada_b)

    st_p = _Stream(bp, sp, 0)
    st_s = _Stream(bs, ss, past_len)
    pasts = ((cache_mla_latent, cache_mla_krope), (cache_dsa_k, cache_dsa_v, cache_dsa_kidx),
             (state_swa_k, state_swa_v), (cache_fox_k, cache_fox_v, cache_fox_logf))
    layers = (_mla_layer, _dsa_layer, _swa_layer, _fox_layer)
    xp = x_prompt.reshape(st_p.r, D_MODEL)
    xs = x_sample.reshape(st_s.r, D_MODEL)
    new_p, new_s = [], []
    for layer in range(depth):
        kind = layer % len(layers)
        g = norm_g[layer].astype(F32).reshape(1, -1)
        xp, n = layers[kind](st_p, xp, _mod_rows(st_p, mod[layer], 0), g, weights[kind], None)
        new_p.append(n)
        xs, n = layers[kind](st_s, xs, _mod_rows(st_s, mod[layer], bp), g, weights[kind], pasts[kind])
        new_s.append(n)

    def shaped(st, new):
        (lat, kr), (dk, dv, dki), (sk, sv), (fk, fv, flf) = new
        b, s = st.b, st.s
        return (lat.reshape(b, s, -1), kr.reshape(b, s, -1),
                dk.reshape(b, s, KV_HEADS, HEAD_DIM), dv.reshape(b, s, KV_HEADS, HEAD_DIM),
                dki.reshape(b, s, -1),
                sk.reshape(b, -1, KV_HEADS, HEAD_DIM), sv.reshape(b, -1, KV_HEADS, HEAD_DIM),
                fk.reshape(b, s, N_HEADS, HEAD_DIM), fv.reshape(b, s, N_HEADS, HEAD_DIM),
                flf.reshape(b, s, -1))

    return (xp.reshape(x_prompt.shape), xs.reshape(x_sample.shape)) + shaped(st_p, new_p) + shaped(st_s, new_s)
```

```python
import functools

import numpy as np
import jax
import jax.numpy as jnp
from jax import lax
from jax.experimental import pallas as pl
from jax.experimental.pallas import tpu as pltpu

F32 = jnp.float32
BF16 = jnp.bfloat16
I32 = jnp.int32

D_MODEL = 1024
HEAD_DIM = 64
N_HEADS = 16
KV_HEADS = 4
CHUNK = 64
CHUNK_SHIFT = 6
WINDOW = 128
WIN_CHUNKS = WINDOW // CHUNK
EPS = 1e-6
ROPE_BASE = 10000.0
MLA_NOPE, MLA_ROPE, MLA_Q_LORA, MLA_KV_LORA = 64, 32, 384, 256
IDX_HEADS, IDX_DIM, TOPK_MAX = 8, 64, 256
LOG2E = 1.4426950408889634
NEG = -1e30
INT_MIN = -(2 ** 31)
INT_MAX = 2 ** 31 - 1
I16 = jnp.int16
I16_MIN, I16_MAX = -(2 ** 15), 2 ** 15 - 1

TS = 512
TQ = 256
TQ_FLASH = 2048
TK = 512
TQ_DEC = 128
BD_LANES = 256
SUM_ROWS = 16
KT_LANES = 256
VMEM_LIMIT = 56 * 1024 * 1024
DSA_GROUPS = 2
DSA_VMEM_LIMIT = 62 * 1024 * 1024


def _row_tile(r):
    return TS if r % TS == 0 else r


def _cparams(sem, vmem=VMEM_LIMIT):
    return pltpu.CompilerParams(dimension_semantics=sem, vmem_limit_bytes=vmem)


def _dotf(a, b):
    return jnp.dot(a, b, preferred_element_type=F32)


def _dot_nt(a, b):
    return lax.dot_general(a, b, (((1,), (1,)), ((), ())), preferred_element_type=F32)


def _dot_tn(a, b):
    return lax.dot_general(a, b, (((0,), (0,)), ((), ())), preferred_element_type=F32)


def _split3(x):
    hi = x.astype(BF16)
    r = x - hi.astype(F32)
    mid = r.astype(BF16)
    lo = (r - mid.astype(F32)).astype(BF16)
    return hi, mid, lo


def _silu(x):
    return x / (1.0 + jnp.exp(-x))


def _full_spec(arr):
    nd = arr.ndim
    return pl.BlockSpec(arr.shape, lambda *_: (0,) * nd)


def _row_spec(arr, ts):
    if arr.shape[0] == 1:
        return pl.BlockSpec((1, arr.shape[1]), lambda i: (0, 0))
    return pl.BlockSpec((ts, arr.shape[1]), lambda i: (i, 0))


def _col_spec(arr, ts):
    return pl.BlockSpec((arr.shape[0], ts), lambda i: (0, i))


def _ada_body(c_ref, w_ref, b_ref, o_ref):
    a = _silu(c_ref[...])
    w = w_ref[0]
    a_hi = a.astype(BF16)
    a_lo = (a - a_hi.astype(F32)).astype(BF16)
    w_hi = w.astype(BF16)
    w_lo = (w - w_hi.astype(F32)).astype(BF16)
    o_ref[0] = _dotf(a_hi, w_hi) + _dotf(a_hi, w_lo) + _dotf(a_lo, w_hi) + b_ref[0]


def _ada_mod(c_all, ada_w, ada_b):
    depth, d, n3 = ada_w.shape
    bp = c_all.shape[0]
    tn = 768
    return pl.pallas_call(
        _ada_body,
        grid=(depth, n3 // tn),
        in_specs=[
            pl.BlockSpec((bp, d), lambda l, j: (0, 0)),
            pl.BlockSpec((1, d, tn), lambda l, j: (l, 0, j)),
            pl.BlockSpec((1, 1, tn), lambda l, j: (l, 0, j)),
        ],
        out_specs=pl.BlockSpec((1, bp, tn), lambda l, j: (l, 0, j)),
        out_shape=jax.ShapeDtypeStruct((depth, bp, n3), F32),
        compiler_params=_cparams(("arbitrary", "arbitrary")),
        name="ada_mod",
    )(c_all, ada_w, ada_b.reshape(depth, 1, n3))


def _prenorm(x_ref, g_ref, sc_ref, sh_ref):
    x = x_ref[...]
    ms = jnp.mean(x * x, axis=-1, keepdims=True)
    xn = x * lax.rsqrt(ms + EPS) * g_ref[...]
    return (xn * (1.0 + sc_ref[...]) + sh_ref[...]).astype(BF16)


def _group_sumsq(y, bd_ref):
    sq = y * y
    hi = sq.astype(BF16)
    lo = (sq - hi.astype(F32)).astype(BF16)
    bd = bd_ref[...]
    chunks = []
    for c in range(y.shape[1] // BD_LANES):
        sl = slice(c * BD_LANES, (c + 1) * BD_LANES)
        chunks.append(_dotf(hi[:, sl], bd) + _dotf(lo[:, sl], bd))
    return jnp.concatenate(chunks, axis=1) if len(chunks) > 1 else chunks[0]


def _head_rms_rows(q, g_col):
    ms = jnp.mean(q * q, axis=0, keepdims=True)
    return q * lax.rsqrt(ms + EPS) * g_col


def _place_rows(piece, row0, total):
    ts = piece.shape[1]
    parts = []
    if row0 > 0:
        parts.append(jnp.zeros((row0, ts), F32))
    parts.append(piece)
    rest = total - row0 - piece.shape[0]
    if rest > 0:
        parts.append(jnp.zeros((rest, ts), F32))
    return jnp.concatenate(parts, axis=0) if len(parts) > 1 else piece


def _proj_call(body, name, r, ts, row_in, col_in, const_in, outs):
    in_specs = ([_row_spec(a, ts) for a in row_in] + [_col_spec(a, ts) for a in col_in]
                + [_full_spec(a) for a in const_in])
    return pl.pallas_call(
        body,
        grid=(r // ts,),
        in_specs=in_specs,
        out_specs=[pl.BlockSpec(blk, im) for (_, _, blk, im) in outs],
        out_shape=[jax.ShapeDtypeStruct(s, dt) for (s, dt, _, _) in outs],
        compiler_params=_cparams(("arbitrary",)),
        name=name,
    )(*row_in, *col_in, *const_in)


def _rows_out(r, c, ts, dtype=F32):
    return ((r, c), dtype, (ts, c), lambda i: (i, 0))


def _cols_out(c, r, ts, dtype=F32):
    return ((c, r), dtype, (c, ts), lambda i: (0, i))


def _qp_out(r, ts):
    return ((N_HEADS, KT_LANES, r), BF16, (N_HEADS, KT_LANES, ts), lambda i: (0, 0, i))


def _mla_in_body(x_ref, sc_ref, sh_ref, cos_ref, sin_ref, g_ref, gqa, gkva, gkr, gkrp,
                 wcq, wckv, wkr, wkrp, wzt, cqn_o, lat_o, kr_o, zt_o):
    h = _prenorm(x_ref, g_ref, sc_ref, sh_ref)
    cq = _dotf(h, wcq[...])
    cqn_o[...] = (cq * lax.rsqrt(jnp.mean(cq * cq, axis=-1, keepdims=True) + EPS) * gqa[...]).astype(BF16)
    ckv = _dotf(h, wckv[...])
    lat_o[...] = ckv * lax.rsqrt(jnp.mean(ckv * ckv, axis=-1, keepdims=True) + EPS) * gkva[...]
    kr = _dotf(h, wkr[...])
    krp = _dotf(h, wkrp[...])
    inv = lax.rsqrt(jnp.sum(kr * kr, axis=-1, keepdims=True) * (1.0 / MLA_ROPE) + EPS)
    kr_o[...] = (kr * gkr[...] * cos_ref[...] + krp * gkrp[...] * sin_ref[...]) * inv
    zt_o[...] = _dot_nt(wzt[...], h)


def _mla_q_body(cqn_ref, cos_ref, sin_ref, wqt, gqn, gqr, qp_o):
    qt = _dot_nt(wqt[...], cqn_ref[...])
    ts = qt.shape[1]
    sc = (MLA_NOPE + MLA_ROPE) ** -0.5 * LOG2E
    c = cos_ref[...]
    s = sin_ref[...]
    half = MLA_ROPE // 2
    width = MLA_NOPE + MLA_ROPE
    for h in range(N_HEADS):
        qn = _head_rms_rows(qt[width * h:width * h + MLA_NOPE], gqn[...]) * sc
        qr = _head_rms_rows(qt[width * h + MLA_NOPE:width * (h + 1)], gqr[...]) * sc
        x1, x2 = qr[:half], qr[half:]
        o1 = x1 * c - x2 * s
        o2 = x2 * c + x1 * s
        a = h % 2
        pieces = [qn, jnp.zeros((HEAD_DIM, ts), F32)]
        if a:
            pieces = pieces[::-1]
        pieces += [o1, o2, jnp.zeros((KT_LANES - 2 * HEAD_DIM - MLA_ROPE, ts), F32)]
        qp_o[h] = jnp.concatenate(pieces, axis=0).astype(BF16)


def _mla_kv_body(lat_ref, kr_ref, gkn, wkn, wvt, bd, kt_o, vt_o):
    lat = lat_ref[...].astype(BF16)
    kn = _dotf(lat, wkn[...])
    ss = _group_sumsq(kn, bd)
    kn = kn * lax.rsqrt(ss * (1.0 / MLA_NOPE) + EPS) * gkn[...]
    kr = kr_ref[...].astype(BF16)
    for p in range(N_HEADS // 2):
        kt_o[p, :, 0:128] = kn[:, 128 * p:128 * (p + 1)].astype(BF16)
        kt_o[p, :, 128:256] = kr
    vt_o[0] = _dot_nt(wvt[...], lat).astype(BF16)


def _q_heads_to_qp(qt, gq, qp_o, row_of_head, extra_of_head=None):
    sc = HEAD_DIM ** -0.5 * LOG2E
    ts = qt.shape[1]
    for h in range(N_HEADS):
        qh = _head_rms_rows(qt[HEAD_DIM * h:HEAD_DIM * (h + 1)], gq[...]) * sc
        blk = _place_rows(qh, row_of_head(h), KT_LANES)
        if extra_of_head is not None:
            lo, hi = extra_of_head(h)
            rows = lax.broadcasted_iota(I32, (KT_LANES, ts), 0)
            blk = blk + jnp.where((rows >= lo) & (rows < hi), 1.0, 0.0)
        qp_o[h] = blk.astype(BF16)


def _dsa_in_body(x_ref, sc_ref, sh_ref, g_ref, gk, gq, wqt, wk, wv, wqit, wki, wwit, wzt, bd,
                 qp_o, k_o, v_o, qi_o, ki_o, wi_o, zt_o, *key_side):
    h = _prenorm(x_ref, g_ref, sc_ref, sh_ref)
    _q_heads_to_qp(_dot_nt(wqt[...], h), gq, qp_o, lambda hh: HEAD_DIM * (hh // (N_HEADS // KV_HEADS)))
    k = _dotf(h, wk[...])
    k = k * lax.rsqrt(_group_sumsq(k, bd) * (1.0 / HEAD_DIM) + EPS) * gk[...]
    v = _dotf(h, wv[...])
    ki = _dotf(h, wki[...])
    k_o[...] = k
    v_o[...] = v
    ki_o[...] = ki
    qit = _dot_nt(wqit[...], h)
    for ih in range(IDX_HEADS):
        qi_o[ih] = qit[IDX_DIM * ih:IDX_DIM * (ih + 1)].astype(BF16)
    wi_o[...] = _dot_nt(wwit[...], h)
    zt_o[...] = _dot_nt(wzt[...], h)
    if key_side:
        kt_o, vt_o, kib_o = key_side
        kt_o[...] = k.astype(BF16)
        vt_o[0] = v.T.astype(BF16)
        kib_o[...] = ki.astype(BF16)


def _swa_in_body(x_ref, sc_ref, sh_ref, g_ref, gk, gq, wqt, wk, wv, wzt, bd, qp_o, k_o, v_o, zt_o):
    h = _prenorm(x_ref, g_ref, sc_ref, sh_ref)
    _q_heads_to_qp(_dot_nt(wqt[...], h), gq, qp_o, lambda hh: HEAD_DIM * (hh // (N_HEADS // KV_HEADS)))
    k = _dotf(h, wk[...])
    k_o[...] = k * lax.rsqrt(_group_sumsq(k, bd) * (1.0 / HEAD_DIM) + EPS) * gk[...]
    v_o[...] = _dotf(h, wv[...])
    zt_o[...] = _dot_nt(wzt[...], h)


def _fox_in_body(x_ref, sc_ref, sh_ref, g_ref, gk, bf, gq, wqt, wk, wv, wf, wzt, bd,
                 qp_o, k_o, v_o, lf_o, zt_o):
    h = _prenorm(x_ref, g_ref, sc_ref, sh_ref)
    _q_heads_to_qp(_dot_nt(wqt[...], h), gq, qp_o, lambda hh: HEAD_DIM * (hh % 2),
                   lambda hh: (128 + 3 * (hh % 2), 128 + 3 * (hh % 2) + 3))
    k = _dotf(h, wk[...])
    k_o[...] = k * lax.rsqrt(_group_sumsq(k, bd) * (1.0 / HEAD_DIM) + EPS) * gk[...]
    v_o[...] = _dotf(h, wv[...])
    f = _dotf(h, wf[...]) + bf[...]
    lf_o[...] = jnp.minimum(f, 0.0) - jnp.log1p(jnp.exp(-jnp.abs(f)))
    zt_o[...] = _dot_nt(wzt[...], h)


def _out_body(x_ref, gate_ref, ot_ref, zt_ref, wo, o_ref):
    u = (ot_ref[...] * _silu(zt_ref[...])).astype(BF16)
    o_ref[...] = x_ref[...] + gate_ref[...] * _dot_tn(u, wo[...])


def _out_proj(x, gate, ot, zt, wo):
    r = x.shape[0]
    ts = _row_tile(r)
    (out,) = _proj_call(_out_body, "out_proj", r, ts, [x, gate], [ot, zt], [wo],
                        [_rows_out(r, D_MODEL, ts)])
    return out


def _src_arrays(src):
    past, new = src
    return [new] if past is None else [past, new]


def _src_specs(src):
    past, new = src
    c = new.shape[2]
    if past is None:
        return [pl.BlockSpec((1, TK, c), lambda bb, j: (bb, j, 0))]
    npb = past.shape[1] // TK
    return [pl.BlockSpec((1, TK, c), lambda bb, j: (bb, jnp.minimum(j, npb - 1), 0)),
            pl.BlockSpec((1, TK, c), lambda bb, j: (bb, jnp.maximum(j - npb, 0), 0))]


def _src_rows(src):
    past, new = src
    return new.shape[1] + (0 if past is None else past.shape[1])


def _src_load(refs, src_past_blocks):
    if src_past_blocks is None:
        return refs[0][0]
    return jnp.where(pl.program_id(1) < src_past_blocks, refs[0][0], refs[1][0])


def _split_refs(refs, past_blocks):
    vals, pos = [], 0
    for npb in past_blocks:
        cnt = 1 if npb is None else 2
        vals.append(_src_load(refs[pos:pos + cnt], npb))
        pos += cnt
    return vals, refs[pos:]


def _past_blocks(srcs):
    return tuple(None if s[0] is None else s[0].shape[1] // TK for s in srcs)


def _gqa_prep_body(*refs, past_blocks):
    vals, outs = _split_refs(refs, past_blocks)
    outs[0][0] = vals[0].astype(BF16)
    outs[1][0, 0] = vals[1].T.astype(BF16)
    if len(vals) > 2:
        outs[2][0] = vals[2].astype(BF16)


def _gqa_prep(k, v, ki=None):
    srcs = [k, v] + ([ki] if ki is not None else [])
    b, c = k[1].shape[0], k[1].shape[2]
    lp = _src_rows(k)
    nkb = lp // TK
    out_specs = [pl.BlockSpec((1, TK, c), lambda bb, j: (bb, j, 0)),
                 pl.BlockSpec((1, 1, c, TK), lambda bb, j: (bb, j, 0, 0))]
    out_shape = [jax.ShapeDtypeStruct((b, lp, c), BF16), jax.ShapeDtypeStruct((b, nkb, c, TK), BF16)]
    if ki is not None:
        ci = ki[1].shape[2]
        out_specs.append(pl.BlockSpec((1, TK, ci), lambda bb, j: (bb, j, 0)))
        out_shape.append(jax.ShapeDtypeStruct((b, lp, ci), BF16))
    return pl.pallas_call(
        functools.partial(_gqa_prep_body, past_blocks=_past_blocks(srcs)),
        grid=(b, nkb), in_specs=[sp for s in srcs for sp in _src_specs(s)],
        out_specs=out_specs, out_shape=out_shape,
        compiler_params=_cparams(("arbitrary", "arbitrary")), name="gqa_prep",
    )(*[a for s in srcs for a in _src_arrays(s)])


def _fox_prep_body(*refs, past_blocks):
    (k, v, lf), (sel_ref, kt_o, vt_o, carry_ref) = _split_refs(refs, past_blocks)
    kb = pl.program_id(1)

    @pl.when(kb == 0)
    def _():
        carry_ref[...] = jnp.zeros_like(carry_ref)

    r = lax.broadcasted_iota(I32, (TK, TK), 0)
    c = lax.broadcasted_iota(I32, (TK, TK), 1)
    tri = jnp.where(c <= r, 1.0, 0.0).astype(BF16)
    cum3 = _dotf(tri, jnp.concatenate(_split3(lf), axis=1))
    cum = cum3[:, 0:128] + cum3[:, 128:256] + cum3[:, 256:384] + carry_ref[...]
    carry_ref[...] = cum[TK - 1:TK, :]
    ex = _dotf(jnp.concatenate(_split3(-(cum * LOG2E)), axis=1), sel_ref[...])
    for p in range(N_HEADS // 2):
        kt_o[p, :, 0:128] = k[:, 128 * p:128 * (p + 1)].astype(BF16)
        kt_o[p, :, 128:256] = ex[:, 128 * p:128 * (p + 1)].astype(BF16)
    vt_o[0, 0] = v.T.astype(BF16)


def _fox_sel():
    sel = np.zeros((3, 128, N_HEADS * HEAD_DIM), np.float32)
    for h in range(N_HEADS):
        for j in range(3):
            sel[j, h, 128 * (h // 2) + 3 * (h % 2) + j] = 1.0
    return jnp.asarray(sel.reshape(3 * 128, N_HEADS * HEAD_DIM), BF16)


def _fox_prep(k, v, lf128):
    srcs = [k, v, lf128]
    b, c = k[1].shape[0], k[1].shape[2]
    lp = _src_rows(k)
    nkb = lp // TK
    npair = N_HEADS // 2
    sel = _fox_sel()
    return pl.pallas_call(
        functools.partial(_fox_prep_body, past_blocks=_past_blocks(srcs)),
        grid=(b, nkb),
        in_specs=[sp for s in srcs for sp in _src_specs(s)] + [_full_spec(sel)],
        out_specs=[pl.BlockSpec((npair, TK, KT_LANES), lambda bb, j: (bb, j, 0)),
                   pl.BlockSpec((1, 1, c, TK), lambda bb, j: (bb, j, 0, 0))],
        out_shape=[jax.ShapeDtypeStruct((b * npair, lp, KT_LANES), BF16),
                   jax.ShapeDtypeStruct((b, nkb, c, TK), BF16)],
        scratch_shapes=[pltpu.VMEM((1, 128), F32)],
        compiler_params=_cparams(("arbitrary", "arbitrary")), name="fox_prep",
    )(*[a for s in srcs for a in _src_arrays(s)], sel)


def _visible_end(qpos, mode, n_keys):
    if mode == "causal":
        end = qpos + 1
    else:
        end = ((qpos >> CHUNK_SHIFT) + 1) << CHUNK_SHIFT
    return jnp.minimum(end, n_keys)


def _softmax_update(s, mx8, m_ref, c8=None, lane0=0):
    m8 = m_ref[:, lane0:]
    m_new8 = jnp.maximum(m8, mx8)
    alpha8 = jnp.exp2(m8 - m_new8)
    shift = m_new8[0:1] if c8 is None else (m_new8 - c8)[0:1]
    m_ref[:, lane0:] = m_new8
    return jnp.exp2(s - shift).astype(BF16), alpha8[0:1]


def _pv_and_sum(vt_blk, pb):
    ones = jnp.ones((SUM_ROWS, vt_blk.shape[1]), BF16)
    return _dotf(jnp.concatenate([vt_blk, ones], axis=0), pb)


def _run_two_stage(n_full, n_masked, stage_a, stage_b, trim=lambda j: 0):
    odd = n_full % 2

    @pl.when(odd == 1)
    def _():
        stage_a(0, 0, False, 0)
        stage_b(0, 0, False, 0)

    n_pairs = (n_full - odd) // 2

    @pl.when(n_pairs >= 1)
    def _():
        stage_a(odd, 0, False, 0)

        def body(u, _):
            kb = odd + 2 * u
            stage_b(kb, 0, False, 0)
            stage_a(kb + 1, 1, False, 0)
            stage_b(kb + 1, 1, False, 0)
            stage_a(kb + 2, 0, False, 0)
            return 0

        lax.fori_loop(0, n_pairs - 1, body, 0)
        stage_b(n_full - 2, 0, False, 0)
        stage_a(n_full - 1, 1, False, 0)
        stage_b(n_full - 1, 1, False, 0)
        stage_a(n_full, 0, True, trim(0))

    @pl.when(n_pairs < 1)
    def _():
        stage_a(n_full, 0, True, trim(0))

    for j in range(1, n_masked):
        stage_b(n_full + j - 1, (j - 1) % 2, True, trim(j - 1))
        stage_a(n_full + j, j % 2, True, trim(j))
    stage_b(n_full + n_masked - 1, (n_masked - 1) % 2, True, trim(n_masked - 1))


def _flash_body(qp_ref, kt_ref, vt_ref, o_ref, s_buf, mx_buf, m_ref, acc_ref, *,
                tq, mode, q0, n_keys, paired, n_masked):
    i = pl.program_id(1)
    q_first = q0 + i * tq
    n_full = _visible_end(q_first, mode, n_keys) // TK
    ns = N_HEADS // 2 if paired else 1
    per = 2 if paired else 1
    ws, rows = per * tq, per * HEAD_DIM
    w = ns * ws
    ik = lax.broadcasted_iota(I32, (TK, ws), 0)
    iq = lax.broadcasted_iota(I32, (TK, tq), 1)
    qpos = q_first + (jnp.concatenate([iq] * per, axis=1) if paired else iq)
    qs = [jnp.concatenate([qp_ref[per * si + a] for a in range(per)], axis=1) if paired else qp_ref[si]
          for si in range(ns)]

    m_ref[...] = jnp.full((8, w), NEG, F32)
    acc_ref[...] = jnp.zeros((rows + SUM_ROWS, w), F32)

    def stage_a(kb, slot, masked, lane0):
        assert lane0 == 0 or ns == 1
        k0 = pl.multiple_of(kb * TK, TK)
        wl = ws - lane0
        if masked:
            if lane0:
                kpos = k0 + lax.broadcasted_iota(I32, (TK, wl), 0)
                qp = q_first + lane0 + lax.broadcasted_iota(I32, (TK, wl), 1)
            else:
                kpos, qp = k0 + ik, qpos
            valid = (kpos <= qp) if mode == "causal" else ((kpos >> CHUNK_SHIFT) <= (qp >> CHUNK_SHIFT))
            valid = valid & (kpos < n_keys)
        for si in range(ns):
            s = _dotf(kt_ref[si, pl.ds(k0, TK), :], qs[si][:, lane0:])
            if masked:
                s = jnp.where(valid, s, NEG)
            s_buf[slot, :, si * ws + lane0:(si + 1) * ws] = s
            mx_buf[slot, :, si * ws + lane0:(si + 1) * ws] = jnp.broadcast_to(
                jnp.max(s, axis=0, keepdims=True), (8, wl))

    def stage_b(kb, slot, masked, lane0):
        del masked
        pb, alpha = _softmax_update(s_buf[slot, :, lane0:], mx_buf[slot, :, lane0:], m_ref, lane0=lane0)
        for si in range(ns):
            loc = slice(si * ws, (si + 1) * ws - lane0)
            glob = slice(si * ws + lane0, (si + 1) * ws)
            acc_ref[:, glob] = alpha[:, loc] * acc_ref[:, glob] + _pv_and_sum(
                vt_ref[0, kb, rows * si:rows * (si + 1), :], pb[:, loc])

    trim = (lambda j: j * TK) if (not paired and tq > TK) else (lambda j: 0)
    _run_two_stage(n_full, n_masked, stage_a, stage_b, trim)
    out = acc_ref[0:rows, :] / acc_ref[rows:rows + 1, :]
    for si in range(ns):
        for a in range(per):
            o_ref[per * si + a] = out[HEAD_DIM * a:HEAD_DIM * (a + 1), si * ws + tq * a:si * ws + tq * (a + 1)]


def _visible_end_static(qpos, mode, n_keys):
    end = qpos + 1 if mode == "causal" else ((qpos >> CHUNK_SHIFT) + 1) << CHUNK_SHIFT
    return min(end, n_keys)


def _flash(qp, kt, vt, *, tq, mode, q0, n_real_q, n_keys, paired):
    bh, _, sq = qp.shape
    lp = kt.shape[1]
    nkb = vt.shape[1]
    assert (q0 % TK == 0) and (tq % TK == 0 or sq == tq)
    n_masked = (-(-_visible_end_static(q0 + n_real_q - 1, mode, n_keys) // TK)
                - _visible_end_static(q0, mode, n_keys) // TK)
    body = functools.partial(_flash_body, tq=tq, mode=mode, q0=q0, n_keys=n_keys, paired=paired,
                             n_masked=n_masked)
    if not paired:
        ns, w, rows = 1, tq, HEAD_DIM
        kt_spec = pl.BlockSpec((1, lp, KT_LANES), lambda g, i: (g // 2, 0, 0))
        vt_spec = pl.BlockSpec((1, nkb, HEAD_DIM, TK), lambda g, i: (g // N_HEADS, 0, g % N_HEADS, 0))
    else:
        ns, w, rows = N_HEADS, N_HEADS * tq, 2 * HEAD_DIM
        kt_spec = pl.BlockSpec((ns // 2, lp, KT_LANES), lambda g, i: (g, 0, 0))
        vt_spec = pl.BlockSpec((1, nkb, ns * HEAD_DIM, TK), lambda g, i: (g, 0, 0, 0))
    return pl.pallas_call(
        body,
        grid=(bh // ns, sq // tq),
        in_specs=[pl.BlockSpec((ns, KT_LANES, tq), lambda g, i: (g, 0, i)), kt_spec, vt_spec],
        out_specs=pl.BlockSpec((ns, HEAD_DIM, tq), lambda g, i: (g, 0, i)),
        out_shape=jax.ShapeDtypeStruct((bh, HEAD_DIM, sq), F32),
        scratch_shapes=[pltpu.VMEM((2, TK, w), F32), pltpu.VMEM((2, 8, w), F32), pltpu.VMEM((8, w), F32),
                        pltpu.VMEM((rows + SUM_ROWS, w), F32)],
        compiler_params=_cparams(("arbitrary", "arbitrary")),
        name="flash_" + mode,
    )(qp, kt, vt)


def _swa_body(hp_ref, qp_ref, *refs, tq, q0, n_pieces, n_rows):
    k_refs, v_refs, o_ref = refs[:n_pieces], refs[n_pieces:2 * n_pieces], refs[2 * n_pieces]
    i = pl.program_id(1)
    q_first = q0 + i * tq
    k = jnp.concatenate([r[0].astype(BF16) for r in k_refs], axis=0)
    vt = jnp.concatenate([r[0] for r in v_refs], axis=0).T.astype(BF16)
    kw = k.shape[0]
    row = lax.broadcasted_iota(I32, (kw, tq), 0)
    kpos = (q_first - WINDOW) + row
    qpos = q_first + lax.broadcasted_iota(I32, (kw, tq), 1)
    qc = qpos >> CHUNK_SHIFT
    kc = kpos >> CHUNK_SHIFT
    valid = (kpos >= 0) & (kc <= qc) & (qc - kc <= WIN_CHUNKS) & (row < n_rows)
    dist = jnp.abs(qpos - kpos).astype(F32)
    group = N_HEADS // KV_HEADS

    def lanes(x):
        return jnp.concatenate([x] * group, axis=1)

    for n in range(KV_HEADS):
        h0 = n * group

        def per_head(col):
            return jnp.concatenate(
                [jnp.broadcast_to(hp_ref[h0 + g][:, col:col + 1], (1, tq)) for g in range(group)], axis=1)

        q = jnp.concatenate([qp_ref[h0 + g] for g in range(group)], axis=1)
        slope2, sink2 = per_head(0), per_head(1)
        s = _dotf(k, q) - slope2 * lanes(dist)
        s = jnp.where(lanes(valid), s, NEG)
        m = jnp.maximum(jnp.max(s, axis=0, keepdims=True), sink2)
        acc = _pv_and_sum(vt[HEAD_DIM * n:HEAD_DIM * (n + 1)], jnp.exp2(s - m).astype(BF16))
        out = acc[0:HEAD_DIM] / (acc[HEAD_DIM:HEAD_DIM + 1] + jnp.exp2(sink2 - m))
        for g in range(group):
            o_ref[h0 + g] = out[:, g * tq:(g + 1) * tq]


def _swa_attend(hp, qp, k_pieces, v_pieces, *, tq, q0, n_rows):
    b = k_pieces[0][0].shape[0]
    sq = qp.shape[2]
    n_pieces = len(k_pieces)

    def spec(piece):
        _, rows, idx = piece
        return pl.BlockSpec((1, rows, KV_HEADS * HEAD_DIM), lambda bb, i: (bb, idx(i), 0))

    body = functools.partial(_swa_body, tq=tq, q0=q0, n_pieces=n_pieces, n_rows=n_rows)
    return pl.pallas_call(
        body,
        grid=(b, sq // tq),
        in_specs=[_full_spec(hp), pl.BlockSpec((N_HEADS, KT_LANES, tq), lambda bb, i: (bb, 0, i))]
        + [spec(p) for p in k_pieces] + [spec(p) for p in v_pieces],
        out_specs=pl.BlockSpec((N_HEADS, HEAD_DIM, tq), lambda bb, i: (bb, 0, i)),
        out_shape=jax.ShapeDtypeStruct((b * N_HEADS, HEAD_DIM, sq), F32),
        compiler_params=_cparams(("arbitrary", "arbitrary")),
        name="swa_attend",
    )(hp, qp, *[p[0] for p in k_pieces], *[p[0] for p in v_pieces])


def _dsa_body(hp_ref, qi_ref, wi_ref, qp_ref, ki_ref, kt_ref, vt_ref, o_ref, hi_ref, lo_ref,
              s_buf, mx_buf, m_ref, acc_ref, *, tq, q0, n_real_q, n_keys, topk, q_period):
    packed = q_period < tq
    i = pl.program_id(1)
    q_first = q0 + i * tq
    q_last = q_first + (n_real_q - 1)
    n_tot = (_visible_end(q_last, "chunk", n_keys) + (TK - 1)) // TK
    n_past = jnp.minimum(q_first, n_keys) // TK
    ik = lax.broadcasted_iota(I32, (TK, tq), 0)
    iq = lax.broadcasted_iota(I32, (TK, tq), 1)
    if packed:
        iq = iq & (q_period - 1)
    qpos = q_first + iq
    tf = float(topk)

    def blk(kb):
        return pl.ds(pl.multiple_of(kb * TK, TK), TK)

    def to_key(x):
        bits = pltpu.bitcast(x, I32)
        return jnp.where(bits < 0, bits ^ INT_MAX, bits)

    def score_blk(kb, diag):
        ki = ki_ref[0, blk(kb), :]
        acc = jnp.zeros((TK, tq), F32)
        for h in range(IDX_HEADS):
            acc = acc + wi_ref[0, h:h + 1, :] * jnp.maximum(_dotf(ki, qi_ref[0, h]), 0.0)
        key = to_key(acc)
        if diag:
            kpos = kb * TK + ik
            valid = ((kpos >> CHUNK_SHIFT) <= (qpos >> CHUNK_SHIFT)) & (kpos < n_keys)
            key = jnp.where(valid, key, INT_MIN)
        hi_ref[blk(kb), :] = (key >> 16).astype(I16)
        lo_ref[blk(kb), :] = ((key & 0xFFFF) - 32768).astype(I16)
        return 0

    lax.fori_loop(0, n_past, lambda kb, c: score_blk(kb, False), 0)
    lax.fori_loop(n_past, n_tot, lambda kb, c: score_blk(kb, True), 0)

    def count_ge(ref, mid):
        mid16 = mid.astype(I16)

        def body(kb, acc):
            ge = jnp.where(ref[blk(kb), :] >= mid16, jnp.int16(1), jnp.int16(0))
            parts = [ge[16 * r:16 * (r + 1)] for r in range(TK // 16)]
            while len(parts) > 1:
                parts = [parts[j] + parts[j + 1] for j in range(0, len(parts), 2)]
            return acc + parts[0]

        acc = lax.fori_loop(0, n_tot, body, jnp.zeros((16, tq), I16))
        return jnp.sum(acc.astype(I32), axis=0, keepdims=True).astype(F32)

    def bisect_step(ref, target, st):
        lo, hi, cl, ch = st
        mid = (lo + hi) >> 1
        cnt = count_ge(ref, mid)
        ge = cnt >= target
        return jnp.where(ge, mid, lo), jnp.where(ge, hi, mid), jnp.where(ge, cnt, cl), jnp.where(ge, ch, cnt)

    qrow = q_first + iq[0:1]
    n_vis = _visible_end(qrow, "chunk", n_keys).astype(F32)
    zero = jnp.zeros((1, tq), F32)
    st1 = (jnp.full((1, tq), I16_MIN + 1, I32), jnp.full((1, tq), I16_MAX + 1, I32), n_vis, zero)
    h_thr, _, cl1, ch1 = lax.fori_loop(0, 16, lambda _, st: bisect_step(hi_ref, tf, st), st1)
    h16 = h_thr.astype(I16)

    def mask_lo(kb, _):
        lo_ref[blk(kb), :] = jnp.where(hi_ref[blk(kb), :] == h16, lo_ref[blk(kb), :], jnp.int16(I16_MIN))
        return 0

    lax.fori_loop(0, n_tot, mask_lo, 0)
    t2 = tf - ch1

    def cond(st):
        return (st[0] < 16) & (st[2] > 0.5)

    def body(st):
        lo, hi, cl, ch = bisect_step(lo_ref, t2, st[1])
        done = (cl <= t2) | (hi - lo <= 1)
        return st[0] + 1, (lo, hi, cl, ch), jnp.sum(jnp.where(done, 0.0, 1.0))

    cl2_0 = cl1 - ch1
    st2 = (jnp.full((1, tq), I16_MIN, I32), jnp.full((1, tq), I16_MAX + 1, I32), cl2_0, zero)
    _, (l_thr, _, cl2, ch2), _ = lax.while_loop(
        cond, body, (jnp.int32(0), st2, jnp.sum(jnp.where(cl2_0 > t2, 1.0, 0.0))))
    l16 = l_thr.astype(I16)

    need = t2 - ch2

    @pl.when(jnp.sum(jnp.where(cl2 > t2, 1.0, 0.0)) > 0.5)
    def _():
        r = lax.broadcasted_iota(I32, (TK, TK), 0)
        c = lax.broadcasted_iota(I32, (TK, TK), 1)
        tri = jnp.where(c < r, 1.0, 0.0).astype(BF16)

        def fix(kb, carry):
            hb = hi_ref[blk(kb), :]
            e16 = jnp.where(hb == h16, jnp.where(lo_ref[blk(kb), :] == l16, jnp.int16(1), jnp.int16(0)),
                            jnp.int16(0))
            e = e16.astype(I32).astype(F32)
            before = _dotf(tri, e.astype(BF16)) + carry
            drop = jnp.where((e > 0.5) & (before >= need), 1, 0).astype(I16)
            hi_ref[blk(kb), :] = jnp.where(drop == jnp.int16(1), jnp.int16(I16_MIN), hb)
            return carry + jnp.sum(e, axis=0, keepdims=True)

        lax.fori_loop(0, n_tot, fix, jnp.zeros((1, tq), F32))

    def to_bias(kb, _):
        hb = hi_ref[blk(kb), :]
        zero_b, neg_b = jnp.bfloat16(0.0), jnp.bfloat16(NEG)
        at_thr = jnp.where(lo_ref[blk(kb), :] >= l16, zero_b, neg_b)
        bias = jnp.where(hb > h16, zero_b, jnp.where(hb == h16, at_thr, neg_b))
        hi_ref[blk(kb), :] = pltpu.bitcast(bias, I16)
        return 0

    lax.fori_loop(0, n_tot, to_bias, 0)

    group = N_HEADS // KV_HEADS
    heads = DSA_GROUPS * group
    wide = tq if packed else heads * tq
    gw = wide // DSA_GROUPS
    vr = DSA_GROUPS * HEAD_DIM if packed else HEAD_DIM

    def lanes(x):
        return x if packed else jnp.concatenate([x] * heads, axis=1)

    def group_body(n, _):
        h0 = n * heads
        if packed:
            q, slope2 = qp_ref[n], hp_ref[n]
        else:
            q = jnp.concatenate([qp_ref[h0 + g] for g in range(heads)], axis=1)
            slope2 = jnp.concatenate(
                [jnp.broadcast_to(hp_ref[h0 + g][:, 0:1], (1, tq)) for g in range(heads)], axis=1)
        slope8 = jnp.broadcast_to(slope2, (8, wide))
        a_tab = slope2 * lanes(ik.astype(F32))
        vrows = [pl.ds(pl.multiple_of((n * DSA_GROUPS + gi) * HEAD_DIM, HEAD_DIM), HEAD_DIM)
                 for gi in range(DSA_GROUPS)]
        m_ref[...] = jnp.full((8, wide), NEG, F32)
        acc_ref[...] = jnp.zeros((vr + SUM_ROWS, wide), F32)

        def c8(kb):
            return slope8 * (kb * TK - q_first).astype(F32)

        def stage_a(kb, slot, diag, lane0):
            del lane0
            bias = pltpu.bitcast(hi_ref[blk(kb), :], jnp.bfloat16).astype(F32)
            s = _dotf(kt_ref[0, blk(kb), :], q) + lanes(bias)
            if diag:
                kpos = kb * TK + ik
                rel = iq.astype(F32) - jnp.abs(qpos - kpos).astype(F32)
                s = s + slope2 * lanes(rel)
            else:
                s = s + a_tab
            mx8 = jnp.broadcast_to(jnp.max(s, axis=0, keepdims=True), (8, wide))
            s_buf[slot] = s
            mx_buf[slot] = mx8 if diag else mx8 + c8(kb)

        def stage_b(kb, slot, diag, lane0):
            del lane0
            pb, alpha = _softmax_update(s_buf[slot], mx_buf[slot], m_ref, None if diag else c8(kb))
            if packed:
                vt2 = jnp.concatenate([vt_ref[0, kb, vrows[gi], :] for gi in range(DSA_GROUPS)], axis=0)
                acc_ref[...] = alpha * acc_ref[...] + _pv_and_sum(vt2, pb)
            else:
                for gi in range(DSA_GROUPS):
                    sl = slice(gi * gw, (gi + 1) * gw)
                    acc_ref[:, sl] = alpha[:, sl] * acc_ref[:, sl] + _pv_and_sum(
                        vt_ref[0, kb, vrows[gi], :], pb[:, sl])

        _run_two_stage(n_past, 1, stage_a, stage_b)
        out = acc_ref[0:vr, :] / acc_ref[vr:vr + 1, :]
        if packed:
            lane = lax.broadcasted_iota(I32, (HEAD_DIM, wide), 1)
            sel = out[0:HEAD_DIM]
            for gi in range(1, DSA_GROUPS):
                sel = jnp.where(lane >= gi * gw, out[gi * HEAD_DIM:(gi + 1) * HEAD_DIM], sel)
            o_ref[n] = sel
        else:
            for g in range(heads):
                o_ref[h0 + g] = out[:, g * tq:(g + 1) * tq]
        return 0

    lax.fori_loop(0, KV_HEADS // DSA_GROUPS, group_body, 0)


def _dsa_attend(hp, qi, wi, qp, ki, kt, vt, *, tq, q0, n_real_q, n_keys, q_period):
    b = ki.shape[0]
    sq = qp.shape[2]
    lp = kt.shape[1]
    nkb = vt.shape[1]
    topk = min(TOPK_MAX, n_keys // 4)
    heads = DSA_GROUPS * (N_HEADS // KV_HEADS)
    steps = KV_HEADS // DSA_GROUPS
    packed = q_period < tq
    assert TK % tq == 0 and q0 % TK == 0 and n_real_q <= q_period
    assert (not packed) or (heads * q_period == tq and sq == tq and q_period & (q_period - 1) == 0)
    wide = tq if packed else heads * tq
    vr = DSA_GROUPS * HEAD_DIM if packed else HEAD_DIM
    nq = steps if packed else N_HEADS
    body = functools.partial(_dsa_body, tq=tq, q0=q0, n_real_q=n_real_q, n_keys=n_keys, topk=topk,
                             q_period=q_period)
    once = pl.Buffered(1)
    return pl.pallas_call(
        body,
        grid=(b, sq // tq),
        in_specs=[_full_spec(hp),
                  pl.BlockSpec((1, IDX_HEADS, IDX_DIM, tq), lambda bb, i: (bb, 0, 0, i)),
                  pl.BlockSpec((1, IDX_HEADS, tq), lambda bb, i: (bb, 0, i)),
                  pl.BlockSpec((nq, KT_LANES, tq), lambda bb, i: (bb, 0, i)),
                  pl.BlockSpec((1, lp, IDX_DIM), lambda bb, i: (bb, 0, 0), pipeline_mode=once),
                  pl.BlockSpec((1, lp, KT_LANES), lambda bb, i: (bb, 0, 0), pipeline_mode=once),
                  pl.BlockSpec((1, nkb, KV_HEADS * HEAD_DIM, TK), lambda bb, i: (bb, 0, 0, 0),
                               pipeline_mode=once)],
        out_specs=pl.BlockSpec((nq, HEAD_DIM, tq), lambda bb, i: (bb, 0, i)),
        out_shape=jax.ShapeDtypeStruct((b * nq, HEAD_DIM, sq), F32),
        scratch_shapes=[pltpu.VMEM((lp, tq), I16), pltpu.VMEM((lp, tq), I16),
                        pltpu.VMEM((2, TK, wide), F32), pltpu.VMEM((2, 8, wide), F32),
                        pltpu.VMEM((8, wide), F32), pltpu.VMEM((vr + SUM_ROWS, wide), F32)],
        compiler_params=_cparams(("arbitrary", "arbitrary"), DSA_VMEM_LIMIT),
        name="dsa_attend",
    )(hp, qi, wi, qp, ki, kt, vt)


def _block_diag(c):
    g = np.arange(c) // HEAD_DIM
    return jnp.asarray(g[:, None] == g[None, :], BF16)


def _tile_gain(g, n):
    return jnp.tile(g.astype(F32), n).reshape(1, -1)


def _col(g):
    return g.astype(F32).reshape(-1, 1)


def _pad_cols(w, n):
    return jnp.pad(w, ((0, 0), (0, n - w.shape[1])))


def _rope_tables(pos):
    half = MLA_ROPE // 2
    inv = ROPE_BASE ** (-jnp.arange(half, dtype=F32) / half)
    ang = pos.astype(F32)[:, None] * inv[None, :]
    return jnp.cos(ang), jnp.sin(ang)


def _alibi_slopes():
    return np.asarray(2.0 ** (-8.0 * np.arange(1, N_HEADS + 1) / N_HEADS), dtype=np.float32)


def _head_params(sinks=None):
    hp = jnp.zeros((N_HEADS, 1, 128), F32)
    hp = hp.at[:, 0, 0].set(jnp.asarray(_alibi_slopes()) * LOG2E)
    if sinks is not None:
        hp = hp.at[:, 0, 1].set(sinks.astype(F32) * LOG2E)
    return hp


class _Stream:
    def __init__(self, batch, seq, past):
        self.b, self.s, self.p = batch, seq, past
        self.r = batch * seq
        self.decode = past > 0
        self.tq = TQ_DEC if self.decode else TQ
        self.tqf = TQ_DEC if self.decode else min(TQ_FLASH, seq)
        self.sq =self.tq if self.decode else seq
        self.n_keys = past + seq
        self.lp = -(-self.n_keys // TK) * TK
        self.pos = past + np.tile(np.arange(seq), batch)

    def pad_keys(self, past_arr, new_arr, n_keys=None, lp=None):
        n_keys = self.n_keys if n_keys is None else n_keys
        lp = self.lp if lp is None else lp
        new_arr = new_arr.reshape(self.b, self.s, -1)
        parts = [new_arr] if past_arr is None else [past_arr.astype(F32), new_arr]
        if lp > n_keys:
            parts.append(jnp.zeros((self.b, lp - n_keys, new_arr.shape[-1]), F32))
        return jnp.concatenate(parts, axis=1) if len(parts) > 1 else new_arr

    def key_source(self, past_arr, new_arr):
        new_arr = new_arr.reshape(self.b, self.s, -1)
        if past_arr is None:
            return (None, new_arr)
        assert past_arr.shape[1] == self.p and self.p % TK == 0
        pad = self.lp - self.n_keys
        return (past_arr.astype(F32), jnp.pad(new_arr, ((0, 0), (0, pad), (0, 0))))

    def qp_blocks(self, qp):
        if not self.decode:
            return qp
        x = qp.reshape(N_HEADS, KT_LANES, self.b, self.s).transpose(2, 0, 1, 3)
        x = jnp.pad(x, ((0, 0), (0, 0), (0, 0), (0, self.tq - self.s)))
        return x.reshape(self.b * N_HEADS, KT_LANES, self.tq)

    def ot_cols(self, ot):
        if not self.decode:
            return ot.reshape(N_HEADS * HEAD_DIM, self.r)
        x = ot.reshape(self.b, N_HEADS, HEAD_DIM, self.tq)[..., :self.s]
        return x.transpose(1, 2, 0, 3).reshape(N_HEADS * HEAD_DIM, self.r)


def _mod_rows(st, mod_l, row0):
    m = mod_l[row0:row0 + st.b]
    if st.b > 1:
        m = jnp.repeat(m, st.s, axis=0)
    return m[:, :D_MODEL], m[:, D_MODEL:2 * D_MODEL], m[:, 2 * D_MODEL:]


def _mla_layer(st, x, mod, g, w, past):
    shift, scale, gate = mod
    r = st.r
    ts = _row_tile(r)
    cos, sin = _rope_tables(jnp.asarray(st.pos))
    zpad = jnp.zeros((r, 128 - MLA_ROPE), F32)
    cos_p = jnp.concatenate([cos, cos, zpad], axis=1)
    sin_p = jnp.concatenate([-sin, sin, zpad], axis=1)
    cqn, lat, kr, zt = _proj_call(
        _mla_in_body, "mla_in", r, ts,
        [x, scale, shift, cos_p, sin_p, g, w["gqa"], w["gkva"], w["gkr"], w["gkrp"]], [],
        [w["wcq"], w["wckv"], w["wkr"], w["wkrp"], w["wzt"]],
        [_rows_out(r, MLA_Q_LORA, ts, BF16), _rows_out(r, MLA_KV_LORA, ts), _rows_out(r, 128, ts),
         _cols_out(D_MODEL, r, ts)])
    (qp,) = _proj_call(_mla_q_body, "mla_q", r, ts, [cqn], [cos.T, sin.T], [w["wqt"], w["gqn"], w["gqr"]],
                       [_qp_out(r, ts)])
    past_lat, past_kr = (None, None) if past is None else past
    if past_kr is not None:
        past_kr = jnp.pad(past_kr.astype(F32), ((0, 0), (0, 0), (0, 128 - MLA_ROPE)))
    lat_all = st.pad_keys(past_lat, lat).reshape(st.b * st.lp, MLA_KV_LORA)
    kr_all = st.pad_keys(past_kr, kr).reshape(st.b * st.lp, 128)
    rk = st.b * st.lp
    npair = N_HEADS // 2
    nkb = st.lp // TK
    kt, vt = _proj_call(
        _mla_kv_body, "mla_kv", rk, TK, [lat_all, kr_all, w["gkn"]], [], [w["wkn"], w["wvt"], w["bd"]],
        [((npair * st.b, st.lp, KT_LANES), BF16, (npair, TK, KT_LANES), lambda i: (i // nkb, i % nkb, 0)),
         ((rk // TK, N_HEADS * HEAD_DIM, TK), BF16, (1, N_HEADS * HEAD_DIM, TK), lambda i: (i, 0, 0))])
    vt = vt.reshape(st.b, nkb, N_HEADS * HEAD_DIM, TK)
    ot = _flash(st.qp_blocks(qp), kt, vt, tq=st.tqf, mode="chunk", q0=st.p,
                n_real_q=st.tqf if not st.decode else st.s, n_keys=st.n_keys, paired=st.decode)
    x = _out_proj(x, gate, st.ot_cols(ot), zt, w["wo"])
    return x, (lat, kr[:, :MLA_ROPE])


def _dsa_layer(st, x, mod, g, w, past):
    shift, scale, gate = mod
    r = st.r
    ts = _row_tile(r)
    c = KV_HEADS * HEAD_DIM
    fused = past is None and ts == TK and st.b == 1
    outs = [_qp_out(r, ts), _rows_out(r, c, ts), _rows_out(r, c, ts),
            ((IDX_HEADS, IDX_DIM, r), BF16, (IDX_HEADS, IDX_DIM, ts), lambda i: (0, 0, i)),
            _rows_out(r, IDX_DIM, ts), _cols_out(IDX_HEADS, r, ts), _cols_out(D_MODEL, r, ts)]
    if fused:
        outs += [_rows_out(r, c, ts, BF16), ((r // TK, c, TK), BF16, (1, c, TK), lambda i: (i, 0, 0)),
                 _rows_out(r, IDX_DIM, ts, BF16)]
    res = _proj_call(
        _dsa_in_body, "dsa_in", r, ts, [x, scale, shift, g, w["gk"]], [],
        [w["gq"], w["wqt"], w["wk"], w["wv"], w["wqit"], w["wki"], w["wwit"], w["wzt"], w["bd"]], outs)
    qp, k, v, qi, ki, wi, zt = res[:7]
    if fused:
        kt, vt, kib = res[7][None], res[8][None], res[9][None]
    else:
        if past is None:
            pk = pv = pki = None
        else:
            pk, pv, pki = past[0].reshape(st.b, -1, c), past[1].reshape(st.b, -1, c), past[2]
        kt, vt, kib = _gqa_prep(st.key_source(pk, k), st.key_source(pv, v), st.key_source(pki, ki))
    if not st.decode:
        ot = _dsa_attend(_head_params(), qi[None], wi[None], qp, kib, kt, vt, tq=st.tq, q0=st.p,
                         n_real_q=st.tq, n_keys=st.n_keys, q_period=st.tq)
        ot = st.ot_cols(ot)
    else:
        heads = DSA_GROUPS * (N_HEADS // KV_HEADS)
        steps = KV_HEADS // DSA_GROUPS
        assert heads * st.s == st.tq

        def rows(a):
            return a.reshape(a.shape[0], a.shape[1], st.b, st.s).transpose(2, 0, 1, 3)

        qi_b = jnp.tile(rows(qi), (1, 1, 1, heads))
        wi_b = jnp.tile(rows(wi[None])[:, 0], (1, 1, heads))
        qp_b = rows(qp).reshape(st.b, steps, heads, KT_LANES, st.s).transpose(0, 1, 3, 2, 4)
        qp_b = qp_b.reshape(st.b * steps, KT_LANES, st.tq)
        slopes = (jnp.asarray(_alibi_slopes()) * LOG2E).reshape(steps, heads)
        hp = jnp.repeat(slopes, st.s, axis=1).reshape(steps, 1, st.tq)
        ot = _dsa_attend(hp, qi_b, wi_b, qp_b, kib, kt, vt, tq=st.tq, q0=st.p, n_real_q=st.s,
                         n_keys=st.n_keys, q_period=st.s)
        ot = ot.reshape(st.b, steps, HEAD_DIM, heads, st.s).transpose(1, 3, 2, 0, 4)
        ot = ot.reshape(N_HEADS * HEAD_DIM, st.r)
    x = _out_proj(x, gate, ot, zt, w["wo"])
    return x, (k, v, ki)


def _swa_layer(st, x, mod, g, w, past):
    shift, scale, gate = mod
    r = st.r
    ts = _row_tile(r)
    c = KV_HEADS * HEAD_DIM
    qp, k, v, zt = _proj_call(
        _swa_in_body, "swa_in", r, ts, [x, scale, shift, g, w["gk"]], [],
        [w["gq"], w["wqt"], w["wk"], w["wv"], w["wzt"], w["bd"]],
        [_qp_out(r, ts), _rows_out(r, c, ts), _rows_out(r, c, ts), _cols_out(D_MODEL, r, ts)])
    k3, v3 = k.reshape(st.b, st.s, c), v.reshape(st.b, st.s, c)
    if past is None:
        per = st.tq // WINDOW
        idx = [lambda i: jnp.maximum(per * i - 1, 0), lambda i: per * i, lambda i: per * i + 1]
        k_pieces = [(k3, WINDOW, f) for f in idx]
        v_pieces = [(v3, WINDOW, f) for f in idx]
        n_rows = WINDOW + st.tq
        new = (k3[:, st.s - WINDOW:], v3[:, st.s - WINDOW:])
    else:
        win = past[0].shape[1]
        assert win == WINDOW and st.s <= WINDOW
        pad = ((0, 0), (0, WINDOW - st.s), (0, 0))
        pk, pv = past[0].reshape(st.b, win, c).astype(F32), past[1].reshape(st.b, win, c).astype(F32)
        zero = lambda i: 0
        k_pieces = [(pk, WINDOW, zero), (jnp.pad(k3, pad), WINDOW, zero)]
        v_pieces = [(pv, WINDOW, zero), (jnp.pad(v3, pad), WINDOW, zero)]
        n_rows = win + st.s
        new = (jnp.concatenate([pk, k3], axis=1)[:, st.s:], jnp.concatenate([pv, v3], axis=1)[:, st.s:])
    ot = _swa_attend(_head_params(w["sinks"]), st.qp_blocks(qp), k_pieces, v_pieces, tq=st.tq, q0=st.p,
                     n_rows=n_rows)
    x = _out_proj(x, gate, st.ot_cols(ot), zt, w["wo"])
    return x, new


def _fox_layer(st, x, mod, g, w, past):
    shift, scale, gate = mod
    r = st.r
    ts = _row_tile(r)
    c = N_HEADS * HEAD_DIM
    qp, k, v, lf, zt = _proj_call(
        _fox_in_body, "fox_in", r, ts, [x, scale, shift, g, w["gk"], w["bf"]], [],
        [w["gq"], w["wqt"], w["wk"], w["wv"], w["wf"], w["wzt"], w["bd"]],
        [_qp_out(r, ts), _rows_out(r, c, ts), _rows_out(r, c, ts), _rows_out(r, 128, ts),
         _cols_out(D_MODEL, r, ts)])
    if past is None:
        pk = pv = plf = None
    else:
        pk, pv = past[0].reshape(st.b, -1, c), past[1].reshape(st.b, -1, c)
        plf = jnp.pad(past[2].astype(F32), ((0, 0), (0, 0), (0, 128 - N_HEADS)))
    kt, vt = _fox_prep(st.key_source(pk, k), st.key_source(pv, v), st.key_source(plf, lf))
    ot = _flash(st.qp_blocks(qp), kt, vt, tq=st.tqf, mode="causal", q0=st.p,
                n_real_q=st.tqf if not st.decode else st.s, n_keys=st.n_keys, paired=st.decode)
    x = _out_proj(x, gate, st.ot_cols(ot), zt, w["wo"])
    return x, (k, v, lf[:, :N_HEADS])


def _prep_weights(mla_w_in, mla_g_qa, mla_w_qb, mla_g_kva, mla_w_kvb, mla_g_qn, mla_g_qr, mla_g_kn,
                  mla_g_kr, mla_w_out, dsa_w_in, dsa_g_q, dsa_g_k, dsa_w_out, swa_w_in, swa_g_q,
                  swa_g_k, swa_sinks, swa_w_out, fox_w_in, fox_b_f, fox_g_q, fox_g_k, fox_w_out):
    bf = lambda a: a.astype(BF16)
    row = lambda a: a.astype(F32).reshape(1, -1)
    half = MLA_ROPE // 2
    c1, c2, c3 = MLA_Q_LORA, MLA_Q_LORA + MLA_KV_LORA, MLA_Q_LORA + MLA_KV_LORA + MLA_ROPE
    wkr = mla_w_in[:, c2:c3]
    wkrp = jnp.concatenate([wkr[:, half:], wkr[:, :half]], axis=1)
    gkr = mla_g_kr.astype(F32)
    gkrp = jnp.concatenate([gkr[half:], gkr[:half]])
    kvb = mla_w_kvb.reshape(MLA_KV_LORA, N_HEADS, MLA_NOPE + HEAD_DIM)
    mla = dict(
        wcq=bf(mla_w_in[:, :c1]), wckv=bf(mla_w_in[:, c1:c2]), wkr=bf(_pad_cols(wkr, 128)),
        wkrp=bf(_pad_cols(wkrp, 128)), wzt=bf(mla_w_in[:, c3:].T),
        gqa=row(mla_g_qa), gkva=row(mla_g_kva), gkr=row(jnp.pad(gkr, (0, 128 - MLA_ROPE))),
        gkrp=row(jnp.pad(gkrp, (0, 128 - MLA_ROPE))),
        wqt=bf(mla_w_qb.T), gqn=_col(mla_g_qn), gqr=_col(mla_g_qr),
        wkn=bf(kvb[:, :, :MLA_NOPE].reshape(MLA_KV_LORA, -1)),
        wvt=bf(kvb[:, :, MLA_NOPE:].reshape(MLA_KV_LORA, -1).T),
        gkn=_tile_gain(mla_g_kn, N_HEADS), bd=_block_diag(BD_LANES), wo=bf(mla_w_out))
    hq, hk = N_HEADS * HEAD_DIM, KV_HEADS * HEAD_DIM
    cuts = np.cumsum([hq, hk, hk, IDX_HEADS * IDX_DIM, IDX_DIM, IDX_HEADS]).tolist()
    dsa = dict(
        wqt=bf(dsa_w_in[:, :cuts[0]].T), wk=bf(dsa_w_in[:, cuts[0]:cuts[1]]),
        wv=bf(dsa_w_in[:, cuts[1]:cuts[2]]), wqit=bf(dsa_w_in[:, cuts[2]:cuts[3]].T),
        wki=bf(dsa_w_in[:, cuts[3]:cuts[4]]), wwit=bf(dsa_w_in[:, cuts[4]:cuts[5]].T),
        wzt=bf(dsa_w_in[:, cuts[5]:].T), gq=_col(dsa_g_q), gk=_tile_gain(dsa_g_k, KV_HEADS),
        bd=_block_diag(BD_LANES), wo=bf(dsa_w_out))
    swa = dict(
        wqt=bf(swa_w_in[:, :hq].T), wk=bf(swa_w_in[:, hq:hq + hk]), wv=bf(swa_w_in[:, hq + hk:hq + 2 * hk]),
        wzt=bf(swa_w_in[:, hq + 2 * hk:].T), gq=_col(swa_g_q), gk=_tile_gain(swa_g_k, KV_HEADS),
        bd=_block_diag(BD_LANES), wo=bf(swa_w_out), sinks=swa_sinks)
    fox = dict(
        wqt=bf(fox_w_in[:, :hq].T), wk=bf(fox_w_in[:, hq:2 * hq]), wv=bf(fox_w_in[:, 2 * hq:3 * hq]),
        wf=bf(_pad_cols(fox_w_in[:, 3 * hq:3 * hq + N_HEADS], 128)), wzt=bf(fox_w_in[:, 3 * hq + N_HEADS:].T),
        bf=row(jnp.pad(fox_b_f.astype(F32), (0, 128 - N_HEADS))), gq=_col(fox_g_q),
        gk=_tile_gain(fox_g_k, N_HEADS), bd=_block_diag(BD_LANES), wo=bf(fox_w_out))
    return [mla, dsa, swa, fox]


def kernel(x_prompt, x_sample, cache_mla_latent, cache_mla_krope, cache_dsa_k, cache_dsa_v, cache_dsa_kidx, state_swa_k, state_swa_v, cache_fox_k, cache_fox_v, cache_fox_logf, c_prompt, c_sample, norm_g, ada_w, ada_b, mla_w_in, mla_g_qa, mla_w_qb, mla_g_kva, mla_w_kvb, mla_g_qn, mla_g_qr, mla_g_kn, mla_g_kr, mla_w_out, dsa_w_in, dsa_g_q, dsa_g_k, dsa_w_out, swa_w_in, swa_g_q, swa_g_k, swa_sinks, swa_w_out, fox_w_in, fox_b_f, fox_g_q, fox_g_k, fox_w_out):
    bp, sp, _ = x_prompt.shape
    bs, ss, _ = x_sample.shape
    past_len = cache_mla_latent.shape[1]
    depth = norm_g.shape[0]
    assert bp == 1 and sp % TQ == 0 and sp % min(TQ_FLASH, sp) == 0 and sp % TS == 0 and (bs * ss) % 8 == 0 and ss <= TQ_DEC
    assert past_len % TK == 0 and past_len >= WINDOW

    weights = _prep_weights(mla_w_in, mla_g_qa, mla_w_qb, mla_g_kva, mla_w_kvb, mla_g_qn, mla_g_qr,
                            mla_g_kn, mla_g_kr, mla_w_out, dsa_w_in, dsa_g_q, dsa_g_k, dsa_w_out,
                            swa_w_in, swa_g_q, swa_g_k, swa_sinks, swa_w_out, fox_w_in, fox_b_f,
                            fox_g_q, fox_g_k, fox_w_out)
    rows = bp + bs
    rows_p = -(-rows // 8) * 8
    c_all = jnp.concatenate([c_prompt, c_sample, jnp.zeros((rows_p - rows, D_MODEL), F32)], axis=0)
    mod = _ada_mod(c_all, ada_w, ada_b)

    st_p = _Stream(bp, sp, 0)
    st_s = _Stream(bs, ss, past_len)
    pasts = ((cache_mla_latent, cache_mla_krope), (cache_dsa_k, cache_dsa_v, cache_dsa_kidx),
             (state_swa_k, state_swa_v), (cache_fox_k, cache_fox_v, cache_fox_logf))
    layers = (_mla_layer, _dsa_layer, _swa_layer, _fox_layer)
    xp = x_prompt.reshape(st_p.r, D_MODEL)
    xs = x_sample.reshape(st_s.r, D_MODEL)
    new_p, new_s = [], []
    for layer in range(depth):
        kind = layer % len(layers)
        g = norm_g[layer].astype(F32).reshape(1, -1)
        xp, n = layers[kind](st_p, xp, _mod_rows(st_p, mod[layer], 0), g, weights[kind], None)
        new_p.append(n)
        xs, n = layers[kind](st_s, xs, _mod_rows(st_s, mod[layer], bp), g, weights[kind], pasts[kind])
        new_s.append(n)

    def shaped(st, new):
        (lat, kr), (dk, dv, dki), (sk, sv), (fk, fv, flf) = new
        b, s = st.b, st.s
        return (lat.reshape(b, s, -1), kr.reshape(b, s, -1),
                dk.reshape(b, s, KV_HEADS, HEAD_DIM), dv.reshape(b, s, KV_HEADS, HEAD_DIM),
                dki.reshape(b, s, -1),
                sk.reshape(b, -1, KV_HEADS, HEAD_DIM), sv.reshape(b, -1, KV_HEADS, HEAD_DIM),
                fk.reshape(b, s, N_HEADS, HEAD_DIM), fv.reshape(b, s, N_HEADS, HEAD_DIM),
                flf.reshape(b, s, -1))

    return (xp.reshape(x_prompt.shape), xs.reshape(x_sample.shape)) + shaped(st_p, new_p) + shaped(st_s, new_s)
```

```python
import functools

import numpy as np
import jax
import jax.numpy as jnp
from jax import lax
from jax.experimental import pallas as pl
from jax.experimental.pallas import tpu as pltpu

F32 = jnp.float32
BF16 = jnp.bfloat16
I32 = jnp.int32

D_MODEL = 1024
HEAD_DIM = 64
N_HEADS = 16
KV_HEADS = 4
CHUNK = 64
CHUNK_SHIFT = 6
WINDOW = 128
WIN_CHUNKS = WINDOW // CHUNK
EPS = 1e-6
ROPE_BASE = 10000.0
MLA_NOPE, MLA_ROPE, MLA_Q_LORA, MLA_KV_LORA = 64, 32, 384, 256
IDX_HEADS, IDX_DIM, TOPK_MAX = 8, 64, 256
LOG2E = 1.4426950408889634
NEG = -1e30
INT_MIN = -(2 ** 31)
INT_MAX = 2 ** 31 - 1
I16 = jnp.int16
I16_MIN, I16_MAX = -(2 ** 15), 2 ** 15 - 1

TS = 512
TQ = 256
TQ_FLASH = 2048
TK = 512
TQ_DEC = 128
BD_LANES = 256
SUM_ROWS = 16
KT_LANES = 256
VMEM_LIMIT = 56 * 1024 * 1024
DSA_GROUPS = 2
DSA_VMEM_LIMIT = 62 * 1024 * 1024


def _row_tile(r, ts=TS):
    return ts if r % ts == 0 else r


def _cparams(sem, vmem=VMEM_LIMIT):
    return pltpu.CompilerParams(dimension_semantics=sem, vmem_limit_bytes=vmem)


def _dotf(a, b):
    return jnp.dot(a, b, preferred_element_type=F32)


def _dot_nt(a, b):
    return lax.dot_general(a, b, (((1,), (1,)), ((), ())), preferred_element_type=F32)


def _dot_tn(a, b):
    return lax.dot_general(a, b, (((0,), (0,)), ((), ())), preferred_element_type=F32)


def _split3(x):
    hi = x.astype(BF16)
    r = x - hi.astype(F32)
    mid = r.astype(BF16)
    lo = (r - mid.astype(F32)).astype(BF16)
    return hi, mid, lo


def _silu(x):
    return x / (1.0 + jnp.exp(-x))


def _full_spec(arr):
    nd = arr.ndim
    return pl.BlockSpec(arr.shape, lambda *_: (0,) * nd)


def _row_spec(arr, ts):
    if arr.shape[0] == 1:
        return pl.BlockSpec((1, arr.shape[1]), lambda i: (0, 0))
    return pl.BlockSpec((ts, arr.shape[1]), lambda i: (i, 0))


def _col_spec(arr, ts):
    return pl.BlockSpec((arr.shape[0], ts), lambda i: (0, i))


def _ada_body(c_ref, w_ref, b_ref, o_ref):
    a = _silu(c_ref[...])
    w = w_ref[0]
    a_hi = a.astype(BF16)
    a_lo = (a - a_hi.astype(F32)).astype(BF16)
    w_hi = w.astype(BF16)
    w_lo = (w - w_hi.astype(F32)).astype(BF16)
    o_ref[0] = _dotf(a_hi, w_hi) + _dotf(a_hi, w_lo) + _dotf(a_lo, w_hi) + b_ref[0]


def _ada_mod(c_all, ada_w, ada_b):
    depth, d, n3 = ada_w.shape
    bp = c_all.shape[0]
    tn = 768
    return pl.pallas_call(
        _ada_body,
        grid=(depth, n3 // tn),
        in_specs=[
            pl.BlockSpec((bp, d), lambda l, j: (0, 0)),
            pl.BlockSpec((1, d, tn), lambda l, j: (l, 0, j)),
            pl.BlockSpec((1, 1, tn), lambda l, j: (l, 0, j)),
        ],
        out_specs=pl.BlockSpec((1, bp, tn), lambda l, j: (l, 0, j)),
        out_shape=jax.ShapeDtypeStruct((depth, bp, n3), F32),
        compiler_params=_cparams(("arbitrary", "arbitrary")),
        name="ada_mod",
    )(c_all, ada_w, ada_b.reshape(depth, 1, n3))


def _prenorm(x_ref, g_ref, sc_ref, sh_ref):
    x = x_ref[...]
    ms = jnp.mean(x * x, axis=-1, keepdims=True)
    xn = x * lax.rsqrt(ms + EPS) * g_ref[...]
    return (xn * (1.0 + sc_ref[...]) + sh_ref[...]).astype(BF16)


def _group_sumsq(y, bd_ref):
    sq = y * y
    hi = sq.astype(BF16)
    lo = (sq - hi.astype(F32)).astype(BF16)
    bd = bd_ref[...]
    chunks = []
    for c in range(y.shape[1] // BD_LANES):
        sl = slice(c * BD_LANES, (c + 1) * BD_LANES)
        chunks.append(_dotf(hi[:, sl], bd) + _dotf(lo[:, sl], bd))
    return jnp.concatenate(chunks, axis=1) if len(chunks) > 1 else chunks[0]


def _head_rms_rows(q, g_col):
    ms = jnp.mean(q * q, axis=0, keepdims=True)
    return q * lax.rsqrt(ms + EPS) * g_col


def _place_rows(piece, row0, total):
    ts = piece.shape[1]
    parts = []
    if row0 > 0:
        parts.append(jnp.zeros((row0, ts), F32))
    parts.append(piece)
    rest = total - row0 - piece.shape[0]
    if rest > 0:
        parts.append(jnp.zeros((rest, ts), F32))
    return jnp.concatenate(parts, axis=0) if len(parts) > 1 else piece


def _proj_call(body, name, r, ts, row_in, col_in, const_in, outs):
    in_specs = ([_row_spec(a, ts) for a in row_in] + [_col_spec(a, ts) for a in col_in]
                + [_full_spec(a) for a in const_in])
    return pl.pallas_call(
        body,
        grid=(r // ts,),
        in_specs=in_specs,
        out_specs=[pl.BlockSpec(blk, im) for (_, _, blk, im) in outs],
        out_shape=[jax.ShapeDtypeStruct(s, dt) for (s, dt, _, _) in outs],
        compiler_params=_cparams(("arbitrary",)),
        name=name,
    )(*row_in, *col_in, *const_in)


def _rows_out(r, c, ts, dtype=F32):
    return ((r, c), dtype, (ts, c), lambda i: (i, 0))


def _cols_out(c, r, ts, dtype=F32):
    return ((c, r), dtype, (c, ts), lambda i: (0, i))


def _qp_out(r, ts):
    return ((N_HEADS, KT_LANES, r), BF16, (N_HEADS, KT_LANES, ts), lambda i: (0, 0, i))


def _mla_in_body(x_ref, sc_ref, sh_ref, cos_ref, sin_ref, g_ref, gqa, gkva, gkr, gkrp,
                 wcq, wckv, wkr, wkrp, wzt, cqn_o, lat_o, kr_o, zt_o):
    h = _prenorm(x_ref, g_ref, sc_ref, sh_ref)
    cq = _dotf(h, wcq[...])
    cqn_o[...] = (cq * lax.rsqrt(jnp.mean(cq * cq, axis=-1, keepdims=True) + EPS) * gqa[...]).astype(BF16)
    ckv = _dotf(h, wckv[...])
    lat_o[...] = ckv * lax.rsqrt(jnp.mean(ckv * ckv, axis=-1, keepdims=True) + EPS) * gkva[...]
    kr = _dotf(h, wkr[...])
    krp = _dotf(h, wkrp[...])
    inv = lax.rsqrt(jnp.sum(kr * kr, axis=-1, keepdims=True) * (1.0 / MLA_ROPE) + EPS)
    kr_o[...] = (kr * gkr[...] * cos_ref[...] + krp * gkrp[...] * sin_ref[...]) * inv
    zt_o[...] = _dot_nt(wzt[...], h)


def _mla_q_body(cqn_ref, cos_ref, sin_ref, wqt, gqn, gqr, qp_o):
    qt = _dot_nt(wqt[...], cqn_ref[...])
    ts = qt.shape[1]
    sc = (MLA_NOPE + MLA_ROPE) ** -0.5 * LOG2E
    c = cos_ref[...]
    s = sin_ref[...]
    half = MLA_ROPE // 2
    width = MLA_NOPE + MLA_ROPE
    for h in range(N_HEADS):
        qn = _head_rms_rows(qt[width * h:width * h + MLA_NOPE], gqn[...]) * sc
        qr = _head_rms_rows(qt[width * h + MLA_NOPE:width * (h + 1)], gqr[...]) * sc
        x1, x2 = qr[:half], qr[half:]
        o1 = x1 * c - x2 * s
        o2 = x2 * c + x1 * s
        a = h % 2
        pieces = [qn, jnp.zeros((HEAD_DIM, ts), F32)]
        if a:
            pieces = pieces[::-1]
        pieces += [o1, o2, jnp.zeros((KT_LANES - 2 * HEAD_DIM - MLA_ROPE, ts), F32)]
        qp_o[h] = jnp.concatenate(pieces, axis=0).astype(BF16)


def _mla_kv_body(lat_ref, kr_ref, gkn, wkn, wvt, bd, kt_o, vt_o):
    lat = lat_ref[...].astype(BF16)
    kn = _dotf(lat, wkn[...])
    ss = _group_sumsq(kn, bd)
    kn = kn * lax.rsqrt(ss * (1.0 / MLA_NOPE) + EPS) * gkn[...]
    kr = kr_ref[...].astype(BF16)
    for p in range(N_HEADS // 2):
        kt_o[p, :, 0:128] = kn[:, 128 * p:128 * (p + 1)].astype(BF16)
        kt_o[p, :, 128:256] = kr
    vt_o[0] = _dot_nt(wvt[...], lat).astype(BF16)


def _q_heads_to_qp(qt, gq, qp_o, row_of_head, extra_of_head=None):
    sc = HEAD_DIM ** -0.5 * LOG2E
    ts = qt.shape[1]
    for h in range(N_HEADS):
        qh = _head_rms_rows(qt[HEAD_DIM * h:HEAD_DIM * (h + 1)], gq[...]) * sc
        blk = _place_rows(qh, row_of_head(h), KT_LANES)
        if extra_of_head is not None:
            lo, hi = extra_of_head(h)
            rows = lax.broadcasted_iota(I32, (KT_LANES, ts), 0)
            blk = blk + jnp.where((rows >= lo) & (rows < hi), 1.0, 0.0)
        qp_o[h] = blk.astype(BF16)


def _dsa_in_body(x_ref, sc_ref, sh_ref, g_ref, gk, gq, wqt, wk, wv, wqit, wki, wwit, wzt, bd,
                 qp_o, k_o, v_o, qi_o, ki_o, wi_o, zt_o, *key_side):
    h = _prenorm(x_ref, g_ref, sc_ref, sh_ref)
    _q_heads_to_qp(_dot_nt(wqt[...], h), gq, qp_o, lambda hh: HEAD_DIM * (hh // (N_HEADS // KV_HEADS)))
    k = _dotf(h, wk[...])
    k = k * lax.rsqrt(_group_sumsq(k, bd) * (1.0 / HEAD_DIM) + EPS) * gk[...]
    v = _dotf(h, wv[...])
    ki = _dotf(h, wki[...])
    k_o[...] = k
    v_o[...] = v
    ki_o[...] = ki
    qit = _dot_nt(wqit[...], h)
    for ih in range(IDX_HEADS):
        qi_o[ih] = qit[IDX_DIM * ih:IDX_DIM * (ih + 1)].astype(BF16)
    wi_o[...] = _dot_nt(wwit[...], h)
    zt_o[...] = _dot_nt(wzt[...], h)
    if key_side:
        kt_o, vt_o, kib_o = key_side
        kt_o[...] = k.astype(BF16)
        vt_o[0] = v.T.astype(BF16)
        kib_o[...] = ki.astype(BF16)


def _swa_in_body(x_ref, sc_ref, sh_ref, g_ref, gk, gq, wqt, wk, wv, wzt, bd, qp_o, k_o, v_o, zt_o):
    h = _prenorm(x_ref, g_ref, sc_ref, sh_ref)
    _q_heads_to_qp(_dot_nt(wqt[...], h), gq, qp_o, lambda hh: HEAD_DIM * (hh // (N_HEADS // KV_HEADS)))
    k = _dotf(h, wk[...])
    k_o[...] = k * lax.rsqrt(_group_sumsq(k, bd) * (1.0 / HEAD_DIM) + EPS) * gk[...]
    v_o[...] = _dotf(h, wv[...])
    zt_o[...] = _dot_nt(wzt[...], h)


def _fox_in_body(x_ref, sc_ref, sh_ref, g_ref, gk, bf, gq, wqt, wk, wv, wf, wzt, bd,
                 qp_o, k_o, v_o, lf_o, zt_o):
    h = _prenorm(x_ref, g_ref, sc_ref, sh_ref)
    _q_heads_to_qp(_dot_nt(wqt[...], h), gq, qp_o, lambda hh: HEAD_DIM * (hh % 2),
                   lambda hh: (128 + 3 * (hh % 2), 128 + 3 * (hh % 2) + 3))
    k = _dotf(h, wk[...])
    k_o[...] = k * lax.rsqrt(_group_sumsq(k, bd) * (1.0 / HEAD_DIM) + EPS) * gk[...]
    v_o[...] = _dotf(h, wv[...])
    f = _dotf(h, wf[...]) + bf[...]
    lf_o[...] = jnp.minimum(f, 0.0) - jnp.log1p(jnp.exp(-jnp.abs(f)))
    zt_o[...] = _dot_nt(wzt[...], h)


def _out_body(x_ref, gate_ref, ot_ref, zt_ref, wo, o_ref):
    u = (ot_ref[...] * _silu(zt_ref[...])).astype(BF16)
    o_ref[...] = x_ref[...] + gate_ref[...] * _dot_tn(u, wo[...])


def _out_proj(x, gate, ot, zt, wo):
    r = x.shape[0]
    ts = _row_tile(r, 2 * TS)
    (out,) = _proj_call(_out_body, "out_proj", r, ts, [x, gate], [ot, zt], [wo],
                        [_rows_out(r, D_MODEL, ts)])
    return out


def _src_arrays(src):
    past, new = src
    return [new] if past is None else [past, new]


def _src_specs(src):
    past, new = src
    c = new.shape[2]
    if past is None:
        return [pl.BlockSpec((1, TK, c), lambda bb, j: (bb, j, 0))]
    npb = past.shape[1] // TK
    return [pl.BlockSpec((1, TK, c), lambda bb, j: (bb, jnp.minimum(j, npb - 1), 0)),
            pl.BlockSpec((1, TK, c), lambda bb, j: (bb, jnp.maximum(j - npb, 0), 0))]


def _src_rows(src):
    past, new = src
    return new.shape[1] + (0 if past is None else past.shape[1])


def _src_load(refs, src_past_blocks):
    if src_past_blocks is None:
        return refs[0][0]
    return jnp.where(pl.program_id(1) < src_past_blocks, refs[0][0], refs[1][0])


def _split_refs(refs, past_blocks):
    vals, pos = [], 0
    for npb in past_blocks:
        cnt = 1 if npb is None else 2
        vals.append(_src_load(refs[pos:pos + cnt], npb))
        pos += cnt
    return vals, refs[pos:]


def _past_blocks(srcs):
    return tuple(None if s[0] is None else s[0].shape[1] // TK for s in srcs)


def _gqa_prep_body(*refs, past_blocks):
    vals, outs = _split_refs(refs, past_blocks)
    outs[0][0] = vals[0].astype(BF16)
    outs[1][0, 0] = vals[1].T.astype(BF16)
    if len(vals) > 2:
        outs[2][0] = vals[2].astype(BF16)


def _gqa_prep(k, v, ki=None):
    srcs = [k, v] + ([ki] if ki is not None else [])
    b, c = k[1].shape[0], k[1].shape[2]
    lp = _src_rows(k)
    nkb = lp // TK
    out_specs = [pl.BlockSpec((1, TK, c), lambda bb, j: (bb, j, 0)),
                 pl.BlockSpec((1, 1, c, TK), lambda bb, j: (bb, j, 0, 0))]
    out_shape = [jax.ShapeDtypeStruct((b, lp, c), BF16), jax.ShapeDtypeStruct((b, nkb, c, TK), BF16)]
    if ki is not None:
        ci = ki[1].shape[2]
        out_specs.append(pl.BlockSpec((1, TK, ci), lambda bb, j: (bb, j, 0)))
        out_shape.append(jax.ShapeDtypeStruct((b, lp, ci), BF16))
    return pl.pallas_call(
        functools.partial(_gqa_prep_body, past_blocks=_past_blocks(srcs)),
        grid=(b, nkb), in_specs=[sp for s in srcs for sp in _src_specs(s)],
        out_specs=out_specs, out_shape=out_shape,
        compiler_params=_cparams(("arbitrary", "arbitrary")), name="gqa_prep",
    )(*[a for s in srcs for a in _src_arrays(s)])


def _fox_prep_body(*refs, past_blocks):
    (k, v, lf), (sel_ref, kt_o, vt_o, carry_ref) = _split_refs(refs, past_blocks)
    kb = pl.program_id(1)

    @pl.when(kb == 0)
    def _():
        carry_ref[...] = jnp.zeros_like(carry_ref)

    r = lax.broadcasted_iota(I32, (TK, TK), 0)
    c = lax.broadcasted_iota(I32, (TK, TK), 1)
    tri = jnp.where(c <= r, 1.0, 0.0).astype(BF16)
    cum3 = _dotf(tri, jnp.concatenate(_split3(lf), axis=1))
    cum = cum3[:, 0:128] + cum3[:, 128:256] + cum3[:, 256:384] + carry_ref[...]
    carry_ref[...] = cum[TK - 1:TK, :]
    ex = _dotf(jnp.concatenate(_split3(-(cum * LOG2E)), axis=1), sel_ref[...])
    for p in range(N_HEADS // 2):
        kt_o[p, :, 0:128] = k[:, 128 * p:128 * (p + 1)].astype(BF16)
        kt_o[p, :, 128:256] = ex[:, 128 * p:128 * (p + 1)].astype(BF16)
    vt_o[0, 0] = v.T.astype(BF16)


def _fox_sel():
    sel = np.zeros((3, 128, N_HEADS * HEAD_DIM), np.float32)
    for h in range(N_HEADS):
        for j in range(3):
            sel[j, h, 128 * (h // 2) + 3 * (h % 2) + j] = 1.0
    return jnp.asarray(sel.reshape(3 * 128, N_HEADS * HEAD_DIM), BF16)


def _fox_prep(k, v, lf128):
    srcs = [k, v, lf128]
    b, c = k[1].shape[0], k[1].shape[2]
    lp = _src_rows(k)
    nkb = lp // TK
    npair = N_HEADS // 2
    sel = _fox_sel()
    return pl.pallas_call(
        functools.partial(_fox_prep_body, past_blocks=_past_blocks(srcs)),
        grid=(b, nkb),
        in_specs=[sp for s in srcs for sp in _src_specs(s)] + [_full_spec(sel)],
        out_specs=[pl.BlockSpec((npair, TK, KT_LANES), lambda bb, j: (bb, j, 0)),
                   pl.BlockSpec((1, 1, c, TK), lambda bb, j: (bb, j, 0, 0))],
        out_shape=[jax.ShapeDtypeStruct((b * npair, lp, KT_LANES), BF16),
                   jax.ShapeDtypeStruct((b, nkb, c, TK), BF16)],
        scratch_shapes=[pltpu.VMEM((1, 128), F32)],
        compiler_params=_cparams(("arbitrary", "arbitrary")), name="fox_prep",
    )(*[a for s in srcs for a in _src_arrays(s)], sel)


def _visible_end(qpos, mode, n_keys):
    if mode == "causal":
        end = qpos + 1
    else:
        end = ((qpos >> CHUNK_SHIFT) + 1) << CHUNK_SHIFT
    return jnp.minimum(end, n_keys)


def _softmax_update(s, mx8, m_ref, c8=None, lane0=0):
    m8 = m_ref[:, lane0:]
    m_new8 = jnp.maximum(m8, mx8)
    alpha8 = jnp.exp2(m8 - m_new8)
    shift = m_new8[0:1] if c8 is None else (m_new8 - c8)[0:1]
    m_ref[:, lane0:] = m_new8
    return jnp.exp2(s - shift).astype(BF16), alpha8[0:1]


def _pv_and_sum(vt_blk, pb):
    ones = jnp.ones((SUM_ROWS, vt_blk.shape[1]), BF16)
    return _dotf(jnp.concatenate([vt_blk, ones], axis=0), pb)


def _run_two_stage(n_full, n_masked, stage_a, stage_b, trim=lambda j: 0):
    odd = n_full % 2

    @pl.when(odd == 1)
    def _():
        stage_a(0, 0, False, 0)
        stage_b(0, 0, False, 0)

    n_pairs = (n_full - odd) // 2

    @pl.when(n_pairs >= 1)
    def _():
        stage_a(odd, 0, False, 0)

        def body(u, _):
            kb = odd + 2 * u
            stage_b(kb, 0, False, 0)
            stage_a(kb + 1, 1, False, 0)
            stage_b(kb + 1, 1, False, 0)
            stage_a(kb + 2, 0, False, 0)
            return 0

        lax.fori_loop(0, n_pairs - 1, body, 0)
        stage_b(n_full - 2, 0, False, 0)
        stage_a(n_full - 1, 1, False, 0)
        stage_b(n_full - 1, 1, False, 0)
        stage_a(n_full, 0, True, trim(0))

    @pl.when(n_pairs < 1)
    def _():
        stage_a(n_full, 0, True, trim(0))

    for j in range(1, n_masked):
        stage_b(n_full + j - 1, (j - 1) % 2, True, trim(j - 1))
        stage_a(n_full + j, j % 2, True, trim(j))
    stage_b(n_full + n_masked - 1, (n_masked - 1) % 2, True, trim(n_masked - 1))


def _flash_body(qp_ref, kt_ref, vt_ref, o_ref, s_buf, mx_buf, m_ref, acc_ref, *,
                tq, mode, q0, n_keys, paired, n_masked):
    i = pl.program_id(1)
    q_first = q0 + i * tq
    n_full = _visible_end(q_first, mode, n_keys) // TK
    ns = N_HEADS // 2 if paired else 1
    per = 2 if paired else 1
    ws, rows = per * tq, per * HEAD_DIM
    w = ns * ws
    ik = lax.broadcasted_iota(I32, (TK, ws), 0)
    iq = lax.broadcasted_iota(I32, (TK, tq), 1)
    qpos = q_first + (jnp.concatenate([iq] * per, axis=1) if paired else iq)
    qs = [jnp.concatenate([qp_ref[per * si + a] for a in range(per)], axis=1) if paired else qp_ref[si]
          for si in range(ns)]

    m_ref[...] = jnp.full((8, w), NEG, F32)
    acc_ref[...] = jnp.zeros((rows + SUM_ROWS, w), F32)

    def stage_a(kb, slot, masked, lane0):
        assert lane0 == 0 or ns == 1
        k0 = pl.multiple_of(kb * TK, TK)
        wl = ws - lane0
        if masked:
            if lane0:
                kpos = k0 + lax.broadcasted_iota(I32, (TK, wl), 0)
                qp = q_first + lane0 + lax.broadcasted_iota(I32, (TK, wl), 1)
            else:
                kpos, qp = k0 + ik, qpos
            valid = (kpos <= qp) if mode == "causal" else ((kpos >> CHUNK_SHIFT) <= (qp >> CHUNK_SHIFT))
            valid = valid & (kpos < n_keys)
        for si in range(ns):
            s = _dotf(kt_ref[si, pl.ds(k0, TK), :], qs[si][:, lane0:])
            if masked:
                s = jnp.where(valid, s, NEG)
            s_buf[slot, :, si * ws + lane0:(si + 1) * ws] = s
            mx_buf[slot, :, si * ws + lane0:(si + 1) * ws] = jnp.broadcast_to(
                jnp.max(s, axis=0, keepdims=True), (8, wl))

    def stage_b(kb, slot, masked, lane0):
        del masked
        pb, alpha = _softmax_update(s_buf[slot, :, lane0:], mx_buf[slot, :, lane0:], m_ref, lane0=lane0)
        for si in range(ns):
            loc = slice(si * ws, (si + 1) * ws - lane0)
            glob = slice(si * ws + lane0, (si + 1) * ws)
            acc_ref[:, glob] = alpha[:, loc] * acc_ref[:, glob] + _pv_and_sum(
                vt_ref[0, kb, rows * si:rows * (si + 1), :], pb[:, loc])

    trim = (lambda j: j * TK) if (not paired and tq > TK) else (lambda j: 0)
    _run_two_stage(n_full, n_masked, stage_a, stage_b, trim)
    out = acc_ref[0:rows, :] / acc_ref[rows:rows + 1, :]
    for si in range(ns):
        for a in range(per):
            o_ref[per * si + a] = out[HEAD_DIM * a:HEAD_DIM * (a + 1), si * ws + tq * a:si * ws + tq * (a + 1)]


def _visible_end_static(qpos, mode, n_keys):
    end = qpos + 1 if mode == "causal" else ((qpos >> CHUNK_SHIFT) + 1) << CHUNK_SHIFT
    return min(end, n_keys)


def _flash(qp, kt, vt, *, tq, mode, q0, n_real_q, n_keys, paired):
    bh, _, sq = qp.shape
    lp = kt.shape[1]
    nkb = vt.shape[1]
    assert (q0 % TK == 0) and (tq % TK == 0 or sq == tq)
    n_masked = (-(-_visible_end_static(q0 + n_real_q - 1, mode, n_keys) // TK)
                - _visible_end_static(q0, mode, n_keys) // TK)
    body = functools.partial(_flash_body, tq=tq, mode=mode, q0=q0, n_keys=n_keys, paired=paired,
                             n_masked=n_masked)
    if not paired:
        ns, w, rows = 1, tq, HEAD_DIM
        kt_spec = pl.BlockSpec((1, lp, KT_LANES), lambda g, i: (g // 2, 0, 0))
        vt_spec = pl.BlockSpec((1, nkb, HEAD_DIM, TK), lambda g, i: (g // N_HEADS, 0, g % N_HEADS, 0))
    else:
        ns, w, rows = N_HEADS, N_HEADS * tq, 2 * HEAD_DIM
        kt_spec = pl.BlockSpec((ns // 2, lp, KT_LANES), lambda g, i: (g, 0, 0))
        vt_spec = pl.BlockSpec((1, nkb, ns * HEAD_DIM, TK), lambda g, i: (g, 0, 0, 0))
    return pl.pallas_call(
        body,
        grid=(bh // ns, sq // tq),
        in_specs=[pl.BlockSpec((ns, KT_LANES, tq), lambda g, i: (g, 0, i)), kt_spec, vt_spec],
        out_specs=pl.BlockSpec((ns, HEAD_DIM, tq), lambda g, i: (g, 0, i)),
        out_shape=jax.ShapeDtypeStruct((bh, HEAD_DIM, sq), F32),
        scratch_shapes=[pltpu.VMEM((2, TK, w), F32), pltpu.VMEM((2, 8, w), F32), pltpu.VMEM((8, w), F32),
                        pltpu.VMEM((rows + SUM_ROWS, w), F32)],
        compiler_params=_cparams(("arbitrary", "arbitrary")),
        name="flash_" + mode,
    )(qp, kt, vt)


def _swa_body(hp_ref, qp_ref, *refs, tq, q0, n_pieces, n_rows):
    k_refs, v_refs, o_ref = refs[:n_pieces], refs[n_pieces:2 * n_pieces], refs[2 * n_pieces]
    i = pl.program_id(1)
    q_first = q0 + i * tq
    k = jnp.concatenate([r[0].astype(BF16) for r in k_refs], axis=0)
    vt = jnp.concatenate([r[0] for r in v_refs], axis=0).T.astype(BF16)
    kw = k.shape[0]
    row = lax.broadcasted_iota(I32, (kw, tq), 0)
    kpos = (q_first - WINDOW) + row
    qpos = q_first + lax.broadcasted_iota(I32, (kw, tq), 1)
    qc = qpos >> CHUNK_SHIFT
    kc = kpos >> CHUNK_SHIFT
    valid = (kpos >= 0) & (kc <= qc) & (qc - kc <= WIN_CHUNKS) & (row < n_rows)
    dist = jnp.abs(qpos - kpos).astype(F32)
    group = N_HEADS // KV_HEADS

    def lanes(x):
        return jnp.concatenate([x] * group, axis=1)

    for n in range(KV_HEADS):
        h0 = n * group

        def per_head(col):
            return jnp.concatenate(
                [jnp.broadcast_to(hp_ref[h0 + g][:, col:col + 1], (1, tq)) for g in range(group)], axis=1)

        q = jnp.concatenate([qp_ref[h0 + g] for g in range(group)], axis=1)
        slope2, sink2 = per_head(0), per_head(1)
        s = _dotf(k, q) - slope2 * lanes(dist)
        s = jnp.where(lanes(valid), s, NEG)
        m = jnp.maximum(jnp.max(s, axis=0, keepdims=True), sink2)
        acc = _pv_and_sum(vt[HEAD_DIM * n:HEAD_DIM * (n + 1)], jnp.exp2(s - m).astype(BF16))
        out = acc[0:HEAD_DIM] / (acc[HEAD_DIM:HEAD_DIM + 1] + jnp.exp2(sink2 - m))
        for g in range(group):
            o_ref[h0 + g] = out[:, g * tq:(g + 1) * tq]


def _swa_attend(hp, qp, k_pieces, v_pieces, *, tq, q0, n_rows):
    b = k_pieces[0][0].shape[0]
    sq = qp.shape[2]
    n_pieces = len(k_pieces)

    def spec(piece):
        _, rows, idx = piece
        return pl.BlockSpec((1, rows, KV_HEADS * HEAD_DIM), lambda bb, i: (bb, idx(i), 0))

    body = functools.partial(_swa_body, tq=tq, q0=q0, n_pieces=n_pieces, n_rows=n_rows)
    return pl.pallas_call(
        body,
        grid=(b, sq // tq),
        in_specs=[_full_spec(hp), pl.BlockSpec((N_HEADS, KT_LANES, tq), lambda bb, i: (bb, 0, i))]
        + [spec(p) for p in k_pieces] + [spec(p) for p in v_pieces],
        out_specs=pl.BlockSpec((N_HEADS, HEAD_DIM, tq), lambda bb, i: (bb, 0, i)),
        out_shape=jax.ShapeDtypeStruct((b * N_HEADS, HEAD_DIM, sq), F32),
        compiler_params=_cparams(("arbitrary", "arbitrary")),
        name="swa_attend",
    )(hp, qp, *[p[0] for p in k_pieces], *[p[0] for p in v_pieces])


def _dsa_body(hp_ref, qi_ref, wi_ref, qp_ref, ki_ref, kt_ref, vt_ref, o_ref, hi_ref, lo_ref,
              s_buf, mx_buf, m_ref, acc_ref, *, tq, q0, n_real_q, n_keys, topk, q_period):
    packed = q_period < tq
    i = pl.program_id(1)
    q_first = q0 + i * tq
    q_last = q_first + (n_real_q - 1)
    n_tot = (_visible_end(q_last, "chunk", n_keys) + (TK - 1)) // TK
    n_past = jnp.minimum(q_first, n_keys) // TK
    ik = lax.broadcasted_iota(I32, (TK, tq), 0)
    iq = lax.broadcasted_iota(I32, (TK, tq), 1)
    if packed:
        iq = iq & (q_period - 1)
    qpos = q_first + iq
    tf = float(topk)

    def blk(kb):
        return pl.ds(pl.multiple_of(kb * TK, TK), TK)

    def to_key(x):
        bits = pltpu.bitcast(x, I32)
        return jnp.where(bits < 0, bits ^ INT_MAX, bits)

    def score_blk(kb, diag):
        ki = ki_ref[0, blk(kb), :]
        acc = jnp.zeros((TK, tq), F32)
        for h in range(IDX_HEADS):
            acc = acc + wi_ref[0, h:h + 1, :] * jnp.maximum(_dotf(ki, qi_ref[0, h]), 0.0)
        key = to_key(acc)
        if diag:
            kpos = kb * TK + ik
            valid = ((kpos >> CHUNK_SHIFT) <= (qpos >> CHUNK_SHIFT)) & (kpos < n_keys)
            key = jnp.where(valid, key, INT_MIN)
        hi_ref[blk(kb), :] = (key >> 16).astype(I16)
        lo_ref[blk(kb), :] = ((key & 0xFFFF) - 32768).astype(I16)
        return 0

    lax.fori_loop(0, n_past, lambda kb, c: score_blk(kb, False), 0)
    lax.fori_loop(n_past, n_tot, lambda kb, c: score_blk(kb, True), 0)

    def count_ge(ref, mid):
        mid16 = mid.astype(I16)

        def body(kb, acc):
            ge = jnp.where(ref[blk(kb), :] >= mid16, jnp.int16(1), jnp.int16(0))
            parts = [ge[16 * r:16 * (r + 1)] for r in range(TK // 16)]
            while len(parts) > 1:
                parts = [parts[j] + parts[j + 1] for j in range(0, len(parts), 2)]
            return acc + parts[0]

        acc = lax.fori_loop(0, n_tot, body, jnp.zeros((16, tq), I16))
        return jnp.sum(acc.astype(I32), axis=0, keepdims=True).astype(F32)

    def bisect_step(ref, target, st):
        lo, hi, cl, ch = st
        mid = (lo + hi) >> 1
        cnt = count_ge(ref, mid)
        ge = cnt >= target
        return jnp.where(ge, mid, lo), jnp.where(ge, hi, mid), jnp.where(ge, cnt, cl), jnp.where(ge, ch, cnt)

    qrow = q_first + iq[0:1]
    n_vis = _visible_end(qrow, "chunk", n_keys).astype(F32)
    zero = jnp.zeros((1, tq), F32)
    st1 = (jnp.full((1, tq), I16_MIN + 1, I32), jnp.full((1, tq), I16_MAX + 1, I32), n_vis, zero)
    h_thr, _, cl1, ch1 = lax.fori_loop(0, 16, lambda _, st: bisect_step(hi_ref, tf, st), st1)
    h16 = h_thr.astype(I16)

    def mask_lo(kb, _):
        lo_ref[blk(kb), :] = jnp.where(hi_ref[blk(kb), :] == h16, lo_ref[blk(kb), :], jnp.int16(I16_MIN))
        return 0

    lax.fori_loop(0, n_tot, mask_lo, 0)
    t2 = tf - ch1

    def cond(st):
        return (st[0] < 16) & (st[2] > 0.5)

    def body(st):
        lo, hi, cl, ch = bisect_step(lo_ref, t2, st[1])
        done = (cl <= t2) | (hi - lo <= 1)
        return st[0] + 1, (lo, hi, cl, ch), jnp.sum(jnp.where(done, 0.0, 1.0))

    cl2_0 = cl1 - ch1
    st2 = (jnp.full((1, tq), I16_MIN, I32), jnp.full((1, tq), I16_MAX + 1, I32), cl2_0, zero)
    _, (l_thr, _, cl2, ch2), _ = lax.while_loop(
        cond, body, (jnp.int32(0), st2, jnp.sum(jnp.where(cl2_0 > t2, 1.0, 0.0))))
    l16 = l_thr.astype(I16)

    need = t2 - ch2

    @pl.when(jnp.sum(jnp.where(cl2 > t2, 1.0, 0.0)) > 0.5)
    def _():
        r = lax.broadcasted_iota(I32, (TK, TK), 0)
        c = lax.broadcasted_iota(I32, (TK, TK), 1)
        tri = jnp.where(c < r, 1.0, 0.0).astype(BF16)

        def fix(kb, carry):
            hb = hi_ref[blk(kb), :]
            e16 = jnp.where(hb == h16, jnp.where(lo_ref[blk(kb), :] == l16, jnp.int16(1), jnp.int16(0)),
                            jnp.int16(0))
            e = e16.astype(I32).astype(F32)
            before = _dotf(tri, e.astype(BF16)) + carry
            drop = jnp.where((e > 0.5) & (before >= need), 1, 0).astype(I16)
            hi_ref[blk(kb), :] = jnp.where(drop == jnp.int16(1), jnp.int16(I16_MIN), hb)
            return carry + jnp.sum(e, axis=0, keepdims=True)

        lax.fori_loop(0, n_tot, fix, jnp.zeros((1, tq), F32))

    def to_bias(kb, _):
        hb = hi_ref[blk(kb), :]
        zero_b, neg_b = jnp.bfloat16(0.0), jnp.bfloat16(NEG)
        at_thr = jnp.where(lo_ref[blk(kb), :] >= l16, zero_b, neg_b)
        bias = jnp.where(hb > h16, zero_b, jnp.where(hb == h16, at_thr, neg_b))
        hi_ref[blk(kb), :] = pltpu.bitcast(bias, I16)
        return 0

    lax.fori_loop(0, n_tot, to_bias, 0)

    group = N_HEADS // KV_HEADS
    heads = DSA_GROUPS * group
    wide = tq if packed else heads * tq
    gw = wide // DSA_GROUPS
    vr = DSA_GROUPS * HEAD_DIM if packed else HEAD_DIM

    def lanes(x):
        return x if packed else jnp.concatenate([x] * heads, axis=1)

    def group_body(n, _):
        h0 = n * heads
        if packed:
            q, slope2 = qp_ref[n], hp_ref[n]
        else:
            q = jnp.concatenate([qp_ref[h0 + g] for g in range(heads)], axis=1)
            slope2 = jnp.concatenate(
                [jnp.broadcast_to(hp_ref[h0 + g][:, 0:1], (1, tq)) for g in range(heads)], axis=1)
        slope8 = jnp.broadcast_to(slope2, (8, wide))
        a_tab = slope2 * lanes(ik.astype(F32))
        vrows = [pl.ds(pl.multiple_of((n * DSA_GROUPS + gi) * HEAD_DIM, HEAD_DIM), HEAD_DIM)
                 for gi in range(DSA_GROUPS)]
        m_ref[...] = jnp.full((8, wide), NEG, F32)
        acc_ref[...] = jnp.zeros((vr + SUM_ROWS, wide), F32)

        def c8(kb):
            return slope8 * (kb * TK - q_first).astype(F32)

        def stage_a(kb, slot, diag, lane0):
            del lane0
            bias = pltpu.bitcast(hi_ref[blk(kb), :], jnp.bfloat16).astype(F32)
            s = _dotf(kt_ref[0, blk(kb), :], q) + lanes(bias)
            if diag:
                kpos = kb * TK + ik
                rel = iq.astype(F32) - jnp.abs(qpos - kpos).astype(F32)
                s = s + slope2 * lanes(rel)
            else:
                s = s + a_tab
            mx8 = jnp.broadcast_to(jnp.max(s, axis=0, keepdims=True), (8, wide))
            s_buf[slot] = s
            mx_buf[slot] = mx8 if diag else mx8 + c8(kb)

        def stage_b(kb, slot, diag, lane0):
            del lane0
            pb, alpha = _softmax_update(s_buf[slot], mx_buf[slot], m_ref, None if diag else c8(kb))
            if packed:
                vt2 = jnp.concatenate([vt_ref[0, kb, vrows[gi], :] for gi in range(DSA_GROUPS)], axis=0)
                acc_ref[...] = alpha * acc_ref[...] + _pv_and_sum(vt2, pb)
            else:
                for gi in range(DSA_GROUPS):
                    sl = slice(gi * gw, (gi + 1) * gw)
                    acc_ref[:, sl] = alpha[:, sl] * acc_ref[:, sl] + _pv_and_sum(
                        vt_ref[0, kb, vrows[gi], :], pb[:, sl])

        _run_two_stage(n_past, 1, stage_a, stage_b)
        out = acc_ref[0:vr, :] / acc_ref[vr:vr + 1, :]
        if packed:
            lane = lax.broadcasted_iota(I32, (HEAD_DIM, wide), 1)
            sel = out[0:HEAD_DIM]
            for gi in range(1, DSA_GROUPS):
                sel = jnp.where(lane >= gi * gw, out[gi * HEAD_DIM:(gi + 1) * HEAD_DIM], sel)
            o_ref[n] = sel
        else:
            for g in range(heads):
                o_ref[h0 + g] = out[:, g * tq:(g + 1) * tq]
        return 0

    lax.fori_loop(0, KV_HEADS // DSA_GROUPS, group_body, 0)


def _dsa_attend(hp, qi, wi, qp, ki, kt, vt, *, tq, q0, n_real_q, n_keys, q_period):
    b = ki.shape[0]
    sq = qp.shape[2]
    lp = kt.shape[1]
    nkb = vt.shape[1]
    topk = min(TOPK_MAX, n_keys // 4)
    heads = DSA_GROUPS * (N_HEADS // KV_HEADS)
    steps = KV_HEADS // DSA_GROUPS
    packed = q_period < tq
    assert TK % tq == 0 and q0 % TK == 0 and n_real_q <= q_period
    assert (not packed) or (heads * q_period == tq and sq == tq and q_period & (q_period - 1) == 0)
    wide = tq if packed else heads * tq
    vr = DSA_GROUPS * HEAD_DIM if packed else HEAD_DIM
    nq = steps if packed else N_HEADS
    body = functools.partial(_dsa_body, tq=tq, q0=q0, n_real_q=n_real_q, n_keys=n_keys, topk=topk,
                             q_period=q_period)
    once = pl.Buffered(1)
    return pl.pallas_call(
        body,
        grid=(b, sq // tq),
        in_specs=[_full_spec(hp),
                  pl.BlockSpec((1, IDX_HEADS, IDX_DIM, tq), lambda bb, i: (bb, 0, 0, i)),
                  pl.BlockSpec((1, IDX_HEADS, tq), lambda bb, i: (bb, 0, i)),
                  pl.BlockSpec((nq, KT_LANES, tq), lambda bb, i: (bb, 0, i)),
                  pl.BlockSpec((1, lp, IDX_DIM), lambda bb, i: (bb, 0, 0), pipeline_mode=once),
                  pl.BlockSpec((1, lp, KT_LANES), lambda bb, i: (bb, 0, 0), pipeline_mode=once),
                  pl.BlockSpec((1, nkb, KV_HEADS * HEAD_DIM, TK), lambda bb, i: (bb, 0, 0, 0),
                               pipeline_mode=once)],
        out_specs=pl.BlockSpec((nq, HEAD_DIM, tq), lambda bb, i: (bb, 0, i)),
        out_shape=jax.ShapeDtypeStruct((b * nq, HEAD_DIM, sq), F32),
        scratch_shapes=[pltpu.VMEM((lp, tq), I16), pltpu.VMEM((lp, tq), I16),
                        pltpu.VMEM((2, TK, wide), F32), pltpu.VMEM((2, 8, wide), F32),
                        pltpu.VMEM((8, wide), F32), pltpu.VMEM((vr + SUM_ROWS, wide), F32)],
        compiler_params=_cparams(("arbitrary", "arbitrary"), DSA_VMEM_LIMIT),
        name="dsa_attend",
    )(hp, qi, wi, qp, ki, kt, vt)


def _block_diag(c):
    g = np.arange(c) // HEAD_DIM
    return jnp.asarray(g[:, None] == g[None, :], BF16)


def _tile_gain(g, n):
    return jnp.tile(g.astype(F32), n).reshape(1, -1)


def _col(g):
    return g.astype(F32).reshape(-1, 1)


def _pad_cols(w, n):
    return jnp.pad(w, ((0, 0), (0, n - w.shape[1])))


def _rope_tables(pos):
    half = MLA_ROPE // 2
    inv = ROPE_BASE ** (-jnp.arange(half, dtype=F32) / half)
    ang = pos.astype(F32)[:, None] * inv[None, :]
    return jnp.cos(ang), jnp.sin(ang)


def _alibi_slopes():
    return np.asarray(2.0 ** (-8.0 * np.arange(1, N_HEADS + 1) / N_HEADS), dtype=np.float32)


def _head_params(sinks=None):
    hp = jnp.zeros((N_HEADS, 1, 128), F32)
    hp = hp.at[:, 0, 0].set(jnp.asarray(_alibi_slopes()) * LOG2E)
    if sinks is not None:
        hp = hp.at[:, 0, 1].set(sinks.astype(F32) * LOG2E)
    return hp


class _Stream:
    def __init__(self, batch, seq, past):
        self.b, self.s, self.p = batch, seq, past
        self.r = batch * seq
        self.decode = past > 0
        self.tq = TQ_DEC if self.decode else TQ
        self.tqf = TQ_DEC if self.decode else min(TQ_FLASH, seq)
        self.sq =self.tq if self.decode else seq
        self.n_keys = past + seq
        self.lp = -(-self.n_keys // TK) * TK
        self.pos = past + np.tile(np.arange(seq), batch)

    def pad_keys(self, past_arr, new_arr, n_keys=None, lp=None):
        n_keys = self.n_keys if n_keys is None else n_keys
        lp = self.lp if lp is None else lp
        new_arr = new_arr.reshape(self.b, self.s, -1)
        parts = [new_arr] if past_arr is None else [past_arr.astype(F32), new_arr]
        if lp > n_keys:
            parts.append(jnp.zeros((self.b, lp - n_keys, new_arr.shape[-1]), F32))
        return jnp.concatenate(parts, axis=1) if len(parts) > 1 else new_arr

    def key_source(self, past_arr, new_arr):
        new_arr = new_arr.reshape(self.b, self.s, -1)
        if past_arr is None:
            return (None, new_arr)
        assert past_arr.shape[1] == self.p and self.p % TK == 0
        pad = self.lp - self.n_keys
        return (past_arr.astype(F32), jnp.pad(new_arr, ((0, 0), (0, pad), (0, 0))))

    def qp_blocks(self, qp):
        if not self.decode:
            return qp
        x = qp.reshape(N_HEADS, KT_LANES, self.b, self.s).transpose(2, 0, 1, 3)
        x = jnp.pad(x, ((0, 0), (0, 0), (0, 0), (0, self.tq - self.s)))
        return x.reshape(self.b * N_HEADS, KT_LANES, self.tq)

    def ot_cols(self, ot):
        if not self.decode:
            return ot.reshape(N_HEADS * HEAD_DIM, self.r)
        x = ot.reshape(self.b, N_HEADS, HEAD_DIM, self.tq)[..., :self.s]
        return x.transpose(1, 2, 0, 3).reshape(N_HEADS * HEAD_DIM, self.r)


def _mod_rows(st, mod_l, row0):
    m = mod_l[row0:row0 + st.b]
    if st.b > 1:
        m = jnp.repeat(m, st.s, axis=0)
    return m[:, :D_MODEL], m[:, D_MODEL:2 * D_MODEL], m[:, 2 * D_MODEL:]


def _mla_layer(st, x, mod, g, w, past):
    shift, scale, gate = mod
    r = st.r
    ts = _row_tile(r)
    cos, sin = _rope_tables(jnp.asarray(st.pos))
    zpad = jnp.zeros((r, 128 - MLA_ROPE), F32)
    cos_p = jnp.concatenate([cos, cos, zpad], axis=1)
    sin_p = jnp.concatenate([-sin, sin, zpad], axis=1)
    cqn, lat, kr, zt = _proj_call(
        _mla_in_body, "mla_in", r, ts,
        [x, scale, shift, cos_p, sin_p, g, w["gqa"], w["gkva"], w["gkr"], w["gkrp"]], [],
        [w["wcq"], w["wckv"], w["wkr"], w["wkrp"], w["wzt"]],
        [_rows_out(r, MLA_Q_LORA, ts, BF16), _rows_out(r, MLA_KV_LORA, ts), _rows_out(r, 128, ts),
         _cols_out(D_MODEL, r, ts)])
    (qp,) = _proj_call(_mla_q_body, "mla_q", r, ts, [cqn], [cos.T, sin.T], [w["wqt"], w["gqn"], w["gqr"]],
                       [_qp_out(r, ts)])
    past_lat, past_kr = (None, None) if past is None else past
    if past_kr is not None:
        past_kr = jnp.pad(past_kr.astype(F32), ((0, 0), (0, 0), (0, 128 - MLA_ROPE)))
    lat_all = st.pad_keys(past_lat, lat).reshape(st.b * st.lp, MLA_KV_LORA)
    kr_all = st.pad_keys(past_kr, kr).reshape(st.b * st.lp, 128)
    rk = st.b * st.lp
    npair = N_HEADS // 2
    nkb = st.lp // TK
    kt, vt = _proj_call(
        _mla_kv_body, "mla_kv", rk, TK, [lat_all, kr_all, w["gkn"]], [], [w["wkn"], w["wvt"], w["bd"]],
        [((npair * st.b, st.lp, KT_LANES), BF16, (npair, TK, KT_LANES), lambda i: (i // nkb, i % nkb, 0)),
         ((rk // TK, N_HEADS * HEAD_DIM, TK), BF16, (1, N_HEADS * HEAD_DIM, TK), lambda i: (i, 0, 0))])
    vt = vt.reshape(st.b, nkb, N_HEADS * HEAD_DIM, TK)
    ot = _flash(st.qp_blocks(qp), kt, vt, tq=st.tqf, mode="chunk", q0=st.p,
                n_real_q=st.tqf if not st.decode else st.s, n_keys=st.n_keys, paired=st.decode)
    x = _out_proj(x, gate, st.ot_cols(ot), zt, w["wo"])
    return x, (lat, kr[:, :MLA_ROPE])


def _dsa_layer(st, x, mod, g, w, past):
    shift, scale, gate = mod
    r = st.r
    ts = _row_tile(r)
    c = KV_HEADS * HEAD_DIM
    fused = past is None and ts == TK and st.b == 1
    outs = [_qp_out(r, ts), _rows_out(r, c, ts), _rows_out(r, c, ts),
            ((IDX_HEADS, IDX_DIM, r), BF16, (IDX_HEADS, IDX_DIM, ts), lambda i: (0, 0, i)),
            _rows_out(r, IDX_DIM, ts), _cols_out(IDX_HEADS, r, ts), _cols_out(D_MODEL, r, ts)]
    if fused:
        outs += [_rows_out(r, c, ts, BF16), ((r // TK, c, TK), BF16, (1, c, TK), lambda i: (i, 0, 0)),
                 _rows_out(r, IDX_DIM, ts, BF16)]
    res = _proj_call(
        _dsa_in_body, "dsa_in", r, ts, [x, scale, shift, g, w["gk"]], [],
        [w["gq"], w["wqt"], w["wk"], w["wv"], w["wqit"], w["wki"], w["wwit"], w["wzt"], w["bd"]], outs)
    qp, k, v, qi, ki, wi, zt = res[:7]
    if fused:
        kt, vt, kib = res[7][None], res[8][None], res[9][None]
    else:
        if past is None:
            pk = pv = pki = None
        else:
            pk, pv, pki = past[0].reshape(st.b, -1, c), past[1].reshape(st.b, -1, c), past[2]
        kt, vt, kib = _gqa_prep(st.key_source(pk, k), st.key_source(pv, v), st.key_source(pki, ki))
    if not st.decode:
        ot = _dsa_attend(_head_params(), qi[None], wi[None], qp, kib, kt, vt, tq=st.tq, q0=st.p,
                         n_real_q=st.tq, n_keys=st.n_keys, q_period=st.tq)
        ot = st.ot_cols(ot)
    else:
        heads = DSA_GROUPS * (N_HEADS // KV_HEADS)
        steps = KV_HEADS // DSA_GROUPS
        assert heads * st.s == st.tq

        def rows(a):
            return a.reshape(a.shape[0], a.shape[1], st.b, st.s).transpose(2, 0, 1, 3)

        qi_b = jnp.tile(rows(qi), (1, 1, 1, heads))
        wi_b = jnp.tile(rows(wi[None])[:, 0], (1, 1, heads))
        qp_b = rows(qp).reshape(st.b, steps, heads, KT_LANES, st.s).transpose(0, 1, 3, 2, 4)
        qp_b = qp_b.reshape(st.b * steps, KT_LANES, st.tq)
        slopes = (jnp.asarray(_alibi_slopes()) * LOG2E).reshape(steps, heads)
        hp = jnp.repeat(slopes, st.s, axis=1).reshape(steps, 1, st.tq)
        ot = _dsa_attend(hp, qi_b, wi_b, qp_b, kib, kt, vt, tq=st.tq, q0=st.p, n_real_q=st.s,
                         n_keys=st.n_keys, q_period=st.s)
        ot = ot.reshape(st.b, steps, HEAD_DIM, heads, st.s).transpose(1, 3, 2, 0, 4)
        ot = ot.reshape(N_HEADS * HEAD_DIM, st.r)
    x = _out_proj(x, gate, ot, zt, w["wo"])
    return x, (k, v, ki)


def _swa_layer(st, x, mod, g, w, past):
    shift, scale, gate = mod
    r = st.r
    ts = _row_tile(r)
    c = KV_HEADS * HEAD_DIM
    qp, k, v, zt = _proj_call(
        _swa_in_body, "swa_in", r, ts, [x, scale, shift, g, w["gk"]], [],
        [w["gq"], w["wqt"], w["wk"], w["wv"], w["wzt"], w["bd"]],
        [_qp_out(r, ts), _rows_out(r, c, ts), _rows_out(r, c, ts), _cols_out(D_MODEL, r, ts)])
    k3, v3 = k.reshape(st.b, st.s, c), v.reshape(st.b, st.s, c)
    if past is None:
        per = st.tq // WINDOW
        idx = [lambda i: jnp.maximum(per * i - 1, 0), lambda i: per * i, lambda i: per * i + 1]
        k_pieces = [(k3, WINDOW, f) for f in idx]
        v_pieces = [(v3, WINDOW, f) for f in idx]
        n_rows = WINDOW + st.tq
        new = (k3[:, st.s - WINDOW:], v3[:, st.s - WINDOW:])
    else:
        win = past[0].shape[1]
        assert win == WINDOW and st.s <= WINDOW
        pad = ((0, 0), (0, WINDOW - st.s), (0, 0))
        pk, pv = past[0].reshape(st.b, win, c).astype(F32), past[1].reshape(st.b, win, c).astype(F32)
        zero = lambda i: 0
        k_pieces = [(pk, WINDOW, zero), (jnp.pad(k3, pad), WINDOW, zero)]
        v_pieces = [(pv, WINDOW, zero), (jnp.pad(v3, pad), WINDOW, zero)]
        n_rows = win + st.s
        new = (jnp.concatenate([pk, k3], axis=1)[:, st.s:], jnp.concatenate([pv, v3], axis=1)[:, st.s:])
    ot = _swa_attend(_head_params(w["sinks"]), st.qp_blocks(qp), k_pieces, v_pieces, tq=st.tq, q0=st.p,
                     n_rows=n_rows)
    x = _out_proj(x, gate, st.ot_cols(ot), zt, w["wo"])
    return x, new


def _fox_layer(st, x, mod, g, w, past):
    shift, scale, gate = mod
    r = st.r
    ts = _row_tile(r)
    c = N_HEADS * HEAD_DIM
    qp, k, v, lf, zt = _proj_call(
        _fox_in_body, "fox_in", r, ts, [x, scale, shift, g, w["gk"], w["bf"]], [],
        [w["gq"], w["wqt"], w["wk"], w["wv"], w["wf"], w["wzt"], w["bd"]],
        [_qp_out(r, ts), _rows_out(r, c, ts), _rows_out(r, c, ts), _rows_out(r, 128, ts),
         _cols_out(D_MODEL, r, ts)])
    if past is None:
        pk = pv = plf = None
    else:
        pk, pv = past[0].reshape(st.b, -1, c), past[1].reshape(st.b, -1, c)
        plf = jnp.pad(past[2].astype(F32), ((0, 0), (0, 0), (0, 128 - N_HEADS)))
    kt, vt = _fox_prep(st.key_source(pk, k), st.key_source(pv, v), st.key_source(plf, lf))
    ot = _flash(st.qp_blocks(qp), kt, vt, tq=st.tqf, mode="causal", q0=st.p,
                n_real_q=st.tqf if not st.decode else st.s, n_keys=st.n_keys, paired=st.decode)
    x = _out_proj(x, gate, st.ot_cols(ot), zt, w["wo"])
    return x, (k, v, lf[:, :N_HEADS])


def _prep_weights(mla_w_in, mla_g_qa, mla_w_qb, mla_g_kva, mla_w_kvb, mla_g_qn, mla_g_qr, mla_g_kn,
                  mla_g_kr, mla_w_out, dsa_w_in, dsa_g_q, dsa_g_k, dsa_w_out, swa_w_in, swa_g_q,
                  swa_g_k, swa_sinks, swa_w_out, fox_w_in, fox_b_f, fox_g_q, fox_g_k, fox_w_out):
    bf = lambda a: a.astype(BF16)
    row = lambda a: a.astype(F32).reshape(1, -1)
    half = MLA_ROPE // 2
    c1, c2, c3 = MLA_Q_LORA, MLA_Q_LORA + MLA_KV_LORA, MLA_Q_LORA + MLA_KV_LORA + MLA_ROPE
    wkr = mla_w_in[:, c2:c3]
    wkrp = jnp.concatenate([wkr[:, half:], wkr[:, :half]], axis=1)
    gkr = mla_g_kr.astype(F32)
    gkrp = jnp.concatenate([gkr[half:], gkr[:half]])
    kvb = mla_w_kvb.reshape(MLA_KV_LORA, N_HEADS, MLA_NOPE + HEAD_DIM)
    mla = dict(
        wcq=bf(mla_w_in[:, :c1]), wckv=bf(mla_w_in[:, c1:c2]), wkr=bf(_pad_cols(wkr, 128)),
        wkrp=bf(_pad_cols(wkrp, 128)), wzt=bf(mla_w_in[:, c3:].T),
        gqa=row(mla_g_qa), gkva=row(mla_g_kva), gkr=row(jnp.pad(gkr, (0, 128 - MLA_ROPE))),
        gkrp=row(jnp.pad(gkrp, (0, 128 - MLA_ROPE))),
        wqt=bf(mla_w_qb.T), gqn=_col(mla_g_qn), gqr=_col(mla_g_qr),
        wkn=bf(kvb[:, :, :MLA_NOPE].reshape(MLA_KV_LORA, -1)),
        wvt=bf(kvb[:, :, MLA_NOPE:].reshape(MLA_KV_LORA, -1).T),
        gkn=_tile_gain(mla_g_kn, N_HEADS), bd=_block_diag(BD_LANES), wo=bf(mla_w_out))
    hq, hk = N_HEADS * HEAD_DIM, KV_HEADS * HEAD_DIM
    cuts = np.cumsum([hq, hk, hk, IDX_HEADS * IDX_DIM, IDX_DIM, IDX_HEADS]).tolist()
    dsa = dict(
        wqt=bf(dsa_w_in[:, :cuts[0]].T), wk=bf(dsa_w_in[:, cuts[0]:cuts[1]]),
        wv=bf(dsa_w_in[:, cuts[1]:cuts[2]]), wqit=bf(dsa_w_in[:, cuts[2]:cuts[3]].T),
        wki=bf(dsa_w_in[:, cuts[3]:cuts[4]]), wwit=bf(dsa_w_in[:, cuts[4]:cuts[5]].T),
        wzt=bf(dsa_w_in[:, cuts[5]:].T), gq=_col(dsa_g_q), gk=_tile_gain(dsa_g_k, KV_HEADS),
        bd=_block_diag(BD_LANES), wo=bf(dsa_w_out))
    swa = dict(
        wqt=bf(swa_w_in[:, :hq].T), wk=bf(swa_w_in[:, hq:hq + hk]), wv=bf(swa_w_in[:, hq + hk:hq + 2 * hk]),
        wzt=bf(swa_w_in[:, hq + 2 * hk:].T), gq=_col(swa_g_q), gk=_tile_gain(swa_g_k, KV_HEADS),
        bd=_block_diag(BD_LANES), wo=bf(swa_w_out), sinks=swa_sinks)
    fox = dict(
        wqt=bf(fox_w_in[:, :hq].T), wk=bf(fox_w_in[:, hq:2 * hq]), wv=bf(fox_w_in[:, 2 * hq:3 * hq]),
        wf=bf(_pad_cols(fox_w_in[:, 3 * hq:3 * hq + N_HEADS], 128)), wzt=bf(fox_w_in[:, 3 * hq + N_HEADS:].T),
        bf=row(jnp.pad(fox_b_f.astype(F32), (0, 128 - N_HEADS))), gq=_col(fox_g_q),
        gk=_tile_gain(fox_g_k, N_HEADS), bd=_block_diag(BD_LANES), wo=bf(fox_w_out))
    return [mla, dsa, swa, fox]


def kernel(x_prompt, x_sample, cache_mla_latent, cache_mla_krope, cache_dsa_k, cache_dsa_v, cache_dsa_kidx, state_swa_k, state_swa_v, cache_fox_k, cache_fox_v, cache_fox_logf, c_prompt, c_sample, norm_g, ada_w, ada_b, mla_w_in, mla_g_qa, mla_w_qb, mla_g_kva, mla_w_kvb, mla_g_qn, mla_g_qr, mla_g_kn, mla_g_kr, mla_w_out, dsa_w_in, dsa_g_q, dsa_g_k, dsa_w_out, swa_w_in, swa_g_q, swa_g_k, swa_sinks, swa_w_out, fox_w_in, fox_b_f, fox_g_q, fox_g_k, fox_w_out):
    bp, sp, _ = x_prompt.shape
    bs, ss, _ = x_sample.shape
    past_len = cache_mla_latent.shape[1]
    depth = norm_g.shape[0]
    assert bp == 1 and sp % TQ == 0 and sp % min(TQ_FLASH, sp) == 0 and sp % TS == 0 and (bs * ss) % 8 == 0 and ss <= TQ_DEC
    assert past_len % TK == 0 and past_len >= WINDOW

    weights = _prep_weights(mla_w_in, mla_g_qa, mla_w_qb, mla_g_kva, mla_w_kvb, mla_g_qn, mla_g_qr,
                            mla_g_kn, mla_g_kr, mla_w_out, dsa_w_in, dsa_g_q, dsa_g_k, dsa_w_out,
                            swa_w_in, swa_g_q, swa_g_k, swa_sinks, swa_w_out, fox_w_in, fox_b_f,
                            fox_g_q, fox_g_k, fox_w_out)
    rows = bp + bs
    rows_p = -(-rows // 8) * 8
    c_all = jnp.concatenate([c_prompt, c_sample, jnp.zeros((rows_p - rows, D_MODEL), F32)], axis=0)
    mod = _ada_mod(c_all, ada_w, ada_b)

    st_p = _Stream(bp, sp, 0)
    st_s = _Stream(bs, ss, past_len)
    pasts = ((cache_mla_latent, cache_mla_krope), (cache_dsa_k, cache_dsa_v, cache_dsa_kidx),
             (state_swa_k, state_swa_v), (cache_fox_k, cache_fox_v, cache_fox_logf))
    layers = (_mla_layer, _dsa_layer, _swa_layer, _fox_layer)
    xp = x_prompt.reshape(st_p.r, D_MODEL)
    xs = x_sample.reshape(st_s.r, D_MODEL)
    new_p, new_s = [], []
    for layer in range(depth):
        kind = layer % len(layers)
        g = norm_g[layer].astype(F32).reshape(1, -1)
        xp, n = layers[kind](st_p, xp, _mod_rows(st_p, mod[layer], 0), g, weights[kind], None)
        new_p.append(n)
        xs, n = layers[kind](st_s, xs, _mod_rows(st_s, mod[layer], bp), g, weights[kind], pasts[kind])
        new_s.append(n)

    def shaped(st, new):
        (lat, kr), (dk, dv, dki), (sk, sv), (fk, fv, flf) = new
        b, s = st.b, st.s
        return (lat.reshape(b, s, -1), kr.reshape(b, s, -1),
                dk.reshape(b, s, KV_HEADS, HEAD_DIM), dv.reshape(b, s, KV_HEADS, HEAD_DIM),
                dki.reshape(b, s, -1),
                sk.reshape(b, -1, KV_HEADS, HEAD_DIM), sv.reshape(b, -1, KV_HEADS, HEAD_DIM),
                fk.reshape(b, s, N_HEADS, HEAD_DIM), fv.reshape(b, s, N_HEADS, HEAD_DIM),
                flf.reshape(b, s, -1))

    return (xp.reshape(x_prompt.shape), xs.reshape(x_sample.shape)) + shaped(st_p, new_p) + shaped(st_s, new_s)
```
